```python
import math
import jax, jax.numpy as jnp
from jax import lax
import numpy as np

D_MODEL = 1024
BATCH = 8
SEQ = 2048
DEPTH = 1
DEC_BATCH = 32
DEC_SEQ = 1
PAST_LEN = 8192
PAGE_SIZE = 128

CONV_WIDTH = D_MODEL // 4
CONV_K = 3
ATTN_HEAD_DIM = 64
ATTN_HEADS = (D_MODEL - CONV_WIDTH) // ATTN_HEAD_DIM
ATTN_WIDTH = ATTN_HEADS * ATTN_HEAD_DIM
D_MIX = CONV_WIDTH + ATTN_WIDTH
D_IN_PROJ = 3 * CONV_WIDTH + 3 * ATTN_WIDTH
DILATED_PATTERNS = ((128, 1), (512, 4), (2048, 16))
WINDOW_MAX = max(w for w, _ in DILATED_PATTERNS)
ATTN_SCALE = 1.0 / math.sqrt(ATTN_HEAD_DIM)
N_BUCKETS = 32
MAX_DISTANCE = WINDOW_MAX
MEM_TOKENS = 256
MEM_HEADS = 4
MEM_HEAD_DIM = D_MODEL // MEM_HEADS
MEM_WIDTH = MEM_HEADS * MEM_HEAD_DIM
MEM_SCALE = 1.0 / math.sqrt(MEM_HEAD_DIM)
N_EXPERTS = 32
TOP_K = 4
D_FF = D_MODEL
SWIGLU_LIMIT = 7.0
SWIGLU_ALPHA = 1.702
EXPERT_BLOCK = 128
LN_EPS = 1e-5
DEEPNORM_ALPHA = (2 * DEPTH) ** 0.25
DEEPNORM_BETA = (8 * DEPTH) ** -0.25
NEG_INF = -1e30

kernel_name = 'hymba_conv_dilated_attn_moe_decoder_step'


def layer_norm(x, g, b):
    xf = x.astype(jnp.float32)
    mu = jnp.mean(xf, axis=-1, keepdims=True)
    var = jnp.mean(jnp.square(xf - mu), axis=-1, keepdims=True)
    return ((xf - mu) * lax.rsqrt(var + LN_EPS) * g.astype(jnp.float32) + b.astype(jnp.float32)).astype(x.dtype)


def rel_bucket(dist):
    max_exact = N_BUCKETS // 2
    df = jnp.maximum(dist, 1).astype(jnp.float32)
    large = max_exact + (jnp.log(df / max_exact) / math.log(MAX_DISTANCE / max_exact)
                         * (N_BUCKETS - max_exact)).astype(jnp.int32)
    return jnp.where(dist < max_exact, dist, jnp.minimum(large, N_BUCKETS - 1))


def masked_softmax(logits, valid):
    logits = jnp.where(valid, logits, NEG_INF)
    m = jnp.max(logits, axis=-1, keepdims=True)
    p = jnp.exp(logits - m)
    s = jnp.sum(p, axis=-1, keepdims=True)
    return p / s, (m + jnp.log(s))[..., 0]


def combine_dilations(outs, lses):
    w = jax.nn.softmax(jnp.stack(lses, axis=0), axis=0)
    return jnp.einsum('pbsh,pbshe->bshe', w.astype(outs[0].dtype), jnp.stack(outs, axis=0))


def split_projection(x, w_in):
    proj = x @ w_in
    gb, gc, h, q, k, v = jnp.split(
        proj, [CONV_WIDTH, 2 * CONV_WIDTH, 3 * CONV_WIDTH,
               3 * CONV_WIDTH + ATTN_WIDTH, 3 * CONV_WIDTH + 2 * ATTN_WIDTH], axis=-1)
    heads = lambda t: t.reshape(t.shape[0], t.shape[1], ATTN_HEADS, ATTN_HEAD_DIM)
    return gb, gc * h, heads(q), heads(k), heads(v)


def short_conv(u_pad, w_conv):
    S = u_pad.shape[1] - (CONV_K - 1)
    out = w_conv[0] * u_pad[:, :S]
    for i in range(1, CONV_K):
        out = out + w_conv[i] * u_pad[:, i:i + S]
    return out


def dilated_attention_prompt(q, k, v, rel_table):
    B, S, H, E = q.shape
    outs, lses = [], []
    for window, dil in DILATED_PATTERNS:
        n = window // dil
        L = S // dil
        nb = -(-L // n)
        Lq = nb * n

        def classes(t, front):
            t = t.reshape(B, L, dil, H, E)
            return jnp.pad(t, ((0, 0), (front, Lq - L), (0, 0), (0, 0), (0, 0)))

        qb = classes(q, 0).reshape(B, nb, n, dil, H, E)
        kp, vp = classes(k, n), classes(v, n)
        kb = jnp.concatenate([kp[:, :Lq].reshape(B, nb, n, dil, H, E),
                              kp[:, n:].reshape(B, nb, n, dil, H, E)], axis=2)
        vb = jnp.concatenate([vp[:, :Lq].reshape(B, nb, n, dil, H, E),
                              vp[:, n:].reshape(B, nb, n, dil, H, E)], axis=2)
        qi = jnp.arange(n)[:, None]
        kj = jnp.arange(2 * n)[None, :]
        dist = qi - kj + n
        key_sub = jnp.arange(nb)[:, None, None] * n + kj - n
        valid = (dist >= 0) & (dist <= n) & (key_sub >= 0)
        bias = rel_table[rel_bucket(jnp.clip(dist, 0, n) * dil)].astype(jnp.float32)
        logits = jnp.einsum('bnqrhe,bnkrhe->bnrhqk', qb, kb).astype(jnp.float32) * ATTN_SCALE
        logits = logits + jnp.transpose(bias, (2, 0, 1))
        probs, lse = masked_softmax(logits, valid[:, None, None])
        o = jnp.einsum('bnrhqk,bnkrhe->bnqrhe', probs.astype(vb.dtype), vb)
        outs.append(o.reshape(B, Lq, dil, H, E)[:, :L].reshape(B, S, H, E))
        lses.append(jnp.transpose(lse, (0, 1, 4, 2, 3)).reshape(B, Lq, dil, H)[:, :L].reshape(B, S, H))
    return combine_dilations(outs, lses)


def dilated_attention_sample(q, k_new, v_new, win_k, win_v, rel_table):
    DB, DS, H, E = q.shape
    W = win_k.shape[1]
    k_all = jnp.concatenate([win_k, k_new], axis=1)
    v_all = jnp.concatenate([win_v, v_new], axis=1)
    pos_q = PAST_LEN + jnp.arange(DS)
    outs, lses = [], []
    for window, dil in DILATED_PATTERNS:
        n = window // dil
        taps = jnp.arange(n + 1)
        pos_k = pos_q[:, None] - taps[None, :] * dil
        idx = jnp.where(pos_k < PAST_LEN, jnp.mod(pos_k, W), W + pos_k - PAST_LEN)
        kg = k_all[:, idx]
        vg = v_all[:, idx]
        bias = rel_table[rel_bucket(taps * dil)].astype(jnp.float32)
        logits = jnp.einsum('bqhe,bqjhe->bhqj', q, kg).astype(jnp.float32) * ATTN_SCALE
        logits = logits + jnp.transpose(bias)[:, None, :]
        probs, lse = masked_softmax(logits, (pos_k >= 0)[None, None])
        outs.append(jnp.einsum('bhqj,bqjhe->bqhe', probs.astype(vg.dtype), vg))
        lses.append(jnp.transpose(lse, (0, 2, 1)))
    return combine_dilations(outs, lses)


def memory_attention(x, mem_k, mem_v, w_q, w_o):
    B, S, _ = x.shape
    q = (x @ w_q).reshape(B, S, MEM_HEADS, MEM_HEAD_DIM)
    logits = jnp.einsum('bshe,bmhe->bhsm', q, mem_k).astype(jnp.float32) * MEM_SCALE
    probs = jax.nn.softmax(logits, axis=-1)
    o = jnp.einsum('bhsm,bmhe->bshe', probs.astype(mem_v.dtype), mem_v).reshape(B, S, MEM_WIDTH)
    return o @ w_o


def moe_ffn(x, w_router, b_router, w_gate_up, b_gate_up, w_down, b_down):
    lead = x.shape[:-1]
    xf = x.reshape(-1, D_MODEL)
    T = xf.shape[0]
    A = T * TOP_K
    logits = (xf @ w_router + b_router).astype(jnp.float32)
    top_logit, top_idx = lax.top_k(logits, TOP_K)
    gates = jax.nn.softmax(top_logit, axis=-1)
    a_exp = top_idx.reshape(A)
    a_tok = jnp.arange(A, dtype=jnp.int32) // TOP_K
    order = jnp.argsort(a_exp)
    s_exp = a_exp[order]
    counts = jnp.bincount(a_exp, length=N_EXPERTS)
    starts = jnp.cumsum(counts) - counts
    padded = (counts + EXPERT_BLOCK - 1) // EXPERT_BLOCK * EXPERT_BLOCK
    pends = jnp.cumsum(padded)
    pstarts = pends - padded
    dest = pstarts[s_exp] + jnp.arange(A, dtype=jnp.int32) - starts[s_exp]
    n_blocks = A // EXPERT_BLOCK + N_EXPERTS
    P = n_blocks * EXPERT_BLOCK
    buf_tok = jnp.full((P,), T, jnp.int32).at[dest].set(a_tok[order])
    buf_gate = jnp.zeros((P,), jnp.float32).at[dest].set(gates.reshape(A)[order])
    blk_exp = jnp.minimum(jnp.searchsorted(pends, jnp.arange(n_blocks, dtype=jnp.int32) * EXPERT_BLOCK,
                                           side='right'), N_EXPERTS - 1)
    xb = jnp.concatenate([xf, jnp.zeros((1, D_MODEL), xf.dtype)], axis=0)[buf_tok]
    xb = xb.reshape(n_blocks, EXPERT_BLOCK, D_MODEL)

    def expert_block(args):
        xblk, e = args
        gu = xblk @ w_gate_up[e] + b_gate_up[e]
        g, u = jnp.split(gu, 2, axis=-1)
        g = jnp.minimum(g, SWIGLU_LIMIT)
        u = jnp.clip(u, -SWIGLU_LIMIT, SWIGLU_LIMIT)
        act = (u + 1.0) * g * jax.nn.sigmoid(SWIGLU_ALPHA * g)
        return act @ w_down[e] + b_down[e]

    yb = lax.map(expert_block, (xb, blk_exp)).reshape(P, D_MODEL)
    y = jnp.zeros((T + 1, D_MODEL), yb.dtype).at[buf_tok].add(yb * buf_gate[:, None].astype(yb.dtype))
    return y[:T].reshape(*lead, D_MODEL)


def layer_tail(x, mixed, mem_k, mem_v, w_out, ln1_g, ln1_b, w_mem_q, w_mem_o, ln2_g, ln2_b,
               w_router, b_router, w_gate_up, b_gate_up, w_down, b_down, ln3_g, ln3_b):
    x = layer_norm(DEEPNORM_ALPHA * x + mixed @ w_out, ln1_g, ln1_b)
    x = layer_norm(DEEPNORM_ALPHA * x + memory_attention(x, mem_k, mem_v, w_mem_q, w_mem_o), ln2_g, ln2_b)
    x = layer_norm(DEEPNORM_ALPHA * x + moe_ffn(x, w_router, b_router, w_gate_up, b_gate_up, w_down, b_down),
                   ln3_g, ln3_b)
    return x


def setup_inputs(seed: int = 0) -> dict:
    key = jax.random.key(seed)
    ks = iter(jax.random.split(key, 32))

    def nrm(shape, scale):
        return scale * jax.random.normal(next(ks), shape, jnp.float32)

    win_s = min(WINDOW_MAX, PAST_LEN)
    return {
        'x_prompt': nrm((BATCH, SEQ, D_MODEL), 1.0),
        'x_sample': nrm((DEC_BATCH, DEC_SEQ, D_MODEL), 1.0),
        'mem_prompt': nrm((BATCH, MEM_TOKENS, D_MODEL), 1.0),
        'cache_win_k': nrm((DEPTH, DEC_BATCH, win_s, ATTN_HEADS, ATTN_HEAD_DIM), 1.0),
        'cache_win_v': nrm((DEPTH, DEC_BATCH, win_s, ATTN_HEADS, ATTN_HEAD_DIM), 1.0),
        'state_conv': nrm((DEPTH, DEC_BATCH, CONV_K - 1, CONV_WIDTH), 1.0),
        'cache_mem_k': nrm((DEPTH, DEC_BATCH, MEM_TOKENS, MEM_HEADS, MEM_HEAD_DIM), 1.0),
        'cache_mem_v': nrm((DEPTH, DEC_BATCH, MEM_TOKENS, MEM_HEADS, MEM_HEAD_DIM), 1.0),
        'rel_bias_table': nrm((N_BUCKETS, ATTN_HEADS), 0.2),
        'w_in': nrm((DEPTH, D_MODEL, D_IN_PROJ), D_MODEL ** -0.5),
        'w_conv': nrm((DEPTH, CONV_K, CONV_WIDTH), CONV_K ** -0.5),
        'w_out': nrm((DEPTH, D_MIX, D_MODEL), DEEPNORM_BETA * D_MIX ** -0.5),
        'ln1_g': 1.0 + nrm((DEPTH, D_MODEL), 0.02),
        'ln1_b': nrm((DEPTH, D_MODEL), 0.02),
        'w_mem_q': nrm((DEPTH, D_MODEL, MEM_WIDTH), D_MODEL ** -0.5),
        'w_mem_k': nrm((DEPTH, D_MODEL, MEM_WIDTH), D_MODEL ** -0.5),
        'w_mem_v': nrm((DEPTH, D_MODEL, MEM_WIDTH), D_MODEL ** -0.5),
        'w_mem_o': nrm((DEPTH, MEM_WIDTH, D_MODEL), DEEPNORM_BETA * MEM_WIDTH ** -0.5),
        'ln2_g': 1.0 + nrm((DEPTH, D_MODEL), 0.02),
        'ln2_b': nrm((DEPTH, D_MODEL), 0.02),
        'w_router': nrm((DEPTH, D_MODEL, N_EXPERTS), D_MODEL ** -0.5),
        'b_router': nrm((DEPTH, N_EXPERTS), 0.01),
        'w_gate_up': nrm((DEPTH, N_EXPERTS, D_MODEL, 2 * D_FF), D_MODEL ** -0.5),
        'b_gate_up': nrm((DEPTH, N_EXPERTS, 2 * D_FF), 0.01),
        'w_down': nrm((DEPTH, N_EXPERTS, D_FF, D_MODEL), DEEPNORM_BETA * D_FF ** -0.5),
        'b_down': nrm((DEPTH, N_EXPERTS, D_MODEL), 0.01),
        'ln3_g': 1.0 + nrm((DEPTH, D_MODEL), 0.02),
        'ln3_b': nrm((DEPTH, D_MODEL), 0.02),
    }


def reference(x_prompt, x_sample, mem_prompt, cache_win_k, cache_win_v, state_conv, cache_mem_k, cache_mem_v,
              rel_bias_table, w_in, w_conv, w_out, ln1_g, ln1_b, w_mem_q, w_mem_k, w_mem_v, w_mem_o,
              ln2_g, ln2_b, w_router, b_router, w_gate_up, b_gate_up, w_down, b_down, ln3_g, ln3_b):
    y_p, y_s = x_prompt, x_sample
    B, S = x_prompt.shape[0], x_prompt.shape[1]
    DB, DS = x_sample.shape[0], x_sample.shape[1]
    win_p = min(WINDOW_MAX, S)
    pk, pv, pc, pmk, pmv, sk, sv, sc = [], [], [], [], [], [], [], []
    for l in range(DEPTH):
        tail_w = (w_out[l], ln1_g[l], ln1_b[l], w_mem_q[l], w_mem_o[l], ln2_g[l], ln2_b[l],
                  w_router[l], b_router[l], w_gate_up[l], b_gate_up[l], w_down[l], b_down[l], ln3_g[l], ln3_b[l])

        gb, u, q, k, v = split_projection(y_p, w_in[l])
        u_pad = jnp.pad(u, ((0, 0), (CONV_K - 1, 0), (0, 0)))
        conv_out = gb * short_conv(u_pad, w_conv[l])
        attn_out = dilated_attention_prompt(q, k, v, rel_bias_table).reshape(B, S, ATTN_WIDTH)
        mem_k = (mem_prompt @ w_mem_k[l]).reshape(B, MEM_TOKENS, MEM_HEADS, MEM_HEAD_DIM)
        mem_v = (mem_prompt @ w_mem_v[l]).reshape(B, MEM_TOKENS, MEM_HEADS, MEM_HEAD_DIM)
        pk.append(jnp.roll(k[:, S - win_p:], S % win_p, axis=1))
        pv.append(jnp.roll(v[:, S - win_p:], S % win_p, axis=1))
        pc.append(u_pad[:, -(CONV_K - 1):])
        pmk.append(mem_k)
        pmv.append(mem_v)
        y_p = layer_tail(y_p, jnp.concatenate([conv_out, attn_out], axis=-1), mem_k, mem_v, *tail_w)

        gb, u, q, k, v = split_projection(y_s, w_in[l])
        u_pad = jnp.concatenate([state_conv[l].astype(u.dtype), u], axis=1)
        conv_out = gb * short_conv(u_pad, w_conv[l])
        attn_out = dilated_attention_sample(q, k, v, cache_win_k[l], cache_win_v[l],
                                            rel_bias_table).reshape(DB, DS, ATTN_WIDTH)
        sk.append(k)
        sv.append(v)
        sc.append(u_pad[:, -(CONV_K - 1):])
        y_s = layer_tail(y_s, jnp.concatenate([conv_out, attn_out], axis=-1), cache_mem_k[l], cache_mem_v[l],
                         *tail_w)

    return (y_p, y_s, jnp.stack(pk), jnp.stack(pv), jnp.stack(pc), jnp.stack(pmk), jnp.stack(pmv),
            jnp.stack(sk), jnp.stack(sv), jnp.stack(sc))
```

```python
import functools
import math

import jax
import jax.numpy as jnp
import numpy as np
from jax import lax
from jax.experimental import pallas as pl
from jax.experimental.pallas import tpu as pltpu

F32 = jnp.float32
BF16 = jnp.bfloat16
I32 = jnp.int32

D_MODEL = 1024
CONV_WIDTH = 256
CONV_K = 3
HEAD_DIM = 64
N_HEADS = 12
ATTN_WIDTH = N_HEADS * HEAD_DIM
PATTERNS = ((128, 1), (512, 4), (2048, 16))
BAND = 128
WINDOW_MAX = 2048
ATTN_SCALE = 1.0 / math.sqrt(HEAD_DIM)
N_BUCKETS = 32
MAX_DISTANCE = WINDOW_MAX
MEM_TOKENS = 256
MEM_HEADS = 4
MEM_HEAD_DIM = 256
MEM_SCALE = 1.0 / math.sqrt(MEM_HEAD_DIM)
N_EXPERTS = 32
TOP_K = 4
D_FF = 1024
SWIGLU_LIMIT = 7.0
SWIGLU_ALPHA = 1.702
LN_EPS = 1e-5
PAST_LEN = 8192
NEG_INF = -1e30

LANES = 128
HEADS_PER_SLAB = LANES // HEAD_DIM
N_SLABS = ATTN_WIDTH // LANES

SEQ_TILE = 512
ROW_BLOCK = 256
COMBINE_TILE = 128

HIGHEST = lax.Precision.HIGHEST


def _layer_norm(x, g, b):
    mu = jnp.mean(x, axis=-1, keepdims=True)
    var = jnp.mean(jnp.square(x - mu), axis=-1, keepdims=True)
    return (x - mu) * lax.rsqrt(var + LN_EPS) * g + b


def _dot(a, b):
    return jnp.dot(a, b, preferred_element_type=F32)


def _dot_nt(a, b):
    return lax.dot_general(a, b, (((1,), (1,)), ((), ())), preferred_element_type=F32)


def _dot_hi(a, b):
    return jnp.dot(a, b, preferred_element_type=F32, precision=HIGHEST)


def _inproj_kernel(x_ref, w_ref, wc_ref, conv_ref, q_ref, k_ref, v_ref, cs_ref, u_s):
    ts = x_ref.shape[0]
    cw = CONV_WIDTH

    @pl.when(pl.program_id(1) == 0)
    def _():
        u_s[0:8, :] = jnp.zeros((8, cw), F32)

    x = x_ref[...].astype(BF16)
    gb = _dot(x, w_ref[:, 0:cw])
    gc = _dot(x, w_ref[:, cw:2 * cw])
    h = _dot(x, w_ref[:, 2 * cw:3 * cw])
    u = gc * h
    u_s[8:8 + ts, :] = u
    wc = wc_ref[...]
    conv = wc[0:1, :] * u_s[6:6 + ts, :] + wc[1:2, :] * u_s[7:7 + ts, :] + wc[2:3, :] * u
    conv_ref[...] = (gb * conv).astype(conv_ref.dtype)
    o = 3 * cw
    q_ref[...] = _dot(x, w_ref[:, o:o + ATTN_WIDTH]) * ATTN_SCALE
    k_ref[...] = _dot(x, w_ref[:, o + ATTN_WIDTH:o + 2 * ATTN_WIDTH])
    v_ref[...] = _dot(x, w_ref[:, o + 2 * ATTN_WIDTH:o + 3 * ATTN_WIDTH])
    tail = u_s[ts:ts + 8, :]
    u_s[0:8, :] = tail
    cs_ref[...] = tail


def _inproj_prompt(x, w_in_bf, w_conv):
    B, S, D = x.shape
    ts = SEQ_TILE
    row = lambda b, s: (b, s, 0)
    return pl.pallas_call(
        _inproj_kernel,
        grid=(B, S // ts),
        in_specs=[pl.BlockSpec((None, ts, D), row),
                  pl.BlockSpec(w_in_bf.shape, lambda b, s: (0, 0)),
                  pl.BlockSpec(w_conv.shape, lambda b, s: (0, 0))],
        out_specs=[pl.BlockSpec((None, ts, CONV_WIDTH), row),
                   pl.BlockSpec((None, ts, ATTN_WIDTH), row),
                   pl.BlockSpec((None, ts, ATTN_WIDTH), row),
                   pl.BlockSpec((None, ts, ATTN_WIDTH), row),
                   pl.BlockSpec((None, 8, CONV_WIDTH), lambda b, s: (b, 0, 0))],
        out_shape=[jax.ShapeDtypeStruct((B, S, CONV_WIDTH), BF16),
                   jax.ShapeDtypeStruct((B, S, ATTN_WIDTH), F32),
                   jax.ShapeDtypeStruct((B, S, ATTN_WIDTH), F32),
                   jax.ShapeDtypeStruct((B, S, ATTN_WIDTH), F32),
                   jax.ShapeDtypeStruct((B, 8, CONV_WIDTH), F32)],
        scratch_shapes=[pltpu.VMEM((ts + 8, CONV_WIDTH), F32)],
        compiler_params=pltpu.CompilerParams(dimension_semantics=("arbitrary", "arbitrary")),
        name="inproj_prompt",
    )(x, w_in_bf, w_conv)


def _attn_kernel(q_ref, k_ref, v_ref, bias_ref, o_ref, m_s, l_s, acc_s):
    S = q_ref.shape[0]
    n = BAND
    lane = lax.broadcasted_iota(I32, (n, LANES), 1)
    head_a = lane < HEAD_DIM

    def rows_of(start, dil):
        return pl.ds(start, n) if dil == 1 else pl.ds(start, n, stride=dil)

    def block(p, dil, start, has_prev, first_pattern):
        rows = rows_of(start, dil)
        qb = q_ref[rows, :]
        kc = k_ref[rows, :].astype(BF16)
        vc = v_ref[rows, :].astype(BF16)
        if has_prev:
            prow = rows_of(start - n * dil, dil)
            kp = k_ref[prow, :].astype(BF16)
            vp = v_ref[prow, :].astype(BF16)
        ms, ls, accs = [], [], []
        for hh in range(HEADS_PER_SLAB):
            sel = head_a if hh == 0 else jnp.logical_not(head_a)
            qh = jnp.where(sel, qb, 0.0).astype(BF16)
            s_cur = _dot_nt(qh, kc) + bias_ref[p, hh, :, n:2 * n]
            m = jnp.max(s_cur, axis=-1, keepdims=True)
            if has_prev:
                s_prev = _dot_nt(qh, kp) + bias_ref[p, hh, :, 0:n]
                m = jnp.maximum(m, jnp.max(s_prev, axis=-1, keepdims=True))
            e_cur = jnp.exp(s_cur - m)
            l = jnp.sum(e_cur, axis=-1, keepdims=True)
            acc = _dot(e_cur.astype(BF16), vc)
            if has_prev:
                e_prev = jnp.exp(s_prev - m)
                l = l + jnp.sum(e_prev, axis=-1, keepdims=True)
                acc = acc + _dot(e_prev.astype(BF16), vp)
            ms.append(m)
            ls.append(l)
            accs.append(acc)
        m_new = jnp.where(head_a, ms[0], ms[1])
        l_new = jnp.where(head_a, ls[0], ls[1])
        acc_new = jnp.where(head_a, accs[0], accs[1])
        if first_pattern:
            m_s[rows, :] = m_new
            l_s[rows, :] = l_new
            acc_s[rows, :] = acc_new
        else:
            m_old = m_s[rows, :]
            m_tot = jnp.maximum(m_old, m_new)
            a = jnp.exp(m_old - m_tot)
            b = jnp.exp(m_new - m_tot)
            m_s[rows, :] = m_tot
            l_s[rows, :] = a * l_s[rows, :] + b * l_new
            acc_s[rows, :] = a * acc_s[rows, :] + b * acc_new

    for p, (window, dil) in enumerate(PATTERNS):
        nb = (S // dil) // n

        def per_class(r, carry, p=p, dil=dil, nb=nb):
            block(p, dil, r, False, p == 0)

            def per_block(i, c):
                block(p, dil, i * (n * dil) + r, True, p == 0)
                return c

            if nb > 1:
                lax.fori_loop(1, nb, per_block, 0)
            return carry

        if dil == 1:
            per_class(0, 0)
        else:
            lax.fori_loop(0, dil, per_class, 0)

    o_ref[...] = (acc_s[...] / l_s[...]).astype(o_ref.dtype)


def _attention_prompt(q, k, v, bias_tab):
    B, S, _ = q.shape
    slab = lambda p, b: (b, 0, p)
    spec = pl.BlockSpec((None, S, LANES), slab)
    return pl.pallas_call(
        _attn_kernel,
        grid=(N_SLABS, B),
        in_specs=[spec, spec, spec,
                  pl.BlockSpec((len(PATTERNS), HEADS_PER_SLAB, BAND, 2 * BAND), lambda p, b: (0, p, 0, 0))],
        out_specs=spec,
        out_shape=jax.ShapeDtypeStruct((B, S, ATTN_WIDTH), BF16),
        scratch_shapes=[pltpu.VMEM((S, LANES), F32)] * 3,
        compiler_params=pltpu.CompilerParams(dimension_semantics=("arbitrary", "arbitrary")),
        name="dilated_attn_prompt",
    )(q, k, v, bias_tab)


def _memkv_kernel(m_ref, wk_ref, wv_ref, k_ref, v_ref, kb_ref, vb_ref):
    x = m_ref[...].astype(BF16)
    k = _dot(x, wk_ref[...])
    v = _dot(x, wv_ref[...])
    k_ref[...] = k
    v_ref[...] = v
    kb_ref[...] = k.astype(BF16)
    vb_ref[...] = v.astype(BF16)


def _memkv_prompt(mem, wk_bf, wv_bf):
    B, M, D = mem.shape
    row = lambda b: (b, 0, 0)
    full = lambda b: (0, 0)
    return pl.pallas_call(
        _memkv_kernel,
        grid=(B,),
        in_specs=[pl.BlockSpec((None, M, D), row), pl.BlockSpec((D, D), full), pl.BlockSpec((D, D), full)],
        out_specs=[pl.BlockSpec((None, M, D), row)] * 4,
        out_shape=[jax.ShapeDtypeStruct((B, M, D), F32)] * 2 + [jax.ShapeDtypeStruct((B, M, D), BF16)] * 2,
        compiler_params=pltpu.CompilerParams(dimension_semantics=("arbitrary",)),
        name="memkv_prompt",
    )(mem, wk_bf, wv_bf)


def _route(logits_t, carry):
    E, nt = logits_t.shape
    eidx = lax.broadcasted_iota(I32, (E, nt), 0)
    l = logits_t
    vals, idxs = [], []
    for _ in range(TOP_K):
        m = jnp.max(l, axis=0, keepdims=True)
        sel = jnp.min(jnp.where(l == m, eidx, E), axis=0, keepdims=True)
        vals.append(m)
        idxs.append(sel)
        l = jnp.where(eidx == sel, -jnp.inf, l)
    es = [jnp.exp(v - vals[0]) for v in vals]
    den = es[0] + es[1] + es[2] + es[3]
    gates = _stack_rows([e / den for e in es])
    chosen = (l == -jnp.inf)
    onehot = jnp.where(chosen, 1.0, 0.0)
    before = lax.broadcasted_iota(I32, (nt, nt), 0) < lax.broadcasted_iota(I32, (nt, nt), 1)
    prefix = _dot(onehot.astype(BF16), jnp.where(before, 1.0, 0.0).astype(BF16)) + carry
    ranks = [jnp.sum(jnp.where(eidx == s, prefix, 0.0), axis=0, keepdims=True) for s in idxs]
    rank = _stack_rows(ranks).astype(I32)
    idx = _stack_rows(idxs)
    return idx, gates, rank, carry + jnp.sum(onehot, axis=1, keepdims=True)


def _stack_rows(rows):
    k, nt = len(rows), rows[0].shape[1]
    r = lax.broadcasted_iota(I32, (k, nt), 0)
    out = jnp.broadcast_to(rows[-1], (k, nt))
    for i in range(k - 2, -1, -1):
        out = jnp.where(r == i, rows[i], out)
    return out


def _mid_kernel(x_ref, conv_ref, attn_ref, woc_ref, woa_ref, g1_ref, b1_ref, mk_ref, mv_ref, wq_ref, wmo_ref,
                g2_ref, b2_ref, wrt_ref, br_ref,
                x2_ref, idx_ref, gate_ref, rank_ref, cnt_ref, carry_s, *, alpha):
    first = jnp.logical_and(pl.program_id(0) == 0, pl.program_id(1) == 0)

    @pl.when(first)
    def _():
        carry_s[...] = jnp.zeros_like(carry_s)

    x = x_ref[...]
    mix = _dot(conv_ref[...], woc_ref[...]) + _dot(attn_ref[...], woa_ref[...])
    x1 = _layer_norm(alpha * x + mix, g1_ref[...], b1_ref[...])
    qm = (_dot(x1.astype(BF16), wq_ref[...]) * MEM_SCALE).astype(BF16)
    outs = []
    for h in range(MEM_HEADS):
        hs = slice(h * MEM_HEAD_DIM, (h + 1) * MEM_HEAD_DIM)
        s = _dot_nt(qm[:, hs], mk_ref[:, hs])
        m = jnp.max(s, axis=-1, keepdims=True)
        e = jnp.exp(s - m)
        l = jnp.sum(e, axis=-1, keepdims=True)
        outs.append((_dot(e.astype(BF16), mv_ref[:, hs]) / l).astype(BF16))
    o = jnp.concatenate(outs, axis=-1)
    x2 = _layer_norm(alpha * x1 + _dot(o, wmo_ref[...]), g2_ref[...], b2_ref[...])
    x2_ref[...] = x2
    logits_t = _dot_nt(wrt_ref[...], x2.astype(BF16)) + br_ref[...]
    idx, gates, rank, carry = _route(logits_t, carry_s[:, 0:1])
    idx_ref[...] = idx
    gate_ref[...] = gates
    rank_ref[...] = rank
    carry_s[...] = jnp.broadcast_to(carry, carry_s.shape)
    cnt_ref[...] = carry_s[...]


def _mid_prompt(x, conv, attn, woc, woa, g1, b1, mk, mv, wq, wmo, g2, b2, wrt, br, alpha):
    B, S, D = x.shape
    ts = SEQ_TILE
    T = B * S
    row = lambda b, s: (b, s, 0)
    full = lambda b, s: (0, 0)
    tok = lambda b, s: (0, b * (S // ts) + s)
    mem = lambda b, s: (b, 0, 0)
    vec = pl.BlockSpec((1, D), full)
    return pl.pallas_call(
        functools.partial(_mid_kernel, alpha=alpha),
        grid=(B, S // ts),
        in_specs=[pl.BlockSpec((None, ts, D), row),
                  pl.BlockSpec((None, ts, CONV_WIDTH), row),
                  pl.BlockSpec((None, ts, ATTN_WIDTH), row),
                  pl.BlockSpec(woc.shape, full), pl.BlockSpec(woa.shape, full), vec, vec,
                  pl.BlockSpec((None, MEM_TOKENS, D), mem), pl.BlockSpec((None, MEM_TOKENS, D), mem),
                  pl.BlockSpec(wq.shape, full), pl.BlockSpec(wmo.shape, full), vec, vec,
                  pl.BlockSpec(wrt.shape, full), pl.BlockSpec(br.shape, full)],
        out_specs=[pl.BlockSpec((None, ts, D), row),
                   pl.BlockSpec((TOP_K, ts), tok), pl.BlockSpec((TOP_K, ts), tok), pl.BlockSpec((TOP_K, ts), tok),
                   pl.BlockSpec((N_EXPERTS, LANES), full)],
        out_shape=[jax.ShapeDtypeStruct((B, S, D), F32),
                   jax.ShapeDtypeStruct((TOP_K, T), I32),
                   jax.ShapeDtypeStruct((TOP_K, T), F32),
                   jax.ShapeDtypeStruct((TOP_K, T), I32),
                   jax.ShapeDtypeStruct((N_EXPERTS, LANES), F32)],
        scratch_shapes=[pltpu.VMEM((N_EXPERTS, LANES), F32)],
        compiler_params=pltpu.CompilerParams(dimension_semantics=("arbitrary", "arbitrary")),
        name="mid_prompt",
    )(x, conv, attn, woc, woa, g1, b1, mk, mv, wq, wmo, g2, b2, wrt, br)


def _round_bf16(x):
    return x.astype(BF16).astype(F32)


def _inproj_sample_kernel(x_ref, w_ref, o_ref):
    o_ref[...] = _dot(x_ref[...].astype(BF16), w_ref[...])


def _inproj_sample(x, w_in):
    n, D = x.shape
    N = w_in.shape[1]
    bn = 512
    return pl.pallas_call(
        _inproj_sample_kernel,
        grid=(N // bn,),
        in_specs=[pl.BlockSpec((n, D), lambda j: (0, 0)), pl.BlockSpec((D, bn), lambda j: (0, j))],
        out_specs=pl.BlockSpec((n, bn), lambda j: (0, j)),
        out_shape=jax.ShapeDtypeStruct((n, N), F32),
        compiler_params=pltpu.CompilerParams(dimension_semantics=("arbitrary",)),
        name="inproj_sample",
    )(x, w_in)


def _attn_sample_kernel(q_ref, kn_ref, vn_ref, k0_ref, k1_ref, k2_ref, v0_ref, v1_ref, v2_ref,
                        bias_ref, bnew_ref, seg_ref, segt_ref, o_ref):
    q = _round_bf16(q_ref[...])
    seg = seg_ref[...]
    segt = segt_ref[...]
    k_refs = (k0_ref, k1_ref, k2_ref)
    v_refs = (v0_ref, v1_ref, v2_ref)
    s_new = _dot_hi(q * _round_bf16(kn_ref[...]), seg) * ATTN_SCALE
    v_new = _round_bf16(vn_ref[...])
    outs, lses = [], []
    for p in range(len(PATTERNS)):
        s = _dot_hi(_round_bf16(k_refs[p][...]) * q, seg) * ATTN_SCALE + bias_ref[p]
        sn = s_new + bnew_ref[p]
        m = jnp.maximum(jnp.max(s, axis=0, keepdims=True), sn)
        e = jnp.exp(s - m)
        en = jnp.exp(sn - m)
        den = jnp.sum(e, axis=0, keepdims=True) + en
        pr = _dot_hi(_round_bf16(e / den), segt)
        prn = _dot_hi(_round_bf16(en / den), segt)
        outs.append(jnp.sum(pr * _round_bf16(v_refs[p][...]), axis=0, keepdims=True) + prn * v_new)
        lses.append(m + jnp.log(den))
    lmax = jnp.maximum(jnp.maximum(lses[0], lses[1]), lses[2])
    ws = [jnp.exp(ls - lmax) for ls in lses]
    wsum = ws[0] + ws[1] + ws[2]
    acc = None
    for p in range(len(PATTERNS)):
        term = _dot_hi(_round_bf16(ws[p] / wsum), segt) * _round_bf16(outs[p])
        acc = term if acc is None else acc + term
    o_ref[...] = acc


def _attention_sample(q, k_new, v_new, win_k, win_v, bias_s, bias_new, seg, segt):
    DB, W, AW = win_k.shape
    n = BAND
    row = lambda b: (b, 0, 0)
    views, specs = [], []
    for window, dil in PATTERNS:
        first_slot = (PAST_LEN - window) % W
        views.append((W // dil, dil * AW))
        specs.append(pl.BlockSpec((None, n, AW), functools.partial(lambda b, blk: (b, blk, 0), blk=first_slot // dil // n)))
    cache_in, cache_specs = [], []
    for arr in (win_k, win_v):
        for (rows, cols), spec in zip(views, specs):
            cache_in.append(arr.reshape(DB, rows, cols))
            cache_specs.append(spec)
    vec = pl.BlockSpec((None, 1, AW), row)
    const3 = lambda a: pl.BlockSpec(a.shape, lambda b: (0,) * a.ndim)
    return pl.pallas_call(
        _attn_sample_kernel,
        grid=(DB,),
        in_specs=[vec, vec, vec] + cache_specs + [const3(bias_s), const3(bias_new), const3(seg), const3(segt)],
        out_specs=vec,
        out_shape=jax.ShapeDtypeStruct((DB, 1, AW), F32),
        compiler_params=pltpu.CompilerParams(dimension_semantics=("arbitrary",)),
        name="dilated_attn_sample",
    )(q, k_new, v_new, *cache_in, bias_s, bias_new, seg, segt)


def _tail_a_sample_kernel(x_ref, proj_ref, s0_ref, s1_ref, wc_ref, attn_ref, wo_ref, g1_ref, b1_ref, wq_ref,
                          x1_ref, qm_ref, u_ref, *, alpha):
    cw = CONV_WIDTH
    gb = proj_ref[:, 0:cw]
    u = proj_ref[:, cw:2 * cw] * proj_ref[:, 2 * cw:3 * cw]
    u_ref[...] = u
    wc = wc_ref[...]
    conv = gb * (wc[0:1, :] * s0_ref[...] + wc[1:2, :] * s1_ref[...] + wc[2:3, :] * u)
    mixed = jnp.concatenate([conv, attn_ref[...]], axis=-1).astype(BF16)
    x1 = _layer_norm(alpha * x_ref[...] + _dot(mixed, wo_ref[...]), g1_ref[...], b1_ref[...])
    x1_ref[...] = x1
    qm_ref[...] = _dot(x1.astype(BF16), wq_ref[...])


def _tail_a_sample(x, proj, s0, s1, w_conv, attn, w_out, g1, b1, w_mem_q, alpha):
    n, D = x.shape
    return pl.pallas_call(
        functools.partial(_tail_a_sample_kernel, alpha=alpha),
        out_shape=[jax.ShapeDtypeStruct((n, D), F32)] * 2 + [jax.ShapeDtypeStruct((n, CONV_WIDTH), F32)],
        name="tail_a_sample",
    )(x, proj, s0, s1, w_conv, attn, w_out, g1, b1, w_mem_q)


def _memattn_sample_kernel(q_ref, k_ref, v_ref, seg_ref, segt_ref, o_ref):
    q = _round_bf16(q_ref[...])
    s = _dot_hi(_round_bf16(k_ref[...]) * q, seg_ref[...]) * MEM_SCALE
    m = jnp.max(s, axis=0, keepdims=True)
    e = jnp.exp(s - m)
    den = jnp.sum(e, axis=0, keepdims=True)
    pr = _dot_hi(_round_bf16(e / den), segt_ref[...])
    o_ref[...] = jnp.sum(pr * _round_bf16(v_ref[...]), axis=0, keepdims=True)


def _memattn_sample(qm, mem_k, mem_v, seg, segt):
    DB, M, D = mem_k.shape
    row = lambda b: (b, 0, 0)
    vec = pl.BlockSpec((None, 1, D), row)
    const = lambda a: pl.BlockSpec(a.shape, lambda b: (0, 0))
    return pl.pallas_call(
        _memattn_sample_kernel,
        grid=(DB,),
        in_specs=[vec, pl.BlockSpec((None, M, D), row), pl.BlockSpec((None, M, D), row), const(seg), const(segt)],
        out_specs=vec,
        out_shape=jax.ShapeDtypeStruct((DB, 1, D), F32),
        compiler_params=pltpu.CompilerParams(dimension_semantics=("arbitrary",)),
        name="memattn_sample",
    )(qm, mem_k, mem_v, seg, segt)


def _tail_b_sample_kernel(x1_ref, o_ref, wmo_ref, g2_ref, b2_ref, wrt_ref, br_ref, cnt_ref,
                          x2_ref, idx_ref, gate_ref, rank_ref, cnt_out_ref, *, alpha):
    x2 = _layer_norm(alpha * x1_ref[...] + _dot(o_ref[...].astype(BF16), wmo_ref[...]), g2_ref[...], b2_ref[...])
    x2_ref[...] = x2
    logits_t = _dot_nt(wrt_ref[...], x2.astype(BF16)) + br_ref[...]
    idx, gates, rank, carry = _route(logits_t, cnt_ref[:, 0:1])
    idx_ref[...] = idx
    gate_ref[...] = gates
    rank_ref[...] = rank
    cnt_out_ref[...] = jnp.broadcast_to(carry, cnt_out_ref.shape)


def _tail_b_sample(x1, o, w_mem_o, g2, b2, wrt, br, cnt, alpha):
    n, D = x1.shape
    return pl.pallas_call(
        functools.partial(_tail_b_sample_kernel, alpha=alpha),
        out_shape=[jax.ShapeDtypeStruct((n, D), F32),
                   jax.ShapeDtypeStruct((TOP_K, n), I32),
                   jax.ShapeDtypeStruct((TOP_K, n), F32),
                   jax.ShapeDtypeStruct((TOP_K, n), I32),
                   jax.ShapeDtypeStruct(cnt.shape, F32)],
        name="tail_b_sample",
    )(x1, o, w_mem_o, g2, b2, wrt, br, cnt)


def _row_copy(src, dst, sem, src_row, dst_row):
    return pltpu.make_async_copy(src.at[pl.ds(src_row, 1)], dst.at[pl.ds(dst_row, 1)], sem)


def _dispatch_kernel(dest_ref, x_ref, xb_in_ref, xb_ref, sem):
    del xb_in_ref
    nt = x_ref.shape[0]

    def issue(t, c):
        for k in range(TOP_K):
            _row_copy(x_ref, xb_ref, sem, t, dest_ref[0, k * nt + t]).start()
        return c

    lax.fori_loop(0, nt, issue, 0)
    for k in range(TOP_K):
        pltpu.make_async_copy(x_ref, xb_ref.at[pl.ds(0, nt)], sem).wait()


def _dispatch(x2, dest_tiles, xb):
    T, D = x2.shape
    nt = dest_tiles.shape[2] // TOP_K
    return pl.pallas_call(
        _dispatch_kernel,
        grid=(T // nt,),
        in_specs=[pl.BlockSpec((None, 1, TOP_K * nt), lambda i: (i, 0, 0), memory_space=pltpu.SMEM),
                  pl.BlockSpec((nt, D), lambda i: (i, 0)),
                  pl.BlockSpec(memory_space=pl.ANY)],
        out_specs=pl.BlockSpec(memory_space=pl.ANY),
        out_shape=jax.ShapeDtypeStruct(xb.shape, xb.dtype),
        scratch_shapes=[pltpu.SemaphoreType.DMA],
        input_output_aliases={2: 0},
        compiler_params=pltpu.CompilerParams(dimension_semantics=("arbitrary",)),
        name="moe_dispatch",
    )(dest_tiles, x2, xb)


def _moe_kernel(be_ref, nu_ref, xb_ref, wgu_ref, bgu_ref, wd_ref, bd_ref, yb_ref, wgu_s, wd_s):
    j = pl.program_id(0)
    prev = be_ref[jnp.maximum(j - 1, 0)]
    new_expert = jnp.logical_or(j == 0, be_ref[j] != prev)
    used = j < nu_ref[0]

    @pl.when(jnp.logical_and(used, new_expert))
    def _():
        wgu_s[...] = wgu_ref[...].astype(BF16)
        wd_s[...] = wd_ref[...].astype(BF16)

    @pl.when(used)
    def _():
        x = xb_ref[...].astype(BF16)
        gu = _dot(x, wgu_s[...]) + bgu_ref[...]
        g = jnp.minimum(gu[:, :D_FF], SWIGLU_LIMIT)
        u = jnp.clip(gu[:, D_FF:], -SWIGLU_LIMIT, SWIGLU_LIMIT)
        act = (u + 1.0) * g * jax.nn.sigmoid(SWIGLU_ALPHA * g)
        yb_ref[...] = _dot(act.astype(BF16), wd_s[...]) + bd_ref[...]

    @pl.when(jnp.logical_not(used))
    def _():
        yb_ref[...] = jnp.zeros_like(yb_ref)


def _moe(blk_exp, n_used, xb, w_gate_up, b_gate_up, w_down, b_down):
    P, D = xb.shape
    bm = ROW_BLOCK
    E, _, F2 = w_gate_up.shape
    ex = lambda j, be, nu: (be[j], 0, 0)
    return pl.pallas_call(
        _moe_kernel,
        grid_spec=pltpu.PrefetchScalarGridSpec(
            num_scalar_prefetch=2,
            grid=(P // bm,),
            in_specs=[pl.BlockSpec((bm, D), lambda j, be, nu: (j, 0)),
                      pl.BlockSpec((None, D, F2), ex), pl.BlockSpec((None, 1, F2), ex),
                      pl.BlockSpec((None, D_FF, D), ex), pl.BlockSpec((None, 1, D), ex)],
            out_specs=pl.BlockSpec((bm, D), lambda j, be, nu: (j, 0)),
            scratch_shapes=[pltpu.VMEM((D, F2), BF16), pltpu.VMEM((D_FF, D), BF16)]),
        out_shape=jax.ShapeDtypeStruct((P, D), F32),
        compiler_params=pltpu.CompilerParams(dimension_semantics=("arbitrary",),
                                             vmem_limit_bytes=56 * 1024 * 1024),
        name="moe_grouped_ffn",
    )(blk_exp, n_used, xb, w_gate_up, b_gate_up.reshape(E, 1, F2), w_down, b_down.reshape(E, 1, D))


def _combine_kernel(dest_ref, yb_ref, x2_ref, gate_ref, g3_ref, b3_ref, o_ref, rows_s, sem, *, alpha):
    nt = x2_ref.shape[0]

    def issue(t, c):
        for k in range(TOP_K):
            _row_copy(yb_ref, rows_s.at[k], sem, dest_ref[0, k * nt + t], t).start()
        return c

    lax.fori_loop(0, nt, issue, 0)
    for k in range(TOP_K):
        pltpu.make_async_copy(yb_ref.at[pl.ds(0, nt)], rows_s.at[k], sem).wait()
    gates = gate_ref[...]
    y = gates[:, 0:1] * rows_s[0]
    for k in range(1, TOP_K):
        y = y + gates[:, k:k + 1] * rows_s[k]
    o_ref[...] = _layer_norm(alpha * x2_ref[...] + y, g3_ref[...], b3_ref[...])


def _combine(yb, dest_tiles, x2, gates_tok, g3, b3, alpha):
    T, D = x2.shape
    nt = dest_tiles.shape[2] // TOP_K
    return pl.pallas_call(
        functools.partial(_combine_kernel, alpha=alpha),
        grid=(T // nt,),
        in_specs=[pl.BlockSpec((None, 1, TOP_K * nt), lambda i: (i, 0, 0), memory_space=pltpu.SMEM),
                  pl.BlockSpec(memory_space=pl.ANY),
                  pl.BlockSpec((nt, D), lambda i: (i, 0)),
                  pl.BlockSpec((nt, TOP_K), lambda i: (i, 0)),
                  pl.BlockSpec((1, D), lambda i: (0, 0)), pl.BlockSpec((1, D), lambda i: (0, 0))],
        out_specs=pl.BlockSpec((nt, D), lambda i: (i, 0)),
        out_shape=jax.ShapeDtypeStruct((T, D), F32),
        scratch_shapes=[pltpu.VMEM((TOP_K, nt, D), F32), pltpu.SemaphoreType.DMA],
        compiler_params=pltpu.CompilerParams(dimension_semantics=("arbitrary",)),
        name="moe_combine",
    )(dest_tiles, yb, x2, gates_tok, g3, b3)


def _rel_bucket(dist):
    max_exact = N_BUCKETS // 2
    df = jnp.maximum(dist, 1).astype(F32)
    large = max_exact + (jnp.log(df / max_exact) / math.log(MAX_DISTANCE / max_exact)
                         * (N_BUCKETS - max_exact)).astype(I32)
    return jnp.where(dist < max_exact, dist, jnp.minimum(large, N_BUCKETS - 1))


def _bias_tables(rel_table):
    n = BAND
    taps = jnp.arange(n + 1)
    by_tap = jnp.stack([rel_table[_rel_bucket(taps * dil)].astype(F32) for _, dil in PATTERNS])
    dist = np.arange(n)[:, None] - np.arange(2 * n)[None, :] + n
    valid = (dist >= 0) & (dist <= n)
    band = by_tap[:, np.clip(dist, 0, n)]
    band = jnp.where(valid[None, :, :, None], band, NEG_INF)
    return by_tap, jnp.transpose(band, (0, 3, 1, 2))


def _head_membership(n_heads, head_dim):
    m = np.zeros((n_heads * head_dim, LANES), np.float32)
    m[np.arange(n_heads * head_dim), np.arange(n_heads * head_dim) // head_dim] = 1.0
    return m


def _dest_tiles(dest, nt):
    T = dest.shape[1]
    return dest.reshape(TOP_K, T // nt, nt).transpose(1, 0, 2).reshape(T // nt, 1, TOP_K * nt)


def kernel(x_prompt, x_sample, mem_prompt, cache_win_k, cache_win_v, state_conv, cache_mem_k, cache_mem_v,
           rel_bias_table, w_in, w_conv, w_out, ln1_g, ln1_b, w_mem_q, w_mem_k, w_mem_v, w_mem_o,
           ln2_g, ln2_b, w_router, b_router, w_gate_up, b_gate_up, w_down, b_down, ln3_g, ln3_b):
    depth = w_in.shape[0]
    assert depth == 1
    alpha = (2 * depth) ** 0.25
    B, S, D = x_prompt.shape
    DB = x_sample.shape[0]
    T = B * S
    l = 0
    vec = lambda a: a[l].reshape(1, -1)

    by_tap, bias_band = _bias_tables(rel_bias_table)
    wrt = w_router[l].T.astype(BF16)
    br = b_router[l].reshape(N_EXPERTS, 1)
    w_in_bf = w_in[l].astype(BF16)
    wo_bf = w_out[l].astype(BF16)
    wq_bf = w_mem_q[l].astype(BF16)
    wmo_bf = w_mem_o[l].astype(BF16)

    conv_p, q_p, k_p, v_p, cs_p = _inproj_prompt(x_prompt, w_in_bf, w_conv[l])
    attn_p = _attention_prompt(q_p, k_p, v_p, bias_band)
    mk, mv, mk_bf, mv_bf = _memkv_prompt(mem_prompt, w_mem_k[l].astype(BF16), w_mem_v[l].astype(BF16))
    x2_p, idx_p, gate_p, rank_p, cnt_p = _mid_prompt(
        x_prompt, conv_p, attn_p, wo_bf[:CONV_WIDTH], wo_bf[CONV_WIDTH:], vec(ln1_g), vec(ln1_b), mk_bf, mv_bf,
        wq_bf, wmo_bf, vec(ln2_g), vec(ln2_b), wrt, br, alpha)

    xs = x_sample.reshape(DB, D)
    proj = _inproj_sample(xs, w_in_bf)
    cw = CONV_WIDTH
    q_s = proj[:, 3 * cw:3 * cw + ATTN_WIDTH]
    k_s = proj[:, 3 * cw + ATTN_WIDTH:3 * cw + 2 * ATTN_WIDTH]
    v_s = proj[:, 3 * cw + 2 * ATTN_WIDTH:]
    seg = jnp.asarray(_head_membership(N_HEADS, HEAD_DIM))
    seg_m = jnp.asarray(_head_membership(MEM_HEADS, MEM_HEAD_DIM))
    pad_heads = lambda a: jnp.pad(a, ((0, 0), (0, 0), (0, LANES - N_HEADS)))
    bias_s = pad_heads(by_tap[:, :0:-1])
    bias_new = pad_heads(by_tap[:, 0:1])
    attn_s = _attention_sample(q_s.reshape(DB, 1, -1), k_s.reshape(DB, 1, -1), v_s.reshape(DB, 1, -1),
                               cache_win_k[l].reshape(DB, WINDOW_MAX, ATTN_WIDTH),
                               cache_win_v[l].reshape(DB, WINDOW_MAX, ATTN_WIDTH),
                               bias_s, bias_new, seg, seg.T).reshape(DB, ATTN_WIDTH)
    sc = state_conv[l]
    x1_s, qm_s, u_s = _tail_a_sample(xs, proj, sc[:, 0], sc[:, 1], w_conv[l], attn_s, wo_bf,
                                     vec(ln1_g), vec(ln1_b), wq_bf, alpha)
    o_s = _memattn_sample(qm_s.reshape(DB, 1, D), cache_mem_k[l].reshape(DB, MEM_TOKENS, D),
                          cache_mem_v[l].reshape(DB, MEM_TOKENS, D), seg_m, seg_m.T).reshape(DB, D)
    x2_s, idx_s, gate_s, rank_s, cnt = _tail_b_sample(x1_s, o_s, wmo_bf, vec(ln2_g), vec(ln2_b), wrt, br,
                                                      cnt_p, alpha)

    bm = ROW_BLOCK
    counts = cnt[:, 0].astype(I32)
    padded = (counts + bm - 1) // bm * bm
    pends = jnp.cumsum(padded)
    pstarts = pends - padded
    n_blocks = -(-(T + DB) * TOP_K // bm) + N_EXPERTS
    P = n_blocks * bm
    blk_exp = jnp.minimum(jnp.searchsorted(pends, jnp.arange(n_blocks, dtype=I32) * bm, side='right'),
                          N_EXPERTS - 1).astype(I32)
    n_used = (pends[-1] // bm).reshape(1).astype(I32)
    dest_p = pstarts[idx_p] + rank_p
    dest_s = pstarts[idx_s] + rank_s
    dtiles_p = _dest_tiles(dest_p, COMBINE_TILE)
    dtiles_s = _dest_tiles(dest_s, DB)

    x2_pf = x2_p.reshape(T, D)
    xb = jnp.zeros((P, D), F32)
    xb = _dispatch(x2_pf, dtiles_p, xb)
    xb = _dispatch(x2_s, dtiles_s, xb)
    yb = _moe(blk_exp, n_used, xb, w_gate_up[l], b_gate_up[l], w_down[l], b_down[l])
    y_p = _combine(yb, dtiles_p, x2_pf, gate_p.T, vec(ln3_g), vec(ln3_b), alpha).reshape(B, S, D)
    y_s = _combine(yb, dtiles_s, x2_s, gate_s.T, vec(ln3_g), vec(ln3_b), alpha).reshape(DB, 1, D)

    H, E = N_HEADS, HEAD_DIM
    conv_state_s = jnp.stack([sc[:, 1], u_s], axis=1)
    return (y_p, y_s,
            k_p.reshape(1, B, S, H, E), v_p.reshape(1, B, S, H, E), cs_p[:, 6:8].reshape(1, B, CONV_K - 1, CONV_WIDTH),
            mk.reshape(1, B, MEM_TOKENS, MEM_HEADS, MEM_HEAD_DIM), mv.reshape(1, B, MEM_TOKENS, MEM_HEADS, MEM_HEAD_DIM),
            k_s.reshape(1, DB, 1, H, E), v_s.reshape(1, DB, 1, H, E), conv_state_s.reshape(1, DB, CONV_K - 1, CONV_WIDTH))
```

```python
import functools
import math

import jax
import jax.numpy as jnp
import numpy as np
from jax import lax
from jax.experimental import pallas as pl
from jax.experimental.pallas import tpu as pltpu

F32 = jnp.float32
BF16 = jnp.bfloat16
I32 = jnp.int32

D_MODEL = 1024
CONV_WIDTH = 256
CONV_K = 3
HEAD_DIM = 64
N_HEADS = 12
ATTN_WIDTH = N_HEADS * HEAD_DIM
PATTERNS = ((128, 1), (512, 4), (2048, 16))
BAND = 128
WINDOW_MAX = 2048
ATTN_SCALE = 1.0 / math.sqrt(HEAD_DIM)
N_BUCKETS = 32
MAX_DISTANCE = WINDOW_MAX
MEM_TOKENS = 256
MEM_HEADS = 4
MEM_HEAD_DIM = 256
MEM_SCALE = 1.0 / math.sqrt(MEM_HEAD_DIM)
N_EXPERTS = 32
TOP_K = 4
D_FF = 1024
SWIGLU_LIMIT = 7.0
SWIGLU_ALPHA = 1.702
LN_EPS = 1e-5
PAST_LEN = 8192
NEG_INF = -1e30

LANES = 128
HEADS_PER_SLAB = LANES // HEAD_DIM
N_SLABS = ATTN_WIDTH // LANES

SEQ_TILE = 512
ROW_BLOCK = 256
COMBINE_TILE = 128

HIGHEST = lax.Precision.HIGHEST


def _layer_norm(x, g, b):
    mu = jnp.mean(x, axis=-1, keepdims=True)
    var = jnp.mean(jnp.square(x - mu), axis=-1, keepdims=True)
    return (x - mu) * lax.rsqrt(var + LN_EPS) * g + b


def _dot(a, b):
    return jnp.dot(a, b, preferred_element_type=F32)


def _dot_nt(a, b):
    return lax.dot_general(a, b, (((1,), (1,)), ((), ())), preferred_element_type=F32)


def _dot_hi(a, b):
    return jnp.dot(a, b, preferred_element_type=F32, precision=HIGHEST)


def _inproj_kernel(x_ref, w_ref, wc_ref, conv_ref, q_ref, k_ref, v_ref, cs_ref, u_s):
    ts = x_ref.shape[0]
    cw = CONV_WIDTH

    @pl.when(pl.program_id(1) == 0)
    def _():
        u_s[0:8, :] = jnp.zeros((8, cw), F32)

    x = x_ref[...].astype(BF16)
    gb = _dot(x, w_ref[:, 0:cw])
    gc = _dot(x, w_ref[:, cw:2 * cw])
    h = _dot(x, w_ref[:, 2 * cw:3 * cw])
    u = gc * h
    u_s[8:8 + ts, :] = u
    wc = wc_ref[...]
    conv = wc[0:1, :] * u_s[6:6 + ts, :] + wc[1:2, :] * u_s[7:7 + ts, :] + wc[2:3, :] * u
    conv_ref[...] = (gb * conv).astype(conv_ref.dtype)
    o = 3 * cw
    q_ref[...] = _dot(x, w_ref[:, o:o + ATTN_WIDTH]) * ATTN_SCALE
    k_ref[...] = _dot(x, w_ref[:, o + ATTN_WIDTH:o + 2 * ATTN_WIDTH])
    v_ref[...] = _dot(x, w_ref[:, o + 2 * ATTN_WIDTH:o + 3 * ATTN_WIDTH])
    tail = u_s[ts:ts + 8, :]
    u_s[0:8, :] = tail
    cs_ref[...] = tail


def _inproj_prompt(x, w_in_bf, w_conv):
    B, S, D = x.shape
    ts = SEQ_TILE
    row = lambda b, s: (b, s, 0)
    return pl.pallas_call(
        _inproj_kernel,
        grid=(B, S // ts),
        in_specs=[pl.BlockSpec((None, ts, D), row),
                  pl.BlockSpec(w_in_bf.shape, lambda b, s: (0, 0)),
                  pl.BlockSpec(w_conv.shape, lambda b, s: (0, 0))],
        out_specs=[pl.BlockSpec((None, ts, CONV_WIDTH), row),
                   pl.BlockSpec((None, ts, ATTN_WIDTH), row),
                   pl.BlockSpec((None, ts, ATTN_WIDTH), row),
                   pl.BlockSpec((None, ts, ATTN_WIDTH), row),
                   pl.BlockSpec((None, 8, CONV_WIDTH), lambda b, s: (b, 0, 0))],
        out_shape=[jax.ShapeDtypeStruct((B, S, CONV_WIDTH), BF16),
                   jax.ShapeDtypeStruct((B, S, ATTN_WIDTH), F32),
                   jax.ShapeDtypeStruct((B, S, ATTN_WIDTH), F32),
                   jax.ShapeDtypeStruct((B, S, ATTN_WIDTH), F32),
                   jax.ShapeDtypeStruct((B, 8, CONV_WIDTH), F32)],
        scratch_shapes=[pltpu.VMEM((ts + 8, CONV_WIDTH), F32)],
        compiler_params=pltpu.CompilerParams(dimension_semantics=("arbitrary", "arbitrary")),
        name="inproj_prompt",
    )(x, w_in_bf, w_conv)


def _attn_kernel(q_ref, k_ref, v_ref, bias_ref, o_ref, m_s, l_s, acc_s):
    S = q_ref.shape[0]
    n = BAND
    lane = lax.broadcasted_iota(I32, (n, LANES), 1)
    head_a = lane < HEAD_DIM

    def rows_of(start, count, dil):
        return pl.ds(start, count) if dil == 1 else pl.ds(start, count, stride=dil)

    def block(p, dil, start, has_prev):
        rows = rows_of(start, n, dil)
        qb = q_ref[rows, :]
        q2 = jnp.concatenate([jnp.where(head_a, qb, 0.0), jnp.where(head_a, 0.0, qb)], axis=0).astype(BF16)
        if has_prev:
            krows = rows_of(start - n * dil, 2 * n, dil)
            bias = bias_ref[p]
        else:
            krows = rows
            bias = bias_ref[p, :, n:2 * n]
        kb = k_ref[krows, :].astype(BF16)
        vb = v_ref[krows, :].astype(BF16)
        s = _dot_nt(q2, kb) + bias
        m = jnp.max(s, axis=-1, keepdims=True)
        e = jnp.exp(s - m)
        l = jnp.sum(e, axis=-1, keepdims=True)
        pv = _dot(e.astype(BF16), vb)
        m_s[p, rows, :] = jnp.where(head_a, m[:n], m[n:])
        l_s[p, rows, :] = jnp.where(head_a, l[:n], l[n:])
        acc_s[p, rows, :] = jnp.where(head_a, pv[:n], pv[n:])

    group = 4
    for p, (window, dil) in enumerate(PATTERNS):
        nb = (S // dil) // n
        if nb == 1:
            def classes(g, c, p=p, dil=dil):
                for j in range(group):
                    block(p, dil, g * group + j, False)
                return c
            lax.fori_loop(0, dil // group, classes, 0)
        elif nb == group:
            def one_class(r, c, p=p, dil=dil, nb=nb):
                block(p, dil, r, False)
                for i in range(1, nb):
                    block(p, dil, i * n * dil + r, True)
                return c
            lax.fori_loop(0, dil, one_class, 0)
        else:
            assert dil == 1 and nb % group == 0
            block(p, dil, 0, False)
            for i in range(1, group):
                block(p, dil, i * n, True)

            def later(g, c, p=p, dil=dil):
                for j in range(group):
                    block(p, dil, (g * group + j) * n, True)
                return c
            lax.fori_loop(1, nb // group, later, 0)

    rows_per_step = 256

    def merge(t, c):
        rows = pl.ds(t * rows_per_step, rows_per_step)
        ms = [m_s[p, rows, :] for p in range(len(PATTERNS))]
        m_all = jnp.maximum(jnp.maximum(ms[0], ms[1]), ms[2])
        num = den = None
        for p in range(len(PATTERNS)):
            w = jnp.exp(ms[p] - m_all)
            num = w * acc_s[p, rows, :] if num is None else num + w * acc_s[p, rows, :]
            den = w * l_s[p, rows, :] if den is None else den + w * l_s[p, rows, :]
        o_ref[rows, :] = (num / den).astype(o_ref.dtype)
        return c

    lax.fori_loop(0, S // rows_per_step, merge, 0)


def _attention_prompt(q, k, v, bias_tab):
    B, S, _ = q.shape
    slab = lambda p, b: (b, 0, p)
    spec = pl.BlockSpec((None, S, LANES), slab)
    return pl.pallas_call(
        _attn_kernel,
        grid=(N_SLABS, B),
        in_specs=[spec, spec, spec,
                  pl.BlockSpec((len(PATTERNS), None, HEADS_PER_SLAB * BAND, 2 * BAND), lambda p, b: (0, p, 0, 0))],
        out_specs=spec,
        out_shape=jax.ShapeDtypeStruct((B, S, ATTN_WIDTH), BF16),
        scratch_shapes=[pltpu.VMEM((len(PATTERNS), S, LANES), F32)] * 3,
        compiler_params=pltpu.CompilerParams(dimension_semantics=("arbitrary", "arbitrary")),
        name="dilated_attn_prompt",
    )(q, k, v, bias_tab)


def _memkv_kernel(m_ref, wk_ref, wv_ref, k_ref, v_ref, kb_ref, vb_ref):
    x = m_ref[...].astype(BF16)
    k = _dot(x, wk_ref[...])
    v = _dot(x, wv_ref[...])
    k_ref[...] = k
    v_ref[...] = v
    kb_ref[...] = k.astype(BF16)
    vb_ref[...] = v.astype(BF16)


def _memkv_prompt(mem, wk_bf, wv_bf):
    B, M, D = mem.shape
    row = lambda b: (b, 0, 0)
    full = lambda b: (0, 0)
    return pl.pallas_call(
        _memkv_kernel,
        grid=(B,),
        in_specs=[pl.BlockSpec((None, M, D), row), pl.BlockSpec((D, D), full), pl.BlockSpec((D, D), full)],
        out_specs=[pl.BlockSpec((None, M, D), row)] * 4,
        out_shape=[jax.ShapeDtypeStruct((B, M, D), F32)] * 2 + [jax.ShapeDtypeStruct((B, M, D), BF16)] * 2,
        compiler_params=pltpu.CompilerParams(dimension_semantics=("arbitrary",)),
        name="memkv_prompt",
    )(mem, wk_bf, wv_bf)


def _route(logits_t, carry):
    E, nt = logits_t.shape
    eidx = lax.broadcasted_iota(I32, (E, nt), 0)
    l = logits_t
    vals, idxs = [], []
    for _ in range(TOP_K):
        m = jnp.max(l, axis=0, keepdims=True)
        sel = jnp.min(jnp.where(l == m, eidx, E), axis=0, keepdims=True)
        vals.append(m)
        idxs.append(sel)
        l = jnp.where(eidx == sel, -jnp.inf, l)
    es = [jnp.exp(v - vals[0]) for v in vals]
    den = es[0] + es[1] + es[2] + es[3]
    gates = _stack_rows([e / den for e in es])
    chosen = (l == -jnp.inf)
    onehot = jnp.where(chosen, 1.0, 0.0)
    before = lax.broadcasted_iota(I32, (nt, nt), 0) < lax.broadcasted_iota(I32, (nt, nt), 1)
    prefix = _dot(onehot.astype(BF16), jnp.where(before, 1.0, 0.0).astype(BF16)) + carry
    ranks = [jnp.sum(jnp.where(eidx == s, prefix, 0.0), axis=0, keepdims=True) for s in idxs]
    rank = _stack_rows(ranks).astype(I32)
    idx = _stack_rows(idxs)
    return idx, gates, rank, carry + jnp.sum(onehot, axis=1, keepdims=True)


def _stack_rows(rows):
    k, nt = len(rows), rows[0].shape[1]
    r = lax.broadcasted_iota(I32, (k, nt), 0)
    out = jnp.broadcast_to(rows[-1], (k, nt))
    for i in range(k - 2, -1, -1):
        out = jnp.where(r == i, rows[i], out)
    return out


def _mid_kernel(x_ref, conv_ref, attn_ref, woc_ref, woa_ref, g1_ref, b1_ref, mk_ref, mv_ref, wq_ref, wmo_ref,
                g2_ref, b2_ref, wrt_ref, br_ref,
                x2_ref, idx_ref, gate_ref, rank_ref, cnt_ref, carry_s, *, alpha):
    first = jnp.logical_and(pl.program_id(0) == 0, pl.program_id(1) == 0)

    @pl.when(first)
    def _():
        carry_s[...] = jnp.zeros_like(carry_s)

    x = x_ref[...]
    mix = _dot(conv_ref[...], woc_ref[...]) + _dot(attn_ref[...], woa_ref[...])
    x1 = _layer_norm(alpha * x + mix, g1_ref[...], b1_ref[...])
    qm = (_dot(x1.astype(BF16), wq_ref[...]) * MEM_SCALE).astype(BF16)
    outs = []
    for h in range(MEM_HEADS):
        hs = slice(h * MEM_HEAD_DIM, (h + 1) * MEM_HEAD_DIM)
        s = _dot_nt(qm[:, hs], mk_ref[:, hs])
        m = jnp.max(s, axis=-1, keepdims=True)
        e = jnp.exp(s - m)
        l = jnp.sum(e, axis=-1, keepdims=True)
        outs.append((_dot(e.astype(BF16), mv_ref[:, hs]) / l).astype(BF16))
    o = jnp.concatenate(outs, axis=-1)
    x2 = _layer_norm(alpha * x1 + _dot(o, wmo_ref[...]), g2_ref[...], b2_ref[...])
    x2_ref[...] = x2
    logits_t = _dot_nt(wrt_ref[...], x2.astype(BF16)) + br_ref[...]
    idx, gates, rank, carry = _route(logits_t, carry_s[:, 0:1])
    idx_ref[...] = idx
    gate_ref[...] = gates
    rank_ref[...] = rank
    carry_s[...] = jnp.broadcast_to(carry, carry_s.shape)
    cnt_ref[...] = carry_s[...]


def _mid_prompt(x, conv, attn, woc, woa, g1, b1, mk, mv, wq, wmo, g2, b2, wrt, br, alpha):
    B, S, D = x.shape
    ts = SEQ_TILE
    T = B * S
    row = lambda b, s: (b, s, 0)
    full = lambda b, s: (0, 0)
    tok = lambda b, s: (0, b * (S // ts) + s)
    mem = lambda b, s: (b, 0, 0)
    vec = pl.BlockSpec((1, D), full)
    return pl.pallas_call(
        functools.partial(_mid_kernel, alpha=alpha),
        grid=(B, S // ts),
        in_specs=[pl.BlockSpec((None, ts, D), row),
                  pl.BlockSpec((None, ts, CONV_WIDTH), row),
                  pl.BlockSpec((None, ts, ATTN_WIDTH), row),
                  pl.BlockSpec(woc.shape, full), pl.BlockSpec(woa.shape, full), vec, vec,
                  pl.BlockSpec((None, MEM_TOKENS, D), mem), pl.BlockSpec((None, MEM_TOKENS, D), mem),
                  pl.BlockSpec(wq.shape, full), pl.BlockSpec(wmo.shape, full), vec, vec,
                  pl.BlockSpec(wrt.shape, full), pl.BlockSpec(br.shape, full)],
        out_specs=[pl.BlockSpec((None, ts, D), row),
                   pl.BlockSpec((TOP_K, ts), tok), pl.BlockSpec((TOP_K, ts), tok), pl.BlockSpec((TOP_K, ts), tok),
                   pl.BlockSpec((N_EXPERTS, LANES), full)],
        out_shape=[jax.ShapeDtypeStruct((B, S, D), F32),
                   jax.ShapeDtypeStruct((TOP_K, T), I32),
                   jax.ShapeDtypeStruct((TOP_K, T), F32),
                   jax.ShapeDtypeStruct((TOP_K, T), I32),
                   jax.ShapeDtypeStruct((N_EXPERTS, LANES), F32)],
        scratch_shapes=[pltpu.VMEM((N_EXPERTS, LANES), F32)],
        compiler_params=pltpu.CompilerParams(dimension_semantics=("arbitrary", "arbitrary")),
        name="mid_prompt",
    )(x, conv, attn, woc, woa, g1, b1, mk, mv, wq, wmo, g2, b2, wrt, br)


def _round_bf16(x):
    return x.astype(BF16).astype(F32)


def _inproj_sample_kernel(x_ref, w_ref, o_ref):
    o_ref[...] = _dot(x_ref[...].astype(BF16), w_ref[...])


def _inproj_sample(x, w_in):
    n, D = x.shape
    N = w_in.shape[1]
    bn = 512
    return pl.pallas_call(
        _inproj_sample_kernel,
        grid=(N // bn,),
        in_specs=[pl.BlockSpec((n, D), lambda j: (0, 0)), pl.BlockSpec((D, bn), lambda j: (0, j))],
        out_specs=pl.BlockSpec((n, bn), lambda j: (0, j)),
        out_shape=jax.ShapeDtypeStruct((n, N), F32),
        compiler_params=pltpu.CompilerParams(dimension_semantics=("arbitrary",)),
        name="inproj_sample",
    )(x, w_in)


def _attn_sample_kernel(q_ref, kn_ref, vn_ref, k0_ref, k1_ref, k2_ref, v0_ref, v1_ref, v2_ref,
                        bias_ref, bnew_ref, o_ref):
    q = _round_bf16(q_ref[...])
    k_refs = (k0_ref, k1_ref, k2_ref)
    v_refs = (v0_ref, v1_ref, v2_ref)
    s_new = jnp.sum(q * _round_bf16(kn_ref[...]), axis=-1, keepdims=True) * ATTN_SCALE
    v_new = _round_bf16(vn_ref[...])
    outs, lses = [], []
    for p in range(len(PATTERNS)):
        kb = _round_bf16(k_refs[p][...])
        s = jnp.sum(kb * q[None], axis=-1, keepdims=True) * ATTN_SCALE + bias_ref[p]
        sn = s_new + bnew_ref[p]
        m = jnp.maximum(jnp.max(s, axis=0), sn)
        e = jnp.exp(s - m[None])
        en = jnp.exp(sn - m)
        den = jnp.sum(e, axis=0) + en
        pr = _round_bf16(e / den[None])
        outs.append(jnp.sum(pr * _round_bf16(v_refs[p][...]), axis=0) + _round_bf16(en / den) * v_new)
        lses.append(m + jnp.log(den))
    lmax = jnp.maximum(jnp.maximum(lses[0], lses[1]), lses[2])
    ws = [jnp.exp(ls - lmax) for ls in lses]
    wsum = ws[0] + ws[1] + ws[2]
    acc = None
    for p in range(len(PATTERNS)):
        term = _round_bf16(ws[p] / wsum) * _round_bf16(outs[p])
        acc = term if acc is None else acc + term
    o_ref[...] = acc


def _attention_sample(q, k_new, v_new, win_k, win_v, bias_s, bias_new):
    DB, W, H, E = win_k.shape
    n = BAND
    assert PAST_LEN % W == 0 and PAST_LEN >= WINDOW_MAX
    cache_in, cache_specs = [], []
    for arr in (win_k, win_v):
        for window, dil in PATTERNS:
            blk = (W - window) // dil // n
            cache_in.append(arr.reshape(DB, W // dil, dil, H, E))
            cache_specs.append(pl.BlockSpec((None, n, None, H, E),
                                            functools.partial(lambda b, blk: (b, blk, 0, 0, 0), blk=blk)))
    vec = pl.BlockSpec((None, H, E), lambda b: (b, 0, 0))
    const = lambda a: pl.BlockSpec(a.shape, lambda b: (0,) * a.ndim)
    return pl.pallas_call(
        _attn_sample_kernel,
        grid=(DB,),
        in_specs=[vec, vec, vec] + cache_specs + [const(bias_s), const(bias_new)],
        out_specs=vec,
        out_shape=jax.ShapeDtypeStruct((DB, H, E), F32),
        compiler_params=pltpu.CompilerParams(dimension_semantics=("arbitrary",)),
        name="dilated_attn_sample",
    )(q, k_new, v_new, *cache_in, bias_s, bias_new)


def _tail_a_sample_kernel(x_ref, proj_ref, s0_ref, s1_ref, wc_ref, attn_ref, wo_ref, g1_ref, b1_ref, wq_ref,
                          x1_ref, qm_ref, u_ref, *, alpha):
    cw = CONV_WIDTH
    gb = proj_ref[:, 0:cw]
    u = proj_ref[:, cw:2 * cw] * proj_ref[:, 2 * cw:3 * cw]
    u_ref[...] = u
    wc = wc_ref[...]
    conv = gb * (wc[0:1, :] * s0_ref[...] + wc[1:2, :] * s1_ref[...] + wc[2:3, :] * u)
    mixed = jnp.concatenate([conv, attn_ref[...]], axis=-1).astype(BF16)
    x1 = _layer_norm(alpha * x_ref[...] + _dot(mixed, wo_ref[...]), g1_ref[...], b1_ref[...])
    x1_ref[...] = x1
    qm_ref[...] = _dot(x1.astype(BF16), wq_ref[...])


def _tail_a_sample(x, proj, s0, s1, w_conv, attn, w_out, g1, b1, w_mem_q, alpha):
    n, D = x.shape
    return pl.pallas_call(
        functools.partial(_tail_a_sample_kernel, alpha=alpha),
        out_shape=[jax.ShapeDtypeStruct((n, D), F32)] * 2 + [jax.ShapeDtypeStruct((n, CONV_WIDTH), F32)],
        name="tail_a_sample",
    )(x, proj, s0, s1, w_conv, attn, w_out, g1, b1, w_mem_q)


def _memattn_sample_kernel(q_ref, k_ref, v_ref, o_ref):
    q = _round_bf16(q_ref[...])
    s = jnp.sum(_round_bf16(k_ref[...]) * q[None], axis=-1, keepdims=True) * MEM_SCALE
    m = jnp.max(s, axis=0)
    e = jnp.exp(s - m[None])
    den = jnp.sum(e, axis=0)
    pr = _round_bf16(e / den[None])
    o_ref[...] = jnp.sum(pr * _round_bf16(v_ref[...]), axis=0)


def _memattn_sample(qm, mem_k, mem_v):
    DB, M, H, E = mem_k.shape
    vec = pl.BlockSpec((None, H, E), lambda b: (b, 0, 0))
    mem = pl.BlockSpec((None, M, H, E), lambda b: (b, 0, 0, 0))
    return pl.pallas_call(
        _memattn_sample_kernel,
        grid=(DB,),
        in_specs=[vec, mem, mem],
        out_specs=vec,
        out_shape=jax.ShapeDtypeStruct((DB, H, E), F32),
        compiler_params=pltpu.CompilerParams(dimension_semantics=("arbitrary",)),
        name="memattn_sample",
    )(qm, mem_k, mem_v)


def _tail_b_sample_kernel(x1_ref, o_ref, wmo_ref, g2_ref, b2_ref, wrt_ref, br_ref, cnt_ref,
                          x2_ref, idx_ref, gate_ref, rank_ref, cnt_out_ref, *, alpha):
    x2 = _layer_norm(alpha * x1_ref[...] + _dot(o_ref[...].astype(BF16), wmo_ref[...]), g2_ref[...], b2_ref[...])
    x2_ref[...] = x2
    logits_t = _dot_nt(wrt_ref[...], x2.astype(BF16)) + br_ref[...]
    idx, gates, rank, carry = _route(logits_t, cnt_ref[:, 0:1])
    idx_ref[...] = idx
    gate_ref[...] = gates
    rank_ref[...] = rank
    cnt_out_ref[...] = jnp.broadcast_to(carry, cnt_out_ref.shape)


def _tail_b_sample(x1, o, w_mem_o, g2, b2, wrt, br, cnt, alpha):
    n, D = x1.shape
    return pl.pallas_call(
        functools.partial(_tail_b_sample_kernel, alpha=alpha),
        out_shape=[jax.ShapeDtypeStruct((n, D), F32),
                   jax.ShapeDtypeStruct((TOP_K, n), I32),
                   jax.ShapeDtypeStruct((TOP_K, n), F32),
                   jax.ShapeDtypeStruct((TOP_K, n), I32),
                   jax.ShapeDtypeStruct(cnt.shape, F32)],
        name="tail_b_sample",
    )(x1, o, w_mem_o, g2, b2, wrt, br, cnt)


def _row_copy(src, dst, sem, src_row, dst_row):
    return pltpu.make_async_copy(src.at[pl.ds(src_row, 1)], dst.at[pl.ds(dst_row, 1)], sem)


def _dispatch_kernel(dest_ref, x_ref, xb_in_ref, xb_ref, sem):
    del xb_in_ref
    nt = x_ref.shape[0]

    def issue(t, c):
        for k in range(TOP_K):
            _row_copy(x_ref, xb_ref, sem, t, dest_ref[0, k * nt + t]).start()
        return c

    lax.fori_loop(0, nt, issue, 0)
    for k in range(TOP_K):
        pltpu.make_async_copy(x_ref, xb_ref.at[pl.ds(0, nt)], sem).wait()


def _dispatch(x2, dest_tiles, xb):
    T, D = x2.shape
    nt = dest_tiles.shape[2] // TOP_K
    return pl.pallas_call(
        _dispatch_kernel,
        grid=(T // nt,),
        in_specs=[pl.BlockSpec((None, 1, TOP_K * nt), lambda i: (i, 0, 0), memory_space=pltpu.SMEM),
                  pl.BlockSpec((nt, D), lambda i: (i, 0)),
                  pl.BlockSpec(memory_space=pl.ANY)],
        out_specs=pl.BlockSpec(memory_space=pl.ANY),
        out_shape=jax.ShapeDtypeStruct(xb.shape, xb.dtype),
        scratch_shapes=[pltpu.SemaphoreType.DMA],
        input_output_aliases={2: 0},
        compiler_params=pltpu.CompilerParams(dimension_semantics=("arbitrary",)),
        name="moe_dispatch",
    )(dest_tiles, x2, xb)


def _moe_kernel(be_ref, nu_ref, xb_ref, wgu_ref, bgu_ref, wd_ref, bd_ref, yb_ref, wgu_s, wd_s):
    j = pl.program_id(0)
    prev = be_ref[jnp.maximum(j - 1, 0)]
    new_expert = jnp.logical_or(j == 0, be_ref[j] != prev)
    used = j < nu_ref[0]

    @pl.when(jnp.logical_and(used, new_expert))
    def _():
        wgu_s[...] = wgu_ref[...].astype(BF16)
        wd_s[...] = wd_ref[...].astype(BF16)

    @pl.when(used)
    def _():
        x = xb_ref[...].astype(BF16)
        gu = _dot(x, wgu_s[...]) + bgu_ref[...]
        g = jnp.minimum(gu[:, :D_FF], SWIGLU_LIMIT)
        u = jnp.clip(gu[:, D_FF:], -SWIGLU_LIMIT, SWIGLU_LIMIT)
        act = (u + 1.0) * g * jax.nn.sigmoid(SWIGLU_ALPHA * g)
        yb_ref[...] = _dot(act.astype(BF16), wd_s[...]) + bd_ref[...]

    @pl.when(jnp.logical_not(used))
    def _():
        yb_ref[...] = jnp.zeros_like(yb_ref)


def _moe(blk_exp, n_used, xb, w_gate_up, b_gate_up, w_down, b_down):
    P, D = xb.shape
    bm = ROW_BLOCK
    E, _, F2 = w_gate_up.shape
    ex = lambda j, be, nu: (be[j], 0, 0)
    return pl.pallas_call(
        _moe_kernel,
        grid_spec=pltpu.PrefetchScalarGridSpec(
            num_scalar_prefetch=2,
            grid=(P // bm,),
            in_specs=[pl.BlockSpec((bm, D), lambda j, be, nu: (j, 0)),
                      pl.BlockSpec((None, D, F2), ex), pl.BlockSpec((None, 1, F2), ex),
                      pl.BlockSpec((None, D_FF, D), ex), pl.BlockSpec((None, 1, D), ex)],
            out_specs=pl.BlockSpec((bm, D), lambda j, be, nu: (j, 0)),
            scratch_shapes=[pltpu.VMEM((D, F2), BF16), pltpu.VMEM((D_FF, D), BF16)]),
        out_shape=jax.ShapeDtypeStruct((P, D), F32),
        compiler_params=pltpu.CompilerParams(dimension_semantics=("arbitrary",),
                                             vmem_limit_bytes=56 * 1024 * 1024),
        name="moe_grouped_ffn",
    )(blk_exp, n_used, xb, w_gate_up, b_gate_up.reshape(E, 1, F2), w_down, b_down.reshape(E, 1, D))


def _combine_kernel(dest_ref, yb_ref, x2_ref, gate_ref, g3_ref, b3_ref, o_ref, rows_s, sem, *, alpha):
    nt = x2_ref.shape[0]

    def issue(t, c):
        for k in range(TOP_K):
            _row_copy(yb_ref, rows_s.at[k], sem, dest_ref[0, k * nt + t], t).start()
        return c

    lax.fori_loop(0, nt, issue, 0)
    for k in range(TOP_K):
        pltpu.make_async_copy(yb_ref.at[pl.ds(0, nt)], rows_s.at[k], sem).wait()
    gates = gate_ref[...]
    y = gates[:, 0:1] * rows_s[0]
    for k in range(1, TOP_K):
        y = y + gates[:, k:k + 1] * rows_s[k]
    o_ref[...] = _layer_norm(alpha * x2_ref[...] + y, g3_ref[...], b3_ref[...])


def _combine(yb, dest_tiles, x2, gates_tok, g3, b3, alpha):
    T, D = x2.shape
    nt = dest_tiles.shape[2] // TOP_K
    return pl.pallas_call(
        functools.partial(_combine_kernel, alpha=alpha),
        grid=(T // nt,),
        in_specs=[pl.BlockSpec((None, 1, TOP_K * nt), lambda i: (i, 0, 0), memory_space=pltpu.SMEM),
                  pl.BlockSpec(memory_space=pl.ANY),
                  pl.BlockSpec((nt, D), lambda i: (i, 0)),
                  pl.BlockSpec((nt, TOP_K), lambda i: (i, 0)),
                  pl.BlockSpec((1, D), lambda i: (0, 0)), pl.BlockSpec((1, D), lambda i: (0, 0))],
        out_specs=pl.BlockSpec((nt, D), lambda i: (i, 0)),
        out_shape=jax.ShapeDtypeStruct((T, D), F32),
        scratch_shapes=[pltpu.VMEM((TOP_K, nt, D), F32), pltpu.SemaphoreType.DMA],
        compiler_params=pltpu.CompilerParams(dimension_semantics=("arbitrary",)),
        name="moe_combine",
    )(dest_tiles, yb, x2, gates_tok, g3, b3)


def _rel_bucket(dist):
    max_exact = N_BUCKETS // 2
    df = jnp.maximum(dist, 1).astype(F32)
    large = max_exact + (jnp.log(df / max_exact) / math.log(MAX_DISTANCE / max_exact)
                         * (N_BUCKETS - max_exact)).astype(I32)
    return jnp.where(dist < max_exact, dist, jnp.minimum(large, N_BUCKETS - 1))


def _bias_tables(rel_table):
    n = BAND
    taps = jnp.arange(n + 1)
    by_tap = jnp.stack([rel_table[_rel_bucket(taps * dil)].astype(F32) for _, dil in PATTERNS])
    w = jnp.concatenate([by_tap[:, ::-1], jnp.full((len(PATTERNS), n, N_HEADS), NEG_INF, F32)], axis=1)
    w = jnp.transpose(w, (0, 2, 1))
    band = jnp.tile(w, (1, 1, n))[:, :, :n * 2 * n].reshape(len(PATTERNS), N_HEADS, n, 2 * n)
    return by_tap, band.reshape(len(PATTERNS), N_SLABS, HEADS_PER_SLAB * n, 2 * n)


def _dest_tiles(dest, nt):
    T = dest.shape[1]
    return dest.reshape(TOP_K, T // nt, nt).transpose(1, 0, 2).reshape(T // nt, 1, TOP_K * nt)


def kernel(x_prompt, x_sample, mem_prompt, cache_win_k, cache_win_v, state_conv, cache_mem_k, cache_mem_v,
           rel_bias_table, w_in, w_conv, w_out, ln1_g, ln1_b, w_mem_q, w_mem_k, w_mem_v, w_mem_o,
           ln2_g, ln2_b, w_router, b_router, w_gate_up, b_gate_up, w_down, b_down, ln3_g, ln3_b):
    depth = w_in.shape[0]
    assert depth == 1
    alpha = (2 * depth) ** 0.25
    B, S, D = x_prompt.shape
    DB = x_sample.shape[0]
    T = B * S
    l = 0
    vec = lambda a: a[l].reshape(1, -1)

    by_tap, bias_band = _bias_tables(rel_bias_table)
    wrt = w_router[l].T.astype(BF16)
    br = b_router[l].reshape(N_EXPERTS, 1)
    w_in_bf = w_in[l].astype(BF16)
    wo_bf = w_out[l].astype(BF16)
    wq_bf = w_mem_q[l].astype(BF16)
    wmo_bf = w_mem_o[l].astype(BF16)

    conv_p, q_p, k_p, v_p, cs_p = _inproj_prompt(x_prompt, w_in_bf, w_conv[l])
    attn_p = _attention_prompt(q_p, k_p, v_p, bias_band)
    mk, mv, mk_bf, mv_bf = _memkv_prompt(mem_prompt, w_mem_k[l].astype(BF16), w_mem_v[l].astype(BF16))
    x2_p, idx_p, gate_p, rank_p, cnt_p = _mid_prompt(
        x_prompt, conv_p, attn_p, wo_bf[:CONV_WIDTH], wo_bf[CONV_WIDTH:], vec(ln1_g), vec(ln1_b), mk_bf, mv_bf,
        wq_bf, wmo_bf, vec(ln2_g), vec(ln2_b), wrt, br, alpha)

    xs = x_sample.reshape(DB, D)
    proj = _inproj_sample(xs, w_in_bf)
    cw = CONV_WIDTH
    q_s = proj[:, 3 * cw:3 * cw + ATTN_WIDTH]
    k_s = proj[:, 3 * cw + ATTN_WIDTH:3 * cw + 2 * ATTN_WIDTH]
    v_s = proj[:, 3 * cw + 2 * ATTN_WIDTH:]
    H, E = N_HEADS, HEAD_DIM
    bias_s = by_tap[:, :0:-1, :, None]
    bias_new = by_tap[:, 0, :, None]
    attn_s = _attention_sample(q_s.reshape(DB, H, E), k_s.reshape(DB, H, E), v_s.reshape(DB, H, E),
                               cache_win_k[l], cache_win_v[l], bias_s, bias_new).reshape(DB, ATTN_WIDTH)
    sc = state_conv[l]
    x1_s, qm_s, u_s = _tail_a_sample(xs, proj, sc[:, 0], sc[:, 1], w_conv[l], attn_s, wo_bf,
                                     vec(ln1_g), vec(ln1_b), wq_bf, alpha)
    o_s = _memattn_sample(qm_s.reshape(DB, MEM_HEADS, MEM_HEAD_DIM), cache_mem_k[l], cache_mem_v[l]).reshape(DB, D)
    x2_s, idx_s, gate_s, rank_s, cnt = _tail_b_sample(x1_s, o_s, wmo_bf, vec(ln2_g), vec(ln2_b), wrt, br,
                                                      cnt_p, alpha)

    bm = ROW_BLOCK
    counts = cnt[:, 0].astype(I32)
    padded = (counts + bm - 1) // bm * bm
    pends = jnp.cumsum(padded)
    pstarts = pends - padded
    n_blocks = -(-(T + DB) * TOP_K // bm) + N_EXPERTS
    P = n_blocks * bm
    first_row = jnp.arange(n_blocks, dtype=I32) * bm
    blk_exp = jnp.minimum(jnp.sum((pends[None, :] <= first_row[:, None]).astype(I32), axis=1), N_EXPERTS - 1)
    n_used = (pends[-1] // bm).reshape(1).astype(I32)
    expert_ids = jnp.arange(N_EXPERTS, dtype=I32)[:, None, None]
    start_of = lambda idx: jnp.sum(jnp.where(idx[None] == expert_ids, pstarts[:, None, None], 0), axis=0)
    dest_p = start_of(idx_p) + rank_p
    dest_s = start_of(idx_s) + rank_s
    dtiles_p = _dest_tiles(dest_p, COMBINE_TILE)
    dtiles_s = _dest_tiles(dest_s, DB)

    x2_pf = x2_p.reshape(T, D)
    xb = jnp.zeros((P, D), F32)
    xb = _dispatch(x2_pf, dtiles_p, xb)
    xb = _dispatch(x2_s, dtiles_s, xb)
    yb = _moe(blk_exp, n_used, xb, w_gate_up[l], b_gate_up[l], w_down[l], b_down[l])
    y_p = _combine(yb, dtiles_p, x2_pf, gate_p.T, vec(ln3_g), vec(ln3_b), alpha).reshape(B, S, D)
    y_s = _combine(yb, dtiles_s, x2_s, gate_s.T, vec(ln3_g), vec(ln3_b), alpha).reshape(DB, 1, D)

    conv_state_s = jnp.stack([sc[:, 1], u_s], axis=1)
    return (y_p, y_s,
            k_p.reshape(1, B, S, H, E), v_p.reshape(1, B, S, H, E), cs_p[:, 6:8].reshape(1, B, CONV_K - 1, CONV_WIDTH),
            mk.reshape(1, B, MEM_TOKENS, MEM_HEADS, MEM_HEAD_DIM), mv.reshape(1, B, MEM_TOKENS, MEM_HEADS, MEM_HEAD_DIM),
            k_s.reshape(1, DB, 1, H, E), v_s.reshape(1, DB, 1, H, E), conv_state_s.reshape(1, DB, CONV_K - 1, CONV_WIDTH))
```

```python
import functools
import math

import jax
import jax.numpy as jnp
import numpy as np
from jax import lax
from jax.experimental import pallas as pl
from jax.experimental.pallas import tpu as pltpu

F32 = jnp.float32
BF16 = jnp.bfloat16
I32 = jnp.int32

D_MODEL = 1024
CONV_WIDTH = 256
CONV_K = 3
HEAD_DIM = 64
N_HEADS = 12
ATTN_WIDTH = N_HEADS * HEAD_DIM
PATTERNS = ((128, 1), (512, 4), (2048, 16))
BAND = 128
WINDOW_MAX = 2048
ATTN_SCALE = 1.0 / math.sqrt(HEAD_DIM)
N_BUCKETS = 32
MAX_DISTANCE = WINDOW_MAX
MEM_TOKENS = 256
MEM_HEADS = 4
MEM_HEAD_DIM = 256
MEM_SCALE = 1.0 / math.sqrt(MEM_HEAD_DIM)
N_EXPERTS = 32
TOP_K = 4
D_FF = 1024
SWIGLU_LIMIT = 7.0
SWIGLU_ALPHA = 1.702
LN_EPS = 1e-5
PAST_LEN = 8192
NEG_INF = -1e30

LANES = 128
HEADS_PER_SLAB = LANES // HEAD_DIM
N_SLABS = ATTN_WIDTH // LANES

PROJ_TILE = 256
SEQ_TILE = 512
ROW_BLOCK = 512
COMBINE_TILE = 128

HIGHEST = lax.Precision.HIGHEST


def _layer_norm(x, g, b):
    mu = jnp.mean(x, axis=-1, keepdims=True)
    var = jnp.mean(jnp.square(x - mu), axis=-1, keepdims=True)
    return (x - mu) * lax.rsqrt(var + LN_EPS) * g + b


def _dot(a, b):
    return jnp.dot(a, b, preferred_element_type=F32)


def _dot_nt(a, b):
    return lax.dot_general(a, b, (((1,), (1,)), ((), ())), preferred_element_type=F32)


def _dot_hi(a, b):
    return jnp.dot(a, b, preferred_element_type=F32, precision=HIGHEST)


def _inproj_kernel(x_ref, w_ref, wc_ref, conv_ref, q_ref, k_ref, v_ref, kh_ref, vh_ref, cs_ref, u_s):
    ts = x_ref.shape[0]
    cw = CONV_WIDTH

    @pl.when(pl.program_id(1) == 0)
    def _():
        u_s[0:8, :] = jnp.zeros((8, cw), F32)

    x = x_ref[...].astype(BF16)
    gb = _dot(x, w_ref[:, 0:cw])
    gc = _dot(x, w_ref[:, cw:2 * cw])
    h = _dot(x, w_ref[:, 2 * cw:3 * cw])
    u = gc * h
    u_s[8:8 + ts, :] = u
    wc = wc_ref[...]
    conv = wc[0:1, :] * u_s[6:6 + ts, :] + wc[1:2, :] * u_s[7:7 + ts, :] + wc[2:3, :] * u
    conv_ref[...] = (gb * conv).astype(conv_ref.dtype)
    o = 3 * cw
    q_ref[...] = _dot(x, w_ref[:, o:o + ATTN_WIDTH]) * ATTN_SCALE
    k = _dot(x, w_ref[:, o + ATTN_WIDTH:o + 2 * ATTN_WIDTH])
    v = _dot(x, w_ref[:, o + 2 * ATTN_WIDTH:o + 3 * ATTN_WIDTH])
    k_ref[...] = k
    v_ref[...] = v
    kh_ref[...] = k.reshape(kh_ref.shape)
    vh_ref[...] = v.reshape(vh_ref.shape)
    tail = u_s[ts:ts + 8, :]
    u_s[0:8, :] = tail
    cs_ref[...] = tail


def _inproj_prompt(x, w_in_bf, w_conv):
    B, S, D = x.shape
    ts = PROJ_TILE
    row = lambda b, s: (b, s, 0)
    heads = pl.BlockSpec((None, ts, N_HEADS, HEAD_DIM), lambda b, s: (b, s, 0, 0))
    return pl.pallas_call(
        _inproj_kernel,
        grid=(B, S // ts),
        in_specs=[pl.BlockSpec((None, ts, D), row),
                  pl.BlockSpec(w_in_bf.shape, lambda b, s: (0, 0)),
                  pl.BlockSpec(w_conv.shape, lambda b, s: (0, 0))],
        out_specs=[pl.BlockSpec((None, ts, CONV_WIDTH), row),
                   pl.BlockSpec((None, ts, ATTN_WIDTH), row),
                   pl.BlockSpec((None, ts, ATTN_WIDTH), row),
                   pl.BlockSpec((None, ts, ATTN_WIDTH), row),
                   heads, heads,
                   pl.BlockSpec((None, 8, CONV_WIDTH), lambda b, s: (b, 0, 0))],
        out_shape=[jax.ShapeDtypeStruct((B, S, CONV_WIDTH), BF16),
                   jax.ShapeDtypeStruct((B, S, ATTN_WIDTH), F32),
                   jax.ShapeDtypeStruct((B, S, ATTN_WIDTH), F32),
                   jax.ShapeDtypeStruct((B, S, ATTN_WIDTH), F32),
                   jax.ShapeDtypeStruct((B, S, N_HEADS, HEAD_DIM), F32),
                   jax.ShapeDtypeStruct((B, S, N_HEADS, HEAD_DIM), F32),
                   jax.ShapeDtypeStruct((B, 8, CONV_WIDTH), F32)],
        scratch_shapes=[pltpu.VMEM((ts + 8, CONV_WIDTH), F32)],
        compiler_params=pltpu.CompilerParams(dimension_semantics=("arbitrary", "arbitrary")),
        name="inproj_prompt",
    )(x, w_in_bf, w_conv)


def _attn_kernel(q_ref, k_ref, v_ref, bias_ref, o_ref, m_s, l_s, acc_s):
    S = q_ref.shape[0]
    n = BAND
    lane = lax.broadcasted_iota(I32, (n, LANES), 1)
    head_a = lane < HEAD_DIM

    def rows_of(start, count, dil):
        return pl.ds(start, count) if dil == 1 else pl.ds(start, count, stride=dil)

    def block(p, dil, start, has_prev):
        rows = rows_of(start, n, dil)
        qb = q_ref[rows, :]
        q2 = jnp.concatenate([jnp.where(head_a, qb, 0.0), jnp.where(head_a, 0.0, qb)], axis=0).astype(BF16)
        if has_prev:
            krows = rows_of(start - n * dil, 2 * n, dil)
            bias = bias_ref[p]
        else:
            krows = rows
            bias = bias_ref[p, :, n:2 * n]
        kb = k_ref[krows, :].astype(BF16)
        vb = v_ref[krows, :].astype(BF16)
        s = _dot_nt(q2, kb) + bias
        m = jnp.max(s, axis=-1, keepdims=True)
        e = jnp.exp(s - m)
        l = jnp.sum(e, axis=-1, keepdims=True)
        pv = _dot(e.astype(BF16), vb)
        m_s[p, rows, :] = jnp.where(head_a, m[:n], m[n:])
        l_s[p, rows, :] = jnp.where(head_a, l[:n], l[n:])
        acc_s[p, rows, :] = jnp.where(head_a, pv[:n], pv[n:])

    group = 4
    for p, (window, dil) in enumerate(PATTERNS):
        nb = (S // dil) // n
        if nb == 1:
            def classes(g, c, p=p, dil=dil):
                for j in range(group):
                    block(p, dil, g * group + j, False)
                return c
            lax.fori_loop(0, dil // group, classes, 0)
        elif nb == group:
            def one_class(r, c, p=p, dil=dil, nb=nb):
                block(p, dil, r, False)
                for i in range(1, nb):
                    block(p, dil, i * n * dil + r, True)
                return c
            lax.fori_loop(0, dil, one_class, 0)
        else:
            assert dil == 1 and nb % group == 0
            block(p, dil, 0, False)
            for i in range(1, group):
                block(p, dil, i * n, True)

            def later(g, c, p=p, dil=dil):
                for j in range(group):
                    block(p, dil, (g * group + j) * n, True)
                return c
            lax.fori_loop(1, nb // group, later, 0)

    rows_per_step = 256

    def merge(t, c):
        rows = pl.ds(t * rows_per_step, rows_per_step)
        ms = [m_s[p, rows, :] for p in range(len(PATTERNS))]
        m_all = jnp.maximum(jnp.maximum(ms[0], ms[1]), ms[2])
        num = den = None
        for p in range(len(PATTERNS)):
            w = jnp.exp(ms[p] - m_all)
            num = w * acc_s[p, rows, :] if num is None else num + w * acc_s[p, rows, :]
            den = w * l_s[p, rows, :] if den is None else den + w * l_s[p, rows, :]
        o_ref[rows, :] = (num / den).astype(o_ref.dtype)
        return c

    lax.fori_loop(0, S // rows_per_step, merge, 0)


def _attention_prompt(q, k, v, bias_tab):
    B, S, _ = q.shape
    slab = lambda p, b: (b, 0, p)
    spec = pl.BlockSpec((None, S, LANES), slab)
    return pl.pallas_call(
        _attn_kernel,
        grid=(N_SLABS, B),
        in_specs=[spec, spec, spec,
                  pl.BlockSpec((len(PATTERNS), None, HEADS_PER_SLAB * BAND, 2 * BAND), lambda p, b: (0, p, 0, 0))],
        out_specs=spec,
        out_shape=jax.ShapeDtypeStruct((B, S, ATTN_WIDTH), BF16),
        scratch_shapes=[pltpu.VMEM((len(PATTERNS), S, LANES), F32)] * 3,
        compiler_params=pltpu.CompilerParams(dimension_semantics=("arbitrary", "arbitrary")),
        name="dilated_attn_prompt",
    )(q, k, v, bias_tab)


def _memkv_kernel(m_ref, wk_ref, wv_ref, k_ref, v_ref, kb_ref, vb_ref):
    x = m_ref[...].astype(BF16)
    k = _dot(x, wk_ref[...])
    v = _dot(x, wv_ref[...])
    k_ref[...] = k
    v_ref[...] = v
    kb_ref[...] = k.astype(BF16)
    vb_ref[...] = v.astype(BF16)


def _memkv_prompt(mem, wk_bf, wv_bf):
    B, M, D = mem.shape
    row = lambda b: (b, 0, 0)
    full = lambda b: (0, 0)
    return pl.pallas_call(
        _memkv_kernel,
        grid=(B,),
        in_specs=[pl.BlockSpec((None, M, D), row), pl.BlockSpec((D, D), full), pl.BlockSpec((D, D), full)],
        out_specs=[pl.BlockSpec((None, M, D), row)] * 4,
        out_shape=[jax.ShapeDtypeStruct((B, M, D), F32)] * 2 + [jax.ShapeDtypeStruct((B, M, D), BF16)] * 2,
        compiler_params=pltpu.CompilerParams(dimension_semantics=("arbitrary",)),
        name="memkv_prompt",
    )(mem, wk_bf, wv_bf)


def _route(logits_t, carry):
    E, nt = logits_t.shape
    eidx = lax.broadcasted_iota(I32, (E, nt), 0)
    l = logits_t
    vals, idxs = [], []
    for _ in range(TOP_K):
        m = jnp.max(l, axis=0, keepdims=True)
        sel = jnp.min(jnp.where(l == m, eidx, E), axis=0, keepdims=True)
        vals.append(m)
        idxs.append(sel)
        l = jnp.where(eidx == sel, -jnp.inf, l)
    es = [jnp.exp(v - vals[0]) for v in vals]
    den = es[0] + es[1] + es[2] + es[3]
    gates = _stack_rows([e / den for e in es])
    chosen = (l == -jnp.inf)
    onehot = jnp.where(chosen, 1.0, 0.0)
    before = lax.broadcasted_iota(I32, (nt, nt), 0) < lax.broadcasted_iota(I32, (nt, nt), 1)
    prefix = _dot(onehot.astype(BF16), jnp.where(before, 1.0, 0.0).astype(BF16)) + carry
    ranks = [jnp.sum(jnp.where(eidx == s, prefix, 0.0), axis=0, keepdims=True) for s in idxs]
    rank = _stack_rows(ranks).astype(I32)
    idx = _stack_rows(idxs)
    return idx, gates, rank, carry + jnp.sum(onehot, axis=1, keepdims=True)


def _stack_rows(rows):
    k, nt = len(rows), rows[0].shape[1]
    r = lax.broadcasted_iota(I32, (k, nt), 0)
    out = jnp.broadcast_to(rows[-1], (k, nt))
    for i in range(k - 2, -1, -1):
        out = jnp.where(r == i, rows[i], out)
    return out


def _mid_kernel(x_ref, conv_ref, attn_ref, woc_ref, woa_ref, g1_ref, b1_ref, mk_ref, mv_ref, wq_ref, wmo_ref,
                g2_ref, b2_ref, wrt_ref, br_ref,
                x2_ref, idx_ref, gate_ref, rank_ref, cnt_ref, carry_s, *, alpha):
    first = jnp.logical_and(pl.program_id(0) == 0, pl.program_id(1) == 0)

    @pl.when(first)
    def _():
        carry_s[...] = jnp.zeros_like(carry_s)

    x = x_ref[...]
    mix = _dot(conv_ref[...], woc_ref[...]) + _dot(attn_ref[...], woa_ref[...])
    x1 = _layer_norm(alpha * x + mix, g1_ref[...], b1_ref[...])
    qm = (_dot(x1.astype(BF16), wq_ref[...]) * MEM_SCALE).astype(BF16)
    outs = []
    for h in range(MEM_HEADS):
        hs = slice(h * MEM_HEAD_DIM, (h + 1) * MEM_HEAD_DIM)
        s = _dot_nt(qm[:, hs], mk_ref[:, hs])
        m = jnp.max(s, axis=-1, keepdims=True)
        e = jnp.exp(s - m)
        l = jnp.sum(e, axis=-1, keepdims=True)
        outs.append((_dot(e.astype(BF16), mv_ref[:, hs]) / l).astype(BF16))
    o = jnp.concatenate(outs, axis=-1)
    x2 = _layer_norm(alpha * x1 + _dot(o, wmo_ref[...]), g2_ref[...], b2_ref[...])
    x2_ref[...] = x2
    logits_t = _dot_nt(wrt_ref[...], x2.astype(BF16)) + br_ref[...]
    idx, gates, rank, carry = _route(logits_t, carry_s[:, 0:1])
    idx_ref[...] = idx
    gate_ref[...] = gates
    rank_ref[...] = rank
    carry_s[...] = jnp.broadcast_to(carry, carry_s.shape)
    cnt_ref[...] = carry_s[...]


def _mid_prompt(x, conv, attn, woc, woa, g1, b1, mk, mv, wq, wmo, g2, b2, wrt, br, alpha):
    B, S, D = x.shape
    ts = SEQ_TILE
    T = B * S
    row = lambda b, s: (b, s, 0)
    full = lambda b, s: (0, 0)
    tok = lambda b, s: (0, b * (S // ts) + s)
    mem = lambda b, s: (b, 0, 0)
    vec = pl.BlockSpec((1, D), full)
    return pl.pallas_call(
        functools.partial(_mid_kernel, alpha=alpha),
        grid=(B, S // ts),
        in_specs=[pl.BlockSpec((None, ts, D), row),
                  pl.BlockSpec((None, ts, CONV_WIDTH), row),
                  pl.BlockSpec((None, ts, ATTN_WIDTH), row),
                  pl.BlockSpec(woc.shape, full), pl.BlockSpec(woa.shape, full), vec, vec,
                  pl.BlockSpec((None, MEM_TOKENS, D), mem), pl.BlockSpec((None, MEM_TOKENS, D), mem),
                  pl.BlockSpec(wq.shape, full), pl.BlockSpec(wmo.shape, full), vec, vec,
                  pl.BlockSpec(wrt.shape, full), pl.BlockSpec(br.shape, full)],
        out_specs=[pl.BlockSpec((None, ts, D), row),
                   pl.BlockSpec((TOP_K, ts), tok), pl.BlockSpec((TOP_K, ts), tok), pl.BlockSpec((TOP_K, ts), tok),
                   pl.BlockSpec((N_EXPERTS, LANES), full)],
        out_shape=[jax.ShapeDtypeStruct((B, S, D), F32),
                   jax.ShapeDtypeStruct((TOP_K, T), I32),
                   jax.ShapeDtypeStruct((TOP_K, T), F32),
                   jax.ShapeDtypeStruct((TOP_K, T), I32),
                   jax.ShapeDtypeStruct((N_EXPERTS, LANES), F32)],
        scratch_shapes=[pltpu.VMEM((N_EXPERTS, LANES), F32)],
        compiler_params=pltpu.CompilerParams(dimension_semantics=("arbitrary", "arbitrary")),
        name="mid_prompt",
    )(x, conv, attn, woc, woa, g1, b1, mk, mv, wq, wmo, g2, b2, wrt, br)


def _round_bf16(x):
    return x.astype(BF16).astype(F32)


def _inproj_sample_kernel(x_ref, w_ref, o_ref):
    o_ref[...] = _dot(x_ref[...].astype(BF16), w_ref[...])


def _inproj_sample(x, w_in):
    n, D = x.shape
    N = w_in.shape[1]
    bn = 512
    return pl.pallas_call(
        _inproj_sample_kernel,
        grid=(N // bn,),
        in_specs=[pl.BlockSpec((n, D), lambda j: (0, 0)), pl.BlockSpec((D, bn), lambda j: (0, j))],
        out_specs=pl.BlockSpec((n, bn), lambda j: (0, j)),
        out_shape=jax.ShapeDtypeStruct((n, N), F32),
        compiler_params=pltpu.CompilerParams(dimension_semantics=("arbitrary",)),
        name="inproj_sample",
    )(x, w_in)


def _attn_sample_kernel(q_ref, kn_ref, vn_ref, k0_ref, k1_ref, k2_ref, v0_ref, v1_ref, v2_ref,
                        bias_ref, bnew_ref, o_ref):
    q = _round_bf16(q_ref[...])
    k_refs = (k0_ref, k1_ref, k2_ref)
    v_refs = (v0_ref, v1_ref, v2_ref)
    s_new = jnp.sum(q * _round_bf16(kn_ref[...]), axis=-1, keepdims=True) * ATTN_SCALE
    v_new = _round_bf16(vn_ref[...])
    outs, lses = [], []
    for p in range(len(PATTERNS)):
        kb = _round_bf16(k_refs[p][...])
        s = jnp.sum(kb * q[None], axis=-1, keepdims=True) * ATTN_SCALE + bias_ref[p]
        sn = s_new + bnew_ref[p]
        m = jnp.maximum(jnp.max(s, axis=0), sn)
        e = jnp.exp(s - m[None])
        en = jnp.exp(sn - m)
        den = jnp.sum(e, axis=0) + en
        pr = _round_bf16(e / den[None])
        outs.append(jnp.sum(pr * _round_bf16(v_refs[p][...]), axis=0) + _round_bf16(en / den) * v_new)
        lses.append(m + jnp.log(den))
    lmax = jnp.maximum(jnp.maximum(lses[0], lses[1]), lses[2])
    ws = [jnp.exp(ls - lmax) for ls in lses]
    wsum = ws[0] + ws[1] + ws[2]
    acc = None
    for p in range(len(PATTERNS)):
        term = _round_bf16(ws[p] / wsum) * _round_bf16(outs[p])
        acc = term if acc is None else acc + term
    o_ref[...] = acc


def _attention_sample(q, k_new, v_new, win_k, win_v, bias_s, bias_new):
    DB, W, H, E = win_k.shape
    n = BAND
    assert PAST_LEN % W == 0 and PAST_LEN >= WINDOW_MAX
    cache_in, cache_specs = [], []
    for arr in (win_k, win_v):
        for window, dil in PATTERNS:
            blk = (W - window) // dil // n
            cache_in.append(arr.reshape(DB, W // dil, dil, H, E))
            cache_specs.append(pl.BlockSpec((None, n, None, H, E),
                                            functools.partial(lambda b, blk: (b, blk, 0, 0, 0), blk=blk)))
    vec = pl.BlockSpec((None, H, E), lambda b: (b, 0, 0))
    const = lambda a: pl.BlockSpec(a.shape, lambda b: (0,) * a.ndim)
    return pl.pallas_call(
        _attn_sample_kernel,
        grid=(DB,),
        in_specs=[vec, vec, vec] + cache_specs + [const(bias_s), const(bias_new)],
        out_specs=vec,
        out_shape=jax.ShapeDtypeStruct((DB, H, E), F32),
        compiler_params=pltpu.CompilerParams(dimension_semantics=("arbitrary",)),
        name="dilated_attn_sample",
    )(q, k_new, v_new, *cache_in, bias_s, bias_new)


def _tail_a_sample_kernel(x_ref, proj_ref, s0_ref, s1_ref, wc_ref, attn_ref, wo_ref, g1_ref, b1_ref, wq_ref,
                          x1_ref, qm_ref, u_ref, *, alpha):
    cw = CONV_WIDTH
    gb = proj_ref[:, 0:cw]
    u = proj_ref[:, cw:2 * cw] * proj_ref[:, 2 * cw:3 * cw]
    u_ref[...] = u
    wc = wc_ref[...]
    conv = gb * (wc[0:1, :] * s0_ref[...] + wc[1:2, :] * s1_ref[...] + wc[2:3, :] * u)
    mixed = jnp.concatenate([conv, attn_ref[...]], axis=-1).astype(BF16)
    x1 = _layer_norm(alpha * x_ref[...] + _dot(mixed, wo_ref[...]), g1_ref[...], b1_ref[...])
    x1_ref[...] = x1
    qm_ref[...] = _dot(x1.astype(BF16), wq_ref[...])


def _tail_a_sample(x, proj, s0, s1, w_conv, attn, w_out, g1, b1, w_mem_q, alpha):
    n, D = x.shape
    return pl.pallas_call(
        functools.partial(_tail_a_sample_kernel, alpha=alpha),
        out_shape=[jax.ShapeDtypeStruct((n, D), F32)] * 2 + [jax.ShapeDtypeStruct((n, CONV_WIDTH), F32)],
        name="tail_a_sample",
    )(x, proj, s0, s1, w_conv, attn, w_out, g1, b1, w_mem_q)


def _memattn_sample_kernel(q_ref, k_ref, v_ref, o_ref):
    q = _round_bf16(q_ref[...])
    s = jnp.sum(_round_bf16(k_ref[...]) * q[None], axis=-1, keepdims=True) * MEM_SCALE
    m = jnp.max(s, axis=0)
    e = jnp.exp(s - m[None])
    den = jnp.sum(e, axis=0)
    pr = _round_bf16(e / den[None])
    o_ref[...] = jnp.sum(pr * _round_bf16(v_ref[...]), axis=0)


def _memattn_sample(qm, mem_k, mem_v):
    DB, M, H, E = mem_k.shape
    vec = pl.BlockSpec((None, H, E), lambda b: (b, 0, 0))
    mem = pl.BlockSpec((None, M, H, E), lambda b: (b, 0, 0, 0))
    return pl.pallas_call(
        _memattn_sample_kernel,
        grid=(DB,),
        in_specs=[vec, mem, mem],
        out_specs=vec,
        out_shape=jax.ShapeDtypeStruct((DB, H, E), F32),
        compiler_params=pltpu.CompilerParams(dimension_semantics=("arbitrary",)),
        name="memattn_sample",
    )(qm, mem_k, mem_v)


def _tail_b_sample_kernel(x1_ref, o_ref, wmo_ref, g2_ref, b2_ref, wrt_ref, br_ref, cnt_ref,
                          x2_ref, idx_ref, gate_ref, rank_ref, cnt_out_ref, *, alpha):
    x2 = _layer_norm(alpha * x1_ref[...] + _dot(o_ref[...].astype(BF16), wmo_ref[...]), g2_ref[...], b2_ref[...])
    x2_ref[...] = x2
    logits_t = _dot_nt(wrt_ref[...], x2.astype(BF16)) + br_ref[...]
    idx, gates, rank, carry = _route(logits_t, cnt_ref[:, 0:1])
    idx_ref[...] = idx
    gate_ref[...] = gates
    rank_ref[...] = rank
    cnt_out_ref[...] = jnp.broadcast_to(carry, cnt_out_ref.shape)


def _tail_b_sample(x1, o, w_mem_o, g2, b2, wrt, br, cnt, alpha):
    n, D = x1.shape
    return pl.pallas_call(
        functools.partial(_tail_b_sample_kernel, alpha=alpha),
        out_shape=[jax.ShapeDtypeStruct((n, D), F32),
                   jax.ShapeDtypeStruct((TOP_K, n), I32),
                   jax.ShapeDtypeStruct((TOP_K, n), F32),
                   jax.ShapeDtypeStruct((TOP_K, n), I32),
                   jax.ShapeDtypeStruct(cnt.shape, F32)],
        name="tail_b_sample",
    )(x1, o, w_mem_o, g2, b2, wrt, br, cnt)


def _row_copy(src, dst, sem, src_row, dst_row):
    return pltpu.make_async_copy(src.at[pl.ds(src_row, 1)], dst.at[pl.ds(dst_row, 1)], sem)


def _dispatch_kernel(dest_ref, x_ref, xb_in_ref, xb_ref, sem):
    del xb_in_ref
    nt = x_ref.shape[0]

    def issue(t, c):
        for k in range(TOP_K):
            _row_copy(x_ref, xb_ref, sem, t, dest_ref[0, k * nt + t]).start()
        return c

    lax.fori_loop(0, nt, issue, 0)
    for k in range(TOP_K):
        pltpu.make_async_copy(x_ref, xb_ref.at[pl.ds(0, nt)], sem).wait()


def _dispatch(x2, dest_tiles, xb):
    T, D = x2.shape
    nt = dest_tiles.shape[2] // TOP_K
    return pl.pallas_call(
        _dispatch_kernel,
        grid=(T // nt,),
        in_specs=[pl.BlockSpec((None, 1, TOP_K * nt), lambda i: (i, 0, 0), memory_space=pltpu.SMEM),
                  pl.BlockSpec((nt, D), lambda i: (i, 0)),
                  pl.BlockSpec(memory_space=pl.ANY)],
        out_specs=pl.BlockSpec(memory_space=pl.ANY),
        out_shape=jax.ShapeDtypeStruct(xb.shape, xb.dtype),
        scratch_shapes=[pltpu.SemaphoreType.DMA],
        input_output_aliases={2: 0},
        compiler_params=pltpu.CompilerParams(dimension_semantics=("arbitrary",)),
        name="moe_dispatch",
    )(dest_tiles, x2, xb)


def _moe_kernel(be_ref, nu_ref, xb_ref, wgu_ref, bgu_ref, wd_ref, bd_ref, yb_ref, wgu_s, wd_s):
    j = pl.program_id(0)
    prev = be_ref[jnp.maximum(j - 1, 0)]
    new_expert = jnp.logical_or(j == 0, be_ref[j] != prev)
    used = j < nu_ref[0]

    @pl.when(jnp.logical_and(used, new_expert))
    def _():
        wgu_s[...] = wgu_ref[...].astype(BF16)
        wd_s[...] = wd_ref[...].astype(BF16)

    @pl.when(used)
    def _():
        x = xb_ref[...].astype(BF16)
        gu = _dot(x, wgu_s[...]) + bgu_ref[...]
        g = jnp.minimum(gu[:, :D_FF], SWIGLU_LIMIT)
        u = jnp.clip(gu[:, D_FF:], -SWIGLU_LIMIT, SWIGLU_LIMIT)
        act = (u + 1.0) * g * jax.nn.sigmoid(SWIGLU_ALPHA * g)
        yb_ref[...] = _dot(act.astype(BF16), wd_s[...]) + bd_ref[...]

    @pl.when(jnp.logical_not(used))
    def _():
        yb_ref[...] = jnp.zeros_like(yb_ref)


def _moe(blk_exp, n_used, xb, w_gate_up, b_gate_up, w_down, b_down):
    P, D = xb.shape
    bm = ROW_BLOCK
    E, _, F2 = w_gate_up.shape
    ex = lambda j, be, nu: (be[j], 0, 0)
    return pl.pallas_call(
        _moe_kernel,
        grid_spec=pltpu.PrefetchScalarGridSpec(
            num_scalar_prefetch=2,
            grid=(P // bm,),
            in_specs=[pl.BlockSpec((bm, D), lambda j, be, nu: (j, 0)),
                      pl.BlockSpec((None, D, F2), ex), pl.BlockSpec((None, 1, F2), ex),
                      pl.BlockSpec((None, D_FF, D), ex), pl.BlockSpec((None, 1, D), ex)],
            out_specs=pl.BlockSpec((bm, D), lambda j, be, nu: (j, 0)),
            scratch_shapes=[pltpu.VMEM((D, F2), BF16), pltpu.VMEM((D_FF, D), BF16)]),
        out_shape=jax.ShapeDtypeStruct((P, D), F32),
        compiler_params=pltpu.CompilerParams(dimension_semantics=("arbitrary",),
                                             vmem_limit_bytes=56 * 1024 * 1024),
        name="moe_grouped_ffn",
    )(blk_exp, n_used, xb, w_gate_up, b_gate_up.reshape(E, 1, F2), w_down, b_down.reshape(E, 1, D))


def _combine_kernel(dest_ref, yb_ref, x2_ref, gate_ref, g3_ref, b3_ref, o_ref, rows_s, sem, *, alpha):
    nt = x2_ref.shape[0]

    def issue(t, c):
        for k in range(TOP_K):
            _row_copy(yb_ref, rows_s.at[k], sem, dest_ref[0, k * nt + t], t).start()
        return c

    lax.fori_loop(0, nt, issue, 0)
    for k in range(TOP_K):
        pltpu.make_async_copy(yb_ref.at[pl.ds(0, nt)], rows_s.at[k], sem).wait()
    gates = gate_ref[...]
    y = gates[:, 0:1] * rows_s[0]
    for k in range(1, TOP_K):
        y = y + gates[:, k:k + 1] * rows_s[k]
    o_ref[...] = _layer_norm(alpha * x2_ref[...] + y, g3_ref[...], b3_ref[...])


def _combine(yb, dest_tiles, x2, gates_tok, g3, b3, alpha):
    T, D = x2.shape
    nt = dest_tiles.shape[2] // TOP_K
    return pl.pallas_call(
        functools.partial(_combine_kernel, alpha=alpha),
        grid=(T // nt,),
        in_specs=[pl.BlockSpec((None, 1, TOP_K * nt), lambda i: (i, 0, 0), memory_space=pltpu.SMEM),
                  pl.BlockSpec(memory_space=pl.ANY),
                  pl.BlockSpec((nt, D), lambda i: (i, 0)),
                  pl.BlockSpec((nt, TOP_K), lambda i: (i, 0)),
                  pl.BlockSpec((1, D), lambda i: (0, 0)), pl.BlockSpec((1, D), lambda i: (0, 0))],
        out_specs=pl.BlockSpec((nt, D), lambda i: (i, 0)),
        out_shape=jax.ShapeDtypeStruct((T, D), F32),
        scratch_shapes=[pltpu.VMEM((TOP_K, nt, D), F32), pltpu.SemaphoreType.DMA],
        compiler_params=pltpu.CompilerParams(dimension_semantics=("arbitrary",)),
        name="moe_combine",
    )(dest_tiles, yb, x2, gates_tok, g3, b3)


def _rel_bucket(dist):
    max_exact = N_BUCKETS // 2
    df = jnp.maximum(dist, 1).astype(F32)
    large = max_exact + (jnp.log(df / max_exact) / math.log(MAX_DISTANCE / max_exact)
                         * (N_BUCKETS - max_exact)).astype(I32)
    return jnp.where(dist < max_exact, dist, jnp.minimum(large, N_BUCKETS - 1))


def _bias_tables(rel_table):
    n = BAND
    taps = jnp.arange(n + 1)
    by_tap = jnp.stack([rel_table[_rel_bucket(taps * dil)].astype(F32) for _, dil in PATTERNS])
    w = jnp.concatenate([by_tap[:, ::-1], jnp.full((len(PATTERNS), n, N_HEADS), NEG_INF, F32)], axis=1)
    w = jnp.transpose(w, (0, 2, 1))
    band = jnp.tile(w, (1, 1, n))[:, :, :n * 2 * n].reshape(len(PATTERNS), N_HEADS, n, 2 * n)
    return by_tap, band.reshape(len(PATTERNS), N_SLABS, HEADS_PER_SLAB * n, 2 * n)


def _dest_tiles(dest, nt):
    T = dest.shape[1]
    return dest.reshape(TOP_K, T // nt, nt).transpose(1, 0, 2).reshape(T // nt, 1, TOP_K * nt)


def kernel(x_prompt, x_sample, mem_prompt, cache_win_k, cache_win_v, state_conv, cache_mem_k, cache_mem_v,
           rel_bias_table, w_in, w_conv, w_out, ln1_g, ln1_b, w_mem_q, w_mem_k, w_mem_v, w_mem_o,
           ln2_g, ln2_b, w_router, b_router, w_gate_up, b_gate_up, w_down, b_down, ln3_g, ln3_b):
    depth = w_in.shape[0]
    assert depth == 1
    alpha = (2 * depth) ** 0.25
    B, S, D = x_prompt.shape
    DB = x_sample.shape[0]
    T = B * S
    l = 0
    vec = lambda a: a[l].reshape(1, -1)

    by_tap, bias_band = _bias_tables(rel_bias_table)
    wrt = w_router[l].T.astype(BF16)
    br = b_router[l].reshape(N_EXPERTS, 1)
    w_in_bf = w_in[l].astype(BF16)
    wo_bf = w_out[l].astype(BF16)
    wq_bf = w_mem_q[l].astype(BF16)
    wmo_bf = w_mem_o[l].astype(BF16)

    conv_p, q_p, k_p, v_p, kh_p, vh_p, cs_p = _inproj_prompt(x_prompt, w_in_bf, w_conv[l])
    attn_p = _attention_prompt(q_p, k_p, v_p, bias_band)
    mk, mv, mk_bf, mv_bf = _memkv_prompt(mem_prompt, w_mem_k[l].astype(BF16), w_mem_v[l].astype(BF16))
    x2_p, idx_p, gate_p, rank_p, cnt_p = _mid_prompt(
        x_prompt, conv_p, attn_p, wo_bf[:CONV_WIDTH], wo_bf[CONV_WIDTH:], vec(ln1_g), vec(ln1_b), mk_bf, mv_bf,
        wq_bf, wmo_bf, vec(ln2_g), vec(ln2_b), wrt, br, alpha)

    xs = x_sample.reshape(DB, D)
    proj = _inproj_sample(xs, w_in_bf)
    cw = CONV_WIDTH
    q_s = proj[:, 3 * cw:3 * cw + ATTN_WIDTH]
    k_s = proj[:, 3 * cw + ATTN_WIDTH:3 * cw + 2 * ATTN_WIDTH]
    v_s = proj[:, 3 * cw + 2 * ATTN_WIDTH:]
    H, E = N_HEADS, HEAD_DIM
    bias_s = by_tap[:, :0:-1, :, None]
    bias_new = by_tap[:, 0, :, None]
    attn_s = _attention_sample(q_s.reshape(DB, H, E), k_s.reshape(DB, H, E), v_s.reshape(DB, H, E),
                               cache_win_k[l], cache_win_v[l], bias_s, bias_new).reshape(DB, ATTN_WIDTH)
    sc = state_conv[l]
    x1_s, qm_s, u_s = _tail_a_sample(xs, proj, sc[:, 0], sc[:, 1], w_conv[l], attn_s, wo_bf,
                                     vec(ln1_g), vec(ln1_b), wq_bf, alpha)
    o_s = _memattn_sample(qm_s.reshape(DB, MEM_HEADS, MEM_HEAD_DIM), cache_mem_k[l], cache_mem_v[l]).reshape(DB, D)
    x2_s, idx_s, gate_s, rank_s, cnt = _tail_b_sample(x1_s, o_s, wmo_bf, vec(ln2_g), vec(ln2_b), wrt, br,
                                                      cnt_p, alpha)

    bm = ROW_BLOCK
    counts = cnt[:, 0].astype(I32)
    padded = (counts + bm - 1) // bm * bm
    pends = jnp.cumsum(padded)
    pstarts = pends - padded
    n_blocks = -(-(T + DB) * TOP_K // bm) + N_EXPERTS
    P = n_blocks * bm
    first_row = jnp.arange(n_blocks, dtype=I32) * bm
    blk_exp = jnp.minimum(jnp.sum((pends[None, :] <= first_row[:, None]).astype(I32), axis=1), N_EXPERTS - 1)
    n_used = (pends[-1] // bm).reshape(1).astype(I32)
    expert_ids = jnp.arange(N_EXPERTS, dtype=I32)[:, None, None]
    start_of = lambda idx: jnp.sum(jnp.where(idx[None] == expert_ids, pstarts[:, None, None], 0), axis=0)
    dest_p = start_of(idx_p) + rank_p
    dest_s = start_of(idx_s) + rank_s
    dtiles_p = _dest_tiles(dest_p, COMBINE_TILE)
    dtiles_s = _dest_tiles(dest_s, DB)

    x2_pf = x2_p.reshape(T, D)
    xb = jnp.zeros((P, D), F32)
    xb = _dispatch(x2_pf, dtiles_p, xb)
    xb = _dispatch(x2_s, dtiles_s, xb)
    yb = _moe(blk_exp, n_used, xb, w_gate_up[l], b_gate_up[l], w_down[l], b_down[l])
    y_p = _combine(yb, dtiles_p, x2_pf, gate_p.T, vec(ln3_g), vec(ln3_b), alpha).reshape(B, S, D)
    y_s = _combine(yb, dtiles_s, x2_s, gate_s.T, vec(ln3_g), vec(ln3_b), alpha).reshape(DB, 1, D)

    conv_state_s = jnp.stack([sc[:, 1], u_s], axis=1)
    return (y_p, y_s,
            kh_p[None], vh_p[None], cs_p[:, 6:8].reshape(1, B, CONV_K - 1, CONV_WIDTH),
            mk.reshape(1, B, MEM_TOKENS, MEM_HEADS, MEM_HEAD_DIM), mv.reshape(1, B, MEM_TOKENS, MEM_HEADS, MEM_HEAD_DIM),
            k_s.reshape(1, DB, 1, H, E), v_s.reshape(1, DB, 1, H, E), conv_state_s.reshape(1, DB, CONV_K - 1, CONV_WIDTH))
```

```python
import functools
import math

import jax
import jax.numpy as jnp
import numpy as np
from jax import lax
from jax.experimental import pallas as pl
from jax.experimental.pallas import tpu as pltpu

F32 = jnp.float32
BF16 = jnp.bfloat16
I32 = jnp.int32

D_MODEL = 1024
CONV_WIDTH = 256
CONV_K = 3
HEAD_DIM = 64
N_HEADS = 12
ATTN_WIDTH = N_HEADS * HEAD_DIM
PATTERNS = ((128, 1), (512, 4), (2048, 16))
BAND = 128
WINDOW_MAX = 2048
ATTN_SCALE = 1.0 / math.sqrt(HEAD_DIM)
N_BUCKETS = 32
MAX_DISTANCE = WINDOW_MAX
MEM_TOKENS = 256
MEM_HEADS = 4
MEM_HEAD_DIM = 256
MEM_SCALE = 1.0 / math.sqrt(MEM_HEAD_DIM)
N_EXPERTS = 32
TOP_K = 4
D_FF = 1024
SWIGLU_LIMIT = 7.0
SWIGLU_ALPHA = 1.702
LN_EPS = 1e-5
PAST_LEN = 8192
NEG_INF = -1e30

LANES = 128
HEADS_PER_SLAB = LANES // HEAD_DIM
N_SLABS = ATTN_WIDTH // LANES

PROJ_TILE = 512
SEQ_TILE = 512
ROW_BLOCK = 512
COMBINE_TILE = 128

HIGHEST = lax.Precision.HIGHEST


def _layer_norm(x, g, b):
    mu = jnp.mean(x, axis=-1, keepdims=True)
    var = jnp.mean(jnp.square(x - mu), axis=-1, keepdims=True)
    return (x - mu) * lax.rsqrt(var + LN_EPS) * g + b


def _dot(a, b):
    return jnp.dot(a, b, preferred_element_type=F32)


def _dot_nt(a, b):
    return lax.dot_general(a, b, (((1,), (1,)), ((), ())), preferred_element_type=F32)


def _dot_hi(a, b):
    return jnp.dot(a, b, preferred_element_type=F32, precision=HIGHEST)


def _inproj_kernel(x_ref, w_ref, wc_ref, conv_ref, q_ref, k_ref, v_ref, cs_ref, u_s):
    ts = x_ref.shape[0]
    cw = CONV_WIDTH

    @pl.when(pl.program_id(1) == 0)
    def _():
        u_s[0:8, :] = jnp.zeros((8, cw), F32)

    x = x_ref[...].astype(BF16)
    gb = _dot(x, w_ref[:, 0:cw])
    gc = _dot(x, w_ref[:, cw:2 * cw])
    h = _dot(x, w_ref[:, 2 * cw:3 * cw])
    u = gc * h
    u_s[8:8 + ts, :] = u
    wc = wc_ref[...]
    conv = wc[0:1, :] * u_s[6:6 + ts, :] + wc[1:2, :] * u_s[7:7 + ts, :] + wc[2:3, :] * u
    conv_ref[...] = (gb * conv).astype(conv_ref.dtype)
    o = 3 * cw
    q_ref[...] = _dot(x, w_ref[:, o:o + ATTN_WIDTH]) * ATTN_SCALE
    k_ref[...] = _dot(x, w_ref[:, o + ATTN_WIDTH:o + 2 * ATTN_WIDTH])
    v_ref[...] = _dot(x, w_ref[:, o + 2 * ATTN_WIDTH:o + 3 * ATTN_WIDTH])
    tail = u_s[ts:ts + 8, :]
    u_s[0:8, :] = tail
    cs_ref[...] = tail


def _inproj_prompt(x, w_in_bf, w_conv):
    B, S, D = x.shape
    ts = PROJ_TILE
    row = lambda b, s: (b, s, 0)
    return pl.pallas_call(
        _inproj_kernel,
        grid=(B, S // ts),
        in_specs=[pl.BlockSpec((None, ts, D), row),
                  pl.BlockSpec(w_in_bf.shape, lambda b, s: (0, 0)),
                  pl.BlockSpec(w_conv.shape, lambda b, s: (0, 0))],
        out_specs=[pl.BlockSpec((None, ts, CONV_WIDTH), row),
                   pl.BlockSpec((None, ts, ATTN_WIDTH), row),
                   pl.BlockSpec((None, ts, ATTN_WIDTH), row),
                   pl.BlockSpec((None, ts, ATTN_WIDTH), row),
                   pl.BlockSpec((None, 8, CONV_WIDTH), lambda b, s: (b, 0, 0))],
        out_shape=[jax.ShapeDtypeStruct((B, S, CONV_WIDTH), BF16),
                   jax.ShapeDtypeStruct((B, S, ATTN_WIDTH), F32),
                   jax.ShapeDtypeStruct((B, S, ATTN_WIDTH), F32),
                   jax.ShapeDtypeStruct((B, S, ATTN_WIDTH), F32),
                   jax.ShapeDtypeStruct((B, 8, CONV_WIDTH), F32)],
        scratch_shapes=[pltpu.VMEM((ts + 8, CONV_WIDTH), F32)],
        compiler_params=pltpu.CompilerParams(dimension_semantics=("arbitrary", "arbitrary")),
        name="inproj_prompt",
    )(x, w_in_bf, w_conv)


def _attn_kernel(q_ref, k_ref, v_ref, bias_ref, o_ref, m_s, l_s, acc_s):
    S = q_ref.shape[0]
    n = BAND
    lane = lax.broadcasted_iota(I32, (n, LANES), 1)
    head_a = lane < HEAD_DIM

    def rows_of(start, count, dil):
        return pl.ds(start, count) if dil == 1 else pl.ds(start, count, stride=dil)

    def block(p, dil, start, has_prev):
        rows = rows_of(start, n, dil)
        qb = q_ref[rows, :]
        q2 = jnp.concatenate([jnp.where(head_a, qb, 0.0), jnp.where(head_a, 0.0, qb)], axis=0).astype(BF16)
        if has_prev:
            krows = rows_of(start - n * dil, 2 * n, dil)
            bias = bias_ref[p]
        else:
            krows = rows
            bias = bias_ref[p, :, n:2 * n]
        kb = k_ref[krows, :].astype(BF16)
        vb = v_ref[krows, :].astype(BF16)
        s = _dot_nt(q2, kb) + bias
        m = jnp.max(s, axis=-1, keepdims=True)
        e = jnp.exp(s - m)
        l = jnp.sum(e, axis=-1, keepdims=True)
        pv = _dot(e.astype(BF16), vb)
        m_s[p, rows, :] = jnp.where(head_a, m[:n], m[n:])
        l_s[p, rows, :] = jnp.where(head_a, l[:n], l[n:])
        acc_s[p, rows, :] = jnp.where(head_a, pv[:n], pv[n:])

    group = 4
    for p, (window, dil) in enumerate(PATTERNS):
        nb = (S // dil) // n
        if nb == 1:
            def classes(g, c, p=p, dil=dil):
                for j in range(group):
                    block(p, dil, g * group + j, False)
                return c
            lax.fori_loop(0, dil // group, classes, 0)
        elif nb == group:
            def one_class(r, c, p=p, dil=dil, nb=nb):
                block(p, dil, r, False)
                for i in range(1, nb):
                    block(p, dil, i * n * dil + r, True)
                return c
            lax.fori_loop(0, dil, one_class, 0)
        else:
            assert dil == 1 and nb % group == 0
            block(p, dil, 0, False)
            for i in range(1, group):
                block(p, dil, i * n, True)

            def later(g, c, p=p, dil=dil):
                for j in range(group):
                    block(p, dil, (g * group + j) * n, True)
                return c
            lax.fori_loop(1, nb // group, later, 0)

    rows_per_step = 256

    def merge(t, c):
        rows = pl.ds(t * rows_per_step, rows_per_step)
        ms = [m_s[p, rows, :] for p in range(len(PATTERNS))]
        m_all = jnp.maximum(jnp.maximum(ms[0], ms[1]), ms[2])
        num = den = None
        for p in range(len(PATTERNS)):
            w = jnp.exp(ms[p] - m_all)
            num = w * acc_s[p, rows, :] if num is None else num + w * acc_s[p, rows, :]
            den = w * l_s[p, rows, :] if den is None else den + w * l_s[p, rows, :]
        o_ref[rows, :] = (num / den).astype(o_ref.dtype)
        return c

    lax.fori_loop(0, S // rows_per_step, merge, 0)


def _attention_prompt(q, k, v, bias_tab):
    B, S, _ = q.shape
    slab = lambda p, b: (b, 0, p)
    spec = pl.BlockSpec((None, S, LANES), slab)
    return pl.pallas_call(
        _attn_kernel,
        grid=(N_SLABS, B),
        in_specs=[spec, spec, spec,
                  pl.BlockSpec((len(PATTERNS), None, HEADS_PER_SLAB * BAND, 2 * BAND), lambda p, b: (0, p, 0, 0))],
        out_specs=spec,
        out_shape=jax.ShapeDtypeStruct((B, S, ATTN_WIDTH), BF16),
        scratch_shapes=[pltpu.VMEM((len(PATTERNS), S, LANES), F32)] * 3,
        compiler_params=pltpu.CompilerParams(dimension_semantics=("arbitrary", "arbitrary")),
        name="dilated_attn_prompt",
    )(q, k, v, bias_tab)


def _memkv_kernel(m_ref, wk_ref, wv_ref, k_ref, v_ref, kb_ref, vb_ref):
    x = m_ref[...].astype(BF16)
    k = _dot(x, wk_ref[...])
    v = _dot(x, wv_ref[...])
    k_ref[...] = k
    v_ref[...] = v
    kb_ref[...] = k.astype(BF16)
    vb_ref[...] = v.astype(BF16)


def _memkv_prompt(mem, wk_bf, wv_bf):
    B, M, D = mem.shape
    row = lambda b: (b, 0, 0)
    full = lambda b: (0, 0)
    return pl.pallas_call(
        _memkv_kernel,
        grid=(B,),
        in_specs=[pl.BlockSpec((None, M, D), row), pl.BlockSpec((D, D), full), pl.BlockSpec((D, D), full)],
        out_specs=[pl.BlockSpec((None, M, D), row)] * 4,
        out_shape=[jax.ShapeDtypeStruct((B, M, D), F32)] * 2 + [jax.ShapeDtypeStruct((B, M, D), BF16)] * 2,
        compiler_params=pltpu.CompilerParams(dimension_semantics=("arbitrary",)),
        name="memkv_prompt",
    )(mem, wk_bf, wv_bf)


def _route(logits_t, carry):
    E, nt = logits_t.shape
    eidx = lax.broadcasted_iota(I32, (E, nt), 0)
    l = logits_t
    vals, idxs = [], []
    for _ in range(TOP_K):
        m = jnp.max(l, axis=0, keepdims=True)
        sel = jnp.min(jnp.where(l == m, eidx, E), axis=0, keepdims=True)
        vals.append(m)
        idxs.append(sel)
        l = jnp.where(eidx == sel, -jnp.inf, l)
    es = [jnp.exp(v - vals[0]) for v in vals]
    den = es[0] + es[1] + es[2] + es[3]
    gates = _stack_rows([e / den for e in es])
    chosen = (l == -jnp.inf)
    onehot = jnp.where(chosen, 1.0, 0.0)
    before = lax.broadcasted_iota(I32, (nt, nt), 0) < lax.broadcasted_iota(I32, (nt, nt), 1)
    prefix = _dot(onehot.astype(BF16), jnp.where(before, 1.0, 0.0).astype(BF16)) + carry
    ranks = [jnp.sum(jnp.where(eidx == s, prefix, 0.0), axis=0, keepdims=True) for s in idxs]
    rank = _stack_rows(ranks).astype(I32)
    idx = _stack_rows(idxs)
    return idx, gates, rank, carry + jnp.sum(onehot, axis=1, keepdims=True)


def _stack_rows(rows):
    k, nt = len(rows), rows[0].shape[1]
    r = lax.broadcasted_iota(I32, (k, nt), 0)
    out = jnp.broadcast_to(rows[-1], (k, nt))
    for i in range(k - 2, -1, -1):
        out = jnp.where(r == i, rows[i], out)
    return out


def _mid_kernel(x_ref, conv_ref, attn_ref, woc_ref, woa_ref, g1_ref, b1_ref, mk_ref, mv_ref, wq_ref, wmo_ref,
                g2_ref, b2_ref, wrt_ref, br_ref,
                x2_ref, idx_ref, gate_ref, rank_ref, cnt_ref, carry_s, *, alpha):
    first = jnp.logical_and(pl.program_id(0) == 0, pl.program_id(1) == 0)

    @pl.when(first)
    def _():
        carry_s[...] = jnp.zeros_like(carry_s)

    x = x_ref[...]
    mix = _dot(conv_ref[...], woc_ref[...]) + _dot(attn_ref[...], woa_ref[...])
    x1 = _layer_norm(alpha * x + mix, g1_ref[...], b1_ref[...])
    qm = (_dot(x1.astype(BF16), wq_ref[...]) * MEM_SCALE).astype(BF16)
    outs = []
    for h in range(MEM_HEADS):
        hs = slice(h * MEM_HEAD_DIM, (h + 1) * MEM_HEAD_DIM)
        s = _dot_nt(qm[:, hs], mk_ref[:, hs])
        m = jnp.max(s, axis=-1, keepdims=True)
        e = jnp.exp(s - m)
        l = jnp.sum(e, axis=-1, keepdims=True)
        outs.append((_dot(e.astype(BF16), mv_ref[:, hs]) / l).astype(BF16))
    o = jnp.concatenate(outs, axis=-1)
    x2 = _layer_norm(alpha * x1 + _dot(o, wmo_ref[...]), g2_ref[...], b2_ref[...])
    x2_ref[...] = x2
    logits_t = _dot_nt(wrt_ref[...], x2.astype(BF16)) + br_ref[...]
    idx, gates, rank, carry = _route(logits_t, carry_s[:, 0:1])
    idx_ref[...] = idx
    gate_ref[...] = gates
    rank_ref[...] = rank
    carry_s[...] = jnp.broadcast_to(carry, carry_s.shape)
    cnt_ref[...] = carry_s[...]


def _mid_prompt(x, conv, attn, woc, woa, g1, b1, mk, mv, wq, wmo, g2, b2, wrt, br, alpha):
    B, S, D = x.shape
    ts = SEQ_TILE
    T = B * S
    row = lambda b, s: (b, s, 0)
    full = lambda b, s: (0, 0)
    tok = lambda b, s: (0, b * (S // ts) + s)
    mem = lambda b, s: (b, 0, 0)
    vec = pl.BlockSpec((1, D), full)
    return pl.pallas_call(
        functools.partial(_mid_kernel, alpha=alpha),
        grid=(B, S // ts),
        in_specs=[pl.BlockSpec((None, ts, D), row),
                  pl.BlockSpec((None, ts, CONV_WIDTH), row),
                  pl.BlockSpec((None, ts, ATTN_WIDTH), row),
                  pl.BlockSpec(woc.shape, full), pl.BlockSpec(woa.shape, full), vec, vec,
                  pl.BlockSpec((None, MEM_TOKENS, D), mem), pl.BlockSpec((None, MEM_TOKENS, D), mem),
                  pl.BlockSpec(wq.shape, full), pl.BlockSpec(wmo.shape, full), vec, vec,
                  pl.BlockSpec(wrt.shape, full), pl.BlockSpec(br.shape, full)],
        out_specs=[pl.BlockSpec((None, ts, D), row),
                   pl.BlockSpec((TOP_K, ts), tok), pl.BlockSpec((TOP_K, ts), tok), pl.BlockSpec((TOP_K, ts), tok),
                   pl.BlockSpec((N_EXPERTS, LANES), full)],
        out_shape=[jax.ShapeDtypeStruct((B, S, D), F32),
                   jax.ShapeDtypeStruct((TOP_K, T), I32),
                   jax.ShapeDtypeStruct((TOP_K, T), F32),
                   jax.ShapeDtypeStruct((TOP_K, T), I32),
                   jax.ShapeDtypeStruct((N_EXPERTS, LANES), F32)],
        scratch_shapes=[pltpu.VMEM((N_EXPERTS, LANES), F32)],
        compiler_params=pltpu.CompilerParams(dimension_semantics=("arbitrary", "arbitrary")),
        name="mid_prompt",
    )(x, conv, attn, woc, woa, g1, b1, mk, mv, wq, wmo, g2, b2, wrt, br)


def _round_bf16(x):
    return x.astype(BF16).astype(F32)


def _inproj_sample_kernel(x_ref, w_ref, o_ref):
    o_ref[...] = _dot(x_ref[...].astype(BF16), w_ref[...])


def _inproj_sample(x, w_in):
    n, D = x.shape
    N = w_in.shape[1]
    bn = 512
    return pl.pallas_call(
        _inproj_sample_kernel,
        grid=(N // bn,),
        in_specs=[pl.BlockSpec((n, D), lambda j: (0, 0)), pl.BlockSpec((D, bn), lambda j: (0, j))],
        out_specs=pl.BlockSpec((n, bn), lambda j: (0, j)),
        out_shape=jax.ShapeDtypeStruct((n, N), F32),
        compiler_params=pltpu.CompilerParams(dimension_semantics=("arbitrary",)),
        name="inproj_sample",
    )(x, w_in)


def _attn_sample_kernel(q_ref, kn_ref, vn_ref, kt_ref, vt_ref, bias_ref, bnew_ref, o_ref):
    H, E, W = kt_ref.shape
    q = _round_bf16(q_ref[...])
    v_new = _round_bf16(vn_ref[...])
    s_all = jnp.sum(_round_bf16(kt_ref[...]) * q, axis=1) * ATTN_SCALE
    s_new = jnp.sum(_round_bf16(kn_ref[...]) * q, axis=1) * ATTN_SCALE
    outs, lses = [], []
    for p, (window, dil) in enumerate(PATTERNS):
        lo = W - window
        s = s_all[:, lo:] + bias_ref[p, :, lo:]
        sn = s_new + bnew_ref[p]
        m = jnp.maximum(jnp.max(s, axis=-1, keepdims=True), sn)
        e = jnp.exp(s - m)
        en = jnp.exp(sn - m)
        den = jnp.sum(e, axis=-1, keepdims=True) + en
        pr = _round_bf16(e / den)
        pv = jnp.sum(_round_bf16(vt_ref[:, :, lo:]) * pr[:, None, :], axis=-1, keepdims=True)
        outs.append(pv + _round_bf16(en / den)[:, :, None] * v_new)
        lses.append(m + jnp.log(den))
    lmax = jnp.maximum(jnp.maximum(lses[0], lses[1]), lses[2])
    ws = [jnp.exp(ls - lmax) for ls in lses]
    wsum = ws[0] + ws[1] + ws[2]
    acc = None
    for p in range(len(PATTERNS)):
        term = _round_bf16(ws[p] / wsum)[:, :, None] * _round_bf16(outs[p])
        acc = term if acc is None else acc + term
    o_ref[...] = acc


def _attention_sample(q, k_new, v_new, win_kt, win_vt, bias_pos, bias_new):
    DB, H, E, W = win_kt.shape
    vec = pl.BlockSpec((None, H, E, 1), lambda b: (b, 0, 0, 0))
    cache = pl.BlockSpec((None, H, E, W), lambda b: (b, 0, 0, 0))
    const = lambda a: pl.BlockSpec(a.shape, lambda b: (0,) * a.ndim)
    return pl.pallas_call(
        _attn_sample_kernel,
        grid=(DB,),
        in_specs=[vec, vec, vec, cache, cache, const(bias_pos), const(bias_new)],
        out_specs=vec,
        out_shape=jax.ShapeDtypeStruct((DB, H, E, 1), F32),
        compiler_params=pltpu.CompilerParams(dimension_semantics=("arbitrary",)),
        name="dilated_attn_sample",
    )(q, k_new, v_new, win_kt, win_vt, bias_pos, bias_new)


def _tail_a_sample_kernel(x_ref, proj_ref, s0_ref, s1_ref, wc_ref, attn_ref, wo_ref, g1_ref, b1_ref, wq_ref,
                          x1_ref, qm_ref, u_ref, *, alpha):
    cw = CONV_WIDTH
    gb = proj_ref[:, 0:cw]
    u = proj_ref[:, cw:2 * cw] * proj_ref[:, 2 * cw:3 * cw]
    u_ref[...] = u
    wc = wc_ref[...]
    conv = gb * (wc[0:1, :] * s0_ref[...] + wc[1:2, :] * s1_ref[...] + wc[2:3, :] * u)
    mixed = jnp.concatenate([conv, attn_ref[...]], axis=-1).astype(BF16)
    x1 = _layer_norm(alpha * x_ref[...] + _dot(mixed, wo_ref[...]), g1_ref[...], b1_ref[...])
    x1_ref[...] = x1
    qm_ref[...] = _dot(x1.astype(BF16), wq_ref[...])


def _tail_a_sample(x, proj, s0, s1, w_conv, attn, w_out, g1, b1, w_mem_q, alpha):
    n, D = x.shape
    return pl.pallas_call(
        functools.partial(_tail_a_sample_kernel, alpha=alpha),
        out_shape=[jax.ShapeDtypeStruct((n, D), F32)] * 2 + [jax.ShapeDtypeStruct((n, CONV_WIDTH), F32)],
        name="tail_a_sample",
    )(x, proj, s0, s1, w_conv, attn, w_out, g1, b1, w_mem_q)


def _memattn_sample_kernel(q_ref, k_ref, v_ref, o_ref):
    q = _round_bf16(q_ref[...])
    s = jnp.sum(_round_bf16(k_ref[...]) * q[None], axis=-1, keepdims=True) * MEM_SCALE
    m = jnp.max(s, axis=0)
    e = jnp.exp(s - m[None])
    den = jnp.sum(e, axis=0)
    pr = _round_bf16(e / den[None])
    o_ref[...] = jnp.sum(pr * _round_bf16(v_ref[...]), axis=0)


def _memattn_sample(qm, mem_k, mem_v):
    DB, M, H, E = mem_k.shape
    vec = pl.BlockSpec((None, H, E), lambda b: (b, 0, 0))
    mem = pl.BlockSpec((None, M, H, E), lambda b: (b, 0, 0, 0))
    return pl.pallas_call(
        _memattn_sample_kernel,
        grid=(DB,),
        in_specs=[vec, mem, mem],
        out_specs=vec,
        out_shape=jax.ShapeDtypeStruct((DB, H, E), F32),
        compiler_params=pltpu.CompilerParams(dimension_semantics=("arbitrary",)),
        name="memattn_sample",
    )(qm, mem_k, mem_v)


def _tail_b_sample_kernel(x1_ref, o_ref, wmo_ref, g2_ref, b2_ref, wrt_ref, br_ref, cnt_ref,
                          x2_ref, idx_ref, gate_ref, rank_ref, cnt_out_ref, *, alpha):
    x2 = _layer_norm(alpha * x1_ref[...] + _dot(o_ref[...].astype(BF16), wmo_ref[...]), g2_ref[...], b2_ref[...])
    x2_ref[...] = x2
    logits_t = _dot_nt(wrt_ref[...], x2.astype(BF16)) + br_ref[...]
    idx, gates, rank, carry = _route(logits_t, cnt_ref[:, 0:1])
    idx_ref[...] = idx
    gate_ref[...] = gates
    rank_ref[...] = rank
    cnt_out_ref[...] = jnp.broadcast_to(carry, cnt_out_ref.shape)


def _tail_b_sample(x1, o, w_mem_o, g2, b2, wrt, br, cnt, alpha):
    n, D = x1.shape
    return pl.pallas_call(
        functools.partial(_tail_b_sample_kernel, alpha=alpha),
        out_shape=[jax.ShapeDtypeStruct((n, D), F32),
                   jax.ShapeDtypeStruct((TOP_K, n), I32),
                   jax.ShapeDtypeStruct((TOP_K, n), F32),
                   jax.ShapeDtypeStruct((TOP_K, n), I32),
                   jax.ShapeDtypeStruct(cnt.shape, F32)],
        name="tail_b_sample",
    )(x1, o, w_mem_o, g2, b2, wrt, br, cnt)


def _row_copy(src, dst, sem, src_row, dst_row):
    return pltpu.make_async_copy(src.at[pl.ds(src_row, 1)], dst.at[pl.ds(dst_row, 1)], sem)


def _dispatch_kernel(dest_ref, x_ref, xb_in_ref, xb_ref, sem):
    del xb_in_ref
    nt = x_ref.shape[0]

    def issue(t, c):
        for k in range(TOP_K):
            _row_copy(x_ref, xb_ref, sem, t, dest_ref[0, k * nt + t]).start()
        return c

    lax.fori_loop(0, nt, issue, 0)
    for k in range(TOP_K):
        pltpu.make_async_copy(x_ref, xb_ref.at[pl.ds(0, nt)], sem).wait()


def _dispatch(x2, dest_tiles, xb):
    T, D = x2.shape
    nt = dest_tiles.shape[2] // TOP_K
    return pl.pallas_call(
        _dispatch_kernel,
        grid=(T // nt,),
        in_specs=[pl.BlockSpec((None, 1, TOP_K * nt), lambda i: (i, 0, 0), memory_space=pltpu.SMEM),
                  pl.BlockSpec((nt, D), lambda i: (i, 0)),
                  pl.BlockSpec(memory_space=pl.ANY)],
        out_specs=pl.BlockSpec(memory_space=pl.ANY),
        out_shape=jax.ShapeDtypeStruct(xb.shape, xb.dtype),
        scratch_shapes=[pltpu.SemaphoreType.DMA],
        input_output_aliases={2: 0},
        compiler_params=pltpu.CompilerParams(dimension_semantics=("arbitrary",)),
        name="moe_dispatch",
    )(dest_tiles, x2, xb)


def _moe_kernel(be_ref, nu_ref, xb_ref, wgu_ref, bgu_ref, wd_ref, bd_ref, yb_ref, wgu_s, wd_s):
    j = pl.program_id(0)
    prev = be_ref[jnp.maximum(j - 1, 0)]
    new_expert = jnp.logical_or(j == 0, be_ref[j] != prev)
    used = j < nu_ref[0]

    @pl.when(jnp.logical_and(used, new_expert))
    def _():
        wgu_s[...] = wgu_ref[...].astype(BF16)
        wd_s[...] = wd_ref[...].astype(BF16)

    @pl.when(used)
    def _():
        x = xb_ref[...].astype(BF16)
        gu = _dot(x, wgu_s[...]) + bgu_ref[...]
        g = jnp.minimum(gu[:, :D_FF], SWIGLU_LIMIT)
        u = jnp.clip(gu[:, D_FF:], -SWIGLU_LIMIT, SWIGLU_LIMIT)
        act = (u + 1.0) * g * jax.nn.sigmoid(SWIGLU_ALPHA * g)
        yb_ref[...] = _dot(act.astype(BF16), wd_s[...]) + bd_ref[...]

    @pl.when(jnp.logical_not(used))
    def _():
        yb_ref[...] = jnp.zeros_like(yb_ref)


def _moe(blk_exp, n_used, xb, w_gate_up, b_gate_up, w_down, b_down):
    P, D = xb.shape
    bm = ROW_BLOCK
    E, _, F2 = w_gate_up.shape
    ex = lambda j, be, nu: (be[j], 0, 0)
    return pl.pallas_call(
        _moe_kernel,
        grid_spec=pltpu.PrefetchScalarGridSpec(
            num_scalar_prefetch=2,
            grid=(P // bm,),
            in_specs=[pl.BlockSpec((bm, D), lambda j, be, nu: (j, 0)),
                      pl.BlockSpec((None, D, F2), ex), pl.BlockSpec((None, 1, F2), ex),
                      pl.BlockSpec((None, D_FF, D), ex), pl.BlockSpec((None, 1, D), ex)],
            out_specs=pl.BlockSpec((bm, D), lambda j, be, nu: (j, 0)),
            scratch_shapes=[pltpu.VMEM((D, F2), BF16), pltpu.VMEM((D_FF, D), BF16)]),
        out_shape=jax.ShapeDtypeStruct((P, D), F32),
        compiler_params=pltpu.CompilerParams(dimension_semantics=("arbitrary",),
                                             vmem_limit_bytes=56 * 1024 * 1024),
        name="moe_grouped_ffn",
    )(blk_exp, n_used, xb, w_gate_up, b_gate_up.reshape(E, 1, F2), w_down, b_down.reshape(E, 1, D))


def _combine_kernel(dest_ref, yb_ref, x2_ref, gate_ref, g3_ref, b3_ref, o_ref, rows_s, sem, *, alpha):
    nt = x2_ref.shape[0]

    def issue(t, c):
        for k in range(TOP_K):
            _row_copy(yb_ref, rows_s.at[k], sem, dest_ref[0, k * nt + t], t).start()
        return c

    lax.fori_loop(0, nt, issue, 0)
    for k in range(TOP_K):
        pltpu.make_async_copy(yb_ref.at[pl.ds(0, nt)], rows_s.at[k], sem).wait()
    gates = gate_ref[...]
    y = gates[:, 0:1] * rows_s[0]
    for k in range(1, TOP_K):
        y = y + gates[:, k:k + 1] * rows_s[k]
    o_ref[...] = _layer_norm(alpha * x2_ref[...] + y, g3_ref[...], b3_ref[...])


def _combine(yb, dest_tiles, x2, gates_tok, g3, b3, alpha):
    T, D = x2.shape
    nt = dest_tiles.shape[2] // TOP_K
    return pl.pallas_call(
        functools.partial(_combine_kernel, alpha=alpha),
        grid=(T // nt,),
        in_specs=[pl.BlockSpec((None, 1, TOP_K * nt), lambda i: (i, 0, 0), memory_space=pltpu.SMEM),
                  pl.BlockSpec(memory_space=pl.ANY),
                  pl.BlockSpec((nt, D), lambda i: (i, 0)),
                  pl.BlockSpec((nt, TOP_K), lambda i: (i, 0)),
                  pl.BlockSpec((1, D), lambda i: (0, 0)), pl.BlockSpec((1, D), lambda i: (0, 0))],
        out_specs=pl.BlockSpec((nt, D), lambda i: (i, 0)),
        out_shape=jax.ShapeDtypeStruct((T, D), F32),
        scratch_shapes=[pltpu.VMEM((TOP_K, nt, D), F32), pltpu.SemaphoreType.DMA],
        compiler_params=pltpu.CompilerParams(dimension_semantics=("arbitrary",)),
        name="moe_combine",
    )(dest_tiles, yb, x2, gates_tok, g3, b3)


def _rel_bucket(dist):
    max_exact = N_BUCKETS // 2
    df = jnp.maximum(dist, 1).astype(F32)
    large = max_exact + (jnp.log(df / max_exact) / math.log(MAX_DISTANCE / max_exact)
                         * (N_BUCKETS - max_exact)).astype(I32)
    return jnp.where(dist < max_exact, dist, jnp.minimum(large, N_BUCKETS - 1))


def _bias_tables(rel_table):
    n = BAND
    taps = jnp.arange(n + 1)
    by_tap = jnp.stack([rel_table[_rel_bucket(taps * dil)].astype(F32) for _, dil in PATTERNS])
    w = jnp.concatenate([by_tap[:, ::-1], jnp.full((len(PATTERNS), n, N_HEADS), NEG_INF, F32)], axis=1)
    w = jnp.transpose(w, (0, 2, 1))
    band = jnp.tile(w, (1, 1, n))[:, :, :n * 2 * n].reshape(len(PATTERNS), N_HEADS, n, 2 * n)
    return by_tap, band.reshape(len(PATTERNS), N_SLABS, HEADS_PER_SLAB * n, 2 * n)


def _tap_bias_by_slot(by_tap, W):
    assert PAST_LEN % W == 0 and PAST_LEN >= WINDOW_MAX
    n = BAND
    out = []
    for p, (window, dil) in enumerate(PATTERNS):
        taps = by_tap[p, :0:-1]
        col = jnp.concatenate([jnp.full((W // dil - n, N_HEADS), NEG_INF, F32), taps], axis=0)
        rest = jnp.full((W // dil, dil - 1, N_HEADS), NEG_INF, F32)
        out.append(jnp.concatenate([col[:, None, :], rest], axis=1).reshape(W, N_HEADS).T)
    return jnp.stack(out)


def _dest_tiles(dest, nt):
    T = dest.shape[1]
    return dest.reshape(TOP_K, T // nt, nt).transpose(1, 0, 2).reshape(T // nt, 1, TOP_K * nt)


def kernel(x_prompt, x_sample, mem_prompt, cache_win_k, cache_win_v, state_conv, cache_mem_k, cache_mem_v,
           rel_bias_table, w_in, w_conv, w_out, ln1_g, ln1_b, w_mem_q, w_mem_k, w_mem_v, w_mem_o,
           ln2_g, ln2_b, w_router, b_router, w_gate_up, b_gate_up, w_down, b_down, ln3_g, ln3_b):
    depth = w_in.shape[0]
    assert depth == 1
    alpha = (2 * depth) ** 0.25
    B, S, D = x_prompt.shape
    DB = x_sample.shape[0]
    T = B * S
    l = 0
    vec = lambda a: a[l].reshape(1, -1)

    by_tap, bias_band = _bias_tables(rel_bias_table)
    wrt = w_router[l].T.astype(BF16)
    br = b_router[l].reshape(N_EXPERTS, 1)
    w_in_bf = w_in[l].astype(BF16)
    wo_bf = w_out[l].astype(BF16)
    wq_bf = w_mem_q[l].astype(BF16)
    wmo_bf = w_mem_o[l].astype(BF16)

    conv_p, q_p, k_p, v_p, cs_p = _inproj_prompt(x_prompt, w_in_bf, w_conv[l])
    attn_p = _attention_prompt(q_p, k_p, v_p, bias_band)
    mk, mv, mk_bf, mv_bf = _memkv_prompt(mem_prompt, w_mem_k[l].astype(BF16), w_mem_v[l].astype(BF16))
    x2_p, idx_p, gate_p, rank_p, cnt_p = _mid_prompt(
        x_prompt, conv_p, attn_p, wo_bf[:CONV_WIDTH], wo_bf[CONV_WIDTH:], vec(ln1_g), vec(ln1_b), mk_bf, mv_bf,
        wq_bf, wmo_bf, vec(ln2_g), vec(ln2_b), wrt, br, alpha)

    xs = x_sample.reshape(DB, D)
    proj = _inproj_sample(xs, w_in_bf)
    cw = CONV_WIDTH
    q_s = proj[:, 3 * cw:3 * cw + ATTN_WIDTH]
    k_s = proj[:, 3 * cw + ATTN_WIDTH:3 * cw + 2 * ATTN_WIDTH]
    v_s = proj[:, 3 * cw + 2 * ATTN_WIDTH:]
    H, E = N_HEADS, HEAD_DIM
    bias_new = by_tap[:, 0, :, None]
    slots_last = lambda c: jnp.transpose(c, (0, 2, 3, 1))
    attn_s = _attention_sample(q_s.reshape(DB, H, E, 1), k_s.reshape(DB, H, E, 1), v_s.reshape(DB, H, E, 1),
                               slots_last(cache_win_k[l]), slots_last(cache_win_v[l]),
                               _tap_bias_by_slot(by_tap, WINDOW_MAX), bias_new).reshape(DB, ATTN_WIDTH)
    sc = state_conv[l]
    x1_s, qm_s, u_s = _tail_a_sample(xs, proj, sc[:, 0], sc[:, 1], w_conv[l], attn_s, wo_bf,
                                     vec(ln1_g), vec(ln1_b), wq_bf, alpha)
    o_s = _memattn_sample(qm_s.reshape(DB, MEM_HEADS, MEM_HEAD_DIM), cache_mem_k[l], cache_mem_v[l]).reshape(DB, D)
    x2_s, idx_s, gate_s, rank_s, cnt = _tail_b_sample(x1_s, o_s, wmo_bf, vec(ln2_g), vec(ln2_b), wrt, br,
                                                      cnt_p, alpha)

    bm = ROW_BLOCK
    counts = cnt[:, 0].astype(I32)
    padded = (counts + bm - 1) // bm * bm
    pends = jnp.cumsum(padded)
    pstarts = pends - padded
    n_blocks = -(-(T + DB) * TOP_K // bm) + N_EXPERTS
    P = n_blocks * bm
    first_row = jnp.arange(n_blocks, dtype=I32) * bm
    blk_exp = jnp.minimum(jnp.sum((pends[None, :] <= first_row[:, None]).astype(I32), axis=1), N_EXPERTS - 1)
    n_used = (pends[-1] // bm).reshape(1).astype(I32)
    expert_ids = jnp.arange(N_EXPERTS, dtype=I32)[:, None, None]
    start_of = lambda idx: jnp.sum(jnp.where(idx[None] == expert_ids, pstarts[:, None, None], 0), axis=0)
    dest_p = start_of(idx_p) + rank_p
    dest_s = start_of(idx_s) + rank_s
    dtiles_p = _dest_tiles(dest_p, COMBINE_TILE)
    dtiles_s = _dest_tiles(dest_s, DB)

    x2_pf = x2_p.reshape(T, D)
    xb = jnp.zeros((P, D), F32)
    xb = _dispatch(x2_pf, dtiles_p, xb)
    xb = _dispatch(x2_s, dtiles_s, xb)
    yb = _moe(blk_exp, n_used, xb, w_gate_up[l], b_gate_up[l], w_down[l], b_down[l])
    y_p = _combine(yb, dtiles_p, x2_pf, gate_p.T, vec(ln3_g), vec(ln3_b), alpha).reshape(B, S, D)
    y_s = _combine(yb, dtiles_s, x2_s, gate_s.T, vec(ln3_g), vec(ln3_b), alpha).reshape(DB, 1, D)

    conv_state_s = jnp.stack([sc[:, 1], u_s], axis=1)
    return (y_p, y_s,
            k_p.reshape(1, B, S, H, E), v_p.reshape(1, B, S, H, E), cs_p[:, 6:8].reshape(1, B, CONV_K - 1, CONV_WIDTH),
            mk.reshape(1, B, MEM_TOKENS, MEM_HEADS, MEM_HEAD_DIM), mv.reshape(1, B, MEM_TOKENS, MEM_HEADS, MEM_HEAD_DIM),
            k_s.reshape(1, DB, 1, H, E), v_s.reshape(1, DB, 1, H, E), conv_state_s.reshape(1, DB, CONV_K - 1, CONV_WIDTH))
```

```python
import functools
import math

import jax
import jax.numpy as jnp
import numpy as np
from jax import lax
from jax.experimental import pallas as pl
from jax.experimental.pallas import tpu as pltpu

F32 = jnp.float32
BF16 = jnp.bfloat16
I32 = jnp.int32

D_MODEL = 1024
CONV_WIDTH = 256
CONV_K = 3
HEAD_DIM = 64
N_HEADS = 12
ATTN_WIDTH = N_HEADS * HEAD_DIM
PATTERNS = ((128, 1), (512, 4), (2048, 16))
BAND = 128
WINDOW_MAX = 2048
ATTN_SCALE = 1.0 / math.sqrt(HEAD_DIM)
N_BUCKETS = 32
MAX_DISTANCE = WINDOW_MAX
MEM_TOKENS = 256
MEM_HEADS = 4
MEM_HEAD_DIM = 256
MEM_SCALE = 1.0 / math.sqrt(MEM_HEAD_DIM)
N_EXPERTS = 32
TOP_K = 4
D_FF = 1024
SWIGLU_LIMIT = 7.0
SWIGLU_ALPHA = 1.702
LN_EPS = 1e-5
PAST_LEN = 8192
NEG_INF = -1e30

LANES = 128
HEADS_PER_SLAB = LANES // HEAD_DIM
N_SLABS = ATTN_WIDTH // LANES

PROJ_TILE = 512
SEQ_TILE = 512
ROW_BLOCK = 512
COMBINE_TILE = 128
ATTN_GROUP = 8

HIGHEST = lax.Precision.HIGHEST


def _layer_norm(x, g, b):
    mu = jnp.mean(x, axis=-1, keepdims=True)
    var = jnp.mean(jnp.square(x - mu), axis=-1, keepdims=True)
    return (x - mu) * lax.rsqrt(var + LN_EPS) * g + b


def _dot(a, b):
    return jnp.dot(a, b, preferred_element_type=F32)


def _dot_nt(a, b):
    return lax.dot_general(a, b, (((1,), (1,)), ((), ())), preferred_element_type=F32)


def _dot_hi(a, b):
    return jnp.dot(a, b, preferred_element_type=F32, precision=HIGHEST)


def _inproj_kernel(x_ref, w_ref, wc_ref, conv_ref, q_ref, k_ref, v_ref, cs_ref, u_s):
    ts = x_ref.shape[0]
    cw = CONV_WIDTH

    @pl.when(pl.program_id(1) == 0)
    def _():
        u_s[0:8, :] = jnp.zeros((8, cw), F32)

    x = x_ref[...].astype(BF16)
    gb = _dot(x, w_ref[:, 0:cw])
    gc = _dot(x, w_ref[:, cw:2 * cw])
    h = _dot(x, w_ref[:, 2 * cw:3 * cw])
    u = gc * h
    u_s[8:8 + ts, :] = u
    wc = wc_ref[...]
    conv = wc[0:1, :] * u_s[6:6 + ts, :] + wc[1:2, :] * u_s[7:7 + ts, :] + wc[2:3, :] * u
    conv_ref[...] = (gb * conv).astype(conv_ref.dtype)
    o = 3 * cw
    q_ref[...] = _dot(x, w_ref[:, o:o + ATTN_WIDTH]) * ATTN_SCALE
    k_ref[...] = _dot(x, w_ref[:, o + ATTN_WIDTH:o + 2 * ATTN_WIDTH])
    v_ref[...] = _dot(x, w_ref[:, o + 2 * ATTN_WIDTH:o + 3 * ATTN_WIDTH])
    tail = u_s[ts:ts + 8, :]
    u_s[0:8, :] = tail
    cs_ref[...] = tail


def _inproj_prompt(x, w_in_bf, w_conv):
    B, S, D = x.shape
    ts = PROJ_TILE
    row = lambda b, s: (b, s, 0)
    return pl.pallas_call(
        _inproj_kernel,
        grid=(B, S // ts),
        in_specs=[pl.BlockSpec((None, ts, D), row),
                  pl.BlockSpec(w_in_bf.shape, lambda b, s: (0, 0)),
                  pl.BlockSpec(w_conv.shape, lambda b, s: (0, 0))],
        out_specs=[pl.BlockSpec((None, ts, CONV_WIDTH), row),
                   pl.BlockSpec((None, ts, ATTN_WIDTH), row),
                   pl.BlockSpec((None, ts, ATTN_WIDTH), row),
                   pl.BlockSpec((None, ts, ATTN_WIDTH), row),
                   pl.BlockSpec((None, 8, CONV_WIDTH), lambda b, s: (b, 0, 0))],
        out_shape=[jax.ShapeDtypeStruct((B, S, CONV_WIDTH), BF16),
                   jax.ShapeDtypeStruct((B, S, ATTN_WIDTH), F32),
                   jax.ShapeDtypeStruct((B, S, ATTN_WIDTH), F32),
                   jax.ShapeDtypeStruct((B, S, ATTN_WIDTH), F32),
                   jax.ShapeDtypeStruct((B, 8, CONV_WIDTH), F32)],
        scratch_shapes=[pltpu.VMEM((ts + 8, CONV_WIDTH), F32)],
        compiler_params=pltpu.CompilerParams(dimension_semantics=("arbitrary", "arbitrary")),
        name="inproj_prompt",
    )(x, w_in_bf, w_conv)


def _attn_kernel(q_ref, k_ref, v_ref, bias_ref, o_ref, m_s, l_s, acc_s):
    S = q_ref.shape[0]
    n = BAND
    lane = lax.broadcasted_iota(I32, (n, LANES), 1)
    head_a = lane < HEAD_DIM

    def rows_of(start, count, dil):
        return pl.ds(start, count) if dil == 1 else pl.ds(start, count, stride=dil)

    def blocks(p, dil, specs):
        rows = [rows_of(start, n, dil) for start, _ in specs]
        scores, values = [], []
        for (start, has_prev), r in zip(specs, rows):
            qb = q_ref[r, :]
            q2 = jnp.concatenate([jnp.where(head_a, qb, 0.0), jnp.where(head_a, 0.0, qb)], axis=0).astype(BF16)
            krows = rows_of(start - n * dil, 2 * n, dil) if has_prev else r
            bias = bias_ref[p] if has_prev else bias_ref[p, :, n:2 * n]
            scores.append(_dot_nt(q2, k_ref[krows, :].astype(BF16)) + bias)
            values.append(v_ref[krows, :].astype(BF16))
        ms = [jnp.max(s, axis=-1, keepdims=True) for s in scores]
        es = [jnp.exp(s - m) for s, m in zip(scores, ms)]
        ls = [jnp.sum(e, axis=-1, keepdims=True) for e in es]
        pvs = [_dot(e.astype(BF16), vb) for e, vb in zip(es, values)]
        for r, m, l, pv in zip(rows, ms, ls, pvs):
            m_s[p, r, :] = jnp.where(head_a, m[:n], m[n:])
            l_s[p, r, :] = jnp.where(head_a, l[:n], l[n:])
            acc_s[p, r, :] = jnp.where(head_a, pv[:n], pv[n:])

    for p, (window, dil) in enumerate(PATTERNS):
        nb = (S // dil) // n
        specs = [(i * n * dil + r, i > 0) for r in range(dil) for i in range(nb)]
        for g in range(0, len(specs), ATTN_GROUP):
            blocks(p, dil, specs[g:g + ATTN_GROUP])

    rows_per_step = 256

    def merge(t, c):
        rows = pl.ds(t * rows_per_step, rows_per_step)
        ms = [m_s[p, rows, :] for p in range(len(PATTERNS))]
        m_all = jnp.maximum(jnp.maximum(ms[0], ms[1]), ms[2])
        num = den = None
        for p in range(len(PATTERNS)):
            w = jnp.exp(ms[p] - m_all)
            num = w * acc_s[p, rows, :] if num is None else num + w * acc_s[p, rows, :]
            den = w * l_s[p, rows, :] if den is None else den + w * l_s[p, rows, :]
        o_ref[rows, :] = (num / den).astype(o_ref.dtype)
        return c

    lax.fori_loop(0, S // rows_per_step, merge, 0)


def _attention_prompt(q, k, v, bias_tab):
    B, S, _ = q.shape
    slab = lambda p, b: (b, 0, p)
    spec = pl.BlockSpec((None, S, LANES), slab)
    return pl.pallas_call(
        _attn_kernel,
        grid=(N_SLABS, B),
        in_specs=[spec, spec, spec,
                  pl.BlockSpec((len(PATTERNS), None, HEADS_PER_SLAB * BAND, 2 * BAND), lambda p, b: (0, p, 0, 0))],
        out_specs=spec,
        out_shape=jax.ShapeDtypeStruct((B, S, ATTN_WIDTH), BF16),
        scratch_shapes=[pltpu.VMEM((len(PATTERNS), S, LANES), F32)] * 3,
        compiler_params=pltpu.CompilerParams(dimension_semantics=("arbitrary", "arbitrary")),
        name="dilated_attn_prompt",
    )(q, k, v, bias_tab)


def _memkv_kernel(m_ref, wk_ref, wv_ref, k_ref, v_ref, kb_ref, vb_ref):
    x = m_ref[...].astype(BF16)
    k = _dot(x, wk_ref[...])
    v = _dot(x, wv_ref[...])
    k_ref[...] = k
    v_ref[...] = v
    kb_ref[...] = k.astype(BF16)
    vb_ref[...] = v.astype(BF16)


def _memkv_prompt(mem, wk_bf, wv_bf):
    B, M, D = mem.shape
    row = lambda b: (b, 0, 0)
    full = lambda b: (0, 0)
    return pl.pallas_call(
        _memkv_kernel,
        grid=(B,),
        in_specs=[pl.BlockSpec((None, M, D), row), pl.BlockSpec((D, D), full), pl.BlockSpec((D, D), full)],
        out_specs=[pl.BlockSpec((None, M, D), row)] * 4,
        out_shape=[jax.ShapeDtypeStruct((B, M, D), F32)] * 2 + [jax.ShapeDtypeStruct((B, M, D), BF16)] * 2,
        compiler_params=pltpu.CompilerParams(dimension_semantics=("arbitrary",)),
        name="memkv_prompt",
    )(mem, wk_bf, wv_bf)


def _route(logits_t, carry):
    E, nt = logits_t.shape
    eidx = lax.broadcasted_iota(I32, (E, nt), 0)
    l = logits_t
    vals, idxs = [], []
    for _ in range(TOP_K):
        m = jnp.max(l, axis=0, keepdims=True)
        sel = jnp.min(jnp.where(l == m, eidx, E), axis=0, keepdims=True)
        vals.append(m)
        idxs.append(sel)
        l = jnp.where(eidx == sel, -jnp.inf, l)
    es = [jnp.exp(v - vals[0]) for v in vals]
    den = es[0] + es[1] + es[2] + es[3]
    gates = _stack_rows([e / den for e in es])
    chosen = (l == -jnp.inf)
    onehot = jnp.where(chosen, 1.0, 0.0)
    before = lax.broadcasted_iota(I32, (nt, nt), 0) < lax.broadcasted_iota(I32, (nt, nt), 1)
    prefix = _dot(onehot.astype(BF16), jnp.where(before, 1.0, 0.0).astype(BF16)) + carry
    ranks = [jnp.sum(jnp.where(eidx == s, prefix, 0.0), axis=0, keepdims=True) for s in idxs]
    rank = _stack_rows(ranks).astype(I32)
    idx = _stack_rows(idxs)
    return idx, gates, rank, carry + jnp.sum(onehot, axis=1, keepdims=True)


def _stack_rows(rows):
    k, nt = len(rows), rows[0].shape[1]
    r = lax.broadcasted_iota(I32, (k, nt), 0)
    out = jnp.broadcast_to(rows[-1], (k, nt))
    for i in range(k - 2, -1, -1):
        out = jnp.where(r == i, rows[i], out)
    return out


def _mid_kernel(x_ref, conv_ref, attn_ref, woc_ref, woa_ref, g1_ref, b1_ref, mk_ref, mv_ref, wq_ref, wmo_ref,
                g2_ref, b2_ref, wrt_ref, br_ref,
                x2_ref, idx_ref, gate_ref, rank_ref, cnt_ref, carry_s, *, alpha):
    first = jnp.logical_and(pl.program_id(0) == 0, pl.program_id(1) == 0)

    @pl.when(first)
    def _():
        carry_s[...] = jnp.zeros_like(carry_s)

    x = x_ref[...]
    mix = _dot(conv_ref[...], woc_ref[...]) + _dot(attn_ref[...], woa_ref[...])
    x1 = _layer_norm(alpha * x + mix, g1_ref[...], b1_ref[...])
    qm = (_dot(x1.astype(BF16), wq_ref[...]) * MEM_SCALE).astype(BF16)
    outs = []
    for h in range(MEM_HEADS):
        hs = slice(h * MEM_HEAD_DIM, (h + 1) * MEM_HEAD_DIM)
        s = _dot_nt(qm[:, hs], mk_ref[:, hs])
        m = jnp.max(s, axis=-1, keepdims=True)
        e = jnp.exp(s - m)
        l = jnp.sum(e, axis=-1, keepdims=True)
        outs.append((_dot(e.astype(BF16), mv_ref[:, hs]) / l).astype(BF16))
    o = jnp.concatenate(outs, axis=-1)
    x2 = _layer_norm(alpha * x1 + _dot(o, wmo_ref[...]), g2_ref[...], b2_ref[...])
    x2_ref[...] = x2
    logits_t = _dot_nt(wrt_ref[...], x2.astype(BF16)) + br_ref[...]
    idx, gates, rank, carry = _route(logits_t, carry_s[:, 0:1])
    idx_ref[...] = idx
    gate_ref[...] = gates
    rank_ref[...] = rank
    carry_s[...] = jnp.broadcast_to(carry, carry_s.shape)
    cnt_ref[...] = carry_s[...]


def _mid_prompt(x, conv, attn, woc, woa, g1, b1, mk, mv, wq, wmo, g2, b2, wrt, br, alpha):
    B, S, D = x.shape
    ts = SEQ_TILE
    T = B * S
    row = lambda b, s: (b, s, 0)
    full = lambda b, s: (0, 0)
    tok = lambda b, s: (0, b * (S // ts) + s)
    mem = lambda b, s: (b, 0, 0)
    vec = pl.BlockSpec((1, D), full)
    return pl.pallas_call(
        functools.partial(_mid_kernel, alpha=alpha),
        grid=(B, S // ts),
        in_specs=[pl.BlockSpec((None, ts, D), row),
                  pl.BlockSpec((None, ts, CONV_WIDTH), row),
                  pl.BlockSpec((None, ts, ATTN_WIDTH), row),
                  pl.BlockSpec(woc.shape, full), pl.BlockSpec(woa.shape, full), vec, vec,
                  pl.BlockSpec((None, MEM_TOKENS, D), mem), pl.BlockSpec((None, MEM_TOKENS, D), mem),
                  pl.BlockSpec(wq.shape, full), pl.BlockSpec(wmo.shape, full), vec, vec,
                  pl.BlockSpec(wrt.shape, full), pl.BlockSpec(br.shape, full)],
        out_specs=[pl.BlockSpec((None, ts, D), row),
                   pl.BlockSpec((TOP_K, ts), tok), pl.BlockSpec((TOP_K, ts), tok), pl.BlockSpec((TOP_K, ts), tok),
                   pl.BlockSpec((N_EXPERTS, LANES), full)],
        out_shape=[jax.ShapeDtypeStruct((B, S, D), F32),
                   jax.ShapeDtypeStruct((TOP_K, T), I32),
                   jax.ShapeDtypeStruct((TOP_K, T), F32),
                   jax.ShapeDtypeStruct((TOP_K, T), I32),
                   jax.ShapeDtypeStruct((N_EXPERTS, LANES), F32)],
        scratch_shapes=[pltpu.VMEM((N_EXPERTS, LANES), F32)],
        compiler_params=pltpu.CompilerParams(dimension_semantics=("arbitrary", "arbitrary")),
        name="mid_prompt",
    )(x, conv, attn, woc, woa, g1, b1, mk, mv, wq, wmo, g2, b2, wrt, br)


def _round_bf16(x):
    return x.astype(BF16).astype(F32)


def _inproj_sample_kernel(x_ref, w_ref, o_ref):
    o_ref[...] = _dot(x_ref[...].astype(BF16), w_ref[...])


def _inproj_sample(x, w_in):
    n, D = x.shape
    N = w_in.shape[1]
    bn = 512
    return pl.pallas_call(
        _inproj_sample_kernel,
        grid=(N // bn,),
        in_specs=[pl.BlockSpec((n, D), lambda j: (0, 0)), pl.BlockSpec((D, bn), lambda j: (0, j))],
        out_specs=pl.BlockSpec((n, bn), lambda j: (0, j)),
        out_shape=jax.ShapeDtypeStruct((n, N), F32),
        compiler_params=pltpu.CompilerParams(dimension_semantics=("arbitrary",)),
        name="inproj_sample",
    )(x, w_in)


def _attn_sample_kernel(q_ref, kn_ref, vn_ref, kt_ref, vt_ref, bias_ref, bnew_ref, o_ref):
    H, E, W = kt_ref.shape
    q = _round_bf16(q_ref[...])
    v_new = _round_bf16(vn_ref[...])
    s_all = jnp.sum(_round_bf16(kt_ref[...]) * q, axis=1) * ATTN_SCALE
    s_new = jnp.sum(_round_bf16(kn_ref[...]) * q, axis=1) * ATTN_SCALE
    outs, lses = [], []
    for p, (window, dil) in enumerate(PATTERNS):
        lo = W - window
        s = s_all[:, lo:] + bias_ref[p, :, lo:]
        sn = s_new + bnew_ref[p]
        m = jnp.maximum(jnp.max(s, axis=-1, keepdims=True), sn)
        e = jnp.exp(s - m)
        en = jnp.exp(sn - m)
        den = jnp.sum(e, axis=-1, keepdims=True) + en
        pr = _round_bf16(e / den)
        pv = jnp.sum(_round_bf16(vt_ref[:, :, lo:]) * pr[:, None, :], axis=-1, keepdims=True)
        outs.append(pv + _round_bf16(en / den)[:, :, None] * v_new)
        lses.append(m + jnp.log(den))
    lmax = jnp.maximum(jnp.maximum(lses[0], lses[1]), lses[2])
    ws = [jnp.exp(ls - lmax) for ls in lses]
    wsum = ws[0] + ws[1] + ws[2]
    acc = None
    for p in range(len(PATTERNS)):
        term = _round_bf16(ws[p] / wsum)[:, :, None] * _round_bf16(outs[p])
        acc = term if acc is None else acc + term
    o_ref[...] = acc


def _attention_sample(q, k_new, v_new, win_kt, win_vt, bias_pos, bias_new):
    DB, H, E, W = win_kt.shape
    vec = pl.BlockSpec((None, H, E, 1), lambda b: (b, 0, 0, 0))
    cache = pl.BlockSpec((None, H, E, W), lambda b: (b, 0, 0, 0))
    const = lambda a: pl.BlockSpec(a.shape, lambda b: (0,) * a.ndim)
    return pl.pallas_call(
        _attn_sample_kernel,
        grid=(DB,),
        in_specs=[vec, vec, vec, cache, cache, const(bias_pos), const(bias_new)],
        out_specs=vec,
        out_shape=jax.ShapeDtypeStruct((DB, H, E, 1), F32),
        compiler_params=pltpu.CompilerParams(dimension_semantics=("arbitrary",)),
        name="dilated_attn_sample",
    )(q, k_new, v_new, win_kt, win_vt, bias_pos, bias_new)


def _tail_a_sample_kernel(x_ref, proj_ref, s0_ref, s1_ref, wc_ref, attn_ref, wo_ref, g1_ref, b1_ref, wq_ref,
                          x1_ref, qm_ref, u_ref, *, alpha):
    cw = CONV_WIDTH
    gb = proj_ref[:, 0:cw]
    u = proj_ref[:, cw:2 * cw] * proj_ref[:, 2 * cw:3 * cw]
    u_ref[...] = u
    wc = wc_ref[...]
    conv = gb * (wc[0:1, :] * s0_ref[...] + wc[1:2, :] * s1_ref[...] + wc[2:3, :] * u)
    mixed = jnp.concatenate([conv, attn_ref[...]], axis=-1).astype(BF16)
    x1 = _layer_norm(alpha * x_ref[...] + _dot(mixed, wo_ref[...]), g1_ref[...], b1_ref[...])
    x1_ref[...] = x1
    qm_ref[...] = _dot(x1.astype(BF16), wq_ref[...])


def _tail_a_sample(x, proj, s0, s1, w_conv, attn, w_out, g1, b1, w_mem_q, alpha):
    n, D = x.shape
    return pl.pallas_call(
        functools.partial(_tail_a_sample_kernel, alpha=alpha),
        out_shape=[jax.ShapeDtypeStruct((n, D), F32)] * 2 + [jax.ShapeDtypeStruct((n, CONV_WIDTH), F32)],
        name="tail_a_sample",
    )(x, proj, s0, s1, w_conv, attn, w_out, g1, b1, w_mem_q)


def _memattn_sample_kernel(q_ref, k_ref, v_ref, o_ref):
    q = _round_bf16(q_ref[...])
    s = jnp.sum(_round_bf16(k_ref[...]) * q[None], axis=-1, keepdims=True) * MEM_SCALE
    m = jnp.max(s, axis=0)
    e = jnp.exp(s - m[None])
    den = jnp.sum(e, axis=0)
    pr = _round_bf16(e / den[None])
    o_ref[...] = jnp.sum(pr * _round_bf16(v_ref[...]), axis=0)


def _memattn_sample(qm, mem_k, mem_v):
    DB, M, H, E = mem_k.shape
    vec = pl.BlockSpec((None, H, E), lambda b: (b, 0, 0))
    mem = pl.BlockSpec((None, M, H, E), lambda b: (b, 0, 0, 0))
    return pl.pallas_call(
        _memattn_sample_kernel,
        grid=(DB,),
        in_specs=[vec, mem, mem],
        out_specs=vec,
        out_shape=jax.ShapeDtypeStruct((DB, H, E), F32),
        compiler_params=pltpu.CompilerParams(dimension_semantics=("arbitrary",)),
        name="memattn_sample",
    )(qm, mem_k, mem_v)


def _tail_b_sample_kernel(x1_ref, o_ref, wmo_ref, g2_ref, b2_ref, wrt_ref, br_ref, cnt_ref,
                          x2_ref, idx_ref, gate_ref, rank_ref, cnt_out_ref, *, alpha):
    x2 = _layer_norm(alpha * x1_ref[...] + _dot(o_ref[...].astype(BF16), wmo_ref[...]), g2_ref[...], b2_ref[...])
    x2_ref[...] = x2
    logits_t = _dot_nt(wrt_ref[...], x2.astype(BF16)) + br_ref[...]
    idx, gates, rank, carry = _route(logits_t, cnt_ref[:, 0:1])
    idx_ref[...] = idx
    gate_ref[...] = gates
    rank_ref[...] = rank
    cnt_out_ref[...] = jnp.broadcast_to(carry, cnt_out_ref.shape)


def _tail_b_sample(x1, o, w_mem_o, g2, b2, wrt, br, cnt, alpha):
    n, D = x1.shape
    return pl.pallas_call(
        functools.partial(_tail_b_sample_kernel, alpha=alpha),
        out_shape=[jax.ShapeDtypeStruct((n, D), F32),
                   jax.ShapeDtypeStruct((TOP_K, n), I32),
                   jax.ShapeDtypeStruct((TOP_K, n), F32),
                   jax.ShapeDtypeStruct((TOP_K, n), I32),
                   jax.ShapeDtypeStruct(cnt.shape, F32)],
        name="tail_b_sample",
    )(x1, o, w_mem_o, g2, b2, wrt, br, cnt)


def _row_copy(src, dst, sem, src_row, dst_row):
    return pltpu.make_async_copy(src.at[pl.ds(src_row, 1)], dst.at[pl.ds(dst_row, 1)], sem)


def _dispatch_kernel(dest_ref, x_ref, dest2_ref, x2_ref, xb_ref, zeros_s, sem, zsem, *, n_tok_rows):
    i = pl.program_id(0)
    last = pl.num_programs(0) - 1

    def scatter(src_ref, idx_ref):
        nt = src_ref.shape[0]

        def issue(t, c):
            for k in range(TOP_K):
                _row_copy(src_ref, xb_ref, sem, t, idx_ref[0, k * nt + t]).start()
            return c

        lax.fori_loop(0, nt, issue, 0)
        for k in range(TOP_K):
            pltpu.make_async_copy(src_ref, xb_ref.at[pl.ds(0, nt)], sem).wait()

    @pl.when(i < last)
    def _():
        scatter(x_ref, dest_ref)

    @pl.when(i == last)
    def _():
        zeros_s[...] = jnp.zeros_like(zeros_s)
        tail = pltpu.make_async_copy(zeros_s, xb_ref.at[pl.ds(n_tok_rows, zeros_s.shape[0])], zsem)
        tail.start()
        scatter(x2_ref, dest2_ref)
        tail.wait()


def _dispatch(x_a, dest_a, x_b, dest_b, n_rows):
    Ta, D = x_a.shape
    Tb = x_b.shape[0]
    nt = dest_a.shape[2] // TOP_K
    steps = Ta // nt
    n_tok_rows = (Ta + Tb) * TOP_K
    tile = lambda i: (jnp.minimum(i, steps - 1), 0, 0)
    return pl.pallas_call(
        functools.partial(_dispatch_kernel, n_tok_rows=n_tok_rows),
        grid=(steps + 1,),
        in_specs=[pl.BlockSpec((None, 1, TOP_K * nt), tile, memory_space=pltpu.SMEM),
                  pl.BlockSpec((nt, D), lambda i: (jnp.minimum(i, steps - 1), 0)),
                  pl.BlockSpec((None, 1, TOP_K * Tb), lambda i: (0, 0, 0), memory_space=pltpu.SMEM),
                  pl.BlockSpec((Tb, D), lambda i: (0, 0))],
        out_specs=pl.BlockSpec(memory_space=pl.ANY),
        out_shape=jax.ShapeDtypeStruct((n_rows, D), x_a.dtype),
        scratch_shapes=[pltpu.VMEM((n_rows - n_tok_rows, D), x_a.dtype),
                        pltpu.SemaphoreType.DMA, pltpu.SemaphoreType.DMA],
        compiler_params=pltpu.CompilerParams(dimension_semantics=("arbitrary",)),
        name="moe_dispatch",
    )(dest_a, x_a, dest_b, x_b)


def _moe_kernel(blk_ref, exp_ref, lo_ref, hi_ref, xb_ref, wgu_ref, bgu_ref, wd_ref, bd_ref, yb_ref, wgu_s, wd_s):
    s = pl.program_id(0)
    before = jnp.maximum(s - 1, 0)
    new_expert = jnp.logical_or(s == 0, exp_ref[s] != exp_ref[before])
    new_block = jnp.logical_or(s == 0, blk_ref[s] != blk_ref[before])
    lo, hi = lo_ref[s], hi_ref[s]
    used = hi > lo

    @pl.when(jnp.logical_and(used, new_expert))
    def _():
        wgu_s[...] = wgu_ref[...].astype(BF16)
        wd_s[...] = wd_ref[...].astype(BF16)

    @pl.when(new_block)
    def _():
        yb_ref[...] = jnp.zeros_like(yb_ref)

    @pl.when(used)
    def _():
        gu = _dot(xb_ref[...].astype(BF16), wgu_s[...]) + bgu_ref[...]
        g = jnp.minimum(gu[:, :D_FF], SWIGLU_LIMIT)
        u = jnp.clip(gu[:, D_FF:], -SWIGLU_LIMIT, SWIGLU_LIMIT)
        act = (u + 1.0) * g * jax.nn.sigmoid(SWIGLU_ALPHA * g)
        y = _dot(act.astype(BF16), wd_s[...]) + bd_ref[...]
        row = lax.broadcasted_iota(I32, (xb_ref.shape[0], 1), 0)
        yb_ref[...] = jnp.where(jnp.logical_and(row >= lo, row < hi), y, yb_ref[...])


def _moe(seg_blk, seg_exp, seg_lo, seg_hi, xb, w_gate_up, b_gate_up, w_down, b_down):
    P, D = xb.shape
    bm = ROW_BLOCK
    E, _, F2 = w_gate_up.shape
    ex = lambda s, blk, exp, lo, hi: (exp[s], 0, 0)
    rows = lambda s, blk, exp, lo, hi: (blk[s], 0)
    expert_bytes = (D * F2 + D_FF * D) * (2 * 4 + 2)
    block_bytes = bm * D * 4 * 2 * 2 + bm * F2 * 4 * 3
    return pl.pallas_call(
        _moe_kernel,
        grid_spec=pltpu.PrefetchScalarGridSpec(
            num_scalar_prefetch=4,
            grid=(seg_blk.shape[0],),
            in_specs=[pl.BlockSpec((bm, D), rows),
                      pl.BlockSpec((None, D, F2), ex), pl.BlockSpec((None, 1, F2), ex),
                      pl.BlockSpec((None, D_FF, D), ex), pl.BlockSpec((None, 1, D), ex)],
            out_specs=pl.BlockSpec((bm, D), rows),
            scratch_shapes=[pltpu.VMEM((D, F2), BF16), pltpu.VMEM((D_FF, D), BF16)]),
        out_shape=jax.ShapeDtypeStruct((P, D), F32),
        compiler_params=pltpu.CompilerParams(dimension_semantics=("arbitrary",),
                                             vmem_limit_bytes=expert_bytes + block_bytes),
        name="moe_grouped_ffn",
    )(seg_blk, seg_exp, seg_lo, seg_hi, xb, w_gate_up, b_gate_up.reshape(E, 1, F2), w_down, b_down.reshape(E, 1, D))


def _combine_kernel(dest_ref, yb_ref, x2_ref, gate_ref, g3_ref, b3_ref, o_ref, rows_s, sem, *, alpha):
    nt = x2_ref.shape[0]

    def issue(t, c):
        for k in range(TOP_K):
            _row_copy(yb_ref, rows_s.at[k], sem, dest_ref[0, k * nt + t], t).start()
        return c

    lax.fori_loop(0, nt, issue, 0)
    for k in range(TOP_K):
        pltpu.make_async_copy(yb_ref.at[pl.ds(0, nt)], rows_s.at[k], sem).wait()
    gates = gate_ref[...]
    y = gates[:, 0:1] * rows_s[0]
    for k in range(1, TOP_K):
        y = y + gates[:, k:k + 1] * rows_s[k]
    o_ref[...] = _layer_norm(alpha * x2_ref[...] + y, g3_ref[...], b3_ref[...])


def _combine(yb, dest_tiles, x2, gates_tok, g3, b3, alpha):
    T, D = x2.shape
    nt = dest_tiles.shape[2] // TOP_K
    return pl.pallas_call(
        functools.partial(_combine_kernel, alpha=alpha),
        grid=(T // nt,),
        in_specs=[pl.BlockSpec((None, 1, TOP_K * nt), lambda i: (i, 0, 0), memory_space=pltpu.SMEM),
                  pl.BlockSpec(memory_space=pl.ANY),
                  pl.BlockSpec((nt, D), lambda i: (i, 0)),
                  pl.BlockSpec((nt, TOP_K), lambda i: (i, 0)),
                  pl.BlockSpec((1, D), lambda i: (0, 0)), pl.BlockSpec((1, D), lambda i: (0, 0))],
        out_specs=pl.BlockSpec((nt, D), lambda i: (i, 0)),
        out_shape=jax.ShapeDtypeStruct((T, D), F32),
        scratch_shapes=[pltpu.VMEM((TOP_K, nt, D), F32), pltpu.SemaphoreType.DMA],
        compiler_params=pltpu.CompilerParams(dimension_semantics=("arbitrary",)),
        name="moe_combine",
    )(dest_tiles, yb, x2, gates_tok, g3, b3)


def _rel_bucket(dist):
    max_exact = N_BUCKETS // 2
    df = jnp.maximum(dist, 1).astype(F32)
    large = max_exact + (jnp.log(df / max_exact) / math.log(MAX_DISTANCE / max_exact)
                         * (N_BUCKETS - max_exact)).astype(I32)
    return jnp.where(dist < max_exact, dist, jnp.minimum(large, N_BUCKETS - 1))


def _bias_tables(rel_table):
    n = BAND
    taps = jnp.arange(n + 1)
    by_tap = jnp.stack([rel_table[_rel_bucket(taps * dil)].astype(F32) for _, dil in PATTERNS])
    w = jnp.concatenate([by_tap[:, ::-1], jnp.full((len(PATTERNS), n, N_HEADS), NEG_INF, F32)], axis=1)
    w = jnp.transpose(w, (0, 2, 1))
    band = jnp.tile(w, (1, 1, n))[:, :, :n * 2 * n].reshape(len(PATTERNS), N_HEADS, n, 2 * n)
    return by_tap, band.reshape(len(PATTERNS), N_SLABS, HEADS_PER_SLAB * n, 2 * n)


def _tap_bias_by_slot(by_tap, W):
    assert PAST_LEN % W == 0 and PAST_LEN >= WINDOW_MAX
    n = BAND
    out = []
    for p, (window, dil) in enumerate(PATTERNS):
        taps = by_tap[p, :0:-1]
        col = jnp.concatenate([jnp.full((W // dil - n, N_HEADS), NEG_INF, F32), taps], axis=0)
        rest = jnp.full((W // dil, dil - 1, N_HEADS), NEG_INF, F32)
        out.append(jnp.concatenate([col[:, None, :], rest], axis=1).reshape(W, N_HEADS).T)
    return jnp.stack(out)


def _dest_tiles(dest, nt):
    T = dest.shape[1]
    return dest.reshape(TOP_K, T // nt, nt).transpose(1, 0, 2).reshape(T // nt, 1, TOP_K * nt)


def kernel(x_prompt, x_sample, mem_prompt, cache_win_k, cache_win_v, state_conv, cache_mem_k, cache_mem_v,
           rel_bias_table, w_in, w_conv, w_out, ln1_g, ln1_b, w_mem_q, w_mem_k, w_mem_v, w_mem_o,
           ln2_g, ln2_b, w_router, b_router, w_gate_up, b_gate_up, w_down, b_down, ln3_g, ln3_b):
    depth = w_in.shape[0]
    assert depth == 1
    alpha = (2 * depth) ** 0.25
    B, S, D = x_prompt.shape
    DB = x_sample.shape[0]
    T = B * S
    l = 0
    vec = lambda a: a[l].reshape(1, -1)

    by_tap, bias_band = _bias_tables(rel_bias_table)
    wrt = w_router[l].T.astype(BF16)
    br = b_router[l].reshape(N_EXPERTS, 1)
    w_in_bf = w_in[l].astype(BF16)
    wo_bf = w_out[l].astype(BF16)
    wq_bf = w_mem_q[l].astype(BF16)
    wmo_bf = w_mem_o[l].astype(BF16)

    conv_p, q_p, k_p, v_p, cs_p = _inproj_prompt(x_prompt, w_in_bf, w_conv[l])
    attn_p = _attention_prompt(q_p, k_p, v_p, bias_band)
    mk, mv, mk_bf, mv_bf = _memkv_prompt(mem_prompt, w_mem_k[l].astype(BF16), w_mem_v[l].astype(BF16))
    x2_p, idx_p, gate_p, rank_p, cnt_p = _mid_prompt(
        x_prompt, conv_p, attn_p, wo_bf[:CONV_WIDTH], wo_bf[CONV_WIDTH:], vec(ln1_g), vec(ln1_b), mk_bf, mv_bf,
        wq_bf, wmo_bf, vec(ln2_g), vec(ln2_b), wrt, br, alpha)

    xs = x_sample.reshape(DB, D)
    proj = _inproj_sample(xs, w_in_bf)
    cw = CONV_WIDTH
    q_s = proj[:, 3 * cw:3 * cw + ATTN_WIDTH]
    k_s = proj[:, 3 * cw + ATTN_WIDTH:3 * cw + 2 * ATTN_WIDTH]
    v_s = proj[:, 3 * cw + 2 * ATTN_WIDTH:]
    H, E = N_HEADS, HEAD_DIM
    bias_new = by_tap[:, 0, :, None]
    slots_last = lambda c: jnp.transpose(c, (0, 2, 3, 1))
    attn_s = _attention_sample(q_s.reshape(DB, H, E, 1), k_s.reshape(DB, H, E, 1), v_s.reshape(DB, H, E, 1),
                               slots_last(cache_win_k[l]), slots_last(cache_win_v[l]),
                               _tap_bias_by_slot(by_tap, WINDOW_MAX), bias_new).reshape(DB, ATTN_WIDTH)
    sc = state_conv[l]
    x1_s, qm_s, u_s = _tail_a_sample(xs, proj, sc[:, 0], sc[:, 1], w_conv[l], attn_s, wo_bf,
                                     vec(ln1_g), vec(ln1_b), wq_bf, alpha)
    o_s = _memattn_sample(qm_s.reshape(DB, MEM_HEADS, MEM_HEAD_DIM), cache_mem_k[l], cache_mem_v[l]).reshape(DB, D)
    x2_s, idx_s, gate_s, rank_s, cnt = _tail_b_sample(x1_s, o_s, wmo_bf, vec(ln2_g), vec(ln2_b), wrt, br,
                                                      cnt_p, alpha)

    bm = ROW_BLOCK
    counts = cnt[:, 0].astype(I32)
    ends = jnp.cumsum(counts)
    starts = ends - counts
    n_tok_rows = (T + DB) * TOP_K
    n_blocks = -(-n_tok_rows // bm)
    P = n_blocks * bm
    cuts = jnp.sort(jnp.concatenate([jnp.arange(n_blocks, dtype=I32) * bm, ends[:-1]]))
    cut_ends = jnp.concatenate([cuts[1:], jnp.full((1,), P, I32)])
    seg_blk = jnp.minimum(cuts // bm, n_blocks - 1)
    seg_exp = jnp.minimum(jnp.sum((ends[None, :] <= cuts[:, None]).astype(I32), axis=1), N_EXPERTS - 1)
    seg_lo = cuts - seg_blk * bm
    seg_hi = cut_ends - seg_blk * bm
    expert_ids = jnp.arange(N_EXPERTS, dtype=I32)[:, None, None]
    start_of = lambda idx: jnp.sum(jnp.where(idx[None] == expert_ids, starts[:, None, None], 0), axis=0)
    dest_p = start_of(idx_p) + rank_p
    dest_s = start_of(idx_s) + rank_s
    dtiles_p = _dest_tiles(dest_p, COMBINE_TILE)
    dtiles_s = _dest_tiles(dest_s, DB)

    x2_pf = x2_p.reshape(T, D)
    xb = _dispatch(x2_pf, dtiles_p, x2_s, dtiles_s, P)
    yb = _moe(seg_blk, seg_exp, seg_lo, seg_hi, xb, w_gate_up[l], b_gate_up[l], w_down[l], b_down[l])
    y_p = _combine(yb, dtiles_p, x2_pf, gate_p.T, vec(ln3_g), vec(ln3_b), alpha).reshape(B, S, D)
    y_s = _combine(yb, dtiles_s, x2_s, gate_s.T, vec(ln3_g), vec(ln3_b), alpha).reshape(DB, 1, D)

    conv_state_s = jnp.stack([sc[:, 1], u_s], axis=1)
    return (y_p, y_s,
            k_p.reshape(1, B, S, H, E), v_p.reshape(1, B, S, H, E), cs_p[:, 6:8].reshape(1, B, CONV_K - 1, CONV_WIDTH),
            mk.reshape(1, B, MEM_TOKENS, MEM_HEADS, MEM_HEAD_DIM), mv.reshape(1, B, MEM_TOKENS, MEM_HEADS, MEM_HEAD_DIM),
            k_s.reshape(1, DB, 1, H, E), v_s.reshape(1, DB, 1, H, E), conv_state_s.reshape(1, DB, CONV_K - 1, CONV_WIDTH))
```

```python
import functools
import math

import jax
import jax.numpy as jnp
import numpy as np
from jax import lax
from jax.experimental import pallas as pl
from jax.experimental.pallas import tpu as pltpu

F32 = jnp.float32
BF16 = jnp.bfloat16
I32 = jnp.int32

D_MODEL = 1024
CONV_WIDTH = 256
CONV_K = 3
HEAD_DIM = 64
N_HEADS = 12
ATTN_WIDTH = N_HEADS * HEAD_DIM
PATTERNS = ((128, 1), (512, 4), (2048, 16))
BAND = 128
WINDOW_MAX = 2048
ATTN_SCALE = 1.0 / math.sqrt(HEAD_DIM)
N_BUCKETS = 32
MAX_DISTANCE = WINDOW_MAX
MEM_TOKENS = 256
MEM_HEADS = 4
MEM_HEAD_DIM = 256
MEM_SCALE = 1.0 / math.sqrt(MEM_HEAD_DIM)
N_EXPERTS = 32
TOP_K = 4
D_FF = 1024
SWIGLU_LIMIT = 7.0
SWIGLU_ALPHA = 1.702
LN_EPS = 1e-5
PAST_LEN = 8192
NEG_INF = -1e30

LANES = 128
HEADS_PER_SLAB = LANES // HEAD_DIM
N_SLABS = ATTN_WIDTH // LANES

PROJ_TILE = 512
SEQ_TILE = 512
ROW_BLOCK = 512
COMBINE_TILE = 128
ATTN_GROUP = 8

HIGHEST = lax.Precision.HIGHEST


def _layer_norm(x, g, b):
    mu = jnp.mean(x, axis=-1, keepdims=True)
    var = jnp.mean(jnp.square(x - mu), axis=-1, keepdims=True)
    return (x - mu) * lax.rsqrt(var + LN_EPS) * g + b


def _dot(a, b):
    return jnp.dot(a, b, preferred_element_type=F32)


def _dot_nt(a, b):
    return lax.dot_general(a, b, (((1,), (1,)), ((), ())), preferred_element_type=F32)


def _dot_hi(a, b):
    return jnp.dot(a, b, preferred_element_type=F32, precision=HIGHEST)


def _inproj_kernel(x_ref, w_ref, wc_ref, conv_ref, q_ref, k_ref, v_ref, cs_ref, u_s):
    ts = x_ref.shape[0]
    cw = CONV_WIDTH

    @pl.when(pl.program_id(1) == 0)
    def _():
        u_s[0:8, :] = jnp.zeros((8, cw), F32)

    x = x_ref[...].astype(BF16)
    gb = _dot(x, w_ref[:, 0:cw])
    gc = _dot(x, w_ref[:, cw:2 * cw])
    h = _dot(x, w_ref[:, 2 * cw:3 * cw])
    u = gc * h
    u_s[8:8 + ts, :] = u
    wc = wc_ref[...]
    conv = wc[0:1, :] * u_s[6:6 + ts, :] + wc[1:2, :] * u_s[7:7 + ts, :] + wc[2:3, :] * u
    conv_ref[...] = (gb * conv).astype(conv_ref.dtype)
    o = 3 * cw
    q_ref[...] = _dot(x, w_ref[:, o:o + ATTN_WIDTH]) * ATTN_SCALE
    k_ref[...] = _dot(x, w_ref[:, o + ATTN_WIDTH:o + 2 * ATTN_WIDTH])
    v_ref[...] = _dot(x, w_ref[:, o + 2 * ATTN_WIDTH:o + 3 * ATTN_WIDTH])
    tail = u_s[ts:ts + 8, :]
    u_s[0:8, :] = tail
    cs_ref[...] = tail


def _inproj_prompt(x, w_in_bf, w_conv):
    B, S, D = x.shape
    ts = PROJ_TILE
    row = lambda b, s: (b, s, 0)
    return pl.pallas_call(
        _inproj_kernel,
        grid=(B, S // ts),
        in_specs=[pl.BlockSpec((None, ts, D), row),
                  pl.BlockSpec(w_in_bf.shape, lambda b, s: (0, 0)),
                  pl.BlockSpec(w_conv.shape, lambda b, s: (0, 0))],
        out_specs=[pl.BlockSpec((None, ts, CONV_WIDTH), row),
                   pl.BlockSpec((None, ts, ATTN_WIDTH), row),
                   pl.BlockSpec((None, ts, ATTN_WIDTH), row),
                   pl.BlockSpec((None, ts, ATTN_WIDTH), row),
                   pl.BlockSpec((None, 8, CONV_WIDTH), lambda b, s: (b, 0, 0))],
        out_shape=[jax.ShapeDtypeStruct((B, S, CONV_WIDTH), BF16),
                   jax.ShapeDtypeStruct((B, S, ATTN_WIDTH), F32),
                   jax.ShapeDtypeStruct((B, S, ATTN_WIDTH), F32),
                   jax.ShapeDtypeStruct((B, S, ATTN_WIDTH), F32),
                   jax.ShapeDtypeStruct((B, 8, CONV_WIDTH), F32)],
        scratch_shapes=[pltpu.VMEM((ts + 8, CONV_WIDTH), F32)],
        compiler_params=pltpu.CompilerParams(dimension_semantics=("arbitrary", "arbitrary")),
        name="inproj_prompt",
    )(x, w_in_bf, w_conv)


def _attn_kernel(q_ref, k_ref, v_ref, bias_ref, o_ref, m_s, l_s, acc_s):
    S = q_ref.shape[0]
    n = BAND
    lane = lax.broadcasted_iota(I32, (n, LANES), 1)
    head_a = lane < HEAD_DIM

    def rows_of(start, count, dil):
        return pl.ds(start, count) if dil == 1 else pl.ds(start, count, stride=dil)

    def blocks(p, dil, specs):
        rows = [rows_of(start, n, dil) for start, _ in specs]
        scores, values = [], []
        for (start, has_prev), r in zip(specs, rows):
            qb = q_ref[r, :]
            q2 = jnp.concatenate([jnp.where(head_a, qb, 0.0), jnp.where(head_a, 0.0, qb)], axis=0).astype(BF16)
            krows = rows_of(start - n * dil, 2 * n, dil) if has_prev else r
            bias = bias_ref[p] if has_prev else bias_ref[p, :, n:2 * n]
            scores.append(_dot_nt(q2, k_ref[krows, :].astype(BF16)) + bias)
            values.append(v_ref[krows, :].astype(BF16))
        ms = [jnp.max(s, axis=-1, keepdims=True) for s in scores]
        es = [jnp.exp(s - m) for s, m in zip(scores, ms)]
        ls = [jnp.sum(e, axis=-1, keepdims=True) for e in es]
        pvs = [_dot(e.astype(BF16), vb) for e, vb in zip(es, values)]
        for r, m, l, pv in zip(rows, ms, ls, pvs):
            m_s[p, r, :] = jnp.where(head_a, m[:n], m[n:])
            l_s[p, r, :] = jnp.where(head_a, l[:n], l[n:])
            acc_s[p, r, :] = jnp.where(head_a, pv[:n], pv[n:])

    for p, (window, dil) in enumerate(PATTERNS):
        nb = (S // dil) // n
        specs = [(i * n * dil + r, i > 0) for r in range(dil) for i in range(nb)]
        for g in range(0, len(specs), ATTN_GROUP):
            blocks(p, dil, specs[g:g + ATTN_GROUP])

    rows_per_step = 256

    def merge(t, c):
        rows = pl.ds(t * rows_per_step, rows_per_step)
        ms = [m_s[p, rows, :] for p in range(len(PATTERNS))]
        m_all = jnp.maximum(jnp.maximum(ms[0], ms[1]), ms[2])
        num = den = None
        for p in range(len(PATTERNS)):
            w = jnp.exp(ms[p] - m_all)
            num = w * acc_s[p, rows, :] if num is None else num + w * acc_s[p, rows, :]
            den = w * l_s[p, rows, :] if den is None else den + w * l_s[p, rows, :]
        o_ref[rows, :] = (num / den).astype(o_ref.dtype)
        return c

    lax.fori_loop(0, S // rows_per_step, merge, 0)


def _attention_prompt(q, k, v, bias_tab):
    B, S, _ = q.shape
    slab = lambda p, b: (b, 0, p)
    spec = pl.BlockSpec((None, S, LANES), slab)
    return pl.pallas_call(
        _attn_kernel,
        grid=(N_SLABS, B),
        in_specs=[spec, spec, spec,
                  pl.BlockSpec((len(PATTERNS), None, HEADS_PER_SLAB * BAND, 2 * BAND), lambda p, b: (0, p, 0, 0))],
        out_specs=spec,
        out_shape=jax.ShapeDtypeStruct((B, S, ATTN_WIDTH), BF16),
        scratch_shapes=[pltpu.VMEM((len(PATTERNS), S, LANES), F32)] * 3,
        compiler_params=pltpu.CompilerParams(dimension_semantics=("arbitrary", "arbitrary")),
        name="dilated_attn_prompt",
    )(q, k, v, bias_tab)


def _memkv_kernel(m_ref, wk_ref, wv_ref, k_ref, v_ref, kb_ref, vb_ref):
    x = m_ref[...].astype(BF16)
    k = _dot(x, wk_ref[...])
    v = _dot(x, wv_ref[...])
    k_ref[...] = k
    v_ref[...] = v
    kb_ref[...] = k.astype(BF16)
    vb_ref[...] = v.astype(BF16)


def _memkv_prompt(mem, wk_bf, wv_bf):
    B, M, D = mem.shape
    row = lambda b: (b, 0, 0)
    full = lambda b: (0, 0)
    return pl.pallas_call(
        _memkv_kernel,
        grid=(B,),
        in_specs=[pl.BlockSpec((None, M, D), row), pl.BlockSpec((D, D), full), pl.BlockSpec((D, D), full)],
        out_specs=[pl.BlockSpec((None, M, D), row)] * 4,
        out_shape=[jax.ShapeDtypeStruct((B, M, D), F32)] * 2 + [jax.ShapeDtypeStruct((B, M, D), BF16)] * 2,
        compiler_params=pltpu.CompilerParams(dimension_semantics=("arbitrary",)),
        name="memkv_prompt",
    )(mem, wk_bf, wv_bf)


def _route(logits_t, carry):
    E, nt = logits_t.shape
    eidx = lax.broadcasted_iota(I32, (E, nt), 0)
    l = logits_t
    vals, idxs = [], []
    for _ in range(TOP_K):
        m = jnp.max(l, axis=0, keepdims=True)
        sel = jnp.min(jnp.where(l == m, eidx, E), axis=0, keepdims=True)
        vals.append(m)
        idxs.append(sel)
        l = jnp.where(eidx == sel, -jnp.inf, l)
    es = [jnp.exp(v - vals[0]) for v in vals]
    den = es[0] + es[1] + es[2] + es[3]
    gates = _stack_rows([e / den for e in es])
    chosen = (l == -jnp.inf)
    onehot = jnp.where(chosen, 1.0, 0.0)
    before = lax.broadcasted_iota(I32, (nt, nt), 0) < lax.broadcasted_iota(I32, (nt, nt), 1)
    prefix = _dot(onehot.astype(BF16), jnp.where(before, 1.0, 0.0).astype(BF16)) + carry
    ranks = [jnp.sum(jnp.where(eidx == s, prefix, 0.0), axis=0, keepdims=True) for s in idxs]
    rank = _stack_rows(ranks).astype(I32)
    idx = _stack_rows(idxs)
    return idx, gates, rank, carry + jnp.sum(onehot, axis=1, keepdims=True)


def _stack_rows(rows):
    k, nt = len(rows), rows[0].shape[1]
    r = lax.broadcasted_iota(I32, (k, nt), 0)
    out = jnp.broadcast_to(rows[-1], (k, nt))
    for i in range(k - 2, -1, -1):
        out = jnp.where(r == i, rows[i], out)
    return out


def _mid_kernel(x_ref, conv_ref, attn_ref, woc_ref, woa_ref, g1_ref, b1_ref, mk_ref, mv_ref, wq_ref, wmo_ref,
                g2_ref, b2_ref, wrt_ref, br_ref,
                x2_ref, idx_ref, gate_ref, rank_ref, cnt_ref, carry_s, *, alpha):
    first = jnp.logical_and(pl.program_id(0) == 0, pl.program_id(1) == 0)

    @pl.when(first)
    def _():
        carry_s[...] = jnp.zeros_like(carry_s)

    x = x_ref[...]
    mix = _dot(conv_ref[...], woc_ref[...]) + _dot(attn_ref[...], woa_ref[...])
    x1 = _layer_norm(alpha * x + mix, g1_ref[...], b1_ref[...])
    qm = (_dot(x1.astype(BF16), wq_ref[...]) * MEM_SCALE).astype(BF16)
    outs = []
    for h in range(MEM_HEADS):
        hs = slice(h * MEM_HEAD_DIM, (h + 1) * MEM_HEAD_DIM)
        s = _dot_nt(qm[:, hs], mk_ref[:, hs])
        m = jnp.max(s, axis=-1, keepdims=True)
        e = jnp.exp(s - m)
        l = jnp.sum(e, axis=-1, keepdims=True)
        outs.append((_dot(e.astype(BF16), mv_ref[:, hs]) / l).astype(BF16))
    o = jnp.concatenate(outs, axis=-1)
    x2 = _layer_norm(alpha * x1 + _dot(o, wmo_ref[...]), g2_ref[...], b2_ref[...])
    x2_ref[...] = x2
    logits_t = _dot_nt(wrt_ref[...], x2.astype(BF16)) + br_ref[...]
    idx, gates, rank, carry = _route(logits_t, carry_s[:, 0:1])
    idx_ref[...] = idx
    gate_ref[...] = gates
    rank_ref[...] = rank
    carry_s[...] = jnp.broadcast_to(carry, carry_s.shape)
    cnt_ref[...] = carry_s[...]


def _mid_prompt(x, conv, attn, woc, woa, g1, b1, mk, mv, wq, wmo, g2, b2, wrt, br, alpha):
    B, S, D = x.shape
    ts = SEQ_TILE
    T = B * S
    row = lambda b, s: (b, s, 0)
    full = lambda b, s: (0, 0)
    tok = lambda b, s: (0, b * (S // ts) + s)
    mem = lambda b, s: (b, 0, 0)
    vec = pl.BlockSpec((1, D), full)
    return pl.pallas_call(
        functools.partial(_mid_kernel, alpha=alpha),
        grid=(B, S // ts),
        in_specs=[pl.BlockSpec((None, ts, D), row),
                  pl.BlockSpec((None, ts, CONV_WIDTH), row),
                  pl.BlockSpec((None, ts, ATTN_WIDTH), row),
                  pl.BlockSpec(woc.shape, full), pl.BlockSpec(woa.shape, full), vec, vec,
                  pl.BlockSpec((None, MEM_TOKENS, D), mem), pl.BlockSpec((None, MEM_TOKENS, D), mem),
                  pl.BlockSpec(wq.shape, full), pl.BlockSpec(wmo.shape, full), vec, vec,
                  pl.BlockSpec(wrt.shape, full), pl.BlockSpec(br.shape, full)],
        out_specs=[pl.BlockSpec((None, ts, D), row),
                   pl.BlockSpec((TOP_K, ts), tok), pl.BlockSpec((TOP_K, ts), tok), pl.BlockSpec((TOP_K, ts), tok),
                   pl.BlockSpec((N_EXPERTS, LANES), full)],
        out_shape=[jax.ShapeDtypeStruct((B, S, D), F32),
                   jax.ShapeDtypeStruct((TOP_K, T), I32),
                   jax.ShapeDtypeStruct((TOP_K, T), F32),
                   jax.ShapeDtypeStruct((TOP_K, T), I32),
                   jax.ShapeDtypeStruct((N_EXPERTS, LANES), F32)],
        scratch_shapes=[pltpu.VMEM((N_EXPERTS, LANES), F32)],
        compiler_params=pltpu.CompilerParams(dimension_semantics=("arbitrary", "arbitrary")),
        name="mid_prompt",
    )(x, conv, attn, woc, woa, g1, b1, mk, mv, wq, wmo, g2, b2, wrt, br)


def _round_bf16(x):
    return x.astype(BF16).astype(F32)


def _inproj_sample_kernel(x_ref, w_ref, o_ref):
    o_ref[...] = _dot(x_ref[...].astype(BF16), w_ref[...])


def _inproj_sample(x, w_in):
    n, D = x.shape
    N = w_in.shape[1]
    bn = 512
    return pl.pallas_call(
        _inproj_sample_kernel,
        grid=(N // bn,),
        in_specs=[pl.BlockSpec((n, D), lambda j: (0, 0)), pl.BlockSpec((D, bn), lambda j: (0, j))],
        out_specs=pl.BlockSpec((n, bn), lambda j: (0, j)),
        out_shape=jax.ShapeDtypeStruct((n, N), F32),
        compiler_params=pltpu.CompilerParams(dimension_semantics=("arbitrary",)),
        name="inproj_sample",
    )(x, w_in)


def _attn_sample_kernel(q_ref, kn_ref, vn_ref, kt_ref, vt_ref, bias_ref, bnew_ref, o_ref):
    H, E, W = kt_ref.shape
    q = _round_bf16(q_ref[...])
    v_new = _round_bf16(vn_ref[...])
    s_all = jnp.sum(_round_bf16(kt_ref[...]) * q, axis=1) * ATTN_SCALE
    s_new = jnp.sum(_round_bf16(kn_ref[...]) * q, axis=1) * ATTN_SCALE
    outs, lses = [], []
    for p, (window, dil) in enumerate(PATTERNS):
        lo = W - window
        s = s_all[:, lo:] + bias_ref[p, :, lo:]
        sn = s_new + bnew_ref[p]
        m = jnp.maximum(jnp.max(s, axis=-1, keepdims=True), sn)
        e = jnp.exp(s - m)
        en = jnp.exp(sn - m)
        den = jnp.sum(e, axis=-1, keepdims=True) + en
        pr = _round_bf16(e / den)
        pv = jnp.sum(_round_bf16(vt_ref[:, :, lo:]) * pr[:, None, :], axis=-1, keepdims=True)
        outs.append(pv + _round_bf16(en / den)[:, :, None] * v_new)
        lses.append(m + jnp.log(den))
    lmax = jnp.maximum(jnp.maximum(lses[0], lses[1]), lses[2])
    ws = [jnp.exp(ls - lmax) for ls in lses]
    wsum = ws[0] + ws[1] + ws[2]
    acc = None
    for p in range(len(PATTERNS)):
        term = _round_bf16(ws[p] / wsum)[:, :, None] * _round_bf16(outs[p])
        acc = term if acc is None else acc + term
    o_ref[...] = acc


def _attention_sample(q, k_new, v_new, win_kt, win_vt, bias_pos, bias_new):
    DB, H, E, W = win_kt.shape
    vec = pl.BlockSpec((None, H, E, 1), lambda b: (b, 0, 0, 0))
    cache = pl.BlockSpec((None, H, E, W), lambda b: (b, 0, 0, 0))
    const = lambda a: pl.BlockSpec(a.shape, lambda b: (0,) * a.ndim)
    return pl.pallas_call(
        _attn_sample_kernel,
        grid=(DB,),
        in_specs=[vec, vec, vec, cache, cache, const(bias_pos), const(bias_new)],
        out_specs=vec,
        out_shape=jax.ShapeDtypeStruct((DB, H, E, 1), F32),
        compiler_params=pltpu.CompilerParams(dimension_semantics=("arbitrary",)),
        name="dilated_attn_sample",
    )(q, k_new, v_new, win_kt, win_vt, bias_pos, bias_new)


def _tail_a_sample_kernel(x_ref, proj_ref, s0_ref, s1_ref, wc_ref, attn_ref, wo_ref, g1_ref, b1_ref, wq_ref,
                          x1_ref, qm_ref, u_ref, *, alpha):
    cw = CONV_WIDTH
    gb = proj_ref[:, 0:cw]
    u = proj_ref[:, cw:2 * cw] * proj_ref[:, 2 * cw:3 * cw]
    u_ref[...] = u
    wc = wc_ref[...]
    conv = gb * (wc[0:1, :] * s0_ref[...] + wc[1:2, :] * s1_ref[...] + wc[2:3, :] * u)
    mixed = jnp.concatenate([conv, attn_ref[...]], axis=-1).astype(BF16)
    x1 = _layer_norm(alpha * x_ref[...] + _dot(mixed, wo_ref[...]), g1_ref[...], b1_ref[...])
    x1_ref[...] = x1
    qm_ref[...] = _dot(x1.astype(BF16), wq_ref[...])


def _tail_a_sample(x, proj, s0, s1, w_conv, attn, w_out, g1, b1, w_mem_q, alpha):
    n, D = x.shape
    return pl.pallas_call(
        functools.partial(_tail_a_sample_kernel, alpha=alpha),
        out_shape=[jax.ShapeDtypeStruct((n, D), F32)] * 2 + [jax.ShapeDtypeStruct((n, CONV_WIDTH), F32)],
        name="tail_a_sample",
    )(x, proj, s0, s1, w_conv, attn, w_out, g1, b1, w_mem_q)


def _memattn_sample_kernel(q_ref, k_ref, v_ref, o_ref):
    q = _round_bf16(q_ref[...])
    s = jnp.sum(_round_bf16(k_ref[...]) * q[None], axis=-1, keepdims=True) * MEM_SCALE
    m = jnp.max(s, axis=0)
    e = jnp.exp(s - m[None])
    den = jnp.sum(e, axis=0)
    pr = _round_bf16(e / den[None])
    o_ref[...] = jnp.sum(pr * _round_bf16(v_ref[...]), axis=0)


def _memattn_sample(qm, mem_k, mem_v):
    DB, M, H, E = mem_k.shape
    vec = pl.BlockSpec((None, H, E), lambda b: (b, 0, 0))
    mem = pl.BlockSpec((None, M, H, E), lambda b: (b, 0, 0, 0))
    return pl.pallas_call(
        _memattn_sample_kernel,
        grid=(DB,),
        in_specs=[vec, mem, mem],
        out_specs=vec,
        out_shape=jax.ShapeDtypeStruct((DB, H, E), F32),
        compiler_params=pltpu.CompilerParams(dimension_semantics=("arbitrary",)),
        name="memattn_sample",
    )(qm, mem_k, mem_v)


def _tail_b_sample_kernel(x1_ref, o_ref, wmo_ref, g2_ref, b2_ref, wrt_ref, br_ref, cnt_ref,
                          x2_ref, idx_ref, gate_ref, rank_ref, cnt_out_ref, *, alpha):
    x2 = _layer_norm(alpha * x1_ref[...] + _dot(o_ref[...].astype(BF16), wmo_ref[...]), g2_ref[...], b2_ref[...])
    x2_ref[...] = x2
    logits_t = _dot_nt(wrt_ref[...], x2.astype(BF16)) + br_ref[...]
    idx, gates, rank, carry = _route(logits_t, cnt_ref[:, 0:1])
    idx_ref[...] = idx
    gate_ref[...] = gates
    rank_ref[...] = rank
    cnt_out_ref[...] = jnp.broadcast_to(carry, cnt_out_ref.shape)


def _tail_b_sample(x1, o, w_mem_o, g2, b2, wrt, br, cnt, alpha):
    n, D = x1.shape
    return pl.pallas_call(
        functools.partial(_tail_b_sample_kernel, alpha=alpha),
        out_shape=[jax.ShapeDtypeStruct((n, D), F32),
                   jax.ShapeDtypeStruct((TOP_K, n), I32),
                   jax.ShapeDtypeStruct((TOP_K, n), F32),
                   jax.ShapeDtypeStruct((TOP_K, n), I32),
                   jax.ShapeDtypeStruct(cnt.shape, F32)],
        name="tail_b_sample",
    )(x1, o, w_mem_o, g2, b2, wrt, br, cnt)


def _row_copy(src, dst, sem, src_row, dst_row):
    return pltpu.make_async_copy(src.at[pl.ds(src_row, 1)], dst.at[pl.ds(dst_row, 1)], sem)


def _dispatch_kernel(dest_ref, x_ref, dest2_ref, x2_ref, xb_ref, zeros_s, sems, zsem, *, n_tok_rows):
    i = pl.program_id(0)
    last = pl.num_programs(0) - 1
    slot = i % 2
    nt = dest_ref.shape[1] // TOP_K

    def start(src_ref, first, count, idx_ref, sem):
        def issue(t, c):
            for k in range(TOP_K):
                _row_copy(src_ref, xb_ref, sem, first + t, idx_ref[0, k * count + t]).start()
            return c

        lax.fori_loop(0, count, issue, 0)

    def drain(src_ref, count, sem):
        for k in range(TOP_K):
            pltpu.make_async_copy(src_ref.at[pl.ds(0, count)], xb_ref.at[pl.ds(0, count)], sem).wait()

    @pl.when(i < last)
    def _():
        start(x_ref, i * nt, nt, dest_ref, sems.at[slot])

    @pl.when(i == last)
    def _():
        zeros_s[...] = jnp.zeros_like(zeros_s)
        pltpu.make_async_copy(zeros_s, xb_ref.at[pl.ds(n_tok_rows, zeros_s.shape[0])], zsem).start()
        start(x2_ref, 0, x2_ref.shape[0], dest2_ref, sems.at[slot])

    @pl.when(i > 0)
    def _():
        drain(x_ref, nt, sems.at[1 - slot])

    @pl.when(i == last)
    def _():
        drain(x2_ref, x2_ref.shape[0], sems.at[slot])
        pltpu.make_async_copy(zeros_s, xb_ref.at[pl.ds(n_tok_rows, zeros_s.shape[0])], zsem).wait()


def _dispatch(x_a, dest_a, x_b, dest_b, n_rows):
    Ta, D = x_a.shape
    Tb = x_b.shape[0]
    nt = dest_a.shape[2] // TOP_K
    steps = Ta // nt
    n_tok_rows = (Ta + Tb) * TOP_K
    tile = lambda i: (jnp.minimum(i, steps - 1), 0, 0)
    return pl.pallas_call(
        functools.partial(_dispatch_kernel, n_tok_rows=n_tok_rows),
        grid=(steps + 1,),
        in_specs=[pl.BlockSpec((None, 1, TOP_K * nt), tile, memory_space=pltpu.SMEM),
                  pl.BlockSpec(memory_space=pl.ANY),
                  pl.BlockSpec((None, 1, TOP_K * Tb), lambda i: (0, 0, 0), memory_space=pltpu.SMEM),
                  pl.BlockSpec(memory_space=pl.ANY)],
        out_specs=pl.BlockSpec(memory_space=pl.ANY),
        out_shape=jax.ShapeDtypeStruct((n_rows, D), x_a.dtype),
        scratch_shapes=[pltpu.VMEM((n_rows - n_tok_rows, D), x_a.dtype),
                        pltpu.SemaphoreType.DMA((2,)), pltpu.SemaphoreType.DMA],
        compiler_params=pltpu.CompilerParams(dimension_semantics=("arbitrary",)),
        name="moe_dispatch",
    )(dest_a, x_a, dest_b, x_b)


def _moe_kernel(blk_ref, exp_ref, lo_ref, hi_ref, xb_ref, wgu_ref, bgu_ref, wd_ref, bd_ref, yb_ref,
                wgu_s, wd_s, cast_s):
    s = pl.program_id(0)
    new_block = jnp.logical_or(s == 0, blk_ref[s] != blk_ref[jnp.maximum(s - 1, 0)])
    lo, hi = lo_ref[s], hi_ref[s]
    used = hi > lo

    @pl.when(s == 0)
    def _():
        cast_s[0] = -1

    @pl.when(jnp.logical_and(used, cast_s[0] != exp_ref[s]))
    def _():
        wgu_s[...] = wgu_ref[...].astype(BF16)
        wd_s[...] = wd_ref[...].astype(BF16)
        cast_s[0] = exp_ref[s]

    @pl.when(new_block)
    def _():
        yb_ref[...] = jnp.zeros_like(yb_ref)

    @pl.when(used)
    def _():
        gu = _dot(xb_ref[...].astype(BF16), wgu_s[...]) + bgu_ref[...]
        g = jnp.minimum(gu[:, :D_FF], SWIGLU_LIMIT)
        u = jnp.clip(gu[:, D_FF:], -SWIGLU_LIMIT, SWIGLU_LIMIT)
        act = (u + 1.0) * g * jax.nn.sigmoid(SWIGLU_ALPHA * g)
        y = _dot(act.astype(BF16), wd_s[...]) + bd_ref[...]
        row = lax.broadcasted_iota(I32, (xb_ref.shape[0], 1), 0)
        yb_ref[...] = jnp.where(jnp.logical_and(row >= lo, row < hi), y, yb_ref[...])


def _moe(seg_blk, seg_exp, seg_lo, seg_hi, xb, w_gate_up, b_gate_up, w_down, b_down):
    P, D = xb.shape
    bm = ROW_BLOCK
    E, _, F2 = w_gate_up.shape
    ex = lambda s, blk, exp, lo, hi: (exp[s], 0, 0)
    rows = lambda s, blk, exp, lo, hi: (blk[s], 0)
    expert_bytes = (D * F2 + D_FF * D) * (2 * 4 + 2)
    block_bytes = bm * D * 4 * 2 * 2 + bm * F2 * 4 * 3
    return pl.pallas_call(
        _moe_kernel,
        grid_spec=pltpu.PrefetchScalarGridSpec(
            num_scalar_prefetch=4,
            grid=(seg_blk.shape[0],),
            in_specs=[pl.BlockSpec((bm, D), rows),
                      pl.BlockSpec((None, D, F2), ex), pl.BlockSpec((None, 1, F2), ex),
                      pl.BlockSpec((None, D_FF, D), ex), pl.BlockSpec((None, 1, D), ex)],
            out_specs=pl.BlockSpec((bm, D), rows),
            scratch_shapes=[pltpu.VMEM((D, F2), BF16), pltpu.VMEM((D_FF, D), BF16), pltpu.SMEM((1,), I32)]),
        out_shape=jax.ShapeDtypeStruct((P, D), F32),
        compiler_params=pltpu.CompilerParams(dimension_semantics=("arbitrary",),
                                             vmem_limit_bytes=expert_bytes + block_bytes),
        name="moe_grouped_ffn",
    )(seg_blk, seg_exp, seg_lo, seg_hi, xb, w_gate_up, b_gate_up.reshape(E, 1, F2), w_down, b_down.reshape(E, 1, D))


def _combine_kernel(dest_ref, next_ref, yb_ref, x2_ref, gate_ref, g3_ref, b3_ref, o_ref, rows_s, sems, *, alpha):
    nt = x2_ref.shape[0]
    i = pl.program_id(0)
    slot = i % 2

    def gather(idx_ref, buf):
        def issue(t, c):
            for k in range(TOP_K):
                _row_copy(yb_ref, rows_s.at[buf, k], sems.at[buf], idx_ref[0, k * nt + t], t).start()
            return c
        lax.fori_loop(0, nt, issue, 0)

    @pl.when(i == 0)
    def _():
        gather(dest_ref, 0)

    @pl.when(i + 1 < pl.num_programs(0))
    def _():
        gather(next_ref, 1 - slot)

    for k in range(TOP_K):
        pltpu.make_async_copy(yb_ref.at[pl.ds(0, nt)], rows_s.at[slot, k], sems.at[slot]).wait()
    gates = gate_ref[...]
    y = gates[:, 0:1] * rows_s[slot, 0]
    for k in range(1, TOP_K):
        y = y + gates[:, k:k + 1] * rows_s[slot, k]
    o_ref[...] = _layer_norm(alpha * x2_ref[...] + y, g3_ref[...], b3_ref[...])


def _combine(yb, dest_tiles, x2, gates_tok, g3, b3, alpha):
    T, D = x2.shape
    steps = dest_tiles.shape[0]
    nt = dest_tiles.shape[2] // TOP_K
    idx_spec = lambda ahead: pl.BlockSpec((None, 1, TOP_K * nt), lambda i: (jnp.minimum(i + ahead, steps - 1), 0, 0),
                                          memory_space=pltpu.SMEM)
    return pl.pallas_call(
        functools.partial(_combine_kernel, alpha=alpha),
        grid=(steps,),
        in_specs=[idx_spec(0), idx_spec(1),
                  pl.BlockSpec(memory_space=pl.ANY),
                  pl.BlockSpec((nt, D), lambda i: (i, 0)),
                  pl.BlockSpec((nt, TOP_K), lambda i: (i, 0)),
                  pl.BlockSpec((1, D), lambda i: (0, 0)), pl.BlockSpec((1, D), lambda i: (0, 0))],
        out_specs=pl.BlockSpec((nt, D), lambda i: (i, 0)),
        out_shape=jax.ShapeDtypeStruct((T, D), F32),
        scratch_shapes=[pltpu.VMEM((2, TOP_K, nt, D), F32), pltpu.SemaphoreType.DMA((2,))],
        compiler_params=pltpu.CompilerParams(dimension_semantics=("arbitrary",)),
        name="moe_combine",
    )(dest_tiles, dest_tiles, yb, x2, gates_tok, g3, b3)


def _rel_bucket(dist):
    max_exact = N_BUCKETS // 2
    df = jnp.maximum(dist, 1).astype(F32)
    large = max_exact + (jnp.log(df / max_exact) / math.log(MAX_DISTANCE / max_exact)
                         * (N_BUCKETS - max_exact)).astype(I32)
    return jnp.where(dist < max_exact, dist, jnp.minimum(large, N_BUCKETS - 1))


def _bias_tables(rel_table):
    n = BAND
    taps = jnp.arange(n + 1)
    by_tap = jnp.stack([rel_table[_rel_bucket(taps * dil)].astype(F32) for _, dil in PATTERNS])
    w = jnp.concatenate([by_tap[:, ::-1], jnp.full((len(PATTERNS), n, N_HEADS), NEG_INF, F32)], axis=1)
    w = jnp.transpose(w, (0, 2, 1))
    band = jnp.tile(w, (1, 1, n))[:, :, :n * 2 * n].reshape(len(PATTERNS), N_HEADS, n, 2 * n)
    return by_tap, band.reshape(len(PATTERNS), N_SLABS, HEADS_PER_SLAB * n, 2 * n)


def _tap_bias_by_slot(by_tap, W):
    assert PAST_LEN % W == 0 and PAST_LEN >= WINDOW_MAX
    n = BAND
    out = []
    for p, (window, dil) in enumerate(PATTERNS):
        taps = by_tap[p, :0:-1]
        col = jnp.concatenate([jnp.full((W // dil - n, N_HEADS), NEG_INF, F32), taps], axis=0)
        rest = jnp.full((W // dil, dil - 1, N_HEADS), NEG_INF, F32)
        out.append(jnp.concatenate([col[:, None, :], rest], axis=1).reshape(W, N_HEADS).T)
    return jnp.stack(out)


def _dest_tiles(dest, nt):
    T = dest.shape[1]
    return dest.reshape(TOP_K, T // nt, nt).transpose(1, 0, 2).reshape(T // nt, 1, TOP_K * nt)


def kernel(x_prompt, x_sample, mem_prompt, cache_win_k, cache_win_v, state_conv, cache_mem_k, cache_mem_v,
           rel_bias_table, w_in, w_conv, w_out, ln1_g, ln1_b, w_mem_q, w_mem_k, w_mem_v, w_mem_o,
           ln2_g, ln2_b, w_router, b_router, w_gate_up, b_gate_up, w_down, b_down, ln3_g, ln3_b):
    depth = w_in.shape[0]
    assert depth == 1
    alpha = (2 * depth) ** 0.25
    B, S, D = x_prompt.shape
    DB = x_sample.shape[0]
    T = B * S
    l = 0
    vec = lambda a: a[l].reshape(1, -1)

    by_tap, bias_band = _bias_tables(rel_bias_table)
    wrt = w_router[l].T.astype(BF16)
    br = b_router[l].reshape(N_EXPERTS, 1)
    w_in_bf = w_in[l].astype(BF16)
    wo_bf = w_out[l].astype(BF16)
    wq_bf = w_mem_q[l].astype(BF16)
    wmo_bf = w_mem_o[l].astype(BF16)

    conv_p, q_p, k_p, v_p, cs_p = _inproj_prompt(x_prompt, w_in_bf, w_conv[l])
    attn_p = _attention_prompt(q_p, k_p, v_p, bias_band)
    mk, mv, mk_bf, mv_bf = _memkv_prompt(mem_prompt, w_mem_k[l].astype(BF16), w_mem_v[l].astype(BF16))
    x2_p, idx_p, gate_p, rank_p, cnt_p = _mid_prompt(
        x_prompt, conv_p, attn_p, wo_bf[:CONV_WIDTH], wo_bf[CONV_WIDTH:], vec(ln1_g), vec(ln1_b), mk_bf, mv_bf,
        wq_bf, wmo_bf, vec(ln2_g), vec(ln2_b), wrt, br, alpha)

    xs = x_sample.reshape(DB, D)
    proj = _inproj_sample(xs, w_in_bf)
    cw = CONV_WIDTH
    q_s = proj[:, 3 * cw:3 * cw + ATTN_WIDTH]
    k_s = proj[:, 3 * cw + ATTN_WIDTH:3 * cw + 2 * ATTN_WIDTH]
    v_s = proj[:, 3 * cw + 2 * ATTN_WIDTH:]
    H, E = N_HEADS, HEAD_DIM
    bias_new = by_tap[:, 0, :, None]
    slots_last = lambda c: jnp.transpose(c, (0, 2, 3, 1))
    attn_s = _attention_sample(q_s.reshape(DB, H, E, 1), k_s.reshape(DB, H, E, 1), v_s.reshape(DB, H, E, 1),
                               slots_last(cache_win_k[l]), slots_last(cache_win_v[l]),
                               _tap_bias_by_slot(by_tap, WINDOW_MAX), bias_new).reshape(DB, ATTN_WIDTH)
    sc = state_conv[l]
    x1_s, qm_s, u_s = _tail_a_sample(xs, proj, sc[:, 0], sc[:, 1], w_conv[l], attn_s, wo_bf,
                                     vec(ln1_g), vec(ln1_b), wq_bf, alpha)
    o_s = _memattn_sample(qm_s.reshape(DB, MEM_HEADS, MEM_HEAD_DIM), cache_mem_k[l], cache_mem_v[l]).reshape(DB, D)
    x2_s, idx_s, gate_s, rank_s, cnt = _tail_b_sample(x1_s, o_s, wmo_bf, vec(ln2_g), vec(ln2_b), wrt, br,
                                                      cnt_p, alpha)

    bm = ROW_BLOCK
    counts = cnt[:, 0].astype(I32)
    ends = jnp.cumsum(counts)
    starts = ends - counts
    n_tok_rows = (T + DB) * TOP_K
    n_blocks = -(-n_tok_rows // bm)
    P = n_blocks * bm
    cuts = jnp.sort(jnp.concatenate([jnp.arange(n_blocks, dtype=I32) * bm, ends[:-1]]))
    cut_ends = jnp.concatenate([cuts[1:], jnp.full((1,), P, I32)])
    seg_blk = jnp.minimum(cuts // bm, n_blocks - 1)
    seg_exp = jnp.minimum(jnp.sum((ends[None, :] <= cuts[:, None]).astype(I32), axis=1), N_EXPERTS - 1)
    seg_lo = cuts - seg_blk * bm
    seg_hi = cut_ends - seg_blk * bm
    expert_ids = jnp.arange(N_EXPERTS, dtype=I32)[:, None, None]
    start_of = lambda idx: jnp.sum(jnp.where(idx[None] == expert_ids, starts[:, None, None], 0), axis=0)
    dest_p = start_of(idx_p) + rank_p
    dest_s = start_of(idx_s) + rank_s
    dtiles_p = _dest_tiles(dest_p, COMBINE_TILE)
    dtiles_s = _dest_tiles(dest_s, DB)

    x2_pf = x2_p.reshape(T, D)
    xb = _dispatch(x2_pf, dtiles_p, x2_s, dtiles_s, P)
    yb = _moe(seg_blk, seg_exp, seg_lo, seg_hi, xb, w_gate_up[l], b_gate_up[l], w_down[l], b_down[l])
    y_p = _combine(yb, dtiles_p, x2_pf, gate_p.T, vec(ln3_g), vec(ln3_b), alpha).reshape(B, S, D)
    y_s = _combine(yb, dtiles_s, x2_s, gate_s.T, vec(ln3_g), vec(ln3_b), alpha).reshape(DB, 1, D)

    conv_state_s = jnp.stack([sc[:, 1], u_s], axis=1)
    return (y_p, y_s,
            k_p.reshape(1, B, S, H, E), v_p.reshape(1, B, S, H, E), cs_p[:, 6:8].reshape(1, B, CONV_K - 1, CONV_WIDTH),
            mk.reshape(1, B, MEM_TOKENS, MEM_HEADS, MEM_HEAD_DIM), mv.reshape(1, B, MEM_TOKENS, MEM_HEADS, MEM_HEAD_DIM),
            k_s.reshape(1, DB, 1, H, E), v_s.reshape(1, DB, 1, H, E), conv_state_s.reshape(1, DB, CONV_K - 1, CONV_WIDTH))
```

```python
import functools
import math

import jax
import jax.numpy as jnp
import numpy as np
from jax import lax
from jax.experimental import pallas as pl
from jax.experimental.pallas import tpu as pltpu

F32 = jnp.float32
BF16 = jnp.bfloat16
I32 = jnp.int32

D_MODEL = 1024
CONV_WIDTH = 256
CONV_K = 3
HEAD_DIM = 64
N_HEADS = 12
ATTN_WIDTH = N_HEADS * HEAD_DIM
PATTERNS = ((128, 1), (512, 4), (2048, 16))
BAND = 128
WINDOW_MAX = 2048
ATTN_SCALE = 1.0 / math.sqrt(HEAD_DIM)
N_BUCKETS = 32
MAX_DISTANCE = WINDOW_MAX
MEM_TOKENS = 256
MEM_HEADS = 4
MEM_HEAD_DIM = 256
MEM_SCALE = 1.0 / math.sqrt(MEM_HEAD_DIM)
N_EXPERTS = 32
TOP_K = 4
D_FF = 1024
SWIGLU_LIMIT = 7.0
SWIGLU_ALPHA = 1.702
LN_EPS = 1e-5
PAST_LEN = 8192
NEG_INF = -1e30

LANES = 128
HEADS_PER_SLAB = LANES // HEAD_DIM
N_SLABS = ATTN_WIDTH // LANES

PROJ_TILE = 512
SEQ_TILE = 512
ROW_BLOCK = 512
COMBINE_TILE = 128
ATTN_GROUP = 8

HIGHEST = lax.Precision.HIGHEST

_FUSED_PROJ_VMEM_BYTES = 58 * 1024 * 1024


def _layer_norm(x, g, b):
    mu = jnp.mean(x, axis=-1, keepdims=True)
    var = jnp.mean(jnp.square(x - mu), axis=-1, keepdims=True)
    return (x - mu) * lax.rsqrt(var + LN_EPS) * g + b


def _dot(a, b):
    return jnp.dot(a, b, preferred_element_type=F32)


def _dot_nt(a, b):
    return lax.dot_general(a, b, (((1,), (1,)), ((), ())), preferred_element_type=F32)


def _dot_hi(a, b):
    return jnp.dot(a, b, preferred_element_type=F32, precision=HIGHEST)


def _inproj_kernel(x_ref, w_ref, wc_ref, conv_ref, q_ref, k_ref, v_ref, cs_ref, u_s):
    ts = x_ref.shape[0]
    cw = CONV_WIDTH

    @pl.when(pl.program_id(1) == 0)
    def _():
        u_s[0:8, :] = jnp.zeros((8, cw), F32)

    x = x_ref[...].astype(BF16)
    gb = _dot(x, w_ref[:, 0:cw])
    gc = _dot(x, w_ref[:, cw:2 * cw])
    h = _dot(x, w_ref[:, 2 * cw:3 * cw])
    u = gc * h
    u_s[8:8 + ts, :] = u
    wc = wc_ref[...]
    conv = wc[0:1, :] * u_s[6:6 + ts, :] + wc[1:2, :] * u_s[7:7 + ts, :] + wc[2:3, :] * u
    conv_ref[...] = (gb * conv).astype(conv_ref.dtype)
    o = 3 * cw
    q_ref[...] = _dot(x, w_ref[:, o:o + ATTN_WIDTH]) * ATTN_SCALE
    k_ref[...] = _dot(x, w_ref[:, o + ATTN_WIDTH:o + 2 * ATTN_WIDTH])
    v_ref[...] = _dot(x, w_ref[:, o + 2 * ATTN_WIDTH:o + 3 * ATTN_WIDTH])
    tail = u_s[ts:ts + 8, :]
    u_s[0:8, :] = tail
    cs_ref[...] = tail


def _attn_kernel(q_ref, k_ref, v_ref, bias_ref, o_ref, m_s, l_s, acc_s):
    S = q_ref.shape[0]
    n = BAND
    lane = lax.broadcasted_iota(I32, (n, LANES), 1)
    head_a = lane < HEAD_DIM

    def rows_of(start, count, dil):
        return pl.ds(start, count) if dil == 1 else pl.ds(start, count, stride=dil)

    def blocks(p, dil, specs):
        rows = [rows_of(start, n, dil) for start, _ in specs]
        scores, values = [], []
        for (start, has_prev), r in zip(specs, rows):
            qb = q_ref[r, :]
            q2 = jnp.concatenate([jnp.where(head_a, qb, 0.0), jnp.where(head_a, 0.0, qb)], axis=0).astype(BF16)
            krows = rows_of(start - n * dil, 2 * n, dil) if has_prev else r
            bias = bias_ref[p] if has_prev else bias_ref[p, :, n:2 * n]
            scores.append(_dot_nt(q2, k_ref[krows, :].astype(BF16)) + bias)
            values.append(v_ref[krows, :].astype(BF16))
        ms = [jnp.max(s, axis=-1, keepdims=True) for s in scores]
        es = [jnp.exp(s - m) for s, m in zip(scores, ms)]
        ls = [jnp.sum(e, axis=-1, keepdims=True) for e in es]
        pvs = [_dot(e.astype(BF16), vb) for e, vb in zip(es, values)]
        for r, m, l, pv in zip(rows, ms, ls, pvs):
            m_s[p, r, :] = jnp.where(head_a, m[:n], m[n:])
            l_s[p, r, :] = jnp.where(head_a, l[:n], l[n:])
            acc_s[p, r, :] = jnp.where(head_a, pv[:n], pv[n:])

    for p, (window, dil) in enumerate(PATTERNS):
        nb = (S // dil) // n
        specs = [(i * n * dil + r, i > 0) for r in range(dil) for i in range(nb)]
        for g in range(0, len(specs), ATTN_GROUP):
            blocks(p, dil, specs[g:g + ATTN_GROUP])

    rows_per_step = 256

    def merge(t, c):
        rows = pl.ds(t * rows_per_step, rows_per_step)
        ms = [m_s[p, rows, :] for p in range(len(PATTERNS))]
        m_all = jnp.maximum(jnp.maximum(ms[0], ms[1]), ms[2])
        num = den = None
        for p in range(len(PATTERNS)):
            w = jnp.exp(ms[p] - m_all)
            num = w * acc_s[p, rows, :] if num is None else num + w * acc_s[p, rows, :]
            den = w * l_s[p, rows, :] if den is None else den + w * l_s[p, rows, :]
        o_ref[rows, :] = (num / den).astype(o_ref.dtype)
        return c

    lax.fori_loop(0, S // rows_per_step, merge, 0)


def _attention_prompt(q, k, v, bias_tab):
    B, S, _ = q.shape
    slab = lambda p, b: (b, 0, p)
    spec = pl.BlockSpec((None, S, LANES), slab)
    return pl.pallas_call(
        _attn_kernel,
        grid=(N_SLABS, B),
        in_specs=[spec, spec, spec,
                  pl.BlockSpec((len(PATTERNS), None, HEADS_PER_SLAB * BAND, 2 * BAND), lambda p, b: (0, p, 0, 0))],
        out_specs=spec,
        out_shape=jax.ShapeDtypeStruct((B, S, ATTN_WIDTH), BF16),
        scratch_shapes=[pltpu.VMEM((len(PATTERNS), S, LANES), F32)] * 3,
        compiler_params=pltpu.CompilerParams(dimension_semantics=("arbitrary", "arbitrary")),
        name="dilated_attn_prompt",
    )(q, k, v, bias_tab)


def _memkv_kernel(m_ref, wk_ref, wv_ref, k_ref, v_ref, kb_ref, vb_ref):
    x = m_ref[...].astype(BF16)
    k = _dot(x, wk_ref[...])
    v = _dot(x, wv_ref[...])
    k_ref[...] = k
    v_ref[...] = v
    kb_ref[...] = k.astype(BF16)
    vb_ref[...] = v.astype(BF16)


def _memkv_prompt(mem, wk_bf, wv_bf):
    B, M, D = mem.shape
    row = lambda b: (b, 0, 0)
    full = lambda b: (0, 0)
    return pl.pallas_call(
        _memkv_kernel,
        grid=(B,),
        in_specs=[pl.BlockSpec((None, M, D), row), pl.BlockSpec((D, D), full), pl.BlockSpec((D, D), full)],
        out_specs=[pl.BlockSpec((None, M, D), row)] * 4,
        out_shape=[jax.ShapeDtypeStruct((B, M, D), F32)] * 2 + [jax.ShapeDtypeStruct((B, M, D), BF16)] * 2,
        compiler_params=pltpu.CompilerParams(dimension_semantics=("arbitrary",)),
        name="memkv_prompt",
    )(mem, wk_bf, wv_bf)


def _route(logits_t, carry):
    E, nt = logits_t.shape
    eidx = lax.broadcasted_iota(I32, (E, nt), 0)
    l = logits_t
    vals, idxs = [], []
    for _ in range(TOP_K):
        m = jnp.max(l, axis=0, keepdims=True)
        sel = jnp.min(jnp.where(l == m, eidx, E), axis=0, keepdims=True)
        vals.append(m)
        idxs.append(sel)
        l = jnp.where(eidx == sel, -jnp.inf, l)
    es = [jnp.exp(v - vals[0]) for v in vals]
    den = es[0] + es[1] + es[2] + es[3]
    gates = _stack_rows([e / den for e in es])
    chosen = (l == -jnp.inf)
    onehot = jnp.where(chosen, 1.0, 0.0)
    before = lax.broadcasted_iota(I32, (nt, nt), 0) < lax.broadcasted_iota(I32, (nt, nt), 1)
    prefix = _dot(onehot.astype(BF16), jnp.where(before, 1.0, 0.0).astype(BF16)) + carry
    ranks = [jnp.sum(jnp.where(eidx == s, prefix, 0.0), axis=0, keepdims=True) for s in idxs]
    rank = _stack_rows(ranks).astype(I32)
    idx = _stack_rows(idxs)
    return idx, gates, rank, carry + jnp.sum(onehot, axis=1, keepdims=True)


def _stack_rows(rows):
    k, nt = len(rows), rows[0].shape[1]
    r = lax.broadcasted_iota(I32, (k, nt), 0)
    out = jnp.broadcast_to(rows[-1], (k, nt))
    for i in range(k - 2, -1, -1):
        out = jnp.where(r == i, rows[i], out)
    return out


def _mid_kernel(x_ref, conv_ref, attn_ref, woc_ref, woa_ref, g1_ref, b1_ref, mk_ref, mv_ref, wq_ref, wmo_ref,
                g2_ref, b2_ref, wrt_ref, br_ref,
                x2_ref, idx_ref, gate_ref, rank_ref, cnt_ref, carry_s, *, alpha):
    first = jnp.logical_and(pl.program_id(0) == 0, pl.program_id(1) == 0)

    @pl.when(first)
    def _():
        carry_s[...] = jnp.zeros_like(carry_s)

    x = x_ref[...]
    mix = _dot(conv_ref[...], woc_ref[...]) + _dot(attn_ref[...], woa_ref[...])
    x1 = _layer_norm(alpha * x + mix, g1_ref[...], b1_ref[...])
    qm = (_dot(x1.astype(BF16), wq_ref[...]) * MEM_SCALE).astype(BF16)
    outs = []
    for h in range(MEM_HEADS):
        hs = slice(h * MEM_HEAD_DIM, (h + 1) * MEM_HEAD_DIM)
        s = _dot_nt(qm[:, hs], mk_ref[:, hs])
        m = jnp.max(s, axis=-1, keepdims=True)
        e = jnp.exp(s - m)
        l = jnp.sum(e, axis=-1, keepdims=True)
        outs.append((_dot(e.astype(BF16), mv_ref[:, hs]) / l).astype(BF16))
    o = jnp.concatenate(outs, axis=-1)
    x2 = _layer_norm(alpha * x1 + _dot(o, wmo_ref[...]), g2_ref[...], b2_ref[...])
    x2_ref[...] = x2
    logits_t = _dot_nt(wrt_ref[...], x2.astype(BF16)) + br_ref[...]
    idx, gates, rank, carry = _route(logits_t, carry_s[:, 0:1])
    idx_ref[...] = idx
    gate_ref[...] = gates
    rank_ref[...] = rank
    carry_s[...] = jnp.broadcast_to(carry, carry_s.shape)
    cnt_ref[...] = carry_s[...]


def _round_bf16(x):
    return x.astype(BF16).astype(F32)


def _inproj_sample_kernel(x_ref, w_ref, o_ref):
    o_ref[...] = _dot(x_ref[...].astype(BF16), w_ref[...])


def _inproj_sample(x, w_in):
    n, D = x.shape
    N = w_in.shape[1]
    bn = 512
    return pl.pallas_call(
        _inproj_sample_kernel,
        grid=(N // bn,),
        in_specs=[pl.BlockSpec((n, D), lambda j: (0, 0)), pl.BlockSpec((D, bn), lambda j: (0, j))],
        out_specs=pl.BlockSpec((n, bn), lambda j: (0, j)),
        out_shape=jax.ShapeDtypeStruct((n, N), F32),
        compiler_params=pltpu.CompilerParams(dimension_semantics=("arbitrary",)),
        name="inproj_sample",
    )(x, w_in)


def _attn_sample_kernel(q_ref, kn_ref, vn_ref, kt_ref, vt_ref, bias_ref, bnew_ref, o_ref):
    H, E, W = kt_ref.shape
    q = _round_bf16(q_ref[...])
    v_new = _round_bf16(vn_ref[...])
    s_all = jnp.sum(_round_bf16(kt_ref[...]) * q, axis=1) * ATTN_SCALE
    s_new = jnp.sum(_round_bf16(kn_ref[...]) * q, axis=1) * ATTN_SCALE
    outs, lses = [], []
    for p, (window, dil) in enumerate(PATTERNS):
        lo = W - window
        s = s_all[:, lo:] + bias_ref[p, :, lo:]
        sn = s_new + bnew_ref[p]
        m = jnp.maximum(jnp.max(s, axis=-1, keepdims=True), sn)
        e = jnp.exp(s - m)
        en = jnp.exp(sn - m)
        den = jnp.sum(e, axis=-1, keepdims=True) + en
        pr = _round_bf16(e / den)
        pv = jnp.sum(_round_bf16(vt_ref[:, :, lo:]) * pr[:, None, :], axis=-1, keepdims=True)
        outs.append(pv + _round_bf16(en / den)[:, :, None] * v_new)
        lses.append(m + jnp.log(den))
    lmax = jnp.maximum(jnp.maximum(lses[0], lses[1]), lses[2])
    ws = [jnp.exp(ls - lmax) for ls in lses]
    wsum = ws[0] + ws[1] + ws[2]
    acc = None
    for p in range(len(PATTERNS)):
        term = _round_bf16(ws[p] / wsum)[:, :, None] * _round_bf16(outs[p])
        acc = term if acc is None else acc + term
    o_ref[...] = acc


def _inproj_and_sample_attn_kernel(x_ref, w_ref, wc_ref, q4_ref, kn_ref, vn_ref, kt_ref, vt_ref, bpos_ref, bnew_ref,
                                   conv_ref, q_ref, k_ref, v_ref, cs_ref, attn_ref, u_s):
    _inproj_kernel(x_ref, w_ref, wc_ref, conv_ref, q_ref, k_ref, v_ref, cs_ref, u_s)
    _attn_sample_kernel(q4_ref, kn_ref, vn_ref, kt_ref, vt_ref, bpos_ref, bnew_ref, attn_ref)


def _inproj_prompt_and_attention_sample(x, w_in_bf, w_conv, q, k_new, v_new, win_kt, win_vt, bias_pos, bias_new):
    B, S, D = x.shape
    ts = PROJ_TILE
    n_s = S // ts
    DB, H, E, W = win_kt.shape
    assert B * n_s == DB
    row = lambda b, s: (b, s, 0)
    tok = lambda b, s: (b * n_s + s, 0, 0, 0)
    vec = pl.BlockSpec((None, H, E, 1), tok)
    cache = pl.BlockSpec((None, H, E, W), tok)
    const = lambda a: pl.BlockSpec(a.shape, lambda b, s: (0,) * a.ndim)
    return pl.pallas_call(
        _inproj_and_sample_attn_kernel,
        grid=(B, n_s),
        in_specs=[pl.BlockSpec((None, ts, D), row), const(w_in_bf), const(w_conv),
                  vec, vec, vec, cache, cache, const(bias_pos), const(bias_new)],
        out_specs=[pl.BlockSpec((None, ts, CONV_WIDTH), row),
                   pl.BlockSpec((None, ts, ATTN_WIDTH), row),
                   pl.BlockSpec((None, ts, ATTN_WIDTH), row),
                   pl.BlockSpec((None, ts, ATTN_WIDTH), row),
                   pl.BlockSpec((None, 8, CONV_WIDTH), lambda b, s: (b, 0, 0)),
                   vec],
        out_shape=[jax.ShapeDtypeStruct((B, S, CONV_WIDTH), BF16),
                   jax.ShapeDtypeStruct((B, S, ATTN_WIDTH), F32),
                   jax.ShapeDtypeStruct((B, S, ATTN_WIDTH), F32),
                   jax.ShapeDtypeStruct((B, S, ATTN_WIDTH), F32),
                   jax.ShapeDtypeStruct((B, 8, CONV_WIDTH), F32),
                   jax.ShapeDtypeStruct((DB, H, E, 1), F32)],
        scratch_shapes=[pltpu.VMEM((ts + 8, CONV_WIDTH), F32)],
        compiler_params=pltpu.CompilerParams(dimension_semantics=("arbitrary", "arbitrary"),
                                             vmem_limit_bytes=_FUSED_PROJ_VMEM_BYTES),
        name="inproj_prompt_attn_sample",
    )(x, w_in_bf, w_conv, q, k_new, v_new, win_kt, win_vt, bias_pos, bias_new)


def _tail_a_sample_kernel(x_ref, proj_ref, s0_ref, s1_ref, wc_ref, attn_ref, wo_ref, g1_ref, b1_ref, wq_ref,
                          x1_ref, qm_ref, u_ref, *, alpha):
    cw = CONV_WIDTH
    gb = proj_ref[:, 0:cw]
    u = proj_ref[:, cw:2 * cw] * proj_ref[:, 2 * cw:3 * cw]
    u_ref[...] = u
    wc = wc_ref[...]
    conv = gb * (wc[0:1, :] * s0_ref[...] + wc[1:2, :] * s1_ref[...] + wc[2:3, :] * u)
    mixed = jnp.concatenate([conv, attn_ref[...]], axis=-1).astype(BF16)
    x1 = _layer_norm(alpha * x_ref[...] + _dot(mixed, wo_ref[...]), g1_ref[...], b1_ref[...])
    x1_ref[...] = x1
    qm_ref[...] = _dot(x1.astype(BF16), wq_ref[...])


def _tail_a_sample(x, proj, s0, s1, w_conv, attn, w_out, g1, b1, w_mem_q, alpha):
    n, D = x.shape
    return pl.pallas_call(
        functools.partial(_tail_a_sample_kernel, alpha=alpha),
        out_shape=[jax.ShapeDtypeStruct((n, D), F32)] * 2 + [jax.ShapeDtypeStruct((n, CONV_WIDTH), F32)],
        name="tail_a_sample",
    )(x, proj, s0, s1, w_conv, attn, w_out, g1, b1, w_mem_q)


def _memattn_sample_kernel(q_ref, k_ref, v_ref, o_ref):
    q = _round_bf16(q_ref[...])
    s = jnp.sum(_round_bf16(k_ref[...]) * q[None], axis=-1, keepdims=True) * MEM_SCALE
    m = jnp.max(s, axis=0)
    e = jnp.exp(s - m[None])
    den = jnp.sum(e, axis=0)
    pr = _round_bf16(e / den[None])
    o_ref[...] = jnp.sum(pr * _round_bf16(v_ref[...]), axis=0)


_N_MID_IN = 15


def _mid_and_sample_memattn_kernel(*refs, alpha):
    mid_in, (qm_ref, smk_ref, smv_ref) = refs[:_N_MID_IN], refs[_N_MID_IN:_N_MID_IN + 3]
    mid_out, so_ref, carry_s = refs[_N_MID_IN + 3:-2], refs[-2], refs[-1]
    _mid_kernel(*mid_in, *mid_out, carry_s, alpha=alpha)
    _memattn_sample_kernel(qm_ref, smk_ref, smv_ref, so_ref)


def _mid_prompt_and_memattn_sample(x, conv, attn, woc, woa, g1, b1, mk, mv, wq, wmo, g2, b2, wrt, br,
                                   qm_s, mem_k_s, mem_v_s, alpha):
    B, S, D = x.shape
    ts = SEQ_TILE
    n_s = S // ts
    T = B * S
    DB, M, H, E = mem_k_s.shape
    assert B * n_s == DB
    row = lambda b, s: (b, s, 0)
    full = lambda b, s: (0, 0)
    tok = lambda b, s: (0, b * n_s + s)
    mem = lambda b, s: (b, 0, 0)
    vec = pl.BlockSpec((1, D), full)
    s_vec = pl.BlockSpec((None, H, E), lambda b, s: (b * n_s + s, 0, 0))
    s_mem = pl.BlockSpec((None, M, H, E), lambda b, s: (b * n_s + s, 0, 0, 0))
    in_specs = [pl.BlockSpec((None, ts, D), row),
                pl.BlockSpec((None, ts, CONV_WIDTH), row),
                pl.BlockSpec((None, ts, ATTN_WIDTH), row),
                pl.BlockSpec(woc.shape, full), pl.BlockSpec(woa.shape, full), vec, vec,
                pl.BlockSpec((None, MEM_TOKENS, D), mem), pl.BlockSpec((None, MEM_TOKENS, D), mem),
                pl.BlockSpec(wq.shape, full), pl.BlockSpec(wmo.shape, full), vec, vec,
                pl.BlockSpec(wrt.shape, full), pl.BlockSpec(br.shape, full)]
    assert len(in_specs) == _N_MID_IN
    return pl.pallas_call(
        functools.partial(_mid_and_sample_memattn_kernel, alpha=alpha),
        grid=(B, n_s),
        in_specs=in_specs + [s_vec, s_mem, s_mem],
        out_specs=[pl.BlockSpec((None, ts, D), row),
                   pl.BlockSpec((TOP_K, ts), tok), pl.BlockSpec((TOP_K, ts), tok), pl.BlockSpec((TOP_K, ts), tok),
                   pl.BlockSpec((N_EXPERTS, LANES), full), s_vec],
        out_shape=[jax.ShapeDtypeStruct((B, S, D), F32),
                   jax.ShapeDtypeStruct((TOP_K, T), I32),
                   jax.ShapeDtypeStruct((TOP_K, T), F32),
                   jax.ShapeDtypeStruct((TOP_K, T), I32),
                   jax.ShapeDtypeStruct((N_EXPERTS, LANES), F32),
                   jax.ShapeDtypeStruct((DB, H, E), F32)],
        scratch_shapes=[pltpu.VMEM((N_EXPERTS, LANES), F32)],
        compiler_params=pltpu.CompilerParams(dimension_semantics=("arbitrary", "arbitrary")),
        name="mid_prompt_memattn_sample",
    )(x, conv, attn, woc, woa, g1, b1, mk, mv, wq, wmo, g2, b2, wrt, br, qm_s, mem_k_s, mem_v_s)


def _tail_b_sample_kernel(x1_ref, o_ref, wmo_ref, g2_ref, b2_ref, wrt_ref, br_ref, cnt_ref,
                          x2_ref, idx_ref, gate_ref, rank_ref, cnt_out_ref, *, alpha):
    x2 = _layer_norm(alpha * x1_ref[...] + _dot(o_ref[...].astype(BF16), wmo_ref[...]), g2_ref[...], b2_ref[...])
    x2_ref[...] = x2
    logits_t = _dot_nt(wrt_ref[...], x2.astype(BF16)) + br_ref[...]
    idx, gates, rank, carry = _route(logits_t, cnt_ref[:, 0:1])
    idx_ref[...] = idx
    gate_ref[...] = gates
    rank_ref[...] = rank
    cnt_out_ref[...] = jnp.broadcast_to(carry, cnt_out_ref.shape)


def _tail_b_sample(x1, o, w_mem_o, g2, b2, wrt, br, cnt, alpha):
    n, D = x1.shape
    return pl.pallas_call(
        functools.partial(_tail_b_sample_kernel, alpha=alpha),
        out_shape=[jax.ShapeDtypeStruct((n, D), F32),
                   jax.ShapeDtypeStruct((TOP_K, n), I32),
                   jax.ShapeDtypeStruct((TOP_K, n), F32),
                   jax.ShapeDtypeStruct((TOP_K, n), I32),
                   jax.ShapeDtypeStruct(cnt.shape, F32)],
        name="tail_b_sample",
    )(x1, o, w_mem_o, g2, b2, wrt, br, cnt)


def _row_copy(src, dst, sem, src_row, dst_row):
    return pltpu.make_async_copy(src.at[pl.ds(src_row, 1)], dst.at[pl.ds(dst_row, 1)], sem)


def _dispatch_kernel(dest_ref, x_ref, dest2_ref, x2_ref, xb_ref, zeros_s, stage_s, sems, zsem, *, n_tok_rows):
    i = pl.program_id(0)
    last = pl.num_programs(0) - 1
    slot = i % 2
    nt = dest_ref.shape[1] // TOP_K

    def start(src_ref, first, count, idx_ref, sem):
        def issue(t, c):
            for k in range(TOP_K):
                _row_copy(src_ref, xb_ref, sem, first + t, idx_ref[0, k * count + t]).start()
            return c

        lax.fori_loop(0, count, issue, 0)

    def drain(src_ref, count, sem):
        for k in range(TOP_K):
            pltpu.make_async_copy(src_ref.at[pl.ds(0, count)], xb_ref.at[pl.ds(0, count)], sem).wait()

    @pl.when(i < last)
    def _():
        stage_s[slot] = x_ref[...]
        start(stage_s.at[slot], 0, nt, dest_ref, sems.at[slot])

    @pl.when(i == last)
    def _():
        zeros_s[...] = jnp.zeros_like(zeros_s)
        pltpu.make_async_copy(zeros_s, xb_ref.at[pl.ds(n_tok_rows, zeros_s.shape[0])], zsem).start()
        start(x2_ref, 0, x2_ref.shape[0], dest2_ref, sems.at[slot])

    @pl.when(i > 0)
    def _():
        drain(stage_s.at[1 - slot], nt, sems.at[1 - slot])

    @pl.when(i == last)
    def _():
        drain(x2_ref, x2_ref.shape[0], sems.at[slot])
        pltpu.make_async_copy(zeros_s, xb_ref.at[pl.ds(n_tok_rows, zeros_s.shape[0])], zsem).wait()


def _dispatch(x_a, dest_a, x_b, dest_b, n_rows):
    Ta, D = x_a.shape
    Tb = x_b.shape[0]
    nt = dest_a.shape[2] // TOP_K
    steps = Ta // nt
    n_tok_rows = (Ta + Tb) * TOP_K
    tile = lambda i: (jnp.minimum(i, steps - 1), 0, 0)
    return pl.pallas_call(
        functools.partial(_dispatch_kernel, n_tok_rows=n_tok_rows),
        grid=(steps + 1,),
        in_specs=[pl.BlockSpec((None, 1, TOP_K * nt), tile, memory_space=pltpu.SMEM),
                  pl.BlockSpec((nt, D), lambda i: (jnp.minimum(i, steps - 1), 0)),
                  pl.BlockSpec((None, 1, TOP_K * Tb), lambda i: (0, 0, 0), memory_space=pltpu.SMEM),
                  pl.BlockSpec((Tb, D), lambda i: (0, 0))],
        out_specs=pl.BlockSpec(memory_space=pl.ANY),
        out_shape=jax.ShapeDtypeStruct((n_rows, D), x_a.dtype),
        scratch_shapes=[pltpu.VMEM((n_rows - n_tok_rows, D), x_a.dtype), pltpu.VMEM((2, nt, D), x_a.dtype),
                        pltpu.SemaphoreType.DMA((2,)), pltpu.SemaphoreType.DMA],
        compiler_params=pltpu.CompilerParams(dimension_semantics=("arbitrary",)),
        name="moe_dispatch",
    )(dest_a, x_a, dest_b, x_b)


def _moe_kernel(blk_ref, exp_ref, lo_ref, hi_ref, xb_ref, wgu_ref, bgu_ref, wd_ref, bd_ref, yb_ref,
                wgu_s, wd_s, cast_s):
    s = pl.program_id(0)
    new_block = jnp.logical_or(s == 0, blk_ref[s] != blk_ref[jnp.maximum(s - 1, 0)])
    lo, hi = lo_ref[s], hi_ref[s]
    used = hi > lo

    @pl.when(s == 0)
    def _():
        cast_s[0] = -1

    @pl.when(jnp.logical_and(used, cast_s[0] != exp_ref[s]))
    def _():
        wgu_s[...] = wgu_ref[...].astype(BF16)
        wd_s[...] = wd_ref[...].astype(BF16)
        cast_s[0] = exp_ref[s]

    @pl.when(new_block)
    def _():
        yb_ref[...] = jnp.zeros_like(yb_ref)

    @pl.when(used)
    def _():
        gu = _dot(xb_ref[...].astype(BF16), wgu_s[...]) + bgu_ref[...]
        g = jnp.minimum(gu[:, :D_FF], SWIGLU_LIMIT)
        u = jnp.clip(gu[:, D_FF:], -SWIGLU_LIMIT, SWIGLU_LIMIT)
        act = (u + 1.0) * g * jax.nn.sigmoid(SWIGLU_ALPHA * g)
        y = _dot(act.astype(BF16), wd_s[...]) + bd_ref[...]
        row = lax.broadcasted_iota(I32, (xb_ref.shape[0], 1), 0)
        yb_ref[...] = jnp.where(jnp.logical_and(row >= lo, row < hi), y, yb_ref[...])


def _moe(seg_blk, seg_exp, seg_lo, seg_hi, xb, w_gate_up, b_gate_up, w_down, b_down):
    P, D = xb.shape
    bm = ROW_BLOCK
    E, _, F2 = w_gate_up.shape
    ex = lambda s, blk, exp, lo, hi: (exp[s], 0, 0)
    rows = lambda s, blk, exp, lo, hi: (blk[s], 0)
    expert_bytes = (D * F2 + D_FF * D) * (2 * 4 + 2)
    block_bytes = bm * D * 4 * 2 * 2 + bm * F2 * 4 * 3
    return pl.pallas_call(
        _moe_kernel,
        grid_spec=pltpu.PrefetchScalarGridSpec(
            num_scalar_prefetch=4,
            grid=(seg_blk.shape[0],),
            in_specs=[pl.BlockSpec((bm, D), rows),
                      pl.BlockSpec((None, D, F2), ex), pl.BlockSpec((None, 1, F2), ex),
                      pl.BlockSpec((None, D_FF, D), ex), pl.BlockSpec((None, 1, D), ex)],
            out_specs=pl.BlockSpec((bm, D), rows),
            scratch_shapes=[pltpu.VMEM((D, F2), BF16), pltpu.VMEM((D_FF, D), BF16), pltpu.SMEM((1,), I32)]),
        out_shape=jax.ShapeDtypeStruct((P, D), F32),
        compiler_params=pltpu.CompilerParams(dimension_semantics=("arbitrary",),
                                             vmem_limit_bytes=expert_bytes + block_bytes),
        name="moe_grouped_ffn",
    )(seg_blk, seg_exp, seg_lo, seg_hi, xb, w_gate_up, b_gate_up.reshape(E, 1, F2), w_down, b_down.reshape(E, 1, D))


def _combine_kernel(dest_ref, next_ref, yb_ref, x2_ref, gate_ref, g3_ref, b3_ref, o_ref, rows_s, sems, *, alpha):
    nt = x2_ref.shape[0]
    i = pl.program_id(0)
    slot = i % 2

    def gather(idx_ref, buf):
        def issue(t, c):
            for k in range(TOP_K):
                _row_copy(yb_ref, rows_s.at[buf, k], sems.at[buf], idx_ref[0, k * nt + t], t).start()
            return c
        lax.fori_loop(0, nt, issue, 0)

    @pl.when(i == 0)
    def _():
        gather(dest_ref, 0)

    @pl.when(i + 1 < pl.num_programs(0))
    def _():
        gather(next_ref, 1 - slot)

    for k in range(TOP_K):
        pltpu.make_async_copy(yb_ref.at[pl.ds(0, nt)], rows_s.at[slot, k], sems.at[slot]).wait()
    gates = gate_ref[...]
    y = gates[:, 0:1] * rows_s[slot, 0]
    for k in range(1, TOP_K):
        y = y + gates[:, k:k + 1] * rows_s[slot, k]
    o_ref[...] = _layer_norm(alpha * x2_ref[...] + y, g3_ref[...], b3_ref[...])


def _combine(yb, dest_tiles, x2, gates_tok, g3, b3, alpha):
    T, D = x2.shape
    steps = dest_tiles.shape[0]
    nt = dest_tiles.shape[2] // TOP_K
    idx_spec = lambda ahead: pl.BlockSpec((None, 1, TOP_K * nt), lambda i: (jnp.minimum(i + ahead, steps - 1), 0, 0),
                                          memory_space=pltpu.SMEM)
    return pl.pallas_call(
        functools.partial(_combine_kernel, alpha=alpha),
        grid=(steps,),
        in_specs=[idx_spec(0), idx_spec(1),
                  pl.BlockSpec(memory_space=pl.ANY),
                  pl.BlockSpec((nt, D), lambda i: (i, 0)),
                  pl.BlockSpec((nt, TOP_K), lambda i: (i, 0)),
                  pl.BlockSpec((1, D), lambda i: (0, 0)), pl.BlockSpec((1, D), lambda i: (0, 0))],
        out_specs=pl.BlockSpec((nt, D), lambda i: (i, 0)),
        out_shape=jax.ShapeDtypeStruct((T, D), F32),
        scratch_shapes=[pltpu.VMEM((2, TOP_K, nt, D), F32), pltpu.SemaphoreType.DMA((2,))],
        compiler_params=pltpu.CompilerParams(dimension_semantics=("arbitrary",)),
        name="moe_combine",
    )(dest_tiles, dest_tiles, yb, x2, gates_tok, g3, b3)


def _rel_bucket(dist):
    max_exact = N_BUCKETS // 2
    df = jnp.maximum(dist, 1).astype(F32)
    large = max_exact + (jnp.log(df / max_exact) / math.log(MAX_DISTANCE / max_exact)
                         * (N_BUCKETS - max_exact)).astype(I32)
    return jnp.where(dist < max_exact, dist, jnp.minimum(large, N_BUCKETS - 1))


def _bias_tables(rel_table):
    n = BAND
    taps = jnp.arange(n + 1)
    by_tap = jnp.stack([rel_table[_rel_bucket(taps * dil)].astype(F32) for _, dil in PATTERNS])
    w = jnp.concatenate([by_tap[:, ::-1], jnp.full((len(PATTERNS), n, N_HEADS), NEG_INF, F32)], axis=1)
    w = jnp.transpose(w, (0, 2, 1))
    band = jnp.tile(w, (1, 1, n))[:, :, :n * 2 * n].reshape(len(PATTERNS), N_HEADS, n, 2 * n)
    return by_tap, band.reshape(len(PATTERNS), N_SLABS, HEADS_PER_SLAB * n, 2 * n)


def _tap_bias_by_slot(by_tap, W):
    assert PAST_LEN % W == 0 and PAST_LEN >= WINDOW_MAX
    n = BAND
    out = []
    for p, (window, dil) in enumerate(PATTERNS):
        taps = by_tap[p, :0:-1]
        col = jnp.concatenate([jnp.full((W // dil - n, N_HEADS), NEG_INF, F32), taps], axis=0)
        rest = jnp.full((W // dil, dil - 1, N_HEADS), NEG_INF, F32)
        out.append(jnp.concatenate([col[:, None, :], rest], axis=1).reshape(W, N_HEADS).T)
    return jnp.stack(out)


def _dest_tiles(dest, nt):
    T = dest.shape[1]
    return dest.reshape(TOP_K, T // nt, nt).transpose(1, 0, 2).reshape(T // nt, 1, TOP_K * nt)


def kernel(x_prompt, x_sample, mem_prompt, cache_win_k, cache_win_v, state_conv, cache_mem_k, cache_mem_v,
           rel_bias_table, w_in, w_conv, w_out, ln1_g, ln1_b, w_mem_q, w_mem_k, w_mem_v, w_mem_o,
           ln2_g, ln2_b, w_router, b_router, w_gate_up, b_gate_up, w_down, b_down, ln3_g, ln3_b):
    depth = w_in.shape[0]
    assert depth == 1
    alpha = (2 * depth) ** 0.25
    B, S, D = x_prompt.shape
    DB = x_sample.shape[0]
    T = B * S
    l = 0
    vec = lambda a: a[l].reshape(1, -1)

    by_tap, bias_band = _bias_tables(rel_bias_table)
    wrt = w_router[l].T.astype(BF16)
    br = b_router[l].reshape(N_EXPERTS, 1)
    w_in_bf = w_in[l].astype(BF16)
    wo_bf = w_out[l].astype(BF16)
    wq_bf = w_mem_q[l].astype(BF16)
    wmo_bf = w_mem_o[l].astype(BF16)

    xs = x_sample.reshape(DB, D)
    proj = _inproj_sample(xs, w_in_bf)
    cw = CONV_WIDTH
    q_s = proj[:, 3 * cw:3 * cw + ATTN_WIDTH]
    k_s = proj[:, 3 * cw + ATTN_WIDTH:3 * cw + 2 * ATTN_WIDTH]
    v_s = proj[:, 3 * cw + 2 * ATTN_WIDTH:]
    H, E = N_HEADS, HEAD_DIM
    bias_new = by_tap[:, 0, :, None]
    slots_last = lambda c: jnp.transpose(c, (0, 2, 3, 1))
    conv_p, q_p, k_p, v_p, cs_p, attn_s = _inproj_prompt_and_attention_sample(
        x_prompt, w_in_bf, w_conv[l], q_s.reshape(DB, H, E, 1), k_s.reshape(DB, H, E, 1), v_s.reshape(DB, H, E, 1),
        slots_last(cache_win_k[l]), slots_last(cache_win_v[l]), _tap_bias_by_slot(by_tap, WINDOW_MAX), bias_new)
    attn_s = attn_s.reshape(DB, ATTN_WIDTH)

    sc = state_conv[l]
    x1_s, qm_s, u_s = _tail_a_sample(xs, proj, sc[:, 0], sc[:, 1], w_conv[l], attn_s, wo_bf,
                                     vec(ln1_g), vec(ln1_b), wq_bf, alpha)
    attn_p = _attention_prompt(q_p, k_p, v_p, bias_band)
    mk, mv, mk_bf, mv_bf = _memkv_prompt(mem_prompt, w_mem_k[l].astype(BF16), w_mem_v[l].astype(BF16))
    x2_p, idx_p, gate_p, rank_p, cnt_p, o_s = _mid_prompt_and_memattn_sample(
        x_prompt, conv_p, attn_p, wo_bf[:CONV_WIDTH], wo_bf[CONV_WIDTH:], vec(ln1_g), vec(ln1_b), mk_bf, mv_bf,
        wq_bf, wmo_bf, vec(ln2_g), vec(ln2_b), wrt, br,
        qm_s.reshape(DB, MEM_HEADS, MEM_HEAD_DIM), cache_mem_k[l], cache_mem_v[l], alpha)
    x2_s, idx_s, gate_s, rank_s, cnt = _tail_b_sample(x1_s, o_s.reshape(DB, D), wmo_bf, vec(ln2_g), vec(ln2_b), wrt, br,
                                                      cnt_p, alpha)

    bm = ROW_BLOCK
    counts = cnt[:, 0].astype(I32)
    ends = jnp.cumsum(counts)
    starts = ends - counts
    n_tok_rows = (T + DB) * TOP_K
    n_blocks = -(-n_tok_rows // bm)
    P = n_blocks * bm
    cuts = jnp.sort(jnp.concatenate([jnp.arange(n_blocks, dtype=I32) * bm, ends[:-1]]))
    cut_ends = jnp.concatenate([cuts[1:], jnp.full((1,), P, I32)])
    seg_blk = jnp.minimum(cuts // bm, n_blocks - 1)
    seg_exp = jnp.minimum(jnp.sum((ends[None, :] <= cuts[:, None]).astype(I32), axis=1), N_EXPERTS - 1)
    seg_lo = cuts - seg_blk * bm
    seg_hi = cut_ends - seg_blk * bm
    expert_ids = jnp.arange(N_EXPERTS, dtype=I32)[:, None, None]
    start_of = lambda idx: jnp.sum(jnp.where(idx[None] == expert_ids, starts[:, None, None], 0), axis=0)
    dest_p = start_of(idx_p) + rank_p
    dest_s = start_of(idx_s) + rank_s
    dtiles_p = _dest_tiles(dest_p, COMBINE_TILE)
    dtiles_s = _dest_tiles(dest_s, DB)

    x2_pf = x2_p.reshape(T, D)
    xb = _dispatch(x2_pf, dtiles_p, x2_s, dtiles_s, P)
    yb = _moe(seg_blk, seg_exp, seg_lo, seg_hi, xb, w_gate_up[l], b_gate_up[l], w_down[l], b_down[l])
    y_p = _combine(yb, dtiles_p, x2_pf, gate_p.T, vec(ln3_g), vec(ln3_b), alpha).reshape(B, S, D)
    y_s = _combine(yb, dtiles_s, x2_s, gate_s.T, vec(ln3_g), vec(ln3_b), alpha).reshape(DB, 1, D)

    conv_state_s = jnp.stack([sc[:, 1], u_s], axis=1)
    return (y_p, y_s,
            k_p.reshape(1, B, S, H, E), v_p.reshape(1, B, S, H, E), cs_p[:, 6:8].reshape(1, B, CONV_K - 1, CONV_WIDTH),
            mk.reshape(1, B, MEM_TOKENS, MEM_HEADS, MEM_HEAD_DIM), mv.reshape(1, B, MEM_TOKENS, MEM_HEADS, MEM_HEAD_DIM),
            k_s.reshape(1, DB, 1, H, E), v_s.reshape(1, DB, 1, H, E), conv_state_s.reshape(1, DB, CONV_K - 1, CONV_WIDTH))
```

```python
import functools
import math

import jax
import jax.numpy as jnp
import numpy as np
from jax import lax
from jax.experimental import pallas as pl
from jax.experimental.pallas import tpu as pltpu

F32 = jnp.float32
BF16 = jnp.bfloat16
I32 = jnp.int32

D_MODEL = 1024
CONV_WIDTH = 256
CONV_K = 3
HEAD_DIM = 64
N_HEADS = 12
ATTN_WIDTH = N_HEADS * HEAD_DIM
PATTERNS = ((128, 1), (512, 4), (2048, 16))
BAND = 128
WINDOW_MAX = 2048
ATTN_SCALE = 1.0 / math.sqrt(HEAD_DIM)
N_BUCKETS = 32
MAX_DISTANCE = WINDOW_MAX
MEM_TOKENS = 256
MEM_HEADS = 4
MEM_HEAD_DIM = 256
MEM_SCALE = 1.0 / math.sqrt(MEM_HEAD_DIM)
N_EXPERTS = 32
TOP_K = 4
D_FF = 1024
SWIGLU_LIMIT = 7.0
SWIGLU_ALPHA = 1.702
LN_EPS = 1e-5
PAST_LEN = 8192
NEG_INF = -1e30

LANES = 128
HEADS_PER_SLAB = LANES // HEAD_DIM
N_SLABS = ATTN_WIDTH // LANES

PROJ_TILE = 512
SEQ_TILE = 512
ROW_BLOCK = 512
COMBINE_TILE = 256
ATTN_GROUP = 8

HIGHEST = lax.Precision.HIGHEST

_FUSED_PROJ_VMEM_BYTES = 58 * 1024 * 1024


def _layer_norm(x, g, b):
    mu = jnp.mean(x, axis=-1, keepdims=True)
    var = jnp.mean(jnp.square(x - mu), axis=-1, keepdims=True)
    return (x - mu) * lax.rsqrt(var + LN_EPS) * g + b


def _dot(a, b):
    return jnp.dot(a, b, preferred_element_type=F32)


def _dot_nt(a, b):
    return lax.dot_general(a, b, (((1,), (1,)), ((), ())), preferred_element_type=F32)


def _dot_hi(a, b):
    return jnp.dot(a, b, preferred_element_type=F32, precision=HIGHEST)


def _inproj_kernel(x_ref, w_ref, wc_ref, conv_ref, q_ref, k_ref, v_ref, cs_ref, u_s):
    ts = x_ref.shape[0]
    cw = CONV_WIDTH

    @pl.when(pl.program_id(1) == 0)
    def _():
        u_s[0:8, :] = jnp.zeros((8, cw), F32)

    x = x_ref[...].astype(BF16)
    gb = _dot(x, w_ref[:, 0:cw])
    gc = _dot(x, w_ref[:, cw:2 * cw])
    h = _dot(x, w_ref[:, 2 * cw:3 * cw])
    u = gc * h
    u_s[8:8 + ts, :] = u
    wc = wc_ref[...]
    conv = wc[0:1, :] * u_s[6:6 + ts, :] + wc[1:2, :] * u_s[7:7 + ts, :] + wc[2:3, :] * u
    conv_ref[...] = (gb * conv).astype(conv_ref.dtype)
    o = 3 * cw
    q_ref[...] = _dot(x, w_ref[:, o:o + ATTN_WIDTH]) * ATTN_SCALE
    k_ref[...] = _dot(x, w_ref[:, o + ATTN_WIDTH:o + 2 * ATTN_WIDTH])
    v_ref[...] = _dot(x, w_ref[:, o + 2 * ATTN_WIDTH:o + 3 * ATTN_WIDTH])
    tail = u_s[ts:ts + 8, :]
    u_s[0:8, :] = tail
    cs_ref[...] = tail


def _attn_kernel(q_ref, k_ref, v_ref, w_ref, o_ref, bias_ref, m_s, l_s, acc_s):
    S = q_ref.shape[0]
    n = BAND
    lane = lax.broadcasted_iota(I32, (n, LANES), 1)
    head_a = lane < HEAD_DIM

    @pl.when(pl.program_id(1) == 0)
    def _():
        for p in range(len(PATTERNS)):
            for hh in range(HEADS_PER_SLAB):
                first_row = jnp.broadcast_to(w_ref[p, hh:hh + 1, :], (n, 2 * n))
                bias_ref[p, hh * n:(hh + 1) * n, :] = pltpu.roll(first_row, 0, 1, stride=1, stride_axis=0)

    def rows_of(start, count, dil):
        return pl.ds(start, count) if dil == 1 else pl.ds(start, count, stride=dil)

    def blocks(p, dil, specs):
        rows = [rows_of(start, n, dil) for start, _ in specs]
        scores, values = [], []
        for (start, has_prev), r in zip(specs, rows):
            qb = q_ref[r, :]
            q2 = jnp.concatenate([jnp.where(head_a, qb, 0.0), jnp.where(head_a, 0.0, qb)], axis=0).astype(BF16)
            krows = rows_of(start - n * dil, 2 * n, dil) if has_prev else r
            bias = bias_ref[p] if has_prev else bias_ref[p, :, n:2 * n]
            scores.append(_dot_nt(q2, k_ref[krows, :].astype(BF16)) + bias)
            values.append(v_ref[krows, :].astype(BF16))
        ms = [jnp.max(s, axis=-1, keepdims=True) for s in scores]
        es = [jnp.exp(s - m) for s, m in zip(scores, ms)]
        ls = [jnp.sum(e, axis=-1, keepdims=True) for e in es]
        pvs = [_dot(e.astype(BF16), vb) for e, vb in zip(es, values)]
        for r, m, l, pv in zip(rows, ms, ls, pvs):
            m_s[p, r, :] = jnp.where(head_a, m[:n], m[n:])
            l_s[p, r, :] = jnp.where(head_a, l[:n], l[n:])
            acc_s[p, r, :] = jnp.where(head_a, pv[:n], pv[n:])

    for p, (window, dil) in enumerate(PATTERNS):
        nb = (S // dil) // n
        specs = [(i * n * dil + r, i > 0) for r in range(dil) for i in range(nb)]
        for g in range(0, len(specs), ATTN_GROUP):
            blocks(p, dil, specs[g:g + ATTN_GROUP])

    rows_per_step = 256

    def merge(t, c):
        rows = pl.ds(t * rows_per_step, rows_per_step)
        ms = [m_s[p, rows, :] for p in range(len(PATTERNS))]
        m_all = jnp.maximum(jnp.maximum(ms[0], ms[1]), ms[2])
        num = den = None
        for p in range(len(PATTERNS)):
            w = jnp.exp(ms[p] - m_all)
            num = w * acc_s[p, rows, :] if num is None else num + w * acc_s[p, rows, :]
            den = w * l_s[p, rows, :] if den is None else den + w * l_s[p, rows, :]
        o_ref[rows, :] = (num / den).astype(o_ref.dtype)
        return c

    lax.fori_loop(0, S // rows_per_step, merge, 0)


def _attention_prompt(q, k, v, bias_tab):
    B, S, _ = q.shape
    slab = lambda p, b: (b, 0, p)
    spec = pl.BlockSpec((None, S, LANES), slab)
    return pl.pallas_call(
        _attn_kernel,
        grid=(N_SLABS, B),
        in_specs=[spec, spec, spec,
                  pl.BlockSpec((len(PATTERNS), None, HEADS_PER_SLAB, 2 * BAND), lambda p, b: (0, p, 0, 0))],
        out_specs=spec,
        out_shape=jax.ShapeDtypeStruct((B, S, ATTN_WIDTH), BF16),
        scratch_shapes=[pltpu.VMEM((len(PATTERNS), HEADS_PER_SLAB * BAND, 2 * BAND), F32)]
        + [pltpu.VMEM((len(PATTERNS), S, LANES), F32)] * 3,
        compiler_params=pltpu.CompilerParams(dimension_semantics=("arbitrary", "arbitrary")),
        name="dilated_attn_prompt",
    )(q, k, v, bias_tab)


def _memkv_kernel(m_ref, wk_ref, wv_ref, k_ref, v_ref, kb_ref, vb_ref):
    x = m_ref[...].astype(BF16)
    k = _dot(x, wk_ref[...])
    v = _dot(x, wv_ref[...])
    k_ref[...] = k
    v_ref[...] = v
    kb_ref[...] = k.astype(BF16)
    vb_ref[...] = v.astype(BF16)


def _memkv_prompt(mem, wk_bf, wv_bf):
    B, M, D = mem.shape
    row = lambda b: (b, 0, 0)
    full = lambda b: (0, 0)
    return pl.pallas_call(
        _memkv_kernel,
        grid=(B,),
        in_specs=[pl.BlockSpec((None, M, D), row), pl.BlockSpec((D, D), full), pl.BlockSpec((D, D), full)],
        out_specs=[pl.BlockSpec((None, M, D), row)] * 4,
        out_shape=[jax.ShapeDtypeStruct((B, M, D), F32)] * 2 + [jax.ShapeDtypeStruct((B, M, D), BF16)] * 2,
        compiler_params=pltpu.CompilerParams(dimension_semantics=("arbitrary",)),
        name="memkv_prompt",
    )(mem, wk_bf, wv_bf)


def _route(logits_t, carry):
    E, nt = logits_t.shape
    eidx = lax.broadcasted_iota(I32, (E, nt), 0)
    l = logits_t
    vals, idxs = [], []
    for _ in range(TOP_K):
        m = jnp.max(l, axis=0, keepdims=True)
        sel = jnp.min(jnp.where(l == m, eidx, E), axis=0, keepdims=True)
        vals.append(m)
        idxs.append(sel)
        l = jnp.where(eidx == sel, -jnp.inf, l)
    es = [jnp.exp(v - vals[0]) for v in vals]
    den = es[0] + es[1] + es[2] + es[3]
    gates = _stack_rows([e / den for e in es])
    chosen = (l == -jnp.inf)
    onehot = jnp.where(chosen, 1.0, 0.0)
    before = lax.broadcasted_iota(I32, (nt, nt), 0) < lax.broadcasted_iota(I32, (nt, nt), 1)
    prefix = _dot(onehot.astype(BF16), jnp.where(before, 1.0, 0.0).astype(BF16)) + carry
    ranks = [jnp.sum(jnp.where(eidx == s, prefix, 0.0), axis=0, keepdims=True) for s in idxs]
    rank = _stack_rows(ranks).astype(I32)
    idx = _stack_rows(idxs)
    return idx, gates, rank, carry + jnp.sum(onehot, axis=1, keepdims=True)


def _stack_rows(rows):
    k, nt = len(rows), rows[0].shape[1]
    r = lax.broadcasted_iota(I32, (k, nt), 0)
    out = jnp.broadcast_to(rows[-1], (k, nt))
    for i in range(k - 2, -1, -1):
        out = jnp.where(r == i, rows[i], out)
    return out


def _mid_kernel(x_ref, conv_ref, attn_ref, woc_ref, woa_ref, g1_ref, b1_ref, mk_ref, mv_ref, wq_ref, wmo_ref,
                g2_ref, b2_ref, wrt_ref, br_ref,
                x2_ref, idx_ref, gate_ref, rank_ref, cnt_ref, carry_s, *, alpha):
    first = jnp.logical_and(pl.program_id(0) == 0, pl.program_id(1) == 0)

    @pl.when(first)
    def _():
        carry_s[...] = jnp.zeros_like(carry_s)

    x = x_ref[...]
    mix = _dot(conv_ref[...], woc_ref[...]) + _dot(attn_ref[...], woa_ref[...])
    x1 = _layer_norm(alpha * x + mix, g1_ref[...], b1_ref[...])
    qm = (_dot(x1.astype(BF16), wq_ref[...]) * MEM_SCALE).astype(BF16)
    outs = []
    for h in range(MEM_HEADS):
        hs = slice(h * MEM_HEAD_DIM, (h + 1) * MEM_HEAD_DIM)
        s = _dot_nt(qm[:, hs], mk_ref[:, hs])
        m = jnp.max(s, axis=-1, keepdims=True)
        e = jnp.exp(s - m)
        l = jnp.sum(e, axis=-1, keepdims=True)
        outs.append((_dot(e.astype(BF16), mv_ref[:, hs]) / l).astype(BF16))
    o = jnp.concatenate(outs, axis=-1)
    x2 = _layer_norm(alpha * x1 + _dot(o, wmo_ref[...]), g2_ref[...], b2_ref[...])
    x2_ref[...] = x2
    logits_t = _dot_nt(wrt_ref[...], x2.astype(BF16)) + br_ref[...]
    idx, gates, rank, carry = _route(logits_t, carry_s[:, 0:1])
    idx_ref[...] = idx
    gate_ref[...] = gates
    rank_ref[...] = rank
    carry_s[...] = jnp.broadcast_to(carry, carry_s.shape)
    cnt_ref[...] = carry_s[...]


def _round_bf16(x):
    return x.astype(BF16).astype(F32)


def _inproj_sample_kernel(x_ref, w_ref, o_ref):
    o_ref[...] = _dot(x_ref[...].astype(BF16), w_ref[...])


def _inproj_sample(x, w_in):
    n, D = x.shape
    N = w_in.shape[1]
    bn = 512
    return pl.pallas_call(
        _inproj_sample_kernel,
        grid=(N // bn,),
        in_specs=[pl.BlockSpec((n, D), lambda j: (0, 0)), pl.BlockSpec((D, bn), lambda j: (0, j))],
        out_specs=pl.BlockSpec((n, bn), lambda j: (0, j)),
        out_shape=jax.ShapeDtypeStruct((n, N), F32),
        compiler_params=pltpu.CompilerParams(dimension_semantics=("arbitrary",)),
        name="inproj_sample",
    )(x, w_in)


def _attn_sample_kernel(q_ref, kn_ref, vn_ref, kt_ref, vt_ref, bias_ref, bnew_ref, o_ref):
    H, E, W = kt_ref.shape
    q = _round_bf16(q_ref[...])
    v_new = _round_bf16(vn_ref[...])
    s_all = jnp.sum(_round_bf16(kt_ref[...]) * q, axis=1) * ATTN_SCALE
    s_new = jnp.sum(_round_bf16(kn_ref[...]) * q, axis=1) * ATTN_SCALE
    outs, lses = [], []
    for p, (window, dil) in enumerate(PATTERNS):
        lo = W - window
        s = s_all[:, lo:] + bias_ref[p, :, lo:]
        sn = s_new + bnew_ref[p]
        m = jnp.maximum(jnp.max(s, axis=-1, keepdims=True), sn)
        e = jnp.exp(s - m)
        en = jnp.exp(sn - m)
        den = jnp.sum(e, axis=-1, keepdims=True) + en
        pr = _round_bf16(e / den)
        pv = jnp.sum(_round_bf16(vt_ref[:, :, lo:]) * pr[:, None, :], axis=-1, keepdims=True)
        outs.append(pv + _round_bf16(en / den)[:, :, None] * v_new)
        lses.append(m + jnp.log(den))
    lmax = jnp.maximum(jnp.maximum(lses[0], lses[1]), lses[2])
    ws = [jnp.exp(ls - lmax) for ls in lses]
    wsum = ws[0] + ws[1] + ws[2]
    acc = None
    for p in range(len(PATTERNS)):
        term = _round_bf16(ws[p] / wsum)[:, :, None] * _round_bf16(outs[p])
        acc = term if acc is None else acc + term
    o_ref[...] = acc


def _inproj_and_sample_attn_kernel(x_ref, w_ref, wc_ref, q4_ref, kn_ref, vn_ref, kt_ref, vt_ref, bpos_ref, bnew_ref,
                                   conv_ref, q_ref, k_ref, v_ref, cs_ref, attn_ref, u_s):
    _inproj_kernel(x_ref, w_ref, wc_ref, conv_ref, q_ref, k_ref, v_ref, cs_ref, u_s)
    _attn_sample_kernel(q4_ref, kn_ref, vn_ref, kt_ref, vt_ref, bpos_ref, bnew_ref, attn_ref)


def _inproj_prompt_and_attention_sample(x, w_in_bf, w_conv, q, k_new, v_new, win_kt, win_vt, bias_pos, bias_new):
    B, S, D = x.shape
    ts = PROJ_TILE
    n_s = S // ts
    DB, H, E, W = win_kt.shape
    assert B * n_s == DB
    row = lambda b, s: (b, s, 0)
    tok = lambda b, s: (b * n_s + s, 0, 0, 0)
    vec = pl.BlockSpec((None, H, E, 1), tok)
    cache = pl.BlockSpec((None, H, E, W), tok)
    const = lambda a: pl.BlockSpec(a.shape, lambda b, s: (0,) * a.ndim)
    return pl.pallas_call(
        _inproj_and_sample_attn_kernel,
        grid=(B, n_s),
        in_specs=[pl.BlockSpec((None, ts, D), row), const(w_in_bf), const(w_conv),
                  vec, vec, vec, cache, cache, const(bias_pos), const(bias_new)],
        out_specs=[pl.BlockSpec((None, ts, CONV_WIDTH), row),
                   pl.BlockSpec((None, ts, ATTN_WIDTH), row),
                   pl.BlockSpec((None, ts, ATTN_WIDTH), row),
                   pl.BlockSpec((None, ts, ATTN_WIDTH), row),
                   pl.BlockSpec((None, 8, CONV_WIDTH), lambda b, s: (b, 0, 0)),
                   vec],
        out_shape=[jax.ShapeDtypeStruct((B, S, CONV_WIDTH), BF16),
                   jax.ShapeDtypeStruct((B, S, ATTN_WIDTH), F32),
                   jax.ShapeDtypeStruct((B, S, ATTN_WIDTH), F32),
                   jax.ShapeDtypeStruct((B, S, ATTN_WIDTH), F32),
                   jax.ShapeDtypeStruct((B, 8, CONV_WIDTH), F32),
                   jax.ShapeDtypeStruct((DB, H, E, 1), F32)],
        scratch_shapes=[pltpu.VMEM((ts + 8, CONV_WIDTH), F32)],
        compiler_params=pltpu.CompilerParams(dimension_semantics=("arbitrary", "arbitrary"),
                                             vmem_limit_bytes=_FUSED_PROJ_VMEM_BYTES),
        name="inproj_prompt_attn_sample",
    )(x, w_in_bf, w_conv, q, k_new, v_new, win_kt, win_vt, bias_pos, bias_new)


def _tail_a_sample_kernel(x_ref, proj_ref, s0_ref, s1_ref, wc_ref, attn_ref, wo_ref, g1_ref, b1_ref, wq_ref,
                          x1_ref, qm_ref, u_ref, *, alpha):
    cw = CONV_WIDTH
    gb = proj_ref[:, 0:cw]
    u = proj_ref[:, cw:2 * cw] * proj_ref[:, 2 * cw:3 * cw]
    u_ref[...] = u
    wc = wc_ref[...]
    conv = gb * (wc[0:1, :] * s0_ref[...] + wc[1:2, :] * s1_ref[...] + wc[2:3, :] * u)
    mixed = jnp.concatenate([conv, attn_ref[...]], axis=-1).astype(BF16)
    x1 = _layer_norm(alpha * x_ref[...] + _dot(mixed, wo_ref[...]), g1_ref[...], b1_ref[...])
    x1_ref[...] = x1
    qm_ref[...] = _dot(x1.astype(BF16), wq_ref[...])


def _tail_a_sample(x, proj, s0, s1, w_conv, attn, w_out, g1, b1, w_mem_q, alpha):
    n, D = x.shape
    return pl.pallas_call(
        functools.partial(_tail_a_sample_kernel, alpha=alpha),
        out_shape=[jax.ShapeDtypeStruct((n, D), F32)] * 2 + [jax.ShapeDtypeStruct((n, CONV_WIDTH), F32)],
        name="tail_a_sample",
    )(x, proj, s0, s1, w_conv, attn, w_out, g1, b1, w_mem_q)


def _memattn_sample_kernel(q_ref, k_ref, v_ref, o_ref):
    q = _round_bf16(q_ref[...])
    s = jnp.sum(_round_bf16(k_ref[...]) * q[None], axis=-1, keepdims=True) * MEM_SCALE
    m = jnp.max(s, axis=0)
    e = jnp.exp(s - m[None])
    den = jnp.sum(e, axis=0)
    pr = _round_bf16(e / den[None])
    o_ref[...] = jnp.sum(pr * _round_bf16(v_ref[...]), axis=0)


_N_MID_IN = 15


def _mid_and_sample_memattn_kernel(*refs, alpha):
    mid_in, (qm_ref, smk_ref, smv_ref) = refs[:_N_MID_IN], refs[_N_MID_IN:_N_MID_IN + 3]
    mid_out, so_ref, carry_s = refs[_N_MID_IN + 3:-2], refs[-2], refs[-1]
    _mid_kernel(*mid_in, *mid_out, carry_s, alpha=alpha)
    _memattn_sample_kernel(qm_ref, smk_ref, smv_ref, so_ref)


def _mid_prompt_and_memattn_sample(x, conv, attn, woc, woa, g1, b1, mk, mv, wq, wmo, g2, b2, wrt, br,
                                   qm_s, mem_k_s, mem_v_s, alpha):
    B, S, D = x.shape
    ts = SEQ_TILE
    n_s = S // ts
    T = B * S
    DB, M, H, E = mem_k_s.shape
    assert B * n_s == DB
    row = lambda b, s: (b, s, 0)
    full = lambda b, s: (0, 0)
    tok = lambda b, s: (0, b * n_s + s)
    mem = lambda b, s: (b, 0, 0)
    vec = pl.BlockSpec((1, D), full)
    s_vec = pl.BlockSpec((None, H, E), lambda b, s: (b * n_s + s, 0, 0))
    s_mem = pl.BlockSpec((None, M, H, E), lambda b, s: (b * n_s + s, 0, 0, 0))
    in_specs = [pl.BlockSpec((None, ts, D), row),
                pl.BlockSpec((None, ts, CONV_WIDTH), row),
                pl.BlockSpec((None, ts, ATTN_WIDTH), row),
                pl.BlockSpec(woc.shape, full), pl.BlockSpec(woa.shape, full), vec, vec,
                pl.BlockSpec((None, MEM_TOKENS, D), mem), pl.BlockSpec((None, MEM_TOKENS, D), mem),
                pl.BlockSpec(wq.shape, full), pl.BlockSpec(wmo.shape, full), vec, vec,
                pl.BlockSpec(wrt.shape, full), pl.BlockSpec(br.shape, full)]
    assert len(in_specs) == _N_MID_IN
    return pl.pallas_call(
        functools.partial(_mid_and_sample_memattn_kernel, alpha=alpha),
        grid=(B, n_s),
        in_specs=in_specs + [s_vec, s_mem, s_mem],
        out_specs=[pl.BlockSpec((None, ts, D), row),
                   pl.BlockSpec((TOP_K, ts), tok), pl.BlockSpec((TOP_K, ts), tok), pl.BlockSpec((TOP_K, ts), tok),
                   pl.BlockSpec((N_EXPERTS, LANES), full), s_vec],
        out_shape=[jax.ShapeDtypeStruct((B, S, D), F32),
                   jax.ShapeDtypeStruct((TOP_K, T), I32),
                   jax.ShapeDtypeStruct((TOP_K, T), F32),
                   jax.ShapeDtypeStruct((TOP_K, T), I32),
                   jax.ShapeDtypeStruct((N_EXPERTS, LANES), F32),
                   jax.ShapeDtypeStruct((DB, H, E), F32)],
        scratch_shapes=[pltpu.VMEM((N_EXPERTS, LANES), F32)],
        compiler_params=pltpu.CompilerParams(dimension_semantics=("arbitrary", "arbitrary")),
        name="mid_prompt_memattn_sample",
    )(x, conv, attn, woc, woa, g1, b1, mk, mv, wq, wmo, g2, b2, wrt, br, qm_s, mem_k_s, mem_v_s)


def _tail_b_sample_kernel(x1_ref, o_ref, wmo_ref, g2_ref, b2_ref, wrt_ref, br_ref, cnt_ref,
                          x2_ref, idx_ref, gate_ref, rank_ref, cnt_out_ref, *, alpha):
    x2 = _layer_norm(alpha * x1_ref[...] + _dot(o_ref[...].astype(BF16), wmo_ref[...]), g2_ref[...], b2_ref[...])
    x2_ref[...] = x2
    logits_t = _dot_nt(wrt_ref[...], x2.astype(BF16)) + br_ref[...]
    idx, gates, rank, carry = _route(logits_t, cnt_ref[:, 0:1])
    idx_ref[...] = idx
    gate_ref[...] = gates
    rank_ref[...] = rank
    cnt_out_ref[...] = jnp.broadcast_to(carry, cnt_out_ref.shape)


def _tail_b_sample(x1, o, w_mem_o, g2, b2, wrt, br, cnt, alpha):
    n, D = x1.shape
    return pl.pallas_call(
        functools.partial(_tail_b_sample_kernel, alpha=alpha),
        out_shape=[jax.ShapeDtypeStruct((n, D), F32),
                   jax.ShapeDtypeStruct((TOP_K, n), I32),
                   jax.ShapeDtypeStruct((TOP_K, n), F32),
                   jax.ShapeDtypeStruct((TOP_K, n), I32),
                   jax.ShapeDtypeStruct(cnt.shape, F32)],
        name="tail_b_sample",
    )(x1, o, w_mem_o, g2, b2, wrt, br, cnt)


def _row_copy(src, dst, sem, src_row, dst_row):
    return pltpu.make_async_copy(src.at[pl.ds(src_row, 1)], dst.at[pl.ds(dst_row, 1)], sem)


def _dispatch_kernel(dest_ref, x_ref, dest2_ref, x2_ref, xb_ref, zeros_s, stage_s, sems, zsem, *, n_tok_rows):
    i = pl.program_id(0)
    last = pl.num_programs(0) - 1
    slot = i % 2
    nt = dest_ref.shape[1] // TOP_K

    def start(src_ref, first, count, idx_ref, sem):
        def issue(t, c):
            for k in range(TOP_K):
                _row_copy(src_ref, xb_ref, sem, first + t, idx_ref[0, k * count + t]).start()
            return c

        lax.fori_loop(0, count, issue, 0)

    def drain(src_ref, count, sem):
        for k in range(TOP_K):
            pltpu.make_async_copy(src_ref.at[pl.ds(0, count)], xb_ref.at[pl.ds(0, count)], sem).wait()

    @pl.when(i < last)
    def _():
        stage_s[slot] = x_ref[...]
        start(stage_s.at[slot], 0, nt, dest_ref, sems.at[slot])

    @pl.when(i == last)
    def _():
        zeros_s[...] = jnp.zeros_like(zeros_s)
        pltpu.make_async_copy(zeros_s, xb_ref.at[pl.ds(n_tok_rows, zeros_s.shape[0])], zsem).start()
        start(x2_ref, 0, x2_ref.shape[0], dest2_ref, sems.at[slot])

    @pl.when(i > 0)
    def _():
        drain(stage_s.at[1 - slot], nt, sems.at[1 - slot])

    @pl.when(i == last)
    def _():
        drain(x2_ref, x2_ref.shape[0], sems.at[slot])
        pltpu.make_async_copy(zeros_s, xb_ref.at[pl.ds(n_tok_rows, zeros_s.shape[0])], zsem).wait()


def _dispatch(x_a, dest_a, x_b, dest_b, n_rows):
    Ta, D = x_a.shape
    Tb = x_b.shape[0]
    nt = dest_a.shape[2] // TOP_K
    steps = Ta // nt
    n_tok_rows = (Ta + Tb) * TOP_K
    tile = lambda i: (jnp.minimum(i, steps - 1), 0, 0)
    return pl.pallas_call(
        functools.partial(_dispatch_kernel, n_tok_rows=n_tok_rows),
        grid=(steps + 1,),
        in_specs=[pl.BlockSpec((None, 1, TOP_K * nt), tile, memory_space=pltpu.SMEM),
                  pl.BlockSpec((nt, D), lambda i: (jnp.minimum(i, steps - 1), 0)),
                  pl.BlockSpec((None, 1, TOP_K * Tb), lambda i: (0, 0, 0), memory_space=pltpu.SMEM),
                  pl.BlockSpec((Tb, D), lambda i: (0, 0))],
        out_specs=pl.BlockSpec(memory_space=pl.ANY),
        out_shape=jax.ShapeDtypeStruct((n_rows, D), x_a.dtype),
        scratch_shapes=[pltpu.VMEM((n_rows - n_tok_rows, D), x_a.dtype), pltpu.VMEM((2, nt, D), x_a.dtype),
                        pltpu.SemaphoreType.DMA((2,)), pltpu.SemaphoreType.DMA],
        compiler_params=pltpu.CompilerParams(dimension_semantics=("arbitrary",)),
        name="moe_dispatch",
    )(dest_a, x_a, dest_b, x_b)


def _moe_kernel(blk_ref, exp_ref, lo_ref, hi_ref, xb_ref, wgu_ref, bgu_ref, wd_ref, bd_ref, yb_ref,
                wgu_s, wd_s, cast_s):
    s = pl.program_id(0)
    new_block = jnp.logical_or(s == 0, blk_ref[s] != blk_ref[jnp.maximum(s - 1, 0)])
    lo, hi = lo_ref[s], hi_ref[s]
    used = hi > lo

    @pl.when(s == 0)
    def _():
        cast_s[0] = -1

    @pl.when(jnp.logical_and(used, cast_s[0] != exp_ref[s]))
    def _():
        wgu_s[...] = wgu_ref[...].astype(BF16)
        wd_s[...] = wd_ref[...].astype(BF16)
        cast_s[0] = exp_ref[s]

    @pl.when(new_block)
    def _():
        yb_ref[...] = jnp.zeros_like(yb_ref)

    @pl.when(used)
    def _():
        gu = _dot(xb_ref[...].astype(BF16), wgu_s[...]) + bgu_ref[...]
        g = jnp.minimum(gu[:, :D_FF], SWIGLU_LIMIT)
        u = jnp.clip(gu[:, D_FF:], -SWIGLU_LIMIT, SWIGLU_LIMIT)
        act = (u + 1.0) * g * jax.nn.sigmoid(SWIGLU_ALPHA * g)
        y = _dot(act.astype(BF16), wd_s[...]) + bd_ref[...]
        row = lax.broadcasted_iota(I32, (xb_ref.shape[0], 1), 0)
        yb_ref[...] = jnp.where(jnp.logical_and(row >= lo, row < hi), y, yb_ref[...])


def _moe(seg_blk, seg_exp, seg_lo, seg_hi, xb, w_gate_up, b_gate_up, w_down, b_down):
    P, D = xb.shape
    bm = ROW_BLOCK
    E, _, F2 = w_gate_up.shape
    ex = lambda s, blk, exp, lo, hi: (exp[s], 0, 0)
    rows = lambda s, blk, exp, lo, hi: (blk[s], 0)
    expert_bytes = (D * F2 + D_FF * D) * (2 * 4 + 2)
    block_bytes = bm * D * 4 * 2 * 2 + bm * F2 * 4 * 3
    return pl.pallas_call(
        _moe_kernel,
        grid_spec=pltpu.PrefetchScalarGridSpec(
            num_scalar_prefetch=4,
            grid=(seg_blk.shape[0],),
            in_specs=[pl.BlockSpec((bm, D), rows),
                      pl.BlockSpec((None, D, F2), ex), pl.BlockSpec((None, 1, F2), ex),
                      pl.BlockSpec((None, D_FF, D), ex), pl.BlockSpec((None, 1, D), ex)],
            out_specs=pl.BlockSpec((bm, D), rows),
            scratch_shapes=[pltpu.VMEM((D, F2), BF16), pltpu.VMEM((D_FF, D), BF16), pltpu.SMEM((1,), I32)]),
        out_shape=jax.ShapeDtypeStruct((P, D), F32),
        compiler_params=pltpu.CompilerParams(dimension_semantics=("arbitrary",),
                                             vmem_limit_bytes=expert_bytes + block_bytes),
        name="moe_grouped_ffn",
    )(seg_blk, seg_exp, seg_lo, seg_hi, xb, w_gate_up, b_gate_up.reshape(E, 1, F2), w_down, b_down.reshape(E, 1, D))


def _combine_kernel(dest_ref, next_ref, yb_ref, x2_ref, gate_ref, g3_ref, b3_ref, o_ref, rows_s, sems, *, alpha):
    nt = x2_ref.shape[0]
    i = pl.program_id(0)
    slot = i % 2

    def gather(idx_ref, buf):
        def issue(t, c):
            for k in range(TOP_K):
                _row_copy(yb_ref, rows_s.at[buf, k], sems.at[buf], idx_ref[0, k * nt + t], t).start()
            return c
        lax.fori_loop(0, nt, issue, 0)

    @pl.when(i == 0)
    def _():
        gather(dest_ref, 0)

    @pl.when(i + 1 < pl.num_programs(0))
    def _():
        gather(next_ref, 1 - slot)

    for k in range(TOP_K):
        pltpu.make_async_copy(yb_ref.at[pl.ds(0, nt)], rows_s.at[slot, k], sems.at[slot]).wait()
    gates = gate_ref[...]
    y = gates[:, 0:1] * rows_s[slot, 0]
    for k in range(1, TOP_K):
        y = y + gates[:, k:k + 1] * rows_s[slot, k]
    o_ref[...] = _layer_norm(alpha * x2_ref[...] + y, g3_ref[...], b3_ref[...])


def _combine(yb, dest_tiles, x2, gates_tok, g3, b3, alpha):
    T, D = x2.shape
    steps = dest_tiles.shape[0]
    nt = dest_tiles.shape[2] // TOP_K
    idx_spec = lambda ahead: pl.BlockSpec((None, 1, TOP_K * nt), lambda i: (jnp.minimum(i + ahead, steps - 1), 0, 0),
                                          memory_space=pltpu.SMEM)
    return pl.pallas_call(
        functools.partial(_combine_kernel, alpha=alpha),
        grid=(steps,),
        in_specs=[idx_spec(0), idx_spec(1),
                  pl.BlockSpec(memory_space=pl.ANY),
                  pl.BlockSpec((nt, D), lambda i: (i, 0)),
                  pl.BlockSpec((nt, TOP_K), lambda i: (i, 0)),
                  pl.BlockSpec((1, D), lambda i: (0, 0)), pl.BlockSpec((1, D), lambda i: (0, 0))],
        out_specs=pl.BlockSpec((nt, D), lambda i: (i, 0)),
        out_shape=jax.ShapeDtypeStruct((T, D), F32),
        scratch_shapes=[pltpu.VMEM((2, TOP_K, nt, D), F32), pltpu.SemaphoreType.DMA((2,))],
        compiler_params=pltpu.CompilerParams(dimension_semantics=("arbitrary",)),
        name="moe_combine",
    )(dest_tiles, dest_tiles, yb, x2, gates_tok, g3, b3)


def _rel_bucket(dist):
    max_exact = N_BUCKETS // 2
    df = jnp.maximum(dist, 1).astype(F32)
    large = max_exact + (jnp.log(df / max_exact) / math.log(MAX_DISTANCE / max_exact)
                         * (N_BUCKETS - max_exact)).astype(I32)
    return jnp.where(dist < max_exact, dist, jnp.minimum(large, N_BUCKETS - 1))


def _bias_tables(rel_table):
    n = BAND
    taps = jnp.arange(n + 1)
    by_tap = jnp.stack([rel_table[_rel_bucket(taps * dil)].astype(F32) for _, dil in PATTERNS])
    w = jnp.concatenate([by_tap[:, ::-1], jnp.full((len(PATTERNS), n - 1, N_HEADS), NEG_INF, F32)], axis=1)
    w = jnp.transpose(w, (0, 2, 1))
    return by_tap, w.reshape(len(PATTERNS), N_SLABS, HEADS_PER_SLAB, 2 * n)


def _tap_bias_by_slot(by_tap, W):
    assert PAST_LEN % W == 0 and PAST_LEN >= WINDOW_MAX
    n = BAND
    out = []
    for p, (window, dil) in enumerate(PATTERNS):
        taps = by_tap[p, :0:-1]
        col = jnp.concatenate([jnp.full((W // dil - n, N_HEADS), NEG_INF, F32), taps], axis=0)
        rest = jnp.full((W // dil, dil - 1, N_HEADS), NEG_INF, F32)
        out.append(jnp.concatenate([col[:, None, :], rest], axis=1).reshape(W, N_HEADS).T)
    return jnp.stack(out)


def _dest_tiles(dest, nt):
    T = dest.shape[1]
    return dest.reshape(TOP_K, T // nt, nt).transpose(1, 0, 2).reshape(T // nt, 1, TOP_K * nt)


def kernel(x_prompt, x_sample, mem_prompt, cache_win_k, cache_win_v, state_conv, cache_mem_k, cache_mem_v,
           rel_bias_table, w_in, w_conv, w_out, ln1_g, ln1_b, w_mem_q, w_mem_k, w_mem_v, w_mem_o,
           ln2_g, ln2_b, w_router, b_router, w_gate_up, b_gate_up, w_down, b_down, ln3_g, ln3_b):
    depth = w_in.shape[0]
    assert depth == 1
    alpha = (2 * depth) ** 0.25
    B, S, D = x_prompt.shape
    DB = x_sample.shape[0]
    T = B * S
    l = 0
    vec = lambda a: a[l].reshape(1, -1)

    by_tap, bias_band = _bias_tables(rel_bias_table)
    wrt = w_router[l].T.astype(BF16)
    br = b_router[l].reshape(N_EXPERTS, 1)
    w_in_bf = w_in[l].astype(BF16)
    wo_bf = w_out[l].astype(BF16)
    wq_bf = w_mem_q[l].astype(BF16)
    wmo_bf = w_mem_o[l].astype(BF16)

    xs = x_sample.reshape(DB, D)
    proj = _inproj_sample(xs, w_in_bf)
    cw = CONV_WIDTH
    q_s = proj[:, 3 * cw:3 * cw + ATTN_WIDTH]
    k_s = proj[:, 3 * cw + ATTN_WIDTH:3 * cw + 2 * ATTN_WIDTH]
    v_s = proj[:, 3 * cw + 2 * ATTN_WIDTH:]
    H, E = N_HEADS, HEAD_DIM
    bias_new = by_tap[:, 0, :, None]
    slots_last = lambda c: jnp.transpose(c, (0, 2, 3, 1))
    conv_p, q_p, k_p, v_p, cs_p, attn_s = _inproj_prompt_and_attention_sample(
        x_prompt, w_in_bf, w_conv[l], q_s.reshape(DB, H, E, 1), k_s.reshape(DB, H, E, 1), v_s.reshape(DB, H, E, 1),
        slots_last(cache_win_k[l]), slots_last(cache_win_v[l]), _tap_bias_by_slot(by_tap, WINDOW_MAX), bias_new)
    attn_s = attn_s.reshape(DB, ATTN_WIDTH)

    sc = state_conv[l]
    x1_s, qm_s, u_s = _tail_a_sample(xs, proj, sc[:, 0], sc[:, 1], w_conv[l], attn_s, wo_bf,
                                     vec(ln1_g), vec(ln1_b), wq_bf, alpha)
    attn_p = _attention_prompt(q_p, k_p, v_p, bias_band)
    mk, mv, mk_bf, mv_bf = _memkv_prompt(mem_prompt, w_mem_k[l].astype(BF16), w_mem_v[l].astype(BF16))
    x2_p, idx_p, gate_p, rank_p, cnt_p, o_s = _mid_prompt_and_memattn_sample(
        x_prompt, conv_p, attn_p, wo_bf[:CONV_WIDTH], wo_bf[CONV_WIDTH:], vec(ln1_g), vec(ln1_b), mk_bf, mv_bf,
        wq_bf, wmo_bf, vec(ln2_g), vec(ln2_b), wrt, br,
        qm_s.reshape(DB, MEM_HEADS, MEM_HEAD_DIM), cache_mem_k[l], cache_mem_v[l], alpha)
    x2_s, idx_s, gate_s, rank_s, cnt = _tail_b_sample(x1_s, o_s.reshape(DB, D), wmo_bf, vec(ln2_g), vec(ln2_b), wrt, br,
                                                      cnt_p, alpha)

    bm = ROW_BLOCK
    counts = cnt[:, 0].astype(I32)
    ends = jnp.cumsum(counts)
    starts = ends - counts
    n_tok_rows = (T + DB) * TOP_K
    n_blocks = -(-n_tok_rows // bm)
    P = n_blocks * bm
    cuts = jnp.sort(jnp.concatenate([jnp.arange(n_blocks, dtype=I32) * bm, ends[:-1]]))
    cut_ends = jnp.concatenate([cuts[1:], jnp.full((1,), P, I32)])
    seg_blk = jnp.minimum(cuts // bm, n_blocks - 1)
    seg_exp = jnp.minimum(jnp.sum((ends[None, :] <= cuts[:, None]).astype(I32), axis=1), N_EXPERTS - 1)
    seg_lo = cuts - seg_blk * bm
    seg_hi = cut_ends - seg_blk * bm
    expert_ids = jnp.arange(N_EXPERTS, dtype=I32)[:, None, None]
    start_of = lambda idx: jnp.sum(jnp.where(idx[None] == expert_ids, starts[:, None, None], 0), axis=0)
    dest_p = start_of(idx_p) + rank_p
    dest_s = start_of(idx_s) + rank_s
    dtiles_p = _dest_tiles(dest_p, COMBINE_TILE)
    dtiles_s = _dest_tiles(dest_s, DB)

    x2_pf = x2_p.reshape(T, D)
    xb = _dispatch(x2_pf, dtiles_p, x2_s, dtiles_s, P)
    yb = _moe(seg_blk, seg_exp, seg_lo, seg_hi, xb, w_gate_up[l], b_gate_up[l], w_down[l], b_down[l])
    y_p = _combine(yb, dtiles_p, x2_pf, gate_p.T, vec(ln3_g), vec(ln3_b), alpha).reshape(B, S, D)
    y_s = _combine(yb, dtiles_s, x2_s, gate_s.T, vec(ln3_g), vec(ln3_b), alpha).reshape(DB, 1, D)

    conv_state_s = jnp.stack([sc[:, 1], u_s], axis=1)
    return (y_p, y_s,
            k_p.reshape(1, B, S, H, E), v_p.reshape(1, B, S, H, E), cs_p[:, 6:8].reshape(1, B, CONV_K - 1, CONV_WIDTH),
            mk.reshape(1, B, MEM_TOKENS, MEM_HEADS, MEM_HEAD_DIM), mv.reshape(1, B, MEM_TOKENS, MEM_HEADS, MEM_HEAD_DIM),
            k_s.reshape(1, DB, 1, H, E), v_s.reshape(1, DB, 1, H, E), conv_state_s.reshape(1, DB, CONV_K - 1, CONV_WIDTH))
```

```python
import functools
import math

import jax
import jax.numpy as jnp
import numpy as np
from jax import lax
from jax.experimental import pallas as pl
from jax.experimental.pallas import tpu as pltpu

F32 = jnp.float32
BF16 = jnp.bfloat16
I32 = jnp.int32

D_MODEL = 1024
CONV_WIDTH = 256
CONV_K = 3
HEAD_DIM = 64
N_HEADS = 12
ATTN_WIDTH = N_HEADS * HEAD_DIM
PATTERNS = ((128, 1), (512, 4), (2048, 16))
BAND = 128
WINDOW_MAX = 2048
ATTN_SCALE = 1.0 / math.sqrt(HEAD_DIM)
N_BUCKETS = 32
MAX_DISTANCE = WINDOW_MAX
MEM_TOKENS = 256
MEM_HEADS = 4
MEM_HEAD_DIM = 256
MEM_SCALE = 1.0 / math.sqrt(MEM_HEAD_DIM)
N_EXPERTS = 32
TOP_K = 4
D_FF = 1024
SWIGLU_LIMIT = 7.0
SWIGLU_ALPHA = 1.702
LN_EPS = 1e-5
PAST_LEN = 8192
NEG_INF = -1e30

LANES = 128
HEADS_PER_SLAB = LANES // HEAD_DIM
N_SLABS = ATTN_WIDTH // LANES

PROJ_TILE = 512
SEQ_TILE = 512
ROW_BLOCK = 512
COMBINE_TILE = 256
ATTN_GROUP = 8

HIGHEST = lax.Precision.HIGHEST

_FUSED_PROJ_VMEM_BYTES = 58 * 1024 * 1024


def _layer_norm(x, g, b):
    mu = jnp.mean(x, axis=-1, keepdims=True)
    var = jnp.mean(jnp.square(x - mu), axis=-1, keepdims=True)
    return (x - mu) * lax.rsqrt(var + LN_EPS) * g + b


def _dot(a, b):
    return jnp.dot(a, b, preferred_element_type=F32)


def _dot_nt(a, b):
    return lax.dot_general(a, b, (((1,), (1,)), ((), ())), preferred_element_type=F32)


def _dot_hi(a, b):
    return jnp.dot(a, b, preferred_element_type=F32, precision=HIGHEST)


def _inproj_kernel(x_ref, w_ref, wc_ref, conv_ref, q_ref, k_ref, v_ref, cs_ref, u_s):
    ts = x_ref.shape[0]
    cw = CONV_WIDTH

    @pl.when(pl.program_id(1) == 0)
    def _():
        u_s[0:8, :] = jnp.zeros((8, cw), F32)

    x = x_ref[...].astype(BF16)
    gb = _dot(x, w_ref[:, 0:cw])
    gc = _dot(x, w_ref[:, cw:2 * cw])
    h = _dot(x, w_ref[:, 2 * cw:3 * cw])
    u = gc * h
    u_s[8:8 + ts, :] = u
    wc = wc_ref[...]
    conv = wc[0:1, :] * u_s[6:6 + ts, :] + wc[1:2, :] * u_s[7:7 + ts, :] + wc[2:3, :] * u
    conv_ref[...] = (gb * conv).astype(conv_ref.dtype)
    o = 3 * cw
    q_ref[...] = _dot(x, w_ref[:, o:o + ATTN_WIDTH]) * ATTN_SCALE
    k_ref[...] = _dot(x, w_ref[:, o + ATTN_WIDTH:o + 2 * ATTN_WIDTH])
    v_ref[...] = _dot(x, w_ref[:, o + 2 * ATTN_WIDTH:o + 3 * ATTN_WIDTH])
    tail = u_s[ts:ts + 8, :]
    u_s[0:8, :] = tail
    cs_ref[...] = tail


def _attn_kernel(q_ref, k_ref, v_ref, w_ref, o_ref, bias_ref, m_s, l_s, acc_s):
    S = q_ref.shape[0]
    n = BAND
    lane = lax.broadcasted_iota(I32, (n, LANES), 1)
    head_a = lane < HEAD_DIM

    @pl.when(pl.program_id(1) == 0)
    def _():
        for p in range(len(PATTERNS)):
            for hh in range(HEADS_PER_SLAB):
                first_row = jnp.broadcast_to(w_ref[p, hh:hh + 1, :], (n, 2 * n))
                bias_ref[p, hh * n:(hh + 1) * n, :] = pltpu.roll(first_row, 0, 1, stride=1, stride_axis=0)

    def rows_of(start, count, dil):
        return pl.ds(start, count) if dil == 1 else pl.ds(start, count, stride=dil)

    def blocks(p, dil, specs):
        rows = [rows_of(start, n, dil) for start, _ in specs]
        scores, values = [], []
        for (start, has_prev), r in zip(specs, rows):
            qb = q_ref[r, :]
            q2 = jnp.concatenate([jnp.where(head_a, qb, 0.0), jnp.where(head_a, 0.0, qb)], axis=0).astype(BF16)
            krows = rows_of(start - n * dil, 2 * n, dil) if has_prev else r
            bias = bias_ref[p] if has_prev else bias_ref[p, :, n:2 * n]
            scores.append(_dot_nt(q2, k_ref[krows, :].astype(BF16)) + bias)
            values.append(v_ref[krows, :].astype(BF16))
        ms = [jnp.max(s, axis=-1, keepdims=True) for s in scores]
        es = [jnp.exp(s - m) for s, m in zip(scores, ms)]
        ls = [jnp.sum(e, axis=-1, keepdims=True) for e in es]
        pvs = [_dot(e.astype(BF16), vb) for e, vb in zip(es, values)]
        for r, m, l, pv in zip(rows, ms, ls, pvs):
            m_s[p, r, :] = jnp.where(head_a, m[:n], m[n:])
            l_s[p, r, :] = jnp.where(head_a, l[:n], l[n:])
            acc_s[p, r, :] = jnp.where(head_a, pv[:n], pv[n:])

    for p, (window, dil) in enumerate(PATTERNS):
        nb = (S // dil) // n
        specs = [(i * n * dil + r, i > 0) for r in range(dil) for i in range(nb)]
        for g in range(0, len(specs), ATTN_GROUP):
            blocks(p, dil, specs[g:g + ATTN_GROUP])

    rows_per_step = 256

    def merge(t, c):
        rows = pl.ds(t * rows_per_step, rows_per_step)
        ms = [m_s[p, rows, :] for p in range(len(PATTERNS))]
        m_all = jnp.maximum(jnp.maximum(ms[0], ms[1]), ms[2])
        num = den = None
        for p in range(len(PATTERNS)):
            w = jnp.exp(ms[p] - m_all)
            num = w * acc_s[p, rows, :] if num is None else num + w * acc_s[p, rows, :]
            den = w * l_s[p, rows, :] if den is None else den + w * l_s[p, rows, :]
        o_ref[rows, :] = (num / den).astype(o_ref.dtype)
        return c

    lax.fori_loop(0, S // rows_per_step, merge, 0)


def _attention_prompt(q, k, v, bias_tab):
    B, S, _ = q.shape
    slab = lambda p, b: (b, 0, p)
    spec = pl.BlockSpec((None, S, LANES), slab)
    return pl.pallas_call(
        _attn_kernel,
        grid=(N_SLABS, B),
        in_specs=[spec, spec, spec,
                  pl.BlockSpec((len(PATTERNS), None, HEADS_PER_SLAB, 2 * BAND), lambda p, b: (0, p, 0, 0))],
        out_specs=spec,
        out_shape=jax.ShapeDtypeStruct((B, S, ATTN_WIDTH), BF16),
        scratch_shapes=[pltpu.VMEM((len(PATTERNS), HEADS_PER_SLAB * BAND, 2 * BAND), F32)]
        + [pltpu.VMEM((len(PATTERNS), S, LANES), F32)] * 3,
        compiler_params=pltpu.CompilerParams(dimension_semantics=("arbitrary", "arbitrary")),
        name="dilated_attn_prompt",
    )(q, k, v, bias_tab)


def _memkv_kernel(m_ref, wk_ref, wv_ref, k_ref, v_ref, kb_ref, vb_ref):
    x = m_ref[...].astype(BF16)
    k = _dot(x, wk_ref[...])
    v = _dot(x, wv_ref[...])
    k_ref[...] = k
    v_ref[...] = v
    kb_ref[...] = k.astype(BF16)
    vb_ref[...] = v.astype(BF16)


def _memkv_prompt(mem, wk_bf, wv_bf):
    B, M, D = mem.shape
    row = lambda b: (b, 0, 0)
    full = lambda b: (0, 0)
    return pl.pallas_call(
        _memkv_kernel,
        grid=(B,),
        in_specs=[pl.BlockSpec((None, M, D), row), pl.BlockSpec((D, D), full), pl.BlockSpec((D, D), full)],
        out_specs=[pl.BlockSpec((None, M, D), row)] * 4,
        out_shape=[jax.ShapeDtypeStruct((B, M, D), F32)] * 2 + [jax.ShapeDtypeStruct((B, M, D), BF16)] * 2,
        compiler_params=pltpu.CompilerParams(dimension_semantics=("arbitrary",)),
        name="memkv_prompt",
    )(mem, wk_bf, wv_bf)


def _earlier_token_matrix(nt):
    before = lax.broadcasted_iota(I32, (nt, nt), 0) < lax.broadcasted_iota(I32, (nt, nt), 1)
    return jnp.where(before, 1.0, 0.0).astype(BF16)


def _route(logits_t, carry, earlier):
    E, nt = logits_t.shape
    eidx = lax.broadcasted_iota(I32, (E, nt), 0)
    l = logits_t
    vals, idxs = [], []
    for _ in range(TOP_K):
        m = jnp.max(l, axis=0, keepdims=True)
        sel = jnp.min(jnp.where(l == m, eidx, E), axis=0, keepdims=True)
        vals.append(m)
        idxs.append(sel)
        l = jnp.where(eidx == sel, -jnp.inf, l)
    es = [jnp.exp(v - vals[0]) for v in vals]
    den = es[0] + es[1] + es[2] + es[3]
    gates = _stack_rows([e / den for e in es])
    chosen = (l == -jnp.inf)
    onehot = jnp.where(chosen, 1.0, 0.0)
    prefix = _dot(onehot.astype(BF16), earlier) + carry
    ranks = [jnp.sum(jnp.where(eidx == s, prefix, 0.0), axis=0, keepdims=True) for s in idxs]
    rank = _stack_rows(ranks).astype(I32)
    idx = _stack_rows(idxs)
    return idx, gates, rank, carry + jnp.sum(onehot, axis=1, keepdims=True)


def _stack_rows(rows):
    k, nt = len(rows), rows[0].shape[1]
    r = lax.broadcasted_iota(I32, (k, nt), 0)
    out = jnp.broadcast_to(rows[-1], (k, nt))
    for i in range(k - 2, -1, -1):
        out = jnp.where(r == i, rows[i], out)
    return out


def _mid_kernel(x_ref, conv_ref, attn_ref, woc_ref, woa_ref, g1_ref, b1_ref, mk_ref, mv_ref, wq_ref, wmo_ref,
                g2_ref, b2_ref, wrt_ref, br_ref,
                x2_ref, idx_ref, gate_ref, rank_ref, cnt_ref, carry_s, earlier_s, *, alpha):
    first = jnp.logical_and(pl.program_id(0) == 0, pl.program_id(1) == 0)

    @pl.when(first)
    def _():
        carry_s[...] = jnp.zeros_like(carry_s)
        earlier_s[...] = _earlier_token_matrix(earlier_s.shape[0])

    x = x_ref[...]
    mix = _dot(conv_ref[...], woc_ref[...]) + _dot(attn_ref[...], woa_ref[...])
    x1 = _layer_norm(alpha * x + mix, g1_ref[...], b1_ref[...])
    qm = (_dot(x1.astype(BF16), wq_ref[...]) * MEM_SCALE).astype(BF16)
    outs = []
    for h in range(MEM_HEADS):
        hs = slice(h * MEM_HEAD_DIM, (h + 1) * MEM_HEAD_DIM)
        s = _dot_nt(qm[:, hs], mk_ref[:, hs])
        m = jnp.max(s, axis=-1, keepdims=True)
        e = jnp.exp(s - m)
        l = jnp.sum(e, axis=-1, keepdims=True)
        outs.append((_dot(e.astype(BF16), mv_ref[:, hs]) / l).astype(BF16))
    o = jnp.concatenate(outs, axis=-1)
    x2 = _layer_norm(alpha * x1 + _dot(o, wmo_ref[...]), g2_ref[...], b2_ref[...])
    x2_ref[...] = x2
    logits_t = _dot_nt(wrt_ref[...], x2.astype(BF16)) + br_ref[...]
    idx, gates, rank, carry = _route(logits_t, carry_s[:, 0:1], earlier_s[...])
    idx_ref[...] = idx
    gate_ref[...] = gates
    rank_ref[...] = rank
    carry_s[...] = jnp.broadcast_to(carry, carry_s.shape)
    cnt_ref[...] = carry_s[...]


def _round_bf16(x):
    return x.astype(BF16).astype(F32)


def _inproj_sample_kernel(x_ref, w_ref, o_ref):
    o_ref[...] = _dot(x_ref[...].astype(BF16), w_ref[...])


def _inproj_sample(x, w_in):
    n, D = x.shape
    N = w_in.shape[1]
    bn = 512
    return pl.pallas_call(
        _inproj_sample_kernel,
        grid=(N // bn,),
        in_specs=[pl.BlockSpec((n, D), lambda j: (0, 0)), pl.BlockSpec((D, bn), lambda j: (0, j))],
        out_specs=pl.BlockSpec((n, bn), lambda j: (0, j)),
        out_shape=jax.ShapeDtypeStruct((n, N), F32),
        compiler_params=pltpu.CompilerParams(dimension_semantics=("arbitrary",)),
        name="inproj_sample",
    )(x, w_in)


def _attn_sample_kernel(q_ref, kn_ref, vn_ref, kt_ref, vt_ref, bias_ref, bnew_ref, o_ref):
    H, E, W = kt_ref.shape

    def columns(ref):
        t = ref[...].T
        return jnp.stack([t[:E, h:h + 1] for h in range(H)], axis=0)

    q = _round_bf16(columns(q_ref))
    v_new = _round_bf16(columns(vn_ref))
    s_all = jnp.sum(_round_bf16(kt_ref[...]) * q, axis=1) * ATTN_SCALE
    s_new = jnp.sum(_round_bf16(columns(kn_ref)) * q, axis=1) * ATTN_SCALE
    outs, lses = [], []
    for p, (window, dil) in enumerate(PATTERNS):
        lo = W - window
        s = s_all[:, lo:] + bias_ref[p, :, lo:]
        sn = s_new + bnew_ref[p]
        m = jnp.maximum(jnp.max(s, axis=-1, keepdims=True), sn)
        e = jnp.exp(s - m)
        en = jnp.exp(sn - m)
        den = jnp.sum(e, axis=-1, keepdims=True) + en
        pr = _round_bf16(e / den)
        pv = jnp.sum(_round_bf16(vt_ref[:, :, lo:]) * pr[:, None, :], axis=-1, keepdims=True)
        outs.append(pv + _round_bf16(en / den)[:, :, None] * v_new)
        lses.append(m + jnp.log(den))
    lmax = jnp.maximum(jnp.maximum(lses[0], lses[1]), lses[2])
    ws = [jnp.exp(ls - lmax) for ls in lses]
    wsum = ws[0] + ws[1] + ws[2]
    acc = None
    for p in range(len(PATTERNS)):
        term = _round_bf16(ws[p] / wsum)[:, :, None] * _round_bf16(outs[p])
        acc = term if acc is None else acc + term
    lane = lax.broadcasted_iota(I32, o_ref.shape, 1)
    out = jnp.zeros(o_ref.shape, F32)
    for h in range(H):
        out = jnp.where(lane == h, acc[h], out)
    o_ref[...] = out


def _inproj_and_sample_attn_kernel(x_ref, w_ref, wc_ref, q4_ref, kn_ref, vn_ref, kt_ref, vt_ref, bpos_ref, bnew_ref,
                                   conv_ref, q_ref, k_ref, v_ref, cs_ref, attn_ref, u_s):
    _inproj_kernel(x_ref, w_ref, wc_ref, conv_ref, q_ref, k_ref, v_ref, cs_ref, u_s)
    _attn_sample_kernel(q4_ref, kn_ref, vn_ref, kt_ref, vt_ref, bpos_ref, bnew_ref, attn_ref)


def _inproj_prompt_and_attention_sample(x, w_in_bf, w_conv, q, k_new, v_new, win_kt, win_vt, bias_pos, bias_new):
    B, S, D = x.shape
    ts = PROJ_TILE
    n_s = S // ts
    DB, H, E, W = win_kt.shape
    assert B * n_s == DB
    row = lambda b, s: (b, s, 0)
    tok = lambda b, s: (b * n_s + s, 0, 0, 0)
    vec = pl.BlockSpec((None, LANES, LANES), lambda b, s: (b * n_s + s, 0, 0))
    vec_out = pl.BlockSpec((None, E, LANES), lambda b, s: (b * n_s + s, 0, 0))
    cache = pl.BlockSpec((None, H, E, W), tok)
    const = lambda a: pl.BlockSpec(a.shape, lambda b, s: (0,) * a.ndim)
    return pl.pallas_call(
        _inproj_and_sample_attn_kernel,
        grid=(B, n_s),
        in_specs=[pl.BlockSpec((None, ts, D), row), const(w_in_bf), const(w_conv),
                  vec, vec, vec, cache, cache, const(bias_pos), const(bias_new)],
        out_specs=[pl.BlockSpec((None, ts, CONV_WIDTH), row),
                   pl.BlockSpec((None, ts, ATTN_WIDTH), row),
                   pl.BlockSpec((None, ts, ATTN_WIDTH), row),
                   pl.BlockSpec((None, ts, ATTN_WIDTH), row),
                   pl.BlockSpec((None, 8, CONV_WIDTH), lambda b, s: (b, 0, 0)),
                   vec_out],
        out_shape=[jax.ShapeDtypeStruct((B, S, CONV_WIDTH), BF16),
                   jax.ShapeDtypeStruct((B, S, ATTN_WIDTH), F32),
                   jax.ShapeDtypeStruct((B, S, ATTN_WIDTH), F32),
                   jax.ShapeDtypeStruct((B, S, ATTN_WIDTH), F32),
                   jax.ShapeDtypeStruct((B, 8, CONV_WIDTH), F32),
                   jax.ShapeDtypeStruct((DB, E, LANES), F32)],
        scratch_shapes=[pltpu.VMEM((ts + 8, CONV_WIDTH), F32)],
        compiler_params=pltpu.CompilerParams(dimension_semantics=("arbitrary", "arbitrary"),
                                             vmem_limit_bytes=_FUSED_PROJ_VMEM_BYTES),
        name="inproj_prompt_attn_sample",
    )(x, w_in_bf, w_conv, q, k_new, v_new, win_kt, win_vt, bias_pos, bias_new)


def _tail_a_sample_kernel(x_ref, proj_ref, s0_ref, s1_ref, wc_ref, attn_ref, wo_ref, g1_ref, b1_ref, wq_ref,
                          x1_ref, qm_ref, u_ref, *, alpha):
    cw = CONV_WIDTH
    gb = proj_ref[:, 0:cw]
    u = proj_ref[:, cw:2 * cw] * proj_ref[:, 2 * cw:3 * cw]
    u_ref[...] = u
    wc = wc_ref[...]
    conv = gb * (wc[0:1, :] * s0_ref[...] + wc[1:2, :] * s1_ref[...] + wc[2:3, :] * u)
    mixed = jnp.concatenate([conv, attn_ref[...]], axis=-1).astype(BF16)
    x1 = _layer_norm(alpha * x_ref[...] + _dot(mixed, wo_ref[...]), g1_ref[...], b1_ref[...])
    x1_ref[...] = x1
    qm_ref[...] = _dot(x1.astype(BF16), wq_ref[...])


def _tail_a_sample(x, proj, s0, s1, w_conv, attn, w_out, g1, b1, w_mem_q, alpha):
    n, D = x.shape
    return pl.pallas_call(
        functools.partial(_tail_a_sample_kernel, alpha=alpha),
        out_shape=[jax.ShapeDtypeStruct((n, D), F32)] * 2 + [jax.ShapeDtypeStruct((n, CONV_WIDTH), F32)],
        name="tail_a_sample",
    )(x, proj, s0, s1, w_conv, attn, w_out, g1, b1, w_mem_q)


def _memattn_sample_kernel(q_ref, k_ref, v_ref, o_ref):
    q = _round_bf16(q_ref[...])
    s = jnp.sum(_round_bf16(k_ref[...]) * q[None], axis=-1, keepdims=True) * MEM_SCALE
    m = jnp.max(s, axis=0)
    e = jnp.exp(s - m[None])
    den = jnp.sum(e, axis=0)
    pr = _round_bf16(e / den[None])
    o_ref[...] = jnp.sum(pr * _round_bf16(v_ref[...]), axis=0)


_N_MID_IN = 15


def _mid_and_sample_memattn_kernel(*refs, alpha):
    mid_in, (qm_ref, smk_ref, smv_ref) = refs[:_N_MID_IN], refs[_N_MID_IN:_N_MID_IN + 3]
    mid_out, so_ref, (carry_s, earlier_s) = refs[_N_MID_IN + 3:-3], refs[-3], refs[-2:]
    _mid_kernel(*mid_in, *mid_out, carry_s, earlier_s, alpha=alpha)
    _memattn_sample_kernel(qm_ref, smk_ref, smv_ref, so_ref)


def _mid_prompt_and_memattn_sample(x, conv, attn, woc, woa, g1, b1, mk, mv, wq, wmo, g2, b2, wrt, br,
                                   qm_s, mem_k_s, mem_v_s, alpha):
    B, S, D = x.shape
    ts = SEQ_TILE
    n_s = S // ts
    T = B * S
    DB, M, H, E = mem_k_s.shape
    assert B * n_s == DB
    row = lambda b, s: (b, s, 0)
    full = lambda b, s: (0, 0)
    tok = lambda b, s: (0, b * n_s + s)
    mem = lambda b, s: (b, 0, 0)
    vec = pl.BlockSpec((1, D), full)
    s_vec = pl.BlockSpec((None, H, E), lambda b, s: (b * n_s + s, 0, 0))
    s_mem = pl.BlockSpec((None, M, H, E), lambda b, s: (b * n_s + s, 0, 0, 0))
    in_specs = [pl.BlockSpec((None, ts, D), row),
                pl.BlockSpec((None, ts, CONV_WIDTH), row),
                pl.BlockSpec((None, ts, ATTN_WIDTH), row),
                pl.BlockSpec(woc.shape, full), pl.BlockSpec(woa.shape, full), vec, vec,
                pl.BlockSpec((None, MEM_TOKENS, D), mem), pl.BlockSpec((None, MEM_TOKENS, D), mem),
                pl.BlockSpec(wq.shape, full), pl.BlockSpec(wmo.shape, full), vec, vec,
                pl.BlockSpec(wrt.shape, full), pl.BlockSpec(br.shape, full)]
    assert len(in_specs) == _N_MID_IN
    return pl.pallas_call(
        functools.partial(_mid_and_sample_memattn_kernel, alpha=alpha),
        grid=(B, n_s),
        in_specs=in_specs + [s_vec, s_mem, s_mem],
        out_specs=[pl.BlockSpec((None, ts, D), row),
                   pl.BlockSpec((TOP_K, ts), tok), pl.BlockSpec((TOP_K, ts), tok), pl.BlockSpec((TOP_K, ts), tok),
                   pl.BlockSpec((N_EXPERTS, LANES), full), s_vec],
        out_shape=[jax.ShapeDtypeStruct((B, S, D), F32),
                   jax.ShapeDtypeStruct((TOP_K, T), I32),
                   jax.ShapeDtypeStruct((TOP_K, T), F32),
                   jax.ShapeDtypeStruct((TOP_K, T), I32),
                   jax.ShapeDtypeStruct((N_EXPERTS, LANES), F32),
                   jax.ShapeDtypeStruct((DB, H, E), F32)],
        scratch_shapes=[pltpu.VMEM((N_EXPERTS, LANES), F32), pltpu.VMEM((ts, ts), BF16)],
        compiler_params=pltpu.CompilerParams(dimension_semantics=("arbitrary", "arbitrary")),
        name="mid_prompt_memattn_sample",
    )(x, conv, attn, woc, woa, g1, b1, mk, mv, wq, wmo, g2, b2, wrt, br, qm_s, mem_k_s, mem_v_s)


def _tail_b_sample_kernel(x1_ref, o_ref, wmo_ref, g2_ref, b2_ref, wrt_ref, br_ref, cnt_ref,
                          x2_ref, idx_ref, gate_ref, rank_ref, cnt_out_ref, *, alpha):
    x2 = _layer_norm(alpha * x1_ref[...] + _dot(o_ref[...].astype(BF16), wmo_ref[...]), g2_ref[...], b2_ref[...])
    x2_ref[...] = x2
    logits_t = _dot_nt(wrt_ref[...], x2.astype(BF16)) + br_ref[...]
    idx, gates, rank, carry = _route(logits_t, cnt_ref[:, 0:1], _earlier_token_matrix(logits_t.shape[1]))
    idx_ref[...] = idx
    gate_ref[...] = gates
    rank_ref[...] = rank
    cnt_out_ref[...] = jnp.broadcast_to(carry, cnt_out_ref.shape)


def _tail_b_sample(x1, o, w_mem_o, g2, b2, wrt, br, cnt, alpha):
    n, D = x1.shape
    return pl.pallas_call(
        functools.partial(_tail_b_sample_kernel, alpha=alpha),
        out_shape=[jax.ShapeDtypeStruct((n, D), F32),
                   jax.ShapeDtypeStruct((TOP_K, n), I32),
                   jax.ShapeDtypeStruct((TOP_K, n), F32),
                   jax.ShapeDtypeStruct((TOP_K, n), I32),
                   jax.ShapeDtypeStruct(cnt.shape, F32)],
        name="tail_b_sample",
    )(x1, o, w_mem_o, g2, b2, wrt, br, cnt)


def _row_copy(src, dst, sem, src_row, dst_row):
    return pltpu.make_async_copy(src.at[pl.ds(src_row, 1)], dst.at[pl.ds(dst_row, 1)], sem)


def _dispatch_kernel(dest_ref, x_ref, dest2_ref, x2_ref, xb_ref, zeros_s, stage_s, sems, zsem, *, n_tok_rows):
    i = pl.program_id(0)
    last = pl.num_programs(0) - 1
    slot = i % 2
    nt = dest_ref.shape[1] // TOP_K

    def start(src_ref, first, count, idx_ref, sem):
        def issue(t, c):
            for k in range(TOP_K):
                _row_copy(src_ref, xb_ref, sem, first + t, idx_ref[0, k * count + t]).start()
            return c

        lax.fori_loop(0, count, issue, 0)

    def drain(src_ref, count, sem):
        for k in range(TOP_K):
            pltpu.make_async_copy(src_ref.at[pl.ds(0, count)], xb_ref.at[pl.ds(0, count)], sem).wait()

    @pl.when(i < last)
    def _():
        stage_s[slot] = x_ref[...]
        start(stage_s.at[slot], 0, nt, dest_ref, sems.at[slot])

    @pl.when(i == last)
    def _():
        zeros_s[...] = jnp.zeros_like(zeros_s)
        pltpu.make_async_copy(zeros_s, xb_ref.at[pl.ds(n_tok_rows, zeros_s.shape[0])], zsem).start()
        start(x2_ref, 0, x2_ref.shape[0], dest2_ref, sems.at[slot])

    @pl.when(i > 0)
    def _():
        drain(stage_s.at[1 - slot], nt, sems.at[1 - slot])

    @pl.when(i == last)
    def _():
        drain(x2_ref, x2_ref.shape[0], sems.at[slot])
        pltpu.make_async_copy(zeros_s, xb_ref.at[pl.ds(n_tok_rows, zeros_s.shape[0])], zsem).wait()


def _dispatch(x_a, dest_a, x_b, dest_b, n_rows):
    Ta, D = x_a.shape
    Tb = x_b.shape[0]
    nt = dest_a.shape[2] // TOP_K
    steps = Ta // nt
    n_tok_rows = (Ta + Tb) * TOP_K
    tile = lambda i: (jnp.minimum(i, steps - 1), 0, 0)
    return pl.pallas_call(
        functools.partial(_dispatch_kernel, n_tok_rows=n_tok_rows),
        grid=(steps + 1,),
        in_specs=[pl.BlockSpec((None, 1, TOP_K * nt), tile, memory_space=pltpu.SMEM),
                  pl.BlockSpec((nt, D), lambda i: (jnp.minimum(i, steps - 1), 0)),
                  pl.BlockSpec((None, 1, TOP_K * Tb), lambda i: (0, 0, 0), memory_space=pltpu.SMEM),
                  pl.BlockSpec((Tb, D), lambda i: (0, 0))],
        out_specs=pl.BlockSpec(memory_space=pl.ANY),
        out_shape=jax.ShapeDtypeStruct((n_rows, D), x_a.dtype),
        scratch_shapes=[pltpu.VMEM((n_rows - n_tok_rows, D), x_a.dtype), pltpu.VMEM((2, nt, D), x_a.dtype),
                        pltpu.SemaphoreType.DMA((2,)), pltpu.SemaphoreType.DMA],
        compiler_params=pltpu.CompilerParams(dimension_semantics=("arbitrary",)),
        name="moe_dispatch",
    )(dest_a, x_a, dest_b, x_b)


def _moe_kernel(blk_ref, exp_ref, lo_ref, hi_ref, xb_ref, wgu_ref, bgu_ref, wd_ref, bd_ref, yb_ref,
                wgu_s, wd_s, cast_s):
    s = pl.program_id(0)
    new_block = jnp.logical_or(s == 0, blk_ref[s] != blk_ref[jnp.maximum(s - 1, 0)])
    lo, hi = lo_ref[s], hi_ref[s]
    used = hi > lo

    @pl.when(s == 0)
    def _():
        cast_s[0] = -1

    @pl.when(jnp.logical_and(used, cast_s[0] != exp_ref[s]))
    def _():
        wgu_s[...] = wgu_ref[...].astype(BF16)
        wd_s[...] = wd_ref[...].astype(BF16)
        cast_s[0] = exp_ref[s]

    @pl.when(new_block)
    def _():
        yb_ref[...] = jnp.zeros_like(yb_ref)

    @pl.when(used)
    def _():
        gu = _dot(xb_ref[...].astype(BF16), wgu_s[...]) + bgu_ref[...]
        g = jnp.minimum(gu[:, :D_FF], SWIGLU_LIMIT)
        u = jnp.clip(gu[:, D_FF:], -SWIGLU_LIMIT, SWIGLU_LIMIT)
        act = (u + 1.0) * g * jax.nn.sigmoid(SWIGLU_ALPHA * g)
        y = _dot(act.astype(BF16), wd_s[...]) + bd_ref[...]
        row = lax.broadcasted_iota(I32, (xb_ref.shape[0], 1), 0)
        yb_ref[...] = jnp.where(jnp.logical_and(row >= lo, row < hi), y, yb_ref[...])


def _moe(seg_blk, seg_exp, seg_lo, seg_hi, xb, w_gate_up, b_gate_up, w_down, b_down):
    P, D = xb.shape
    bm = ROW_BLOCK
    E, _, F2 = w_gate_up.shape
    ex = lambda s, blk, exp, lo, hi: (exp[s], 0, 0)
    rows = lambda s, blk, exp, lo, hi: (blk[s], 0)
    expert_bytes = (D * F2 + D_FF * D) * (2 * 4 + 2)
    block_bytes = bm * D * 4 * 2 * 2 + bm * F2 * 4 * 3
    return pl.pallas_call(
        _moe_kernel,
        grid_spec=pltpu.PrefetchScalarGridSpec(
            num_scalar_prefetch=4,
            grid=(seg_blk.shape[0],),
            in_specs=[pl.BlockSpec((bm, D), rows),
                      pl.BlockSpec((None, D, F2), ex), pl.BlockSpec((None, 1, F2), ex),
                      pl.BlockSpec((None, D_FF, D), ex), pl.BlockSpec((None, 1, D), ex)],
            out_specs=pl.BlockSpec((bm, D), rows),
            scratch_shapes=[pltpu.VMEM((D, F2), BF16), pltpu.VMEM((D_FF, D), BF16), pltpu.SMEM((1,), I32)]),
        out_shape=jax.ShapeDtypeStruct((P, D), F32),
        compiler_params=pltpu.CompilerParams(dimension_semantics=("arbitrary",),
                                             vmem_limit_bytes=expert_bytes + block_bytes),
        name="moe_grouped_ffn",
    )(seg_blk, seg_exp, seg_lo, seg_hi, xb, w_gate_up, b_gate_up.reshape(E, 1, F2), w_down, b_down.reshape(E, 1, D))


def _combine_kernel(dest_ref, next_ref, yb_ref, x2_ref, gate_ref, g3_ref, b3_ref, o_ref, rows_s, sems, *, alpha):
    nt = x2_ref.shape[0]
    i = pl.program_id(0)
    slot = i % 2

    def gather(idx_ref, buf):
        def issue(t, c):
            for k in range(TOP_K):
                _row_copy(yb_ref, rows_s.at[buf, k], sems.at[buf], idx_ref[0, k * nt + t], t).start()
            return c
        lax.fori_loop(0, nt, issue, 0)

    @pl.when(i == 0)
    def _():
        gather(dest_ref, 0)

    @pl.when(i + 1 < pl.num_programs(0))
    def _():
        gather(next_ref, 1 - slot)

    for k in range(TOP_K):
        pltpu.make_async_copy(yb_ref.at[pl.ds(0, nt)], rows_s.at[slot, k], sems.at[slot]).wait()
    gates = gate_ref[...]
    y = gates[:, 0:1] * rows_s[slot, 0]
    for k in range(1, TOP_K):
        y = y + gates[:, k:k + 1] * rows_s[slot, k]
    o_ref[...] = _layer_norm(alpha * x2_ref[...] + y, g3_ref[...], b3_ref[...])


def _combine(yb, dest_tiles, x2, gates_tok, g3, b3, alpha):
    T, D = x2.shape
    steps = dest_tiles.shape[0]
    nt = dest_tiles.shape[2] // TOP_K
    idx_spec = lambda ahead: pl.BlockSpec((None, 1, TOP_K * nt), lambda i: (jnp.minimum(i + ahead, steps - 1), 0, 0),
                                          memory_space=pltpu.SMEM)
    return pl.pallas_call(
        functools.partial(_combine_kernel, alpha=alpha),
        grid=(steps,),
        in_specs=[idx_spec(0), idx_spec(1),
                  pl.BlockSpec(memory_space=pl.ANY),
                  pl.BlockSpec((nt, D), lambda i: (i, 0)),
                  pl.BlockSpec((nt, TOP_K), lambda i: (i, 0)),
                  pl.BlockSpec((1, D), lambda i: (0, 0)), pl.BlockSpec((1, D), lambda i: (0, 0))],
        out_specs=pl.BlockSpec((nt, D), lambda i: (i, 0)),
        out_shape=jax.ShapeDtypeStruct((T, D), F32),
        scratch_shapes=[pltpu.VMEM((2, TOP_K, nt, D), F32), pltpu.SemaphoreType.DMA((2,))],
        compiler_params=pltpu.CompilerParams(dimension_semantics=("arbitrary",)),
        name="moe_combine",
    )(dest_tiles, dest_tiles, yb, x2, gates_tok, g3, b3)


def _rel_bucket(dist):
    max_exact = N_BUCKETS // 2
    df = jnp.maximum(dist, 1).astype(F32)
    large = max_exact + (jnp.log(df / max_exact) / math.log(MAX_DISTANCE / max_exact)
                         * (N_BUCKETS - max_exact)).astype(I32)
    return jnp.where(dist < max_exact, dist, jnp.minimum(large, N_BUCKETS - 1))


def _bias_tables(rel_table):
    n = BAND
    taps = jnp.arange(n + 1)
    by_tap = jnp.stack([rel_table[_rel_bucket(taps * dil)].astype(F32) for _, dil in PATTERNS])
    w = jnp.concatenate([by_tap[:, ::-1], jnp.full((len(PATTERNS), n - 1, N_HEADS), NEG_INF, F32)], axis=1)
    w = jnp.transpose(w, (0, 2, 1))
    return by_tap, w.reshape(len(PATTERNS), N_SLABS, HEADS_PER_SLAB, 2 * n)


def _tap_bias_by_slot(by_tap, W):
    assert PAST_LEN % W == 0 and PAST_LEN >= WINDOW_MAX
    n = BAND
    out = []
    for p, (window, dil) in enumerate(PATTERNS):
        taps = by_tap[p, :0:-1]
        col = jnp.concatenate([jnp.full((W // dil - n, N_HEADS), NEG_INF, F32), taps], axis=0)
        rest = jnp.full((W // dil, dil - 1, N_HEADS), NEG_INF, F32)
        out.append(jnp.concatenate([col[:, None, :], rest], axis=1).reshape(W, N_HEADS).T)
    return jnp.stack(out)


def _dest_tiles(dest, nt):
    T = dest.shape[1]
    return dest.reshape(TOP_K, T // nt, nt).transpose(1, 0, 2).reshape(T // nt, 1, TOP_K * nt)


def kernel(x_prompt, x_sample, mem_prompt, cache_win_k, cache_win_v, state_conv, cache_mem_k, cache_mem_v,
           rel_bias_table, w_in, w_conv, w_out, ln1_g, ln1_b, w_mem_q, w_mem_k, w_mem_v, w_mem_o,
           ln2_g, ln2_b, w_router, b_router, w_gate_up, b_gate_up, w_down, b_down, ln3_g, ln3_b):
    depth = w_in.shape[0]
    assert depth == 1
    alpha = (2 * depth) ** 0.25
    B, S, D = x_prompt.shape
    DB = x_sample.shape[0]
    T = B * S
    l = 0
    vec = lambda a: a[l].reshape(1, -1)

    by_tap, bias_band = _bias_tables(rel_bias_table)
    wrt = w_router[l].T.astype(BF16)
    br = b_router[l].reshape(N_EXPERTS, 1)
    w_in_bf = w_in[l].astype(BF16)
    wo_bf = w_out[l].astype(BF16)
    wq_bf = w_mem_q[l].astype(BF16)
    wmo_bf = w_mem_o[l].astype(BF16)

    xs = x_sample.reshape(DB, D)
    proj = _inproj_sample(xs, w_in_bf)
    cw = CONV_WIDTH
    q_s = proj[:, 3 * cw:3 * cw + ATTN_WIDTH]
    k_s = proj[:, 3 * cw + ATTN_WIDTH:3 * cw + 2 * ATTN_WIDTH]
    v_s = proj[:, 3 * cw + 2 * ATTN_WIDTH:]
    H, E = N_HEADS, HEAD_DIM
    bias_new = by_tap[:, 0, :, None]
    slots_last = lambda c: jnp.transpose(c, (0, 2, 3, 1))
    corner = lambda a: jnp.pad(a.reshape(DB, H, E), ((0, 0), (0, LANES - H), (0, LANES - E)))
    conv_p, q_p, k_p, v_p, cs_p, attn_s = _inproj_prompt_and_attention_sample(
        x_prompt, w_in_bf, w_conv[l], corner(q_s), corner(k_s), corner(v_s),
        slots_last(cache_win_k[l]), slots_last(cache_win_v[l]), _tap_bias_by_slot(by_tap, WINDOW_MAX), bias_new)
    attn_s = jnp.transpose(attn_s[:, :, :H], (0, 2, 1)).reshape(DB, ATTN_WIDTH)

    sc = state_conv[l]
    x1_s, qm_s, u_s = _tail_a_sample(xs, proj, sc[:, 0], sc[:, 1], w_conv[l], attn_s, wo_bf,
                                     vec(ln1_g), vec(ln1_b), wq_bf, alpha)
    attn_p = _attention_prompt(q_p, k_p, v_p, bias_band)
    mk, mv, mk_bf, mv_bf = _memkv_prompt(mem_prompt, w_mem_k[l].astype(BF16), w_mem_v[l].astype(BF16))
    x2_p, idx_p, gate_p, rank_p, cnt_p, o_s = _mid_prompt_and_memattn_sample(
        x_prompt, conv_p, attn_p, wo_bf[:CONV_WIDTH], wo_bf[CONV_WIDTH:], vec(ln1_g), vec(ln1_b), mk_bf, mv_bf,
        wq_bf, wmo_bf, vec(ln2_g), vec(ln2_b), wrt, br,
        qm_s.reshape(DB, MEM_HEADS, MEM_HEAD_DIM), cache_mem_k[l], cache_mem_v[l], alpha)
    x2_s, idx_s, gate_s, rank_s, cnt = _tail_b_sample(x1_s, o_s.reshape(DB, D), wmo_bf, vec(ln2_g), vec(ln2_b), wrt, br,
                                                      cnt_p, alpha)

    bm = ROW_BLOCK
    counts = cnt[:, 0].astype(I32)
    ends = jnp.cumsum(counts)
    starts = ends - counts
    n_tok_rows = (T + DB) * TOP_K
    n_blocks = -(-n_tok_rows // bm)
    P = n_blocks * bm
    cuts = jnp.sort(jnp.concatenate([jnp.arange(n_blocks, dtype=I32) * bm, ends[:-1]]))
    cut_ends = jnp.concatenate([cuts[1:], jnp.full((1,), P, I32)])
    seg_blk = jnp.minimum(cuts // bm, n_blocks - 1)
    seg_exp = jnp.minimum(jnp.sum((ends[None, :] <= cuts[:, None]).astype(I32), axis=1), N_EXPERTS - 1)
    seg_lo = cuts - seg_blk * bm
    seg_hi = cut_ends - seg_blk * bm
    expert_ids = jnp.arange(N_EXPERTS, dtype=I32)[:, None, None]
    start_of = lambda idx: jnp.sum(jnp.where(idx[None] == expert_ids, starts[:, None, None], 0), axis=0)
    dest_p = start_of(idx_p) + rank_p
    dest_s = start_of(idx_s) + rank_s
    dtiles_p = _dest_tiles(dest_p, COMBINE_TILE)
    dtiles_s = _dest_tiles(dest_s, DB)

    x2_pf = x2_p.reshape(T, D)
    xb = _dispatch(x2_pf, dtiles_p, x2_s, dtiles_s, P)
    yb = _moe(seg_blk, seg_exp, seg_lo, seg_hi, xb, w_gate_up[l], b_gate_up[l], w_down[l], b_down[l])
    y_p = _combine(yb, dtiles_p, x2_pf, gate_p.T, vec(ln3_g), vec(ln3_b), alpha).reshape(B, S, D)
    y_s = _combine(yb, dtiles_s, x2_s, gate_s.T, vec(ln3_g), vec(ln3_b), alpha).reshape(DB, 1, D)

    conv_state_s = jnp.stack([sc[:, 1], u_s], axis=1)
    return (y_p, y_s,
            k_p.reshape(1, B, S, H, E), v_p.reshape(1, B, S, H, E), cs_p[:, 6:8].reshape(1, B, CONV_K - 1, CONV_WIDTH),
            mk.reshape(1, B, MEM_TOKENS, MEM_HEADS, MEM_HEAD_DIM), mv.reshape(1, B, MEM_TOKENS, MEM_HEADS, MEM_HEAD_DIM),
            k_s.reshape(1, DB, 1, H, E), v_s.reshape(1, DB, 1, H, E), conv_state_s.reshape(1, DB, CONV_K - 1, CONV_WIDTH))
```

```python
import functools
import math

import jax
import jax.numpy as jnp
import numpy as np
from jax import lax
from jax.experimental import pallas as pl
from jax.experimental.pallas import tpu as pltpu

F32 = jnp.float32
BF16 = jnp.bfloat16
I32 = jnp.int32

D_MODEL = 1024
CONV_WIDTH = 256
CONV_K = 3
HEAD_DIM = 64
N_HEADS = 12
ATTN_WIDTH = N_HEADS * HEAD_DIM
PATTERNS = ((128, 1), (512, 4), (2048, 16))
BAND = 128
WINDOW_MAX = 2048
ATTN_SCALE = 1.0 / math.sqrt(HEAD_DIM)
N_BUCKETS = 32
MAX_DISTANCE = WINDOW_MAX
MEM_TOKENS = 256
MEM_HEADS = 4
MEM_HEAD_DIM = 256
MEM_SCALE = 1.0 / math.sqrt(MEM_HEAD_DIM)
N_EXPERTS = 32
TOP_K = 4
D_FF = 1024
SWIGLU_LIMIT = 7.0
SWIGLU_ALPHA = 1.702
LN_EPS = 1e-5
PAST_LEN = 8192
NEG_INF = -1e30

LANES = 128
HEADS_PER_SLAB = LANES // HEAD_DIM
N_SLABS = ATTN_WIDTH // LANES

PROJ_TILE = 512
SEQ_TILE = 512
ROW_BLOCK = 512
COMBINE_TILE = 256
ATTN_GROUP = 8

HIGHEST = lax.Precision.HIGHEST

_FUSED_PROJ_VMEM_BYTES = 58 * 1024 * 1024


def _layer_norm(x, g, b):
    mu = jnp.mean(x, axis=-1, keepdims=True)
    var = jnp.mean(jnp.square(x - mu), axis=-1, keepdims=True)
    return (x - mu) * lax.rsqrt(var + LN_EPS) * g + b


def _dot(a, b):
    return jnp.dot(a, b, preferred_element_type=F32)


def _dot_nt(a, b):
    return lax.dot_general(a, b, (((1,), (1,)), ((), ())), preferred_element_type=F32)


def _dot_hi(a, b):
    return jnp.dot(a, b, preferred_element_type=F32, precision=HIGHEST)


def _inproj_kernel(x_ref, w_ref, wc_ref, conv_ref, q_ref, k_ref, v_ref, cs_ref, u_s):
    ts = x_ref.shape[0]
    cw = CONV_WIDTH

    @pl.when(pl.program_id(1) == 0)
    def _():
        u_s[0:8, :] = jnp.zeros((8, cw), F32)

    x = x_ref[...].astype(BF16)
    gb = _dot(x, w_ref[:, 0:cw])
    gc = _dot(x, w_ref[:, cw:2 * cw])
    h = _dot(x, w_ref[:, 2 * cw:3 * cw])
    u = gc * h
    u_s[8:8 + ts, :] = u
    wc = wc_ref[...]
    conv = wc[0:1, :] * u_s[6:6 + ts, :] + wc[1:2, :] * u_s[7:7 + ts, :] + wc[2:3, :] * u
    conv_ref[...] = (gb * conv).astype(conv_ref.dtype)
    o = 3 * cw
    q_ref[...] = _dot(x, w_ref[:, o:o + ATTN_WIDTH]) * ATTN_SCALE
    k_ref[...] = _dot(x, w_ref[:, o + ATTN_WIDTH:o + 2 * ATTN_WIDTH])
    v_ref[...] = _dot(x, w_ref[:, o + 2 * ATTN_WIDTH:o + 3 * ATTN_WIDTH])
    tail = u_s[ts:ts + 8, :]
    u_s[0:8, :] = tail
    cs_ref[...] = tail


def _attn_kernel(q_ref, k_ref, v_ref, w_ref, o_ref, bias_ref, m_s, l_s, acc_s):
    S = q_ref.shape[0]
    n = BAND
    lane = lax.broadcasted_iota(I32, (n, LANES), 1)
    head_a = lane < HEAD_DIM

    @pl.when(pl.program_id(1) == 0)
    def _():
        for p in range(len(PATTERNS)):
            for hh in range(HEADS_PER_SLAB):
                first_row = jnp.broadcast_to(w_ref[p, hh:hh + 1, :], (n, 2 * n))
                bias_ref[p, hh * n:(hh + 1) * n, :] = pltpu.roll(first_row, 0, 1, stride=1, stride_axis=0)

    def rows_of(start, count, dil):
        return pl.ds(start, count) if dil == 1 else pl.ds(start, count, stride=dil)

    def blocks(p, dil, specs):
        rows = [rows_of(start, n, dil) for start, _ in specs]
        scores, values = [], []
        for (start, has_prev), r in zip(specs, rows):
            qb = q_ref[r, :]
            q2 = jnp.concatenate([jnp.where(head_a, qb, 0.0), jnp.where(head_a, 0.0, qb)], axis=0).astype(BF16)
            krows = rows_of(start - n * dil, 2 * n, dil) if has_prev else r
            bias = bias_ref[p] if has_prev else bias_ref[p, :, n:2 * n]
            scores.append(_dot_nt(q2, k_ref[krows, :].astype(BF16)) + bias)
            values.append(v_ref[krows, :].astype(BF16))
        ms = [jnp.max(s, axis=-1, keepdims=True) for s in scores]
        es = [jnp.exp(s - m) for s, m in zip(scores, ms)]
        ls = [jnp.sum(e, axis=-1, keepdims=True) for e in es]
        pvs = [_dot(e.astype(BF16), vb) for e, vb in zip(es, values)]
        for r, m, l, pv in zip(rows, ms, ls, pvs):
            m_s[p, r, :] = jnp.where(head_a, m[:n], m[n:])
            l_s[p, r, :] = jnp.where(head_a, l[:n], l[n:])
            acc_s[p, r, :] = jnp.where(head_a, pv[:n], pv[n:])

    for p, (window, dil) in enumerate(PATTERNS):
        nb = (S // dil) // n
        specs = [(i * n * dil + r, i > 0) for r in range(dil) for i in range(nb)]
        for g in range(0, len(specs), ATTN_GROUP):
            blocks(p, dil, specs[g:g + ATTN_GROUP])

    rows_per_step = 256

    def merge(t, c):
        rows = pl.ds(t * rows_per_step, rows_per_step)
        ms = [m_s[p, rows, :] for p in range(len(PATTERNS))]
        m_all = jnp.maximum(jnp.maximum(ms[0], ms[1]), ms[2])
        num = den = None
        for p in range(len(PATTERNS)):
            w = jnp.exp(ms[p] - m_all)
            num = w * acc_s[p, rows, :] if num is None else num + w * acc_s[p, rows, :]
            den = w * l_s[p, rows, :] if den is None else den + w * l_s[p, rows, :]
        o_ref[rows, :] = (num / den).astype(o_ref.dtype)
        return c

    lax.fori_loop(0, S // rows_per_step, merge, 0)


def _attention_prompt(q, k, v, bias_tab):
    B, S, _ = q.shape
    slab = lambda p, b: (b, 0, p)
    spec = pl.BlockSpec((None, S, LANES), slab)
    return pl.pallas_call(
        _attn_kernel,
        grid=(N_SLABS, B),
        in_specs=[spec, spec, spec,
                  pl.BlockSpec((len(PATTERNS), None, HEADS_PER_SLAB, 2 * BAND), lambda p, b: (0, p, 0, 0))],
        out_specs=spec,
        out_shape=jax.ShapeDtypeStruct((B, S, ATTN_WIDTH), BF16),
        scratch_shapes=[pltpu.VMEM((len(PATTERNS), HEADS_PER_SLAB * BAND, 2 * BAND), F32)]
        + [pltpu.VMEM((len(PATTERNS), S, LANES), F32)] * 3,
        compiler_params=pltpu.CompilerParams(dimension_semantics=("arbitrary", "arbitrary")),
        name="dilated_attn_prompt",
    )(q, k, v, bias_tab)


def _memkv_kernel(m_ref, wk_ref, wv_ref, k_ref, v_ref, kb_ref, vb_ref):
    x = m_ref[...].astype(BF16)
    k = _dot(x, wk_ref[...])
    v = _dot(x, wv_ref[...])
    k_ref[...] = k
    v_ref[...] = v
    kb_ref[...] = k.astype(BF16)
    vb_ref[...] = v.astype(BF16)


def _memkv_prompt(mem, wk_bf, wv_bf):
    B, M, D = mem.shape
    row = lambda b: (b, 0, 0)
    full = lambda b: (0, 0)
    return pl.pallas_call(
        _memkv_kernel,
        grid=(B,),
        in_specs=[pl.BlockSpec((None, M, D), row), pl.BlockSpec((D, D), full), pl.BlockSpec((D, D), full)],
        out_specs=[pl.BlockSpec((None, M, D), row)] * 4,
        out_shape=[jax.ShapeDtypeStruct((B, M, D), F32)] * 2 + [jax.ShapeDtypeStruct((B, M, D), BF16)] * 2,
        compiler_params=pltpu.CompilerParams(dimension_semantics=("arbitrary",)),
        name="memkv_prompt",
    )(mem, wk_bf, wv_bf)


def _earlier_token_matrix(nt):
    before = lax.broadcasted_iota(I32, (nt, nt), 0) < lax.broadcasted_iota(I32, (nt, nt), 1)
    return jnp.where(before, 1.0, 0.0).astype(BF16)


def _route(logits_t, carry, earlier):
    E, nt = logits_t.shape
    eidx = lax.broadcasted_iota(I32, (E, nt), 0)
    l = logits_t
    vals, idxs = [], []
    for _ in range(TOP_K):
        m = jnp.max(l, axis=0, keepdims=True)
        sel = jnp.min(jnp.where(l == m, eidx, E), axis=0, keepdims=True)
        vals.append(m)
        idxs.append(sel)
        l = jnp.where(eidx == sel, -jnp.inf, l)
    es = [jnp.exp(v - vals[0]) for v in vals]
    den = es[0] + es[1] + es[2] + es[3]
    gates = _stack_rows([e / den for e in es])
    chosen = (l == -jnp.inf)
    onehot = jnp.where(chosen, 1.0, 0.0)
    prefix = _dot(onehot.astype(BF16), earlier) + carry
    ranks = [jnp.sum(jnp.where(eidx == s, prefix, 0.0), axis=0, keepdims=True) for s in idxs]
    rank = _stack_rows(ranks).astype(I32)
    idx = _stack_rows(idxs)
    return idx, gates, rank, carry + jnp.sum(onehot, axis=1, keepdims=True)


def _stack_rows(rows):
    k, nt = len(rows), rows[0].shape[1]
    r = lax.broadcasted_iota(I32, (k, nt), 0)
    out = jnp.broadcast_to(rows[-1], (k, nt))
    for i in range(k - 2, -1, -1):
        out = jnp.where(r == i, rows[i], out)
    return out


def _mid_kernel(x_ref, conv_ref, attn_ref, woc_ref, woa_ref, g1_ref, b1_ref, mk_ref, mv_ref, wq_ref, wmo_ref,
                g2_ref, b2_ref, wrt_ref, br_ref,
                x2_ref, idx_ref, gate_ref, rank_ref, cnt_ref, carry_s, earlier_s, *, alpha):
    first = jnp.logical_and(pl.program_id(0) == 0, pl.program_id(1) == 0)

    @pl.when(first)
    def _():
        carry_s[...] = jnp.zeros_like(carry_s)
        earlier_s[...] = _earlier_token_matrix(earlier_s.shape[0])

    x = x_ref[...]
    mix = _dot(conv_ref[...], woc_ref[...]) + _dot(attn_ref[...], woa_ref[...])
    x1 = _layer_norm(alpha * x + mix, g1_ref[...], b1_ref[...])
    qm = (_dot(x1.astype(BF16), wq_ref[...]) * MEM_SCALE).astype(BF16)
    outs = []
    for h in range(MEM_HEADS):
        hs = slice(h * MEM_HEAD_DIM, (h + 1) * MEM_HEAD_DIM)
        s = _dot_nt(qm[:, hs], mk_ref[:, hs])
        m = jnp.max(s, axis=-1, keepdims=True)
        e = jnp.exp(s - m)
        l = jnp.sum(e, axis=-1, keepdims=True)
        outs.append((_dot(e.astype(BF16), mv_ref[:, hs]) / l).astype(BF16))
    o = jnp.concatenate(outs, axis=-1)
    x2 = _layer_norm(alpha * x1 + _dot(o, wmo_ref[...]), g2_ref[...], b2_ref[...])
    x2_ref[...] = x2
    logits_t = _dot_nt(wrt_ref[...], x2.astype(BF16)) + br_ref[...]
    idx, gates, rank, carry = _route(logits_t, carry_s[:, 0:1], earlier_s[...])
    idx_ref[...] = idx
    gate_ref[...] = gates
    rank_ref[...] = rank
    carry_s[...] = jnp.broadcast_to(carry, carry_s.shape)
    cnt_ref[...] = carry_s[...]


def _round_bf16(x):
    return x.astype(BF16).astype(F32)


def _inproj_sample_kernel(x_ref, w_ref, o_ref):
    o_ref[...] = _dot(x_ref[...].astype(BF16), w_ref[...])


def _inproj_sample(x, w_in):
    n, D = x.shape
    N = w_in.shape[1]
    bn = 512
    return pl.pallas_call(
        _inproj_sample_kernel,
        grid=(N // bn,),
        in_specs=[pl.BlockSpec((n, D), lambda j: (0, 0)), pl.BlockSpec((D, bn), lambda j: (0, j))],
        out_specs=pl.BlockSpec((n, bn), lambda j: (0, j)),
        out_shape=jax.ShapeDtypeStruct((n, N), F32),
        compiler_params=pltpu.CompilerParams(dimension_semantics=("arbitrary",)),
        name="inproj_sample",
    )(x, w_in)


def _attn_sample_kernel(q_ref, kn_ref, vn_ref, kt_ref, vt_ref, bias_ref, bnew_ref, o_ref):
    H, E, W = kt_ref.shape

    def columns(ref):
        t = ref[...].T
        return jnp.stack([t[:E, h:h + 1] for h in range(H)], axis=0)

    q = _round_bf16(columns(q_ref))
    v_new = _round_bf16(columns(vn_ref))
    s_all = jnp.sum(_round_bf16(kt_ref[...]) * q, axis=1) * ATTN_SCALE
    s_new = jnp.sum(_round_bf16(columns(kn_ref)) * q, axis=1) * ATTN_SCALE
    outs, lses = [], []
    for p, (window, dil) in enumerate(PATTERNS):
        lo = W - window
        s = s_all[:, lo:] + bias_ref[p, :, lo:]
        sn = s_new + bnew_ref[p]
        m = jnp.maximum(jnp.max(s, axis=-1, keepdims=True), sn)
        e = jnp.exp(s - m)
        en = jnp.exp(sn - m)
        den = jnp.sum(e, axis=-1, keepdims=True) + en
        pr = _round_bf16(e / den)
        pv = jnp.sum(_round_bf16(vt_ref[:, :, lo:]) * pr[:, None, :], axis=-1, keepdims=True)
        outs.append(pv + _round_bf16(en / den)[:, :, None] * v_new)
        lses.append(m + jnp.log(den))
    lmax = jnp.maximum(jnp.maximum(lses[0], lses[1]), lses[2])
    ws = [jnp.exp(ls - lmax) for ls in lses]
    wsum = ws[0] + ws[1] + ws[2]
    acc = None
    for p in range(len(PATTERNS)):
        term = _round_bf16(ws[p] / wsum)[:, :, None] * _round_bf16(outs[p])
        acc = term if acc is None else acc + term
    lane = lax.broadcasted_iota(I32, o_ref.shape, 1)
    out = jnp.zeros(o_ref.shape, F32)
    for h in range(H):
        out = jnp.where(lane == h, acc[h], out)
    o_ref[...] = out


def _inproj_and_sample_attn_kernel(x_ref, w_ref, wc_ref, q4_ref, kn_ref, vn_ref, kt_ref, vt_ref, bpos_ref, bnew_ref,
                                   conv_ref, q_ref, k_ref, v_ref, cs_ref, attn_ref, u_s):
    _inproj_kernel(x_ref, w_ref, wc_ref, conv_ref, q_ref, k_ref, v_ref, cs_ref, u_s)
    _attn_sample_kernel(q4_ref, kn_ref, vn_ref, kt_ref, vt_ref, bpos_ref, bnew_ref, attn_ref)


def _inproj_prompt_and_attention_sample(x, w_in_bf, w_conv, q, k_new, v_new, win_kt, win_vt, bias_pos, bias_new):
    B, S, D = x.shape
    ts = PROJ_TILE
    n_s = S // ts
    DB, H, E, W = win_kt.shape
    assert B * n_s == DB
    row = lambda b, s: (b, s, 0)
    tok = lambda b, s: (b * n_s + s, 0, 0, 0)
    vec = pl.BlockSpec((None, LANES, LANES), lambda b, s: (b * n_s + s, 0, 0))
    vec_out = pl.BlockSpec((None, E, LANES), lambda b, s: (b * n_s + s, 0, 0))
    cache = pl.BlockSpec((None, H, E, W), tok)
    const = lambda a: pl.BlockSpec(a.shape, lambda b, s: (0,) * a.ndim)
    return pl.pallas_call(
        _inproj_and_sample_attn_kernel,
        grid=(B, n_s),
        in_specs=[pl.BlockSpec((None, ts, D), row), const(w_in_bf), const(w_conv),
                  vec, vec, vec, cache, cache, const(bias_pos), const(bias_new)],
        out_specs=[pl.BlockSpec((None, ts, CONV_WIDTH), row),
                   pl.BlockSpec((None, ts, ATTN_WIDTH), row),
                   pl.BlockSpec((None, ts, ATTN_WIDTH), row),
                   pl.BlockSpec((None, ts, ATTN_WIDTH), row),
                   pl.BlockSpec((None, 8, CONV_WIDTH), lambda b, s: (b, 0, 0)),
                   vec_out],
        out_shape=[jax.ShapeDtypeStruct((B, S, CONV_WIDTH), BF16),
                   jax.ShapeDtypeStruct((B, S, ATTN_WIDTH), F32),
                   jax.ShapeDtypeStruct((B, S, ATTN_WIDTH), F32),
                   jax.ShapeDtypeStruct((B, S, ATTN_WIDTH), F32),
                   jax.ShapeDtypeStruct((B, 8, CONV_WIDTH), F32),
                   jax.ShapeDtypeStruct((DB, E, LANES), F32)],
        scratch_shapes=[pltpu.VMEM((ts + 8, CONV_WIDTH), F32)],
        compiler_params=pltpu.CompilerParams(dimension_semantics=("arbitrary", "arbitrary"),
                                             vmem_limit_bytes=_FUSED_PROJ_VMEM_BYTES),
        name="inproj_prompt_attn_sample",
    )(x, w_in_bf, w_conv, q, k_new, v_new, win_kt, win_vt, bias_pos, bias_new)


def _tail_a_sample_kernel(x_ref, proj_ref, s0_ref, s1_ref, wc_ref, attn_ref, wo_ref, g1_ref, b1_ref, wq_ref,
                          x1_ref, qm_ref, u_ref, *, alpha):
    cw = CONV_WIDTH
    gb = proj_ref[:, 0:cw]
    u = proj_ref[:, cw:2 * cw] * proj_ref[:, 2 * cw:3 * cw]
    u_ref[...] = u
    wc = wc_ref[...]
    conv = gb * (wc[0:1, :] * s0_ref[...] + wc[1:2, :] * s1_ref[...] + wc[2:3, :] * u)
    mixed = jnp.concatenate([conv, attn_ref[...]], axis=-1).astype(BF16)
    x1 = _layer_norm(alpha * x_ref[...] + _dot(mixed, wo_ref[...]), g1_ref[...], b1_ref[...])
    x1_ref[...] = x1
    qm_ref[...] = _dot(x1.astype(BF16), wq_ref[...])


def _tail_a_sample(x, proj, s0, s1, w_conv, attn, w_out, g1, b1, w_mem_q, alpha):
    n, D = x.shape
    return pl.pallas_call(
        functools.partial(_tail_a_sample_kernel, alpha=alpha),
        out_shape=[jax.ShapeDtypeStruct((n, D), F32)] * 2 + [jax.ShapeDtypeStruct((n, CONV_WIDTH), F32)],
        name="tail_a_sample",
    )(x, proj, s0, s1, w_conv, attn, w_out, g1, b1, w_mem_q)


def _memattn_sample_kernel(q_ref, k_ref, v_ref, o_ref):
    q = _round_bf16(q_ref[...])
    s = jnp.sum(_round_bf16(k_ref[...]) * q[None], axis=-1, keepdims=True) * MEM_SCALE
    m = jnp.max(s, axis=0)
    e = jnp.exp(s - m[None])
    den = jnp.sum(e, axis=0)
    pr = _round_bf16(e / den[None])
    o_ref[...] = jnp.sum(pr * _round_bf16(v_ref[...]), axis=0)


_N_MID_IN = 15


def _mid_and_sample_memattn_kernel(*refs, alpha):
    mid_in, (qm_ref, smk_ref, smv_ref) = refs[:_N_MID_IN], refs[_N_MID_IN:_N_MID_IN + 3]
    mid_out, so_ref, (carry_s, earlier_s) = refs[_N_MID_IN + 3:-3], refs[-3], refs[-2:]
    _mid_kernel(*mid_in, *mid_out, carry_s, earlier_s, alpha=alpha)
    _memattn_sample_kernel(qm_ref, smk_ref, smv_ref, so_ref)


def _mid_prompt_and_memattn_sample(x, conv, attn, woc, woa, g1, b1, mk, mv, wq, wmo, g2, b2, wrt, br,
                                   qm_s, mem_k_s, mem_v_s, alpha):
    B, S, D = x.shape
    ts = SEQ_TILE
    n_s = S // ts
    T = B * S
    DB, M, H, E = mem_k_s.shape
    assert B * n_s == DB
    row = lambda b, s: (b, s, 0)
    full = lambda b, s: (0, 0)
    tok = lambda b, s: (0, b * n_s + s)
    mem = lambda b, s: (b, 0, 0)
    vec = pl.BlockSpec((1, D), full)
    s_vec = pl.BlockSpec((None, H, E), lambda b, s: (b * n_s + s, 0, 0))
    s_mem = pl.BlockSpec((None, M, H, E), lambda b, s: (b * n_s + s, 0, 0, 0))
    in_specs = [pl.BlockSpec((None, ts, D), row),
                pl.BlockSpec((None, ts, CONV_WIDTH), row),
                pl.BlockSpec((None, ts, ATTN_WIDTH), row),
                pl.BlockSpec(woc.shape, full), pl.BlockSpec(woa.shape, full), vec, vec,
                pl.BlockSpec((None, MEM_TOKENS, D), mem), pl.BlockSpec((None, MEM_TOKENS, D), mem),
                pl.BlockSpec(wq.shape, full), pl.BlockSpec(wmo.shape, full), vec, vec,
                pl.BlockSpec(wrt.shape, full), pl.BlockSpec(br.shape, full)]
    assert len(in_specs) == _N_MID_IN
    return pl.pallas_call(
        functools.partial(_mid_and_sample_memattn_kernel, alpha=alpha),
        grid=(B, n_s),
        in_specs=in_specs + [s_vec, s_mem, s_mem],
        out_specs=[pl.BlockSpec((None, ts, D), row),
                   pl.BlockSpec((TOP_K, ts), tok), pl.BlockSpec((TOP_K, ts), tok), pl.BlockSpec((TOP_K, ts), tok),
                   pl.BlockSpec((N_EXPERTS, LANES), full), s_vec],
        out_shape=[jax.ShapeDtypeStruct((B, S, D), F32),
                   jax.ShapeDtypeStruct((TOP_K, T), I32),
                   jax.ShapeDtypeStruct((TOP_K, T), F32),
                   jax.ShapeDtypeStruct((TOP_K, T), I32),
                   jax.ShapeDtypeStruct((N_EXPERTS, LANES), F32),
                   jax.ShapeDtypeStruct((DB, H, E), F32)],
        scratch_shapes=[pltpu.VMEM((N_EXPERTS, LANES), F32), pltpu.VMEM((ts, ts), BF16)],
        compiler_params=pltpu.CompilerParams(dimension_semantics=("arbitrary", "arbitrary")),
        name="mid_prompt_memattn_sample",
    )(x, conv, attn, woc, woa, g1, b1, mk, mv, wq, wmo, g2, b2, wrt, br, qm_s, mem_k_s, mem_v_s)


def _tail_b_sample_kernel(x1_ref, o_ref, wmo_ref, g2_ref, b2_ref, wrt_ref, br_ref, cnt_ref,
                          x2_ref, idx_ref, gate_ref, rank_ref, cnt_out_ref, *, alpha):
    x2 = _layer_norm(alpha * x1_ref[...] + _dot(o_ref[...].astype(BF16), wmo_ref[...]), g2_ref[...], b2_ref[...])
    x2_ref[...] = x2
    logits_t = _dot_nt(wrt_ref[...], x2.astype(BF16)) + br_ref[...]
    idx, gates, rank, carry = _route(logits_t, cnt_ref[:, 0:1], _earlier_token_matrix(logits_t.shape[1]))
    idx_ref[...] = idx
    gate_ref[...] = gates
    rank_ref[...] = rank
    cnt_out_ref[...] = jnp.broadcast_to(carry, cnt_out_ref.shape)


def _tail_b_sample(x1, o, w_mem_o, g2, b2, wrt, br, cnt, alpha):
    n, D = x1.shape
    return pl.pallas_call(
        functools.partial(_tail_b_sample_kernel, alpha=alpha),
        out_shape=[jax.ShapeDtypeStruct((n, D), F32),
                   jax.ShapeDtypeStruct((TOP_K, n), I32),
                   jax.ShapeDtypeStruct((TOP_K, n), F32),
                   jax.ShapeDtypeStruct((TOP_K, n), I32),
                   jax.ShapeDtypeStruct(cnt.shape, F32)],
        name="tail_b_sample",
    )(x1, o, w_mem_o, g2, b2, wrt, br, cnt)


def _row_copy(src, dst, sem, src_row, dst_row):
    return pltpu.make_async_copy(src.at[pl.ds(src_row, 1)], dst.at[pl.ds(dst_row, 1)], sem)


def _dispatch_kernel(dest_ref, x_ref, dest2_ref, x2_ref, xb_ref, zeros_s, stage_s, sems, zsem, *, n_tok_rows):
    i = pl.program_id(0)
    last = pl.num_programs(0) - 1
    slot = i % 2
    nt = dest_ref.shape[1] // TOP_K

    def start(src_ref, first, count, idx_ref, sem):
        def issue(t, c):
            for k in range(TOP_K):
                _row_copy(src_ref, xb_ref, sem, first + t, idx_ref[0, k * count + t]).start(priority=k % 2)
            return c

        lax.fori_loop(0, count, issue, 0)

    def drain(src_ref, count, sem):
        for k in range(TOP_K):
            pltpu.make_async_copy(src_ref.at[pl.ds(0, count)], xb_ref.at[pl.ds(0, count)], sem).wait()

    @pl.when(i < last)
    def _():
        stage_s[slot] = x_ref[...]
        start(stage_s.at[slot], 0, nt, dest_ref, sems.at[slot])

    @pl.when(i == last)
    def _():
        zeros_s[...] = jnp.zeros_like(zeros_s)
        pltpu.make_async_copy(zeros_s, xb_ref.at[pl.ds(n_tok_rows, zeros_s.shape[0])], zsem).start()
        start(x2_ref, 0, x2_ref.shape[0], dest2_ref, sems.at[slot])

    @pl.when(i > 0)
    def _():
        drain(stage_s.at[1 - slot], nt, sems.at[1 - slot])

    @pl.when(i == last)
    def _():
        drain(x2_ref, x2_ref.shape[0], sems.at[slot])
        pltpu.make_async_copy(zeros_s, xb_ref.at[pl.ds(n_tok_rows, zeros_s.shape[0])], zsem).wait()


def _dispatch(x_a, dest_a, x_b, dest_b, n_rows):
    Ta, D = x_a.shape
    Tb = x_b.shape[0]
    nt = dest_a.shape[2] // TOP_K
    steps = Ta // nt
    n_tok_rows = (Ta + Tb) * TOP_K
    tile = lambda i: (jnp.minimum(i, steps - 1), 0, 0)
    return pl.pallas_call(
        functools.partial(_dispatch_kernel, n_tok_rows=n_tok_rows),
        grid=(steps + 1,),
        in_specs=[pl.BlockSpec((None, 1, TOP_K * nt), tile, memory_space=pltpu.SMEM),
                  pl.BlockSpec((nt, D), lambda i: (jnp.minimum(i, steps - 1), 0)),
                  pl.BlockSpec((None, 1, TOP_K * Tb), lambda i: (0, 0, 0), memory_space=pltpu.SMEM),
                  pl.BlockSpec((Tb, D), lambda i: (0, 0))],
        out_specs=pl.BlockSpec(memory_space=pl.ANY),
        out_shape=jax.ShapeDtypeStruct((n_rows, D), x_a.dtype),
        scratch_shapes=[pltpu.VMEM((n_rows - n_tok_rows, D), x_a.dtype), pltpu.VMEM((2, nt, D), x_a.dtype),
                        pltpu.SemaphoreType.DMA((2,)), pltpu.SemaphoreType.DMA],
        compiler_params=pltpu.CompilerParams(dimension_semantics=("arbitrary",)),
        name="moe_dispatch",
    )(dest_a, x_a, dest_b, x_b)


def _moe_kernel(blk_ref, exp_ref, lo_ref, hi_ref, xb_ref, wgu_ref, bgu_ref, wd_ref, bd_ref, yb_ref,
                wgu_s, wd_s, cast_s):
    s = pl.program_id(0)
    new_block = jnp.logical_or(s == 0, blk_ref[s] != blk_ref[jnp.maximum(s - 1, 0)])
    lo, hi = lo_ref[s], hi_ref[s]
    used = hi > lo

    @pl.when(s == 0)
    def _():
        cast_s[0] = -1

    @pl.when(jnp.logical_and(used, cast_s[0] != exp_ref[s]))
    def _():
        wgu_s[...] = wgu_ref[...].astype(BF16)
        wd_s[...] = wd_ref[...].astype(BF16)
        cast_s[0] = exp_ref[s]

    @pl.when(new_block)
    def _():
        yb_ref[...] = jnp.zeros_like(yb_ref)

    @pl.when(used)
    def _():
        gu = _dot(xb_ref[...].astype(BF16), wgu_s[...]) + bgu_ref[...]
        g = jnp.minimum(gu[:, :D_FF], SWIGLU_LIMIT)
        u = jnp.clip(gu[:, D_FF:], -SWIGLU_LIMIT, SWIGLU_LIMIT)
        act = (u + 1.0) * g * jax.nn.sigmoid(SWIGLU_ALPHA * g)
        y = _dot(act.astype(BF16), wd_s[...]) + bd_ref[...]
        row = lax.broadcasted_iota(I32, (xb_ref.shape[0], 1), 0)
        yb_ref[...] = jnp.where(jnp.logical_and(row >= lo, row < hi), y, yb_ref[...])


def _moe(seg_blk, seg_exp, seg_lo, seg_hi, xb, w_gate_up, b_gate_up, w_down, b_down):
    P, D = xb.shape
    bm = ROW_BLOCK
    E, _, F2 = w_gate_up.shape
    ex = lambda s, blk, exp, lo, hi: (exp[s], 0, 0)
    rows = lambda s, blk, exp, lo, hi: (blk[s], 0)
    expert_bytes = (D * F2 + D_FF * D) * (2 * 4 + 2)
    block_bytes = bm * D * 4 * 2 * 2 + bm * F2 * 4 * 3
    return pl.pallas_call(
        _moe_kernel,
        grid_spec=pltpu.PrefetchScalarGridSpec(
            num_scalar_prefetch=4,
            grid=(seg_blk.shape[0],),
            in_specs=[pl.BlockSpec((bm, D), rows),
                      pl.BlockSpec((None, D, F2), ex), pl.BlockSpec((None, 1, F2), ex),
                      pl.BlockSpec((None, D_FF, D), ex), pl.BlockSpec((None, 1, D), ex)],
            out_specs=pl.BlockSpec((bm, D), rows),
            scratch_shapes=[pltpu.VMEM((D, F2), BF16), pltpu.VMEM((D_FF, D), BF16), pltpu.SMEM((1,), I32)]),
        out_shape=jax.ShapeDtypeStruct((P, D), F32),
        compiler_params=pltpu.CompilerParams(dimension_semantics=("arbitrary",),
                                             vmem_limit_bytes=expert_bytes + block_bytes),
        name="moe_grouped_ffn",
    )(seg_blk, seg_exp, seg_lo, seg_hi, xb, w_gate_up, b_gate_up.reshape(E, 1, F2), w_down, b_down.reshape(E, 1, D))


def _combine_kernel(dest_ref, next_ref, yb_ref, x2_ref, gate_ref, g3_ref, b3_ref, o_ref, rows_s, sems, *, alpha):
    nt = x2_ref.shape[0]
    i = pl.program_id(0)
    slot = i % 2

    def gather(idx_ref, buf):
        def issue(t, c):
            for k in range(TOP_K):
                _row_copy(yb_ref, rows_s.at[buf, k], sems.at[buf], idx_ref[0, k * nt + t], t).start(priority=k % 2)
            return c
        lax.fori_loop(0, nt, issue, 0)

    @pl.when(i == 0)
    def _():
        gather(dest_ref, 0)

    @pl.when(i + 1 < pl.num_programs(0))
    def _():
        gather(next_ref, 1 - slot)

    for k in range(TOP_K):
        pltpu.make_async_copy(yb_ref.at[pl.ds(0, nt)], rows_s.at[slot, k], sems.at[slot]).wait()
    gates = gate_ref[...]
    y = gates[:, 0:1] * rows_s[slot, 0]
    for k in range(1, TOP_K):
        y = y + gates[:, k:k + 1] * rows_s[slot, k]
    o_ref[...] = _layer_norm(alpha * x2_ref[...] + y, g3_ref[...], b3_ref[...])


def _combine(yb, dest_tiles, x2, gates_tok, g3, b3, alpha):
    T, D = x2.shape
    steps = dest_tiles.shape[0]
    nt = dest_tiles.shape[2] // TOP_K
    idx_spec = lambda ahead: pl.BlockSpec((None, 1, TOP_K * nt), lambda i: (jnp.minimum(i + ahead, steps - 1), 0, 0),
                                          memory_space=pltpu.SMEM)
    return pl.pallas_call(
        functools.partial(_combine_kernel, alpha=alpha),
        grid=(steps,),
        in_specs=[idx_spec(0), idx_spec(1),
                  pl.BlockSpec(memory_space=pl.ANY),
                  pl.BlockSpec((nt, D), lambda i: (i, 0)),
                  pl.BlockSpec((nt, TOP_K), lambda i: (i, 0)),
                  pl.BlockSpec((1, D), lambda i: (0, 0)), pl.BlockSpec((1, D), lambda i: (0, 0))],
        out_specs=pl.BlockSpec((nt, D), lambda i: (i, 0)),
        out_shape=jax.ShapeDtypeStruct((T, D), F32),
        scratch_shapes=[pltpu.VMEM((2, TOP_K, nt, D), F32), pltpu.SemaphoreType.DMA((2,))],
        compiler_params=pltpu.CompilerParams(dimension_semantics=("arbitrary",)),
        name="moe_combine",
    )(dest_tiles, dest_tiles, yb, x2, gates_tok, g3, b3)


def _rel_bucket(dist):
    max_exact = N_BUCKETS // 2
    df = jnp.maximum(dist, 1).astype(F32)
    large = max_exact + (jnp.log(df / max_exact) / math.log(MAX_DISTANCE / max_exact)
                         * (N_BUCKETS - max_exact)).astype(I32)
    return jnp.where(dist < max_exact, dist, jnp.minimum(large, N_BUCKETS - 1))


def _bias_tables(rel_table):
    n = BAND
    taps = jnp.arange(n + 1)
    by_tap = jnp.stack([rel_table[_rel_bucket(taps * dil)].astype(F32) for _, dil in PATTERNS])
    w = jnp.concatenate([by_tap[:, ::-1], jnp.full((len(PATTERNS), n - 1, N_HEADS), NEG_INF, F32)], axis=1)
    w = jnp.transpose(w, (0, 2, 1))
    return by_tap, w.reshape(len(PATTERNS), N_SLABS, HEADS_PER_SLAB, 2 * n)


def _tap_bias_by_slot(by_tap, W):
    assert PAST_LEN % W == 0 and PAST_LEN >= WINDOW_MAX
    n = BAND
    out = []
    for p, (window, dil) in enumerate(PATTERNS):
        taps = by_tap[p, :0:-1]
        col = jnp.concatenate([jnp.full((W // dil - n, N_HEADS), NEG_INF, F32), taps], axis=0)
        rest = jnp.full((W // dil, dil - 1, N_HEADS), NEG_INF, F32)
        out.append(jnp.concatenate([col[:, None, :], rest], axis=1).reshape(W, N_HEADS).T)
    return jnp.stack(out)


def _dest_tiles(dest, nt):
    T = dest.shape[1]
    return dest.reshape(TOP_K, T // nt, nt).transpose(1, 0, 2).reshape(T // nt, 1, TOP_K * nt)


def kernel(x_prompt, x_sample, mem_prompt, cache_win_k, cache_win_v, state_conv, cache_mem_k, cache_mem_v,
           rel_bias_table, w_in, w_conv, w_out, ln1_g, ln1_b, w_mem_q, w_mem_k, w_mem_v, w_mem_o,
           ln2_g, ln2_b, w_router, b_router, w_gate_up, b_gate_up, w_down, b_down, ln3_g, ln3_b):
    depth = w_in.shape[0]
    assert depth == 1
    alpha = (2 * depth) ** 0.25
    B, S, D = x_prompt.shape
    DB = x_sample.shape[0]
    T = B * S
    l = 0
    vec = lambda a: a[l].reshape(1, -1)

    by_tap, bias_band = _bias_tables(rel_bias_table)
    wrt = w_router[l].T.astype(BF16)
    br = b_router[l].reshape(N_EXPERTS, 1)
    w_in_bf = w_in[l].astype(BF16)
    wo_bf = w_out[l].astype(BF16)
    wq_bf = w_mem_q[l].astype(BF16)
    wmo_bf = w_mem_o[l].astype(BF16)

    xs = x_sample.reshape(DB, D)
    proj = _inproj_sample(xs, w_in_bf)
    cw = CONV_WIDTH
    q_s = proj[:, 3 * cw:3 * cw + ATTN_WIDTH]
    k_s = proj[:, 3 * cw + ATTN_WIDTH:3 * cw + 2 * ATTN_WIDTH]
    v_s = proj[:, 3 * cw + 2 * ATTN_WIDTH:]
    H, E = N_HEADS, HEAD_DIM
    bias_new = by_tap[:, 0, :, None]
    slots_last = lambda c: jnp.transpose(c, (0, 2, 3, 1))
    corner = lambda a: jnp.pad(a.reshape(DB, H, E), ((0, 0), (0, LANES - H), (0, LANES - E)))
    conv_p, q_p, k_p, v_p, cs_p, attn_s = _inproj_prompt_and_attention_sample(
        x_prompt, w_in_bf, w_conv[l], corner(q_s), corner(k_s), corner(v_s),
        slots_last(cache_win_k[l]), slots_last(cache_win_v[l]), _tap_bias_by_slot(by_tap, WINDOW_MAX), bias_new)
    attn_s = jnp.transpose(attn_s[:, :, :H], (0, 2, 1)).reshape(DB, ATTN_WIDTH)

    sc = state_conv[l]
    x1_s, qm_s, u_s = _tail_a_sample(xs, proj, sc[:, 0], sc[:, 1], w_conv[l], attn_s, wo_bf,
                                     vec(ln1_g), vec(ln1_b), wq_bf, alpha)
    attn_p = _attention_prompt(q_p, k_p, v_p, bias_band)
    mk, mv, mk_bf, mv_bf = _memkv_prompt(mem_prompt, w_mem_k[l].astype(BF16), w_mem_v[l].astype(BF16))
    x2_p, idx_p, gate_p, rank_p, cnt_p, o_s = _mid_prompt_and_memattn_sample(
        x_prompt, conv_p, attn_p, wo_bf[:CONV_WIDTH], wo_bf[CONV_WIDTH:], vec(ln1_g), vec(ln1_b), mk_bf, mv_bf,
        wq_bf, wmo_bf, vec(ln2_g), vec(ln2_b), wrt, br,
        qm_s.reshape(DB, MEM_HEADS, MEM_HEAD_DIM), cache_mem_k[l], cache_mem_v[l], alpha)
    x2_s, idx_s, gate_s, rank_s, cnt = _tail_b_sample(x1_s, o_s.reshape(DB, D), wmo_bf, vec(ln2_g), vec(ln2_b), wrt, br,
                                                      cnt_p, alpha)

    bm = ROW_BLOCK
    counts = cnt[:, 0].astype(I32)
    ends = jnp.cumsum(counts)
    starts = ends - counts
    n_tok_rows = (T + DB) * TOP_K
    n_blocks = -(-n_tok_rows // bm)
    P = n_blocks * bm
    cuts = jnp.sort(jnp.concatenate([jnp.arange(n_blocks, dtype=I32) * bm, ends[:-1]]))
    cut_ends = jnp.concatenate([cuts[1:], jnp.full((1,), P, I32)])
    seg_blk = jnp.minimum(cuts // bm, n_blocks - 1)
    seg_exp = jnp.minimum(jnp.sum((ends[None, :] <= cuts[:, None]).astype(I32), axis=1), N_EXPERTS - 1)
    seg_lo = cuts - seg_blk * bm
    seg_hi = cut_ends - seg_blk * bm
    expert_ids = jnp.arange(N_EXPERTS, dtype=I32)[:, None, None]
    start_of = lambda idx: jnp.sum(jnp.where(idx[None] == expert_ids, starts[:, None, None], 0), axis=0)
    dest_p = start_of(idx_p) + rank_p
    dest_s = start_of(idx_s) + rank_s
    dtiles_p = _dest_tiles(dest_p, COMBINE_TILE)
    dtiles_s = _dest_tiles(dest_s, DB)

    x2_pf = x2_p.reshape(T, D)
    xb = _dispatch(x2_pf, dtiles_p, x2_s, dtiles_s, P)
    yb = _moe(seg_blk, seg_exp, seg_lo, seg_hi, xb, w_gate_up[l], b_gate_up[l], w_down[l], b_down[l])
    y_p = _combine(yb, dtiles_p, x2_pf, gate_p.T, vec(ln3_g), vec(ln3_b), alpha).reshape(B, S, D)
    y_s = _combine(yb, dtiles_s, x2_s, gate_s.T, vec(ln3_g), vec(ln3_b), alpha).reshape(DB, 1, D)

    conv_state_s = jnp.stack([sc[:, 1], u_s], axis=1)
    return (y_p, y_s,
            k_p.reshape(1, B, S, H, E), v_p.reshape(1, B, S, H, E), cs_p[:, 6:8].reshape(1, B, CONV_K - 1, CONV_WIDTH),
            mk.reshape(1, B, MEM_TOKENS, MEM_HEADS, MEM_HEAD_DIM), mv.reshape(1, B, MEM_TOKENS, MEM_HEADS, MEM_HEAD_DIM),
            k_s.reshape(1, DB, 1, H, E), v_s.reshape(1, DB, 1, H, E), conv_state_s.reshape(1, DB, CONV_K - 1, CONV_WIDTH))
```

```python
import functools
import math

import jax
import jax.numpy as jnp
import numpy as np
from jax import lax
from jax.experimental import pallas as pl
from jax.experimental.pallas import tpu as pltpu

F32 = jnp.float32
BF16 = jnp.bfloat16
I32 = jnp.int32

D_MODEL = 1024
CONV_WIDTH = 256
CONV_K = 3
HEAD_DIM = 64
N_HEADS = 12
ATTN_WIDTH = N_HEADS * HEAD_DIM
PATTERNS = ((128, 1), (512, 4), (2048, 16))
BAND = 128
WINDOW_MAX = 2048
ATTN_SCALE = 1.0 / math.sqrt(HEAD_DIM)
N_BUCKETS = 32
MAX_DISTANCE = WINDOW_MAX
MEM_TOKENS = 256
MEM_HEADS = 4
MEM_HEAD_DIM = 256
MEM_SCALE = 1.0 / math.sqrt(MEM_HEAD_DIM)
N_EXPERTS = 32
TOP_K = 4
D_FF = 1024
SWIGLU_LIMIT = 7.0
SWIGLU_ALPHA = 1.702
LN_EPS = 1e-5
PAST_LEN = 8192
NEG_INF = -1e30

LANES = 128
HEADS_PER_SLAB = LANES // HEAD_DIM
N_SLABS = ATTN_WIDTH // LANES

PROJ_TILE = 512
SEQ_TILE = 512
ROW_BLOCK = 512
COMBINE_TILE = 256
ROW_COPY_UNROLL = 32
ATTN_GROUP = 8

HIGHEST = lax.Precision.HIGHEST

_FUSED_PROJ_VMEM_BYTES = 58 * 1024 * 1024


def _layer_norm(x, g, b):
    mu = jnp.mean(x, axis=-1, keepdims=True)
    var = jnp.mean(jnp.square(x - mu), axis=-1, keepdims=True)
    return (x - mu) * lax.rsqrt(var + LN_EPS) * g + b


def _dot(a, b):
    return jnp.dot(a, b, preferred_element_type=F32)


def _dot_nt(a, b):
    return lax.dot_general(a, b, (((1,), (1,)), ((), ())), preferred_element_type=F32)


def _dot_hi(a, b):
    return jnp.dot(a, b, preferred_element_type=F32, precision=HIGHEST)


def _inproj_kernel(x_ref, w_ref, wc_ref, conv_ref, q_ref, k_ref, v_ref, cs_ref, u_s):
    ts = x_ref.shape[0]
    cw = CONV_WIDTH

    @pl.when(pl.program_id(1) == 0)
    def _():
        u_s[0:8, :] = jnp.zeros((8, cw), F32)

    x = x_ref[...].astype(BF16)
    gb = _dot(x, w_ref[:, 0:cw])
    gc = _dot(x, w_ref[:, cw:2 * cw])
    h = _dot(x, w_ref[:, 2 * cw:3 * cw])
    u = gc * h
    u_s[8:8 + ts, :] = u
    wc = wc_ref[...]
    conv = wc[0:1, :] * u_s[6:6 + ts, :] + wc[1:2, :] * u_s[7:7 + ts, :] + wc[2:3, :] * u
    conv_ref[...] = (gb * conv).astype(conv_ref.dtype)
    o = 3 * cw
    q_ref[...] = _dot(x, w_ref[:, o:o + ATTN_WIDTH]) * ATTN_SCALE
    k_ref[...] = _dot(x, w_ref[:, o + ATTN_WIDTH:o + 2 * ATTN_WIDTH])
    v_ref[...] = _dot(x, w_ref[:, o + 2 * ATTN_WIDTH:o + 3 * ATTN_WIDTH])
    tail = u_s[ts:ts + 8, :]
    u_s[0:8, :] = tail
    cs_ref[...] = tail


def _attn_kernel(q_ref, k_ref, v_ref, w_ref, o_ref, bias_ref, m_s, l_s, acc_s):
    S = q_ref.shape[0]
    n = BAND
    lane = lax.broadcasted_iota(I32, (n, LANES), 1)
    head_a = lane < HEAD_DIM

    @pl.when(pl.program_id(1) == 0)
    def _():
        for p in range(len(PATTERNS)):
            for hh in range(HEADS_PER_SLAB):
                first_row = jnp.broadcast_to(w_ref[p, hh:hh + 1, :], (n, 2 * n))
                bias_ref[p, hh * n:(hh + 1) * n, :] = pltpu.roll(first_row, 0, 1, stride=1, stride_axis=0)

    def rows_of(start, count, dil):
        return pl.ds(start, count) if dil == 1 else pl.ds(start, count, stride=dil)

    def blocks(p, dil, specs):
        rows = [rows_of(start, n, dil) for start, _ in specs]
        scores, values = [], []
        for (start, has_prev), r in zip(specs, rows):
            qb = q_ref[r, :]
            q2 = jnp.concatenate([jnp.where(head_a, qb, 0.0), jnp.where(head_a, 0.0, qb)], axis=0).astype(BF16)
            krows = rows_of(start - n * dil, 2 * n, dil) if has_prev else r
            bias = bias_ref[p] if has_prev else bias_ref[p, :, n:2 * n]
            scores.append(_dot_nt(q2, k_ref[krows, :].astype(BF16)) + bias)
            values.append(v_ref[krows, :].astype(BF16))
        ms = [jnp.max(s, axis=-1, keepdims=True) for s in scores]
        es = [jnp.exp(s - m) for s, m in zip(scores, ms)]
        ls = [jnp.sum(e, axis=-1, keepdims=True) for e in es]
        pvs = [_dot(e.astype(BF16), vb) for e, vb in zip(es, values)]
        for r, m, l, pv in zip(rows, ms, ls, pvs):
            m_s[p, r, :] = jnp.where(head_a, m[:n], m[n:])
            l_s[p, r, :] = jnp.where(head_a, l[:n], l[n:])
            acc_s[p, r, :] = jnp.where(head_a, pv[:n], pv[n:])

    for p, (window, dil) in enumerate(PATTERNS):
        nb = (S // dil) // n
        specs = [(i * n * dil + r, i > 0) for r in range(dil) for i in range(nb)]
        for g in range(0, len(specs), ATTN_GROUP):
            blocks(p, dil, specs[g:g + ATTN_GROUP])

    rows_per_step = 256

    def merge(t, c):
        rows = pl.ds(t * rows_per_step, rows_per_step)
        ms = [m_s[p, rows, :] for p in range(len(PATTERNS))]
        m_all = jnp.maximum(jnp.maximum(ms[0], ms[1]), ms[2])
        num = den = None
        for p in range(len(PATTERNS)):
            w = jnp.exp(ms[p] - m_all)
            num = w * acc_s[p, rows, :] if num is None else num + w * acc_s[p, rows, :]
            den = w * l_s[p, rows, :] if den is None else den + w * l_s[p, rows, :]
        o_ref[rows, :] = (num / den).astype(o_ref.dtype)
        return c

    lax.fori_loop(0, S // rows_per_step, merge, 0)


def _attention_prompt(q, k, v, bias_tab):
    B, S, _ = q.shape
    slab = lambda p, b: (b, 0, p)
    spec = pl.BlockSpec((None, S, LANES), slab)
    return pl.pallas_call(
        _attn_kernel,
        grid=(N_SLABS, B),
        in_specs=[spec, spec, spec,
                  pl.BlockSpec((len(PATTERNS), None, HEADS_PER_SLAB, 2 * BAND), lambda p, b: (0, p, 0, 0))],
        out_specs=spec,
        out_shape=jax.ShapeDtypeStruct((B, S, ATTN_WIDTH), BF16),
        scratch_shapes=[pltpu.VMEM((len(PATTERNS), HEADS_PER_SLAB * BAND, 2 * BAND), F32)]
        + [pltpu.VMEM((len(PATTERNS), S, LANES), F32)] * 3,
        compiler_params=pltpu.CompilerParams(dimension_semantics=("arbitrary", "arbitrary")),
        name="dilated_attn_prompt",
    )(q, k, v, bias_tab)


def _memkv_kernel(m_ref, wk_ref, wv_ref, k_ref, v_ref, kb_ref, vb_ref):
    x = m_ref[...].astype(BF16)
    k = _dot(x, wk_ref[...])
    v = _dot(x, wv_ref[...])
    k_ref[...] = k
    v_ref[...] = v
    kb_ref[...] = k.astype(BF16)
    vb_ref[...] = v.astype(BF16)


def _memkv_prompt(mem, wk_bf, wv_bf):
    B, M, D = mem.shape
    row = lambda b: (b, 0, 0)
    full = lambda b: (0, 0)
    return pl.pallas_call(
        _memkv_kernel,
        grid=(B,),
        in_specs=[pl.BlockSpec((None, M, D), row), pl.BlockSpec((D, D), full), pl.BlockSpec((D, D), full)],
        out_specs=[pl.BlockSpec((None, M, D), row)] * 4,
        out_shape=[jax.ShapeDtypeStruct((B, M, D), F32)] * 2 + [jax.ShapeDtypeStruct((B, M, D), BF16)] * 2,
        compiler_params=pltpu.CompilerParams(dimension_semantics=("arbitrary",)),
        name="memkv_prompt",
    )(mem, wk_bf, wv_bf)


def _earlier_token_matrix(nt):
    before = lax.broadcasted_iota(I32, (nt, nt), 0) < lax.broadcasted_iota(I32, (nt, nt), 1)
    return jnp.where(before, 1.0, 0.0).astype(BF16)


def _route(logits_t, carry, earlier):
    E, nt = logits_t.shape
    eidx = lax.broadcasted_iota(I32, (E, nt), 0)
    l = logits_t
    vals, idxs = [], []
    for _ in range(TOP_K):
        m = jnp.max(l, axis=0, keepdims=True)
        sel = jnp.min(jnp.where(l == m, eidx, E), axis=0, keepdims=True)
        vals.append(m)
        idxs.append(sel)
        l = jnp.where(eidx == sel, -jnp.inf, l)
    es = [jnp.exp(v - vals[0]) for v in vals]
    den = es[0] + es[1] + es[2] + es[3]
    gates = _stack_rows([e / den for e in es])
    chosen = (l == -jnp.inf)
    onehot = jnp.where(chosen, 1.0, 0.0)
    prefix = _dot(onehot.astype(BF16), earlier) + carry
    ranks = [jnp.sum(jnp.where(eidx == s, prefix, 0.0), axis=0, keepdims=True) for s in idxs]
    rank = _stack_rows(ranks).astype(I32)
    idx = _stack_rows(idxs)
    return idx, gates, rank, carry + jnp.sum(onehot, axis=1, keepdims=True)


def _stack_rows(rows):
    k, nt = len(rows), rows[0].shape[1]
    r = lax.broadcasted_iota(I32, (k, nt), 0)
    out = jnp.broadcast_to(rows[-1], (k, nt))
    for i in range(k - 2, -1, -1):
        out = jnp.where(r == i, rows[i], out)
    return out


def _mid_kernel(x_ref, conv_ref, attn_ref, woc_ref, woa_ref, g1_ref, b1_ref, mk_ref, mv_ref, wq_ref, wmo_ref,
                g2_ref, b2_ref, wrt_ref, br_ref,
                x2_ref, idx_ref, gate_ref, rank_ref, cnt_ref, carry_s, earlier_s, *, alpha):
    first = jnp.logical_and(pl.program_id(0) == 0, pl.program_id(1) == 0)

    @pl.when(first)
    def _():
        carry_s[...] = jnp.zeros_like(carry_s)
        earlier_s[...] = _earlier_token_matrix(earlier_s.shape[0])

    x = x_ref[...]
    mix = _dot(conv_ref[...], woc_ref[...]) + _dot(attn_ref[...], woa_ref[...])
    x1 = _layer_norm(alpha * x + mix, g1_ref[...], b1_ref[...])
    qm = (_dot(x1.astype(BF16), wq_ref[...]) * MEM_SCALE).astype(BF16)
    outs = []
    for h in range(MEM_HEADS):
        hs = slice(h * MEM_HEAD_DIM, (h + 1) * MEM_HEAD_DIM)
        s = _dot_nt(qm[:, hs], mk_ref[:, hs])
        m = jnp.max(s, axis=-1, keepdims=True)
        e = jnp.exp(s - m)
        l = jnp.sum(e, axis=-1, keepdims=True)
        outs.append((_dot(e.astype(BF16), mv_ref[:, hs]) / l).astype(BF16))
    o = jnp.concatenate(outs, axis=-1)
    x2 = _layer_norm(alpha * x1 + _dot(o, wmo_ref[...]), g2_ref[...], b2_ref[...])
    x2_ref[...] = x2
    logits_t = _dot_nt(wrt_ref[...], x2.astype(BF16)) + br_ref[...]
    idx, gates, rank, carry = _route(logits_t, carry_s[:, 0:1], earlier_s[...])
    idx_ref[...] = idx
    gate_ref[...] = gates
    rank_ref[...] = rank
    carry_s[...] = jnp.broadcast_to(carry, carry_s.shape)
    cnt_ref[...] = carry_s[...]


def _round_bf16(x):
    return x.astype(BF16).astype(F32)


def _inproj_sample_kernel(x_ref, w_ref, o_ref):
    o_ref[...] = _dot(x_ref[...].astype(BF16), w_ref[...])


def _inproj_sample(x, w_in):
    n, D = x.shape
    N = w_in.shape[1]
    bn = 512
    return pl.pallas_call(
        _inproj_sample_kernel,
        grid=(N // bn,),
        in_specs=[pl.BlockSpec((n, D), lambda j: (0, 0)), pl.BlockSpec((D, bn), lambda j: (0, j))],
        out_specs=pl.BlockSpec((n, bn), lambda j: (0, j)),
        out_shape=jax.ShapeDtypeStruct((n, N), F32),
        compiler_params=pltpu.CompilerParams(dimension_semantics=("arbitrary",)),
        name="inproj_sample",
    )(x, w_in)


def _attn_sample_kernel(q_ref, kn_ref, vn_ref, kt_ref, vt_ref, bias_ref, bnew_ref, o_ref):
    H, E, W = kt_ref.shape

    def columns(ref):
        t = ref[...].T
        return jnp.stack([t[:E, h:h + 1] for h in range(H)], axis=0)

    q = _round_bf16(columns(q_ref))
    v_new = _round_bf16(columns(vn_ref))
    s_all = jnp.sum(_round_bf16(kt_ref[...]) * q, axis=1) * ATTN_SCALE
    s_new = jnp.sum(_round_bf16(columns(kn_ref)) * q, axis=1) * ATTN_SCALE
    outs, lses = [], []
    for p, (window, dil) in enumerate(PATTERNS):
        lo = W - window
        s = s_all[:, lo:] + bias_ref[p, :, lo:]
        sn = s_new + bnew_ref[p]
        m = jnp.maximum(jnp.max(s, axis=-1, keepdims=True), sn)
        e = jnp.exp(s - m)
        en = jnp.exp(sn - m)
        den = jnp.sum(e, axis=-1, keepdims=True) + en
        pr = _round_bf16(e / den)
        pv = jnp.sum(_round_bf16(vt_ref[:, :, lo:]) * pr[:, None, :], axis=-1, keepdims=True)
        outs.append(pv + _round_bf16(en / den)[:, :, None] * v_new)
        lses.append(m + jnp.log(den))
    lmax = jnp.maximum(jnp.maximum(lses[0], lses[1]), lses[2])
    ws = [jnp.exp(ls - lmax) for ls in lses]
    wsum = ws[0] + ws[1] + ws[2]
    acc = None
    for p in range(len(PATTERNS)):
        term = _round_bf16(ws[p] / wsum)[:, :, None] * _round_bf16(outs[p])
        acc = term if acc is None else acc + term
    lane = lax.broadcasted_iota(I32, o_ref.shape, 1)
    out = jnp.zeros(o_ref.shape, F32)
    for h in range(H):
        out = jnp.where(lane == h, acc[h], out)
    o_ref[...] = out


def _inproj_and_sample_attn_kernel(x_ref, w_ref, wc_ref, q4_ref, kn_ref, vn_ref, kt_ref, vt_ref, bpos_ref, bnew_ref,
                                   conv_ref, q_ref, k_ref, v_ref, cs_ref, attn_ref, u_s):
    _inproj_kernel(x_ref, w_ref, wc_ref, conv_ref, q_ref, k_ref, v_ref, cs_ref, u_s)
    _attn_sample_kernel(q4_ref, kn_ref, vn_ref, kt_ref, vt_ref, bpos_ref, bnew_ref, attn_ref)


def _inproj_prompt_and_attention_sample(x, w_in_bf, w_conv, q, k_new, v_new, win_kt, win_vt, bias_pos, bias_new):
    B, S, D = x.shape
    ts = PROJ_TILE
    n_s = S // ts
    DB, H, E, W = win_kt.shape
    assert B * n_s == DB
    row = lambda b, s: (b, s, 0)
    tok = lambda b, s: (b * n_s + s, 0, 0, 0)
    vec = pl.BlockSpec((None, LANES, LANES), lambda b, s: (b * n_s + s, 0, 0))
    vec_out = pl.BlockSpec((None, E, LANES), lambda b, s: (b * n_s + s, 0, 0))
    cache = pl.BlockSpec((None, H, E, W), tok)
    const = lambda a: pl.BlockSpec(a.shape, lambda b, s: (0,) * a.ndim)
    return pl.pallas_call(
        _inproj_and_sample_attn_kernel,
        grid=(B, n_s),
        in_specs=[pl.BlockSpec((None, ts, D), row), const(w_in_bf), const(w_conv),
                  vec, vec, vec, cache, cache, const(bias_pos), const(bias_new)],
        out_specs=[pl.BlockSpec((None, ts, CONV_WIDTH), row),
                   pl.BlockSpec((None, ts, ATTN_WIDTH), row),
                   pl.BlockSpec((None, ts, ATTN_WIDTH), row),
                   pl.BlockSpec((None, ts, ATTN_WIDTH), row),
                   pl.BlockSpec((None, 8, CONV_WIDTH), lambda b, s: (b, 0, 0)),
                   vec_out],
        out_shape=[jax.ShapeDtypeStruct((B, S, CONV_WIDTH), BF16),
                   jax.ShapeDtypeStruct((B, S, ATTN_WIDTH), F32),
                   jax.ShapeDtypeStruct((B, S, ATTN_WIDTH), F32),
                   jax.ShapeDtypeStruct((B, S, ATTN_WIDTH), F32),
                   jax.ShapeDtypeStruct((B, 8, CONV_WIDTH), F32),
                   jax.ShapeDtypeStruct((DB, E, LANES), F32)],
        scratch_shapes=[pltpu.VMEM((ts + 8, CONV_WIDTH), F32)],
        compiler_params=pltpu.CompilerParams(dimension_semantics=("arbitrary", "arbitrary"),
                                             vmem_limit_bytes=_FUSED_PROJ_VMEM_BYTES),
        name="inproj_prompt_attn_sample",
    )(x, w_in_bf, w_conv, q, k_new, v_new, win_kt, win_vt, bias_pos, bias_new)


def _tail_a_sample_kernel(x_ref, proj_ref, s0_ref, s1_ref, wc_ref, attn_ref, wo_ref, g1_ref, b1_ref, wq_ref,
                          x1_ref, qm_ref, u_ref, *, alpha):
    cw = CONV_WIDTH
    gb = proj_ref[:, 0:cw]
    u = proj_ref[:, cw:2 * cw] * proj_ref[:, 2 * cw:3 * cw]
    u_ref[...] = u
    wc = wc_ref[...]
    conv = gb * (wc[0:1, :] * s0_ref[...] + wc[1:2, :] * s1_ref[...] + wc[2:3, :] * u)
    mixed = jnp.concatenate([conv, attn_ref[...]], axis=-1).astype(BF16)
    x1 = _layer_norm(alpha * x_ref[...] + _dot(mixed, wo_ref[...]), g1_ref[...], b1_ref[...])
    x1_ref[...] = x1
    qm_ref[...] = _dot(x1.astype(BF16), wq_ref[...])


def _tail_a_sample(x, proj, s0, s1, w_conv, attn, w_out, g1, b1, w_mem_q, alpha):
    n, D = x.shape
    return pl.pallas_call(
        functools.partial(_tail_a_sample_kernel, alpha=alpha),
        out_shape=[jax.ShapeDtypeStruct((n, D), F32)] * 2 + [jax.ShapeDtypeStruct((n, CONV_WIDTH), F32)],
        name="tail_a_sample",
    )(x, proj, s0, s1, w_conv, attn, w_out, g1, b1, w_mem_q)


def _memattn_sample_kernel(q_ref, k_ref, v_ref, o_ref):
    q = _round_bf16(q_ref[...])
    s = jnp.sum(_round_bf16(k_ref[...]) * q[None], axis=-1, keepdims=True) * MEM_SCALE
    m = jnp.max(s, axis=0)
    e = jnp.exp(s - m[None])
    den = jnp.sum(e, axis=0)
    pr = _round_bf16(e / den[None])
    o_ref[...] = jnp.sum(pr * _round_bf16(v_ref[...]), axis=0)


_N_MID_IN = 15


def _mid_and_sample_memattn_kernel(*refs, alpha):
    mid_in, (qm_ref, smk_ref, smv_ref) = refs[:_N_MID_IN], refs[_N_MID_IN:_N_MID_IN + 3]
    mid_out, so_ref, (carry_s, earlier_s) = refs[_N_MID_IN + 3:-3], refs[-3], refs[-2:]
    _mid_kernel(*mid_in, *mid_out, carry_s, earlier_s, alpha=alpha)
    _memattn_sample_kernel(qm_ref, smk_ref, smv_ref, so_ref)


def _mid_prompt_and_memattn_sample(x, conv, attn, woc, woa, g1, b1, mk, mv, wq, wmo, g2, b2, wrt, br,
                                   qm_s, mem_k_s, mem_v_s, alpha):
    B, S, D = x.shape
    ts = SEQ_TILE
    n_s = S // ts
    T = B * S
    DB, M, H, E = mem_k_s.shape
    assert B * n_s == DB
    row = lambda b, s: (b, s, 0)
    full = lambda b, s: (0, 0)
    tok = lambda b, s: (0, b * n_s + s)
    mem = lambda b, s: (b, 0, 0)
    vec = pl.BlockSpec((1, D), full)
    s_vec = pl.BlockSpec((None, H, E), lambda b, s: (b * n_s + s, 0, 0))
    s_mem = pl.BlockSpec((None, M, H, E), lambda b, s: (b * n_s + s, 0, 0, 0))
    in_specs = [pl.BlockSpec((None, ts, D), row),
                pl.BlockSpec((None, ts, CONV_WIDTH), row),
                pl.BlockSpec((None, ts, ATTN_WIDTH), row),
                pl.BlockSpec(woc.shape, full), pl.BlockSpec(woa.shape, full), vec, vec,
                pl.BlockSpec((None, MEM_TOKENS, D), mem), pl.BlockSpec((None, MEM_TOKENS, D), mem),
                pl.BlockSpec(wq.shape, full), pl.BlockSpec(wmo.shape, full), vec, vec,
                pl.BlockSpec(wrt.shape, full), pl.BlockSpec(br.shape, full)]
    assert len(in_specs) == _N_MID_IN
    return pl.pallas_call(
        functools.partial(_mid_and_sample_memattn_kernel, alpha=alpha),
        grid=(B, n_s),
        in_specs=in_specs + [s_vec, s_mem, s_mem],
        out_specs=[pl.BlockSpec((None, ts, D), row),
                   pl.BlockSpec((TOP_K, ts), tok), pl.BlockSpec((TOP_K, ts), tok), pl.BlockSpec((TOP_K, ts), tok),
                   pl.BlockSpec((N_EXPERTS, LANES), full), s_vec],
        out_shape=[jax.ShapeDtypeStruct((B, S, D), F32),
                   jax.ShapeDtypeStruct((TOP_K, T), I32),
                   jax.ShapeDtypeStruct((TOP_K, T), F32),
                   jax.ShapeDtypeStruct((TOP_K, T), I32),
                   jax.ShapeDtypeStruct((N_EXPERTS, LANES), F32),
                   jax.ShapeDtypeStruct((DB, H, E), F32)],
        scratch_shapes=[pltpu.VMEM((N_EXPERTS, LANES), F32), pltpu.VMEM((ts, ts), BF16)],
        compiler_params=pltpu.CompilerParams(dimension_semantics=("arbitrary", "arbitrary")),
        name="mid_prompt_memattn_sample",
    )(x, conv, attn, woc, woa, g1, b1, mk, mv, wq, wmo, g2, b2, wrt, br, qm_s, mem_k_s, mem_v_s)


def _tail_b_sample_kernel(x1_ref, o_ref, wmo_ref, g2_ref, b2_ref, wrt_ref, br_ref, cnt_ref,
                          x2_ref, idx_ref, gate_ref, rank_ref, cnt_out_ref, *, alpha):
    x2 = _layer_norm(alpha * x1_ref[...] + _dot(o_ref[...].astype(BF16), wmo_ref[...]), g2_ref[...], b2_ref[...])
    x2_ref[...] = x2
    logits_t = _dot_nt(wrt_ref[...], x2.astype(BF16)) + br_ref[...]
    idx, gates, rank, carry = _route(logits_t, cnt_ref[:, 0:1], _earlier_token_matrix(logits_t.shape[1]))
    idx_ref[...] = idx
    gate_ref[...] = gates
    rank_ref[...] = rank
    cnt_out_ref[...] = jnp.broadcast_to(carry, cnt_out_ref.shape)


def _tail_b_sample(x1, o, w_mem_o, g2, b2, wrt, br, cnt, alpha):
    n, D = x1.shape
    return pl.pallas_call(
        functools.partial(_tail_b_sample_kernel, alpha=alpha),
        out_shape=[jax.ShapeDtypeStruct((n, D), F32),
                   jax.ShapeDtypeStruct((TOP_K, n), I32),
                   jax.ShapeDtypeStruct((TOP_K, n), F32),
                   jax.ShapeDtypeStruct((TOP_K, n), I32),
                   jax.ShapeDtypeStruct(cnt.shape, F32)],
        name="tail_b_sample",
    )(x1, o, w_mem_o, g2, b2, wrt, br, cnt)


def _row_copy(src, dst, sem, src_row, dst_row):
    return pltpu.make_async_copy(src.at[pl.ds(src_row, 1)], dst.at[pl.ds(dst_row, 1)], sem)


def _dispatch_kernel(dest_ref, x_ref, dest2_ref, x2_ref, xb_ref, zeros_s, stage_s, sems, zsem, *, n_tok_rows):
    i = pl.program_id(0)
    last = pl.num_programs(0) - 1
    slot = i % 2
    nt = dest_ref.shape[1] // TOP_K

    def start(src_ref, first, count, idx_ref, sem):
        def issue(g, c):
            for u in range(ROW_COPY_UNROLL):
                t = g * ROW_COPY_UNROLL + u
                for k in range(TOP_K):
                    _row_copy(src_ref, xb_ref, sem, first + t, idx_ref[0, k * count + t]).start(priority=k % 2)
            return c

        lax.fori_loop(0, count // ROW_COPY_UNROLL, issue, 0)

    def drain(src_ref, count, sem):
        for k in range(TOP_K):
            pltpu.make_async_copy(src_ref.at[pl.ds(0, count)], xb_ref.at[pl.ds(0, count)], sem).wait()

    @pl.when(i < last)
    def _():
        stage_s[slot] = x_ref[...]
        start(stage_s.at[slot], 0, nt, dest_ref, sems.at[slot])

    @pl.when(i == last)
    def _():
        zeros_s[...] = jnp.zeros_like(zeros_s)
        pltpu.make_async_copy(zeros_s, xb_ref.at[pl.ds(n_tok_rows, zeros_s.shape[0])], zsem).start()
        start(x2_ref, 0, x2_ref.shape[0], dest2_ref, sems.at[slot])

    @pl.when(i > 0)
    def _():
        drain(stage_s.at[1 - slot], nt, sems.at[1 - slot])

    @pl.when(i == last)
    def _():
        drain(x2_ref, x2_ref.shape[0], sems.at[slot])
        pltpu.make_async_copy(zeros_s, xb_ref.at[pl.ds(n_tok_rows, zeros_s.shape[0])], zsem).wait()


def _dispatch(x_a, dest_a, x_b, dest_b, n_rows):
    Ta, D = x_a.shape
    Tb = x_b.shape[0]
    nt = dest_a.shape[2] // TOP_K
    steps = Ta // nt
    assert nt % ROW_COPY_UNROLL == 0 and Tb % ROW_COPY_UNROLL == 0
    n_tok_rows = (Ta + Tb) * TOP_K
    tile = lambda i: (jnp.minimum(i, steps - 1), 0, 0)
    return pl.pallas_call(
        functools.partial(_dispatch_kernel, n_tok_rows=n_tok_rows),
        grid=(steps + 1,),
        in_specs=[pl.BlockSpec((None, 1, TOP_K * nt), tile, memory_space=pltpu.SMEM),
                  pl.BlockSpec((nt, D), lambda i: (jnp.minimum(i, steps - 1), 0)),
                  pl.BlockSpec((None, 1, TOP_K * Tb), lambda i: (0, 0, 0), memory_space=pltpu.SMEM),
                  pl.BlockSpec((Tb, D), lambda i: (0, 0))],
        out_specs=pl.BlockSpec(memory_space=pl.ANY),
        out_shape=jax.ShapeDtypeStruct((n_rows, D), x_a.dtype),
        scratch_shapes=[pltpu.VMEM((n_rows - n_tok_rows, D), x_a.dtype), pltpu.VMEM((2, nt, D), x_a.dtype),
                        pltpu.SemaphoreType.DMA((2,)), pltpu.SemaphoreType.DMA],
        compiler_params=pltpu.CompilerParams(dimension_semantics=("arbitrary",)),
        name="moe_dispatch",
    )(dest_a, x_a, dest_b, x_b)


def _moe_kernel(blk_ref, exp_ref, lo_ref, hi_ref, xb_ref, wgu_ref, bgu_ref, wd_ref, bd_ref, yb_ref,
                wgu_s, wd_s, cast_s):
    s = pl.program_id(0)
    new_block = jnp.logical_or(s == 0, blk_ref[s] != blk_ref[jnp.maximum(s - 1, 0)])
    lo, hi = lo_ref[s], hi_ref[s]
    used = hi > lo

    @pl.when(s == 0)
    def _():
        cast_s[0] = -1

    @pl.when(jnp.logical_and(used, cast_s[0] != exp_ref[s]))
    def _():
        wgu_s[...] = wgu_ref[...].astype(BF16)
        wd_s[...] = wd_ref[...].astype(BF16)
        cast_s[0] = exp_ref[s]

    @pl.when(new_block)
    def _():
        yb_ref[...] = jnp.zeros_like(yb_ref)

    @pl.when(used)
    def _():
        gu = _dot(xb_ref[...].astype(BF16), wgu_s[...]) + bgu_ref[...]
        g = jnp.minimum(gu[:, :D_FF], SWIGLU_LIMIT)
        u = jnp.clip(gu[:, D_FF:], -SWIGLU_LIMIT, SWIGLU_LIMIT)
        act = (u + 1.0) * g * jax.nn.sigmoid(SWIGLU_ALPHA * g)
        y = _dot(act.astype(BF16), wd_s[...]) + bd_ref[...]
        row = lax.broadcasted_iota(I32, (xb_ref.shape[0], 1), 0)
        yb_ref[...] = jnp.where(jnp.logical_and(row >= lo, row < hi), y, yb_ref[...])


def _moe(seg_blk, seg_exp, seg_lo, seg_hi, xb, w_gate_up, b_gate_up, w_down, b_down):
    P, D = xb.shape
    bm = ROW_BLOCK
    E, _, F2 = w_gate_up.shape
    ex = lambda s, blk, exp, lo, hi: (exp[s], 0, 0)
    rows = lambda s, blk, exp, lo, hi: (blk[s], 0)
    expert_bytes = (D * F2 + D_FF * D) * (2 * 4 + 2)
    block_bytes = bm * D * 4 * 2 * 2 + bm * F2 * 4 * 3
    return pl.pallas_call(
        _moe_kernel,
        grid_spec=pltpu.PrefetchScalarGridSpec(
            num_scalar_prefetch=4,
            grid=(seg_blk.shape[0],),
            in_specs=[pl.BlockSpec((bm, D), rows),
                      pl.BlockSpec((None, D, F2), ex), pl.BlockSpec((None, 1, F2), ex),
                      pl.BlockSpec((None, D_FF, D), ex), pl.BlockSpec((None, 1, D), ex)],
            out_specs=pl.BlockSpec((bm, D), rows),
            scratch_shapes=[pltpu.VMEM((D, F2), BF16), pltpu.VMEM((D_FF, D), BF16), pltpu.SMEM((1,), I32)]),
        out_shape=jax.ShapeDtypeStruct((P, D), F32),
        compiler_params=pltpu.CompilerParams(dimension_semantics=("arbitrary",),
                                             vmem_limit_bytes=expert_bytes + block_bytes),
        name="moe_grouped_ffn",
    )(seg_blk, seg_exp, seg_lo, seg_hi, xb, w_gate_up, b_gate_up.reshape(E, 1, F2), w_down, b_down.reshape(E, 1, D))


def _combine_kernel(dest_ref, next_ref, yb_ref, x2_ref, gate_ref, g3_ref, b3_ref, o_ref, rows_s, sems, *, alpha):
    nt = x2_ref.shape[0]
    i = pl.program_id(0)
    slot = i % 2

    def gather(idx_ref, buf):
        def issue(g, c):
            for u in range(ROW_COPY_UNROLL):
                t = g * ROW_COPY_UNROLL + u
                for k in range(TOP_K):
                    _row_copy(yb_ref, rows_s.at[buf, k], sems.at[buf], idx_ref[0, k * nt + t], t).start(priority=k % 2)
            return c
        lax.fori_loop(0, nt // ROW_COPY_UNROLL, issue, 0)

    @pl.when(i == 0)
    def _():
        gather(dest_ref, 0)

    @pl.when(i + 1 < pl.num_programs(0))
    def _():
        gather(next_ref, 1 - slot)

    for k in range(TOP_K):
        pltpu.make_async_copy(yb_ref.at[pl.ds(0, nt)], rows_s.at[slot, k], sems.at[slot]).wait()
    gates = gate_ref[...]
    y = gates[:, 0:1] * rows_s[slot, 0]
    for k in range(1, TOP_K):
        y = y + gates[:, k:k + 1] * rows_s[slot, k]
    o_ref[...] = _layer_norm(alpha * x2_ref[...] + y, g3_ref[...], b3_ref[...])


def _combine(yb, dest_tiles, x2, gates_tok, g3, b3, alpha):
    T, D = x2.shape
    steps = dest_tiles.shape[0]
    nt = dest_tiles.shape[2] // TOP_K
    assert nt % ROW_COPY_UNROLL == 0
    idx_spec = lambda ahead: pl.BlockSpec((None, 1, TOP_K * nt), lambda i: (jnp.minimum(i + ahead, steps - 1), 0, 0),
                                          memory_space=pltpu.SMEM)
    return pl.pallas_call(
        functools.partial(_combine_kernel, alpha=alpha),
        grid=(steps,),
        in_specs=[idx_spec(0), idx_spec(1),
                  pl.BlockSpec(memory_space=pl.ANY),
                  pl.BlockSpec((nt, D), lambda i: (i, 0)),
                  pl.BlockSpec((nt, TOP_K), lambda i: (i, 0)),
                  pl.BlockSpec((1, D), lambda i: (0, 0)), pl.BlockSpec((1, D), lambda i: (0, 0))],
        out_specs=pl.BlockSpec((nt, D), lambda i: (i, 0)),
        out_shape=jax.ShapeDtypeStruct((T, D), F32),
        scratch_shapes=[pltpu.VMEM((2, TOP_K, nt, D), F32), pltpu.SemaphoreType.DMA((2,))],
        compiler_params=pltpu.CompilerParams(dimension_semantics=("arbitrary",)),
        name="moe_combine",
    )(dest_tiles, dest_tiles, yb, x2, gates_tok, g3, b3)


def _rel_bucket(dist):
    max_exact = N_BUCKETS // 2
    df = jnp.maximum(dist, 1).astype(F32)
    large = max_exact + (jnp.log(df / max_exact) / math.log(MAX_DISTANCE / max_exact)
                         * (N_BUCKETS - max_exact)).astype(I32)
    return jnp.where(dist < max_exact, dist, jnp.minimum(large, N_BUCKETS - 1))


def _bias_tables(rel_table):
    n = BAND
    taps = jnp.arange(n + 1)
    by_tap = jnp.stack([rel_table[_rel_bucket(taps * dil)].astype(F32) for _, dil in PATTERNS])
    w = jnp.concatenate([by_tap[:, ::-1], jnp.full((len(PATTERNS), n - 1, N_HEADS), NEG_INF, F32)], axis=1)
    w = jnp.transpose(w, (0, 2, 1))
    return by_tap, w.reshape(len(PATTERNS), N_SLABS, HEADS_PER_SLAB, 2 * n)


def _tap_bias_by_slot(by_tap, W):
    assert PAST_LEN % W == 0 and PAST_LEN >= WINDOW_MAX
    n = BAND
    out = []
    for p, (window, dil) in enumerate(PATTERNS):
        taps = by_tap[p, :0:-1]
        col = jnp.concatenate([jnp.full((W // dil - n, N_HEADS), NEG_INF, F32), taps], axis=0)
        rest = jnp.full((W // dil, dil - 1, N_HEADS), NEG_INF, F32)
        out.append(jnp.concatenate([col[:, None, :], rest], axis=1).reshape(W, N_HEADS).T)
    return jnp.stack(out)


def _dest_tiles(dest, nt):
    T = dest.shape[1]
    return dest.reshape(TOP_K, T // nt, nt).transpose(1, 0, 2).reshape(T // nt, 1, TOP_K * nt)


def kernel(x_prompt, x_sample, mem_prompt, cache_win_k, cache_win_v, state_conv, cache_mem_k, cache_mem_v,
           rel_bias_table, w_in, w_conv, w_out, ln1_g, ln1_b, w_mem_q, w_mem_k, w_mem_v, w_mem_o,
           ln2_g, ln2_b, w_router, b_router, w_gate_up, b_gate_up, w_down, b_down, ln3_g, ln3_b):
    depth = w_in.shape[0]
    assert depth == 1
    alpha = (2 * depth) ** 0.25
    B, S, D = x_prompt.shape
    DB = x_sample.shape[0]
    T = B * S
    l = 0
    vec = lambda a: a[l].reshape(1, -1)

    by_tap, bias_band = _bias_tables(rel_bias_table)
    wrt = w_router[l].T.astype(BF16)
    br = b_router[l].reshape(N_EXPERTS, 1)
    w_in_bf = w_in[l].astype(BF16)
    wo_bf = w_out[l].astype(BF16)
    wq_bf = w_mem_q[l].astype(BF16)
    wmo_bf = w_mem_o[l].astype(BF16)

    xs = x_sample.reshape(DB, D)
    proj = _inproj_sample(xs, w_in_bf)
    cw = CONV_WIDTH
    q_s = proj[:, 3 * cw:3 * cw + ATTN_WIDTH]
    k_s = proj[:, 3 * cw + ATTN_WIDTH:3 * cw + 2 * ATTN_WIDTH]
    v_s = proj[:, 3 * cw + 2 * ATTN_WIDTH:]
    H, E = N_HEADS, HEAD_DIM
    bias_new = by_tap[:, 0, :, None]
    slots_last = lambda c: jnp.transpose(c, (0, 2, 3, 1))
    corner = lambda a: jnp.pad(a.reshape(DB, H, E), ((0, 0), (0, LANES - H), (0, LANES - E)))
    conv_p, q_p, k_p, v_p, cs_p, attn_s = _inproj_prompt_and_attention_sample(
        x_prompt, w_in_bf, w_conv[l], corner(q_s), corner(k_s), corner(v_s),
        slots_last(cache_win_k[l]), slots_last(cache_win_v[l]), _tap_bias_by_slot(by_tap, WINDOW_MAX), bias_new)
    attn_s = jnp.transpose(attn_s[:, :, :H], (0, 2, 1)).reshape(DB, ATTN_WIDTH)

    sc = state_conv[l]
    x1_s, qm_s, u_s = _tail_a_sample(xs, proj, sc[:, 0], sc[:, 1], w_conv[l], attn_s, wo_bf,
                                     vec(ln1_g), vec(ln1_b), wq_bf, alpha)
    attn_p = _attention_prompt(q_p, k_p, v_p, bias_band)
    mk, mv, mk_bf, mv_bf = _memkv_prompt(mem_prompt, w_mem_k[l].astype(BF16), w_mem_v[l].astype(BF16))
    x2_p, idx_p, gate_p, rank_p, cnt_p, o_s = _mid_prompt_and_memattn_sample(
        x_prompt, conv_p, attn_p, wo_bf[:CONV_WIDTH], wo_bf[CONV_WIDTH:], vec(ln1_g), vec(ln1_b), mk_bf, mv_bf,
        wq_bf, wmo_bf, vec(ln2_g), vec(ln2_b), wrt, br,
        qm_s.reshape(DB, MEM_HEADS, MEM_HEAD_DIM), cache_mem_k[l], cache_mem_v[l], alpha)
    x2_s, idx_s, gate_s, rank_s, cnt = _tail_b_sample(x1_s, o_s.reshape(DB, D), wmo_bf, vec(ln2_g), vec(ln2_b), wrt, br,
                                                      cnt_p, alpha)

    bm = ROW_BLOCK
    counts = cnt[:, 0].astype(I32)
    ends = jnp.cumsum(counts)
    starts = ends - counts
    n_tok_rows = (T + DB) * TOP_K
    n_blocks = -(-n_tok_rows // bm)
    P = n_blocks * bm
    cuts = jnp.sort(jnp.concatenate([jnp.arange(n_blocks, dtype=I32) * bm, ends[:-1]]))
    cut_ends = jnp.concatenate([cuts[1:], jnp.full((1,), P, I32)])
    seg_blk = jnp.minimum(cuts // bm, n_blocks - 1)
    seg_exp = jnp.minimum(jnp.sum((ends[None, :] <= cuts[:, None]).astype(I32), axis=1), N_EXPERTS - 1)
    seg_lo = cuts - seg_blk * bm
    seg_hi = cut_ends - seg_blk * bm
    expert_ids = jnp.arange(N_EXPERTS, dtype=I32)[:, None, None]
    start_of = lambda idx: jnp.sum(jnp.where(idx[None] == expert_ids, starts[:, None, None], 0), axis=0)
    dest_p = start_of(idx_p) + rank_p
    dest_s = start_of(idx_s) + rank_s
    dtiles_p = _dest_tiles(dest_p, COMBINE_TILE)
    dtiles_s = _dest_tiles(dest_s, DB)

    x2_pf = x2_p.reshape(T, D)
    xb = _dispatch(x2_pf, dtiles_p, x2_s, dtiles_s, P)
    yb = _moe(seg_blk, seg_exp, seg_lo, seg_hi, xb, w_gate_up[l], b_gate_up[l], w_down[l], b_down[l])
    y_p = _combine(yb, dtiles_p, x2_pf, gate_p.T, vec(ln3_g), vec(ln3_b), alpha).reshape(B, S, D)
    y_s = _combine(yb, dtiles_s, x2_s, gate_s.T, vec(ln3_g), vec(ln3_b), alpha).reshape(DB, 1, D)

    conv_state_s = jnp.stack([sc[:, 1], u_s], axis=1)
    return (y_p, y_s,
            k_p.reshape(1, B, S, H, E), v_p.reshape(1, B, S, H, E), cs_p[:, 6:8].reshape(1, B, CONV_K - 1, CONV_WIDTH),
            mk.reshape(1, B, MEM_TOKENS, MEM_HEADS, MEM_HEAD_DIM), mv.reshape(1, B, MEM_TOKENS, MEM_HEADS, MEM_HEAD_DIM),
            k_s.reshape(1, DB, 1, H, E), v_s.reshape(1, DB, 1, H, E), conv_state_s.reshape(1, DB, CONV_K - 1, CONV_WIDTH))
```

```python
import functools
import math

import jax
import jax.numpy as jnp
import numpy as np
from jax import lax
from jax.experimental import pallas as pl
from jax.experimental.pallas import tpu as pltpu

F32 = jnp.float32
BF16 = jnp.bfloat16
I32 = jnp.int32

D_MODEL = 1024
CONV_WIDTH = 256
CONV_K = 3
HEAD_DIM = 64
N_HEADS = 12
ATTN_WIDTH = N_HEADS * HEAD_DIM
PATTERNS = ((128, 1), (512, 4), (2048, 16))
BAND = 128
WINDOW_MAX = 2048
ATTN_SCALE = 1.0 / math.sqrt(HEAD_DIM)
N_BUCKETS = 32
MAX_DISTANCE = WINDOW_MAX
MEM_TOKENS = 256
MEM_HEADS = 4
MEM_HEAD_DIM = 256
MEM_SCALE = 1.0 / math.sqrt(MEM_HEAD_DIM)
N_EXPERTS = 32
TOP_K = 4
D_FF = 1024
SWIGLU_LIMIT = 7.0
SWIGLU_ALPHA = 1.702
LN_EPS = 1e-5
PAST_LEN = 8192
NEG_INF = -1e30

LANES = 128
HEADS_PER_SLAB = LANES // HEAD_DIM
N_SLABS = ATTN_WIDTH // LANES

PROJ_TILE = 512
SEQ_TILE = 512
ROW_BLOCK = 512
COMBINE_TILE = 256
ROW_COPY_UNROLL = 32
ATTN_GROUP = 6

HIGHEST = lax.Precision.HIGHEST

_FUSED_PROJ_VMEM_BYTES = 58 * 1024 * 1024


def _layer_norm(x, g, b):
    mu = jnp.mean(x, axis=-1, keepdims=True)
    var = jnp.mean(jnp.square(x - mu), axis=-1, keepdims=True)
    return (x - mu) * lax.rsqrt(var + LN_EPS) * g + b


def _dot(a, b):
    return jnp.dot(a, b, preferred_element_type=F32)


def _dot_nt(a, b):
    return lax.dot_general(a, b, (((1,), (1,)), ((), ())), preferred_element_type=F32)


def _dot_hi(a, b):
    return jnp.dot(a, b, preferred_element_type=F32, precision=HIGHEST)


def _inproj_kernel(x_ref, w_ref, wc_ref, conv_ref, q_ref, k_ref, v_ref, cs_ref, u_s):
    ts = x_ref.shape[0]
    cw = CONV_WIDTH

    @pl.when(pl.program_id(1) == 0)
    def _():
        u_s[0:8, :] = jnp.zeros((8, cw), F32)

    x = x_ref[...].astype(BF16)
    gb = _dot(x, w_ref[:, 0:cw])
    gc = _dot(x, w_ref[:, cw:2 * cw])
    h = _dot(x, w_ref[:, 2 * cw:3 * cw])
    u = gc * h
    u_s[8:8 + ts, :] = u
    wc = wc_ref[...]
    conv = wc[0:1, :] * u_s[6:6 + ts, :] + wc[1:2, :] * u_s[7:7 + ts, :] + wc[2:3, :] * u
    conv_ref[...] = (gb * conv).astype(conv_ref.dtype)
    o = 3 * cw
    q_ref[...] = _dot(x, w_ref[:, o:o + ATTN_WIDTH]) * ATTN_SCALE
    k_ref[...] = _dot(x, w_ref[:, o + ATTN_WIDTH:o + 2 * ATTN_WIDTH])
    v_ref[...] = _dot(x, w_ref[:, o + 2 * ATTN_WIDTH:o + 3 * ATTN_WIDTH])
    tail = u_s[ts:ts + 8, :]
    u_s[0:8, :] = tail
    cs_ref[...] = tail


def _attn_kernel(q_ref, k_ref, v_ref, w_ref, o_ref, bias_ref, m_s, l_s, acc_s):
    S = q_ref.shape[0]
    n = BAND
    lane = lax.broadcasted_iota(I32, (n, LANES), 1)
    head_a = lane < HEAD_DIM

    @pl.when(pl.program_id(1) == 0)
    def _():
        for p in range(len(PATTERNS)):
            for hh in range(HEADS_PER_SLAB):
                first_row = jnp.broadcast_to(w_ref[p, hh:hh + 1, :], (n, 2 * n))
                bias_ref[p, hh * n:(hh + 1) * n, :] = pltpu.roll(first_row, 0, 1, stride=1, stride_axis=0)

    def rows_of(start, count, dil):
        return pl.ds(start, count) if dil == 1 else pl.ds(start, count, stride=dil)

    def blocks(p, dil, specs):
        rows = [rows_of(start, n, dil) for start, _ in specs]
        scores, values = [], []
        for (start, has_prev), r in zip(specs, rows):
            qb = q_ref[r, :]
            q2 = jnp.concatenate([jnp.where(head_a, qb, 0.0), jnp.where(head_a, 0.0, qb)], axis=0).astype(BF16)
            krows = rows_of(start - n * dil, 2 * n, dil) if has_prev else r
            bias = bias_ref[p] if has_prev else bias_ref[p, :, n:2 * n]
            scores.append(_dot_nt(q2, k_ref[krows, :].astype(BF16)) + bias)
            values.append(v_ref[krows, :].astype(BF16))
        ms = [jnp.max(s, axis=-1, keepdims=True) for s in scores]
        es = [jnp.exp(s - m) for s, m in zip(scores, ms)]
        ls = [jnp.sum(e, axis=-1, keepdims=True) for e in es]
        pvs = [_dot(e.astype(BF16), vb) for e, vb in zip(es, values)]
        for r, m, l, pv in zip(rows, ms, ls, pvs):
            m_s[p, r, :] = jnp.where(head_a, m[:n], m[n:])
            l_s[p, r, :] = jnp.where(head_a, l[:n], l[n:])
            acc_s[p, r, :] = jnp.where(head_a, pv[:n], pv[n:])

    for p, (window, dil) in enumerate(PATTERNS):
        nb = (S // dil) // n
        specs = [(i * n * dil + r, i > 0) for r in range(dil) for i in range(nb)]
        for g in range(0, len(specs), ATTN_GROUP):
            blocks(p, dil, specs[g:g + ATTN_GROUP])

    rows_per_step = 256

    def merge(t, c):
        rows = pl.ds(t * rows_per_step, rows_per_step)
        ms = [m_s[p, rows, :] for p in range(len(PATTERNS))]
        m_all = jnp.maximum(jnp.maximum(ms[0], ms[1]), ms[2])
        num = den = None
        for p in range(len(PATTERNS)):
            w = jnp.exp(ms[p] - m_all)
            num = w * acc_s[p, rows, :] if num is None else num + w * acc_s[p, rows, :]
            den = w * l_s[p, rows, :] if den is None else den + w * l_s[p, rows, :]
        o_ref[rows, :] = (num / den).astype(o_ref.dtype)
        return c

    lax.fori_loop(0, S // rows_per_step, merge, 0)


def _attention_prompt(q, k, v, bias_tab):
    B, S, _ = q.shape
    slab = lambda p, b: (b, 0, p)
    spec = pl.BlockSpec((None, S, LANES), slab)
    return pl.pallas_call(
        _attn_kernel,
        grid=(N_SLABS, B),
        in_specs=[spec, spec, spec,
                  pl.BlockSpec((len(PATTERNS), None, HEADS_PER_SLAB, 2 * BAND), lambda p, b: (0, p, 0, 0))],
        out_specs=spec,
        out_shape=jax.ShapeDtypeStruct((B, S, ATTN_WIDTH), BF16),
        scratch_shapes=[pltpu.VMEM((len(PATTERNS), HEADS_PER_SLAB * BAND, 2 * BAND), F32)]
        + [pltpu.VMEM((len(PATTERNS), S, LANES), F32)] * 3,
        compiler_params=pltpu.CompilerParams(dimension_semantics=("arbitrary", "arbitrary")),
        name="dilated_attn_prompt",
    )(q, k, v, bias_tab)


def _memkv_kernel(m_ref, wk_ref, wv_ref, k_ref, v_ref, kb_ref, vb_ref):
    x = m_ref[...].astype(BF16)
    k = _dot(x, wk_ref[...])
    v = _dot(x, wv_ref[...])
    k_ref[...] = k
    v_ref[...] = v
    kb_ref[...] = k.astype(BF16)
    vb_ref[...] = v.astype(BF16)


def _memkv_prompt(mem, wk_bf, wv_bf):
    B, M, D = mem.shape
    row = lambda b: (b, 0, 0)
    full = lambda b: (0, 0)
    return pl.pallas_call(
        _memkv_kernel,
        grid=(B,),
        in_specs=[pl.BlockSpec((None, M, D), row), pl.BlockSpec((D, D), full), pl.BlockSpec((D, D), full)],
        out_specs=[pl.BlockSpec((None, M, D), row)] * 4,
        out_shape=[jax.ShapeDtypeStruct((B, M, D), F32)] * 2 + [jax.ShapeDtypeStruct((B, M, D), BF16)] * 2,
        compiler_params=pltpu.CompilerParams(dimension_semantics=("arbitrary",)),
        name="memkv_prompt",
    )(mem, wk_bf, wv_bf)


def _earlier_token_matrix(nt):
    before = lax.broadcasted_iota(I32, (nt, nt), 0) < lax.broadcasted_iota(I32, (nt, nt), 1)
    return jnp.where(before, 1.0, 0.0).astype(BF16)


def _route(logits_t, carry, earlier):
    E, nt = logits_t.shape
    eidx = lax.broadcasted_iota(I32, (E, nt), 0)
    l = logits_t
    vals, idxs = [], []
    for _ in range(TOP_K):
        m = jnp.max(l, axis=0, keepdims=True)
        sel = jnp.min(jnp.where(l == m, eidx, E), axis=0, keepdims=True)
        vals.append(m)
        idxs.append(sel)
        l = jnp.where(eidx == sel, -jnp.inf, l)
    es = [jnp.exp(v - vals[0]) for v in vals]
    den = es[0] + es[1] + es[2] + es[3]
    gates = _stack_rows([e / den for e in es])
    chosen = (l == -jnp.inf)
    onehot = jnp.where(chosen, 1.0, 0.0)
    prefix = _dot(onehot.astype(BF16), earlier) + carry
    ranks = [jnp.sum(jnp.where(eidx == s, prefix, 0.0), axis=0, keepdims=True) for s in idxs]
    rank = _stack_rows(ranks).astype(I32)
    idx = _stack_rows(idxs)
    return idx, gates, rank, carry + jnp.sum(onehot, axis=1, keepdims=True)


def _stack_rows(rows):
    k, nt = len(rows), rows[0].shape[1]
    r = lax.broadcasted_iota(I32, (k, nt), 0)
    out = jnp.broadcast_to(rows[-1], (k, nt))
    for i in range(k - 2, -1, -1):
        out = jnp.where(r == i, rows[i], out)
    return out


def _mid_kernel(x_ref, conv_ref, attn_ref, woc_ref, woa_ref, g1_ref, b1_ref, mk_ref, mv_ref, wq_ref, wmo_ref,
                g2_ref, b2_ref, wrt_ref, br_ref,
                x2_ref, idx_ref, gate_ref, rank_ref, cnt_ref, carry_s, earlier_s, *, alpha):
    first = jnp.logical_and(pl.program_id(0) == 0, pl.program_id(1) == 0)

    @pl.when(first)
    def _():
        carry_s[...] = jnp.zeros_like(carry_s)
        earlier_s[...] = _earlier_token_matrix(earlier_s.shape[0])

    x = x_ref[...]
    mix = _dot(conv_ref[...], woc_ref[...]) + _dot(attn_ref[...], woa_ref[...])
    x1 = _layer_norm(alpha * x + mix, g1_ref[...], b1_ref[...])
    qm = (_dot(x1.astype(BF16), wq_ref[...]) * MEM_SCALE).astype(BF16)
    outs = []
    for h in range(MEM_HEADS):
        hs = slice(h * MEM_HEAD_DIM, (h + 1) * MEM_HEAD_DIM)
        s = _dot_nt(qm[:, hs], mk_ref[:, hs])
        m = jnp.max(s, axis=-1, keepdims=True)
        e = jnp.exp(s - m)
        l = jnp.sum(e, axis=-1, keepdims=True)
        outs.append((_dot(e.astype(BF16), mv_ref[:, hs]) / l).astype(BF16))
    o = jnp.concatenate(outs, axis=-1)
    x2 = _layer_norm(alpha * x1 + _dot(o, wmo_ref[...]), g2_ref[...], b2_ref[...])
    x2_ref[...] = x2
    logits_t = _dot_nt(wrt_ref[...], x2.astype(BF16)) + br_ref[...]
    idx, gates, rank, carry = _route(logits_t, carry_s[:, 0:1], earlier_s[...])
    idx_ref[...] = idx
    gate_ref[...] = gates
    rank_ref[...] = rank
    carry_s[...] = jnp.broadcast_to(carry, carry_s.shape)
    cnt_ref[...] = carry_s[...]


def _round_bf16(x):
    return x.astype(BF16).astype(F32)


def _inproj_sample_kernel(x_ref, w_ref, o_ref):
    o_ref[...] = _dot(x_ref[...].astype(BF16), w_ref[...])


def _inproj_sample(x, w_in):
    n, D = x.shape
    N = w_in.shape[1]
    bn = 512
    return pl.pallas_call(
        _inproj_sample_kernel,
        grid=(N // bn,),
        in_specs=[pl.BlockSpec((n, D), lambda j: (0, 0)), pl.BlockSpec((D, bn), lambda j: (0, j))],
        out_specs=pl.BlockSpec((n, bn), lambda j: (0, j)),
        out_shape=jax.ShapeDtypeStruct((n, N), F32),
        compiler_params=pltpu.CompilerParams(dimension_semantics=("arbitrary",)),
        name="inproj_sample",
    )(x, w_in)


def _attn_sample_kernel(q_ref, kn_ref, vn_ref, kt_ref, vt_ref, bias_ref, bnew_ref, o_ref):
    H, E, W = kt_ref.shape

    def columns(ref):
        t = ref[...].T
        return jnp.stack([t[:E, h:h + 1] for h in range(H)], axis=0)

    q = _round_bf16(columns(q_ref))
    v_new = _round_bf16(columns(vn_ref))
    s_all = jnp.sum(_round_bf16(kt_ref[...]) * q, axis=1) * ATTN_SCALE
    s_new = jnp.sum(_round_bf16(columns(kn_ref)) * q, axis=1) * ATTN_SCALE
    outs, lses = [], []
    for p, (window, dil) in enumerate(PATTERNS):
        lo = W - window
        s = s_all[:, lo:] + bias_ref[p, :, lo:]
        sn = s_new + bnew_ref[p]
        m = jnp.maximum(jnp.max(s, axis=-1, keepdims=True), sn)
        e = jnp.exp(s - m)
        en = jnp.exp(sn - m)
        den = jnp.sum(e, axis=-1, keepdims=True) + en
        pr = _round_bf16(e / den)
        pv = jnp.sum(_round_bf16(vt_ref[:, :, lo:]) * pr[:, None, :], axis=-1, keepdims=True)
        outs.append(pv + _round_bf16(en / den)[:, :, None] * v_new)
        lses.append(m + jnp.log(den))
    lmax = jnp.maximum(jnp.maximum(lses[0], lses[1]), lses[2])
    ws = [jnp.exp(ls - lmax) for ls in lses]
    wsum = ws[0] + ws[1] + ws[2]
    acc = None
    for p in range(len(PATTERNS)):
        term = _round_bf16(ws[p] / wsum)[:, :, None] * _round_bf16(outs[p])
        acc = term if acc is None else acc + term
    lane = lax.broadcasted_iota(I32, o_ref.shape, 1)
    out = jnp.zeros(o_ref.shape, F32)
    for h in range(H):
        out = jnp.where(lane == h, acc[h], out)
    o_ref[...] = out


def _inproj_and_sample_attn_kernel(x_ref, w_ref, wc_ref, q4_ref, kn_ref, vn_ref, kt_ref, vt_ref, bpos_ref, bnew_ref,
                                   conv_ref, q_ref, k_ref, v_ref, cs_ref, attn_ref, u_s):
    _inproj_kernel(x_ref, w_ref, wc_ref, conv_ref, q_ref, k_ref, v_ref, cs_ref, u_s)
    _attn_sample_kernel(q4_ref, kn_ref, vn_ref, kt_ref, vt_ref, bpos_ref, bnew_ref, attn_ref)


def _inproj_prompt_and_attention_sample(x, w_in_bf, w_conv, q, k_new, v_new, win_kt, win_vt, bias_pos, bias_new):
    B, S, D = x.shape
    ts = PROJ_TILE
    n_s = S // ts
    DB, H, E, W = win_kt.shape
    assert B * n_s == DB
    row = lambda b, s: (b, s, 0)
    tok = lambda b, s: (b * n_s + s, 0, 0, 0)
    vec = pl.BlockSpec((None, LANES, LANES), lambda b, s: (b * n_s + s, 0, 0))
    vec_out = pl.BlockSpec((None, E, LANES), lambda b, s: (b * n_s + s, 0, 0))
    cache = pl.BlockSpec((None, H, E, W), tok)
    const = lambda a: pl.BlockSpec(a.shape, lambda b, s: (0,) * a.ndim)
    return pl.pallas_call(
        _inproj_and_sample_attn_kernel,
        grid=(B, n_s),
        in_specs=[pl.BlockSpec((None, ts, D), row), const(w_in_bf), const(w_conv),
                  vec, vec, vec, cache, cache, const(bias_pos), const(bias_new)],
        out_specs=[pl.BlockSpec((None, ts, CONV_WIDTH), row),
                   pl.BlockSpec((None, ts, ATTN_WIDTH), row),
                   pl.BlockSpec((None, ts, ATTN_WIDTH), row),
                   pl.BlockSpec((None, ts, ATTN_WIDTH), row),
                   pl.BlockSpec((None, 8, CONV_WIDTH), lambda b, s: (b, 0, 0)),
                   vec_out],
        out_shape=[jax.ShapeDtypeStruct((B, S, CONV_WIDTH), BF16),
                   jax.ShapeDtypeStruct((B, S, ATTN_WIDTH), F32),
                   jax.ShapeDtypeStruct((B, S, ATTN_WIDTH), F32),
                   jax.ShapeDtypeStruct((B, S, ATTN_WIDTH), F32),
                   jax.ShapeDtypeStruct((B, 8, CONV_WIDTH), F32),
                   jax.ShapeDtypeStruct((DB, E, LANES), F32)],
        scratch_shapes=[pltpu.VMEM((ts + 8, CONV_WIDTH), F32)],
        compiler_params=pltpu.CompilerParams(dimension_semantics=("arbitrary", "arbitrary"),
                                             vmem_limit_bytes=_FUSED_PROJ_VMEM_BYTES),
        name="inproj_prompt_attn_sample",
    )(x, w_in_bf, w_conv, q, k_new, v_new, win_kt, win_vt, bias_pos, bias_new)


def _tail_a_sample_kernel(x_ref, proj_ref, s0_ref, s1_ref, wc_ref, attn_ref, wo_ref, g1_ref, b1_ref, wq_ref,
                          x1_ref, qm_ref, u_ref, *, alpha):
    cw = CONV_WIDTH
    gb = proj_ref[:, 0:cw]
    u = proj_ref[:, cw:2 * cw] * proj_ref[:, 2 * cw:3 * cw]
    u_ref[...] = u
    wc = wc_ref[...]
    conv = gb * (wc[0:1, :] * s0_ref[...] + wc[1:2, :] * s1_ref[...] + wc[2:3, :] * u)
    mixed = jnp.concatenate([conv, attn_ref[...]], axis=-1).astype(BF16)
    x1 = _layer_norm(alpha * x_ref[...] + _dot(mixed, wo_ref[...]), g1_ref[...], b1_ref[...])
    x1_ref[...] = x1
    qm_ref[...] = _dot(x1.astype(BF16), wq_ref[...])


def _tail_a_sample(x, proj, s0, s1, w_conv, attn, w_out, g1, b1, w_mem_q, alpha):
    n, D = x.shape
    return pl.pallas_call(
        functools.partial(_tail_a_sample_kernel, alpha=alpha),
        out_shape=[jax.ShapeDtypeStruct((n, D), F32)] * 2 + [jax.ShapeDtypeStruct((n, CONV_WIDTH), F32)],
        name="tail_a_sample",
    )(x, proj, s0, s1, w_conv, attn, w_out, g1, b1, w_mem_q)


def _memattn_sample_kernel(q_ref, k_ref, v_ref, o_ref):
    q = _round_bf16(q_ref[...])
    s = jnp.sum(_round_bf16(k_ref[...]) * q[None], axis=-1, keepdims=True) * MEM_SCALE
    m = jnp.max(s, axis=0)
    e = jnp.exp(s - m[None])
    den = jnp.sum(e, axis=0)
    pr = _round_bf16(e / den[None])
    o_ref[...] = jnp.sum(pr * _round_bf16(v_ref[...]), axis=0)


_N_MID_IN = 15


def _mid_and_sample_memattn_kernel(*refs, alpha):
    mid_in, (qm_ref, smk_ref, smv_ref) = refs[:_N_MID_IN], refs[_N_MID_IN:_N_MID_IN + 3]
    mid_out, so_ref, (carry_s, earlier_s) = refs[_N_MID_IN + 3:-3], refs[-3], refs[-2:]
    _mid_kernel(*mid_in, *mid_out, carry_s, earlier_s, alpha=alpha)
    _memattn_sample_kernel(qm_ref, smk_ref, smv_ref, so_ref)


def _mid_prompt_and_memattn_sample(x, conv, attn, woc, woa, g1, b1, mk, mv, wq, wmo, g2, b2, wrt, br,
                                   qm_s, mem_k_s, mem_v_s, alpha):
    B, S, D = x.shape
    ts = SEQ_TILE
    n_s = S // ts
    T = B * S
    DB, M, H, E = mem_k_s.shape
    assert B * n_s == DB
    row = lambda b, s: (b, s, 0)
    full = lambda b, s: (0, 0)
    tok = lambda b, s: (0, b * n_s + s)
    mem = lambda b, s: (b, 0, 0)
    vec = pl.BlockSpec((1, D), full)
    s_vec = pl.BlockSpec((None, H, E), lambda b, s: (b * n_s + s, 0, 0))
    s_mem = pl.BlockSpec((None, M, H, E), lambda b, s: (b * n_s + s, 0, 0, 0))
    in_specs = [pl.BlockSpec((None, ts, D), row),
                pl.BlockSpec((None, ts, CONV_WIDTH), row),
                pl.BlockSpec((None, ts, ATTN_WIDTH), row),
                pl.BlockSpec(woc.shape, full), pl.BlockSpec(woa.shape, full), vec, vec,
                pl.BlockSpec((None, MEM_TOKENS, D), mem), pl.BlockSpec((None, MEM_TOKENS, D), mem),
                pl.BlockSpec(wq.shape, full), pl.BlockSpec(wmo.shape, full), vec, vec,
                pl.BlockSpec(wrt.shape, full), pl.BlockSpec(br.shape, full)]
    assert len(in_specs) == _N_MID_IN
    return pl.pallas_call(
        functools.partial(_mid_and_sample_memattn_kernel, alpha=alpha),
        grid=(B, n_s),
        in_specs=in_specs + [s_vec, s_mem, s_mem],
        out_specs=[pl.BlockSpec((None, ts, D), row),
                   pl.BlockSpec((TOP_K, ts), tok), pl.BlockSpec((TOP_K, ts), tok), pl.BlockSpec((TOP_K, ts), tok),
                   pl.BlockSpec((N_EXPERTS, LANES), full), s_vec],
        out_shape=[jax.ShapeDtypeStruct((B, S, D), F32),
                   jax.ShapeDtypeStruct((TOP_K, T), I32),
                   jax.ShapeDtypeStruct((TOP_K, T), F32),
                   jax.ShapeDtypeStruct((TOP_K, T), I32),
                   jax.ShapeDtypeStruct((N_EXPERTS, LANES), F32),
                   jax.ShapeDtypeStruct((DB, H, E), F32)],
        scratch_shapes=[pltpu.VMEM((N_EXPERTS, LANES), F32), pltpu.VMEM((ts, ts), BF16)],
        compiler_params=pltpu.CompilerParams(dimension_semantics=("arbitrary", "arbitrary")),
        name="mid_prompt_memattn_sample",
    )(x, conv, attn, woc, woa, g1, b1, mk, mv, wq, wmo, g2, b2, wrt, br, qm_s, mem_k_s, mem_v_s)


def _tail_b_sample_kernel(x1_ref, o_ref, wmo_ref, g2_ref, b2_ref, wrt_ref, br_ref, cnt_ref,
                          x2_ref, idx_ref, gate_ref, rank_ref, cnt_out_ref, *, alpha):
    x2 = _layer_norm(alpha * x1_ref[...] + _dot(o_ref[...].astype(BF16), wmo_ref[...]), g2_ref[...], b2_ref[...])
    x2_ref[...] = x2
    logits_t = _dot_nt(wrt_ref[...], x2.astype(BF16)) + br_ref[...]
    idx, gates, rank, carry = _route(logits_t, cnt_ref[:, 0:1], _earlier_token_matrix(logits_t.shape[1]))
    idx_ref[...] = idx
    gate_ref[...] = gates
    rank_ref[...] = rank
    cnt_out_ref[...] = jnp.broadcast_to(carry, cnt_out_ref.shape)


def _tail_b_sample(x1, o, w_mem_o, g2, b2, wrt, br, cnt, alpha):
    n, D = x1.shape
    return pl.pallas_call(
        functools.partial(_tail_b_sample_kernel, alpha=alpha),
        out_shape=[jax.ShapeDtypeStruct((n, D), F32),
                   jax.ShapeDtypeStruct((TOP_K, n), I32),
                   jax.ShapeDtypeStruct((TOP_K, n), F32),
                   jax.ShapeDtypeStruct((TOP_K, n), I32),
                   jax.ShapeDtypeStruct(cnt.shape, F32)],
        name="tail_b_sample",
    )(x1, o, w_mem_o, g2, b2, wrt, br, cnt)


def _row_copy(src, dst, sem, src_row, dst_row):
    return pltpu.make_async_copy(src.at[pl.ds(src_row, 1)], dst.at[pl.ds(dst_row, 1)], sem)


def _dispatch_kernel(dest_ref, x_ref, dest2_ref, x2_ref, xb_ref, zeros_s, stage_s, sems, zsem, *, n_tok_rows):
    i = pl.program_id(0)
    last = pl.num_programs(0) - 1
    slot = i % 2
    nt = dest_ref.shape[1] // TOP_K

    def start(src_ref, first, count, idx_ref, sem):
        def issue(g, c):
            for u in range(ROW_COPY_UNROLL):
                t = g * ROW_COPY_UNROLL + u
                for k in range(TOP_K):
                    _row_copy(src_ref, xb_ref, sem, first + t, idx_ref[0, k * count + t]).start(priority=k % 2)
            return c

        lax.fori_loop(0, count // ROW_COPY_UNROLL, issue, 0)

    def drain(src_ref, count, sem):
        for k in range(TOP_K):
            pltpu.make_async_copy(src_ref.at[pl.ds(0, count)], xb_ref.at[pl.ds(0, count)], sem).wait()

    @pl.when(i < last)
    def _():
        stage_s[slot] = x_ref[...]
        start(stage_s.at[slot], 0, nt, dest_ref, sems.at[slot])

    @pl.when(i == last)
    def _():
        zeros_s[...] = jnp.zeros_like(zeros_s)
        pltpu.make_async_copy(zeros_s, xb_ref.at[pl.ds(n_tok_rows, zeros_s.shape[0])], zsem).start()
        start(x2_ref, 0, x2_ref.shape[0], dest2_ref, sems.at[slot])

    @pl.when(i > 0)
    def _():
        drain(stage_s.at[1 - slot], nt, sems.at[1 - slot])

    @pl.when(i == last)
    def _():
        drain(x2_ref, x2_ref.shape[0], sems.at[slot])
        pltpu.make_async_copy(zeros_s, xb_ref.at[pl.ds(n_tok_rows, zeros_s.shape[0])], zsem).wait()


def _dispatch(x_a, dest_a, x_b, dest_b, n_rows):
    Ta, D = x_a.shape
    Tb = x_b.shape[0]
    nt = dest_a.shape[2] // TOP_K
    steps = Ta // nt
    assert nt % ROW_COPY_UNROLL == 0 and Tb % ROW_COPY_UNROLL == 0
    n_tok_rows = (Ta + Tb) * TOP_K
    tile = lambda i: (jnp.minimum(i, steps - 1), 0, 0)
    return pl.pallas_call(
        functools.partial(_dispatch_kernel, n_tok_rows=n_tok_rows),
        grid=(steps + 1,),
        in_specs=[pl.BlockSpec((None, 1, TOP_K * nt), tile, memory_space=pltpu.SMEM),
                  pl.BlockSpec((nt, D), lambda i: (jnp.minimum(i, steps - 1), 0)),
                  pl.BlockSpec((None, 1, TOP_K * Tb), lambda i: (0, 0, 0), memory_space=pltpu.SMEM),
                  pl.BlockSpec((Tb, D), lambda i: (0, 0))],
        out_specs=pl.BlockSpec(memory_space=pl.ANY),
        out_shape=jax.ShapeDtypeStruct((n_rows, D), x_a.dtype),
        scratch_shapes=[pltpu.VMEM((n_rows - n_tok_rows, D), x_a.dtype), pltpu.VMEM((2, nt, D), x_a.dtype),
                        pltpu.SemaphoreType.DMA((2,)), pltpu.SemaphoreType.DMA],
        compiler_params=pltpu.CompilerParams(dimension_semantics=("arbitrary",)),
        name="moe_dispatch",
    )(dest_a, x_a, dest_b, x_b)


def _moe_kernel(blk_ref, exp_ref, lo_ref, hi_ref, xb_ref, wgu_ref, bgu_ref, wd_ref, bd_ref, yb_ref,
                wgu_s, wd_s, cast_s):
    s = pl.program_id(0)
    new_block = jnp.logical_or(s == 0, blk_ref[s] != blk_ref[jnp.maximum(s - 1, 0)])
    lo, hi = lo_ref[s], hi_ref[s]
    used = hi > lo

    @pl.when(s == 0)
    def _():
        cast_s[0] = -1

    @pl.when(jnp.logical_and(used, cast_s[0] != exp_ref[s]))
    def _():
        wgu_s[...] = wgu_ref[...].astype(BF16)
        wd_s[...] = wd_ref[...].astype(BF16)
        cast_s[0] = exp_ref[s]

    whole = jnp.logical_and(lo == 0, hi == xb_ref.shape[0])

    @pl.when(jnp.logical_and(new_block, jnp.logical_not(whole)))
    def _():
        yb_ref[...] = jnp.zeros_like(yb_ref)

    @pl.when(used)
    def _():
        gu = _dot(xb_ref[...].astype(BF16), wgu_s[...]) + bgu_ref[...]
        g = jnp.minimum(gu[:, :D_FF], SWIGLU_LIMIT)
        u = jnp.clip(gu[:, D_FF:], -SWIGLU_LIMIT, SWIGLU_LIMIT)
        act = (u + 1.0) * g * jax.nn.sigmoid(SWIGLU_ALPHA * g)
        y = _dot(act.astype(BF16), wd_s[...]) + bd_ref[...]

        @pl.when(whole)
        def _():
            yb_ref[...] = y

        @pl.when(jnp.logical_not(whole))
        def _():
            row = lax.broadcasted_iota(I32, (xb_ref.shape[0], 1), 0)
            yb_ref[...] = jnp.where(jnp.logical_and(row >= lo, row < hi), y, yb_ref[...])


def _moe(seg_blk, seg_exp, seg_lo, seg_hi, xb, w_gate_up, b_gate_up, w_down, b_down):
    P, D = xb.shape
    bm = ROW_BLOCK
    E, _, F2 = w_gate_up.shape
    ex = lambda s, blk, exp, lo, hi: (exp[s], 0, 0)
    rows = lambda s, blk, exp, lo, hi: (blk[s], 0)
    expert_bytes = (D * F2 + D_FF * D) * (2 * 4 + 2)
    block_bytes = bm * D * 4 * 2 * 2 + bm * F2 * 4 * 3
    return pl.pallas_call(
        _moe_kernel,
        grid_spec=pltpu.PrefetchScalarGridSpec(
            num_scalar_prefetch=4,
            grid=(seg_blk.shape[0],),
            in_specs=[pl.BlockSpec((bm, D), rows),
                      pl.BlockSpec((None, D, F2), ex), pl.BlockSpec((None, 1, F2), ex),
                      pl.BlockSpec((None, D_FF, D), ex), pl.BlockSpec((None, 1, D), ex)],
            out_specs=pl.BlockSpec((bm, D), rows),
            scratch_shapes=[pltpu.VMEM((D, F2), BF16), pltpu.VMEM((D_FF, D), BF16), pltpu.SMEM((1,), I32)]),
        out_shape=jax.ShapeDtypeStruct((P, D), F32),
        compiler_params=pltpu.CompilerParams(dimension_semantics=("arbitrary",),
                                             vmem_limit_bytes=expert_bytes + block_bytes),
        name="moe_grouped_ffn",
    )(seg_blk, seg_exp, seg_lo, seg_hi, xb, w_gate_up, b_gate_up.reshape(E, 1, F2), w_down, b_down.reshape(E, 1, D))


def _combine_kernel(dest_ref, next_ref, yb_ref, x2_ref, gate_ref, g3_ref, b3_ref, o_ref, rows_s, sems, *, alpha):
    nt = x2_ref.shape[0]
    i = pl.program_id(0)
    slot = i % 2

    def gather(idx_ref, buf):
        def issue(g, c):
            for u in range(ROW_COPY_UNROLL):
                t = g * ROW_COPY_UNROLL + u
                for k in range(TOP_K):
                    _row_copy(yb_ref, rows_s.at[buf, k], sems.at[buf], idx_ref[0, k * nt + t], t).start(priority=k % 2)
            return c
        lax.fori_loop(0, nt // ROW_COPY_UNROLL, issue, 0)

    @pl.when(i == 0)
    def _():
        gather(dest_ref, 0)

    @pl.when(i + 1 < pl.num_programs(0))
    def _():
        gather(next_ref, 1 - slot)

    for k in range(TOP_K):
        pltpu.make_async_copy(yb_ref.at[pl.ds(0, nt)], rows_s.at[slot, k], sems.at[slot]).wait()
    gates = gate_ref[...]
    y = gates[:, 0:1] * rows_s[slot, 0]
    for k in range(1, TOP_K):
        y = y + gates[:, k:k + 1] * rows_s[slot, k]
    o_ref[...] = _layer_norm(alpha * x2_ref[...] + y, g3_ref[...], b3_ref[...])


def _combine(yb, dest_tiles, x2, gates_tok, g3, b3, alpha):
    T, D = x2.shape
    steps = dest_tiles.shape[0]
    nt = dest_tiles.shape[2] // TOP_K
    assert nt % ROW_COPY_UNROLL == 0
    idx_spec = lambda ahead: pl.BlockSpec((None, 1, TOP_K * nt), lambda i: (jnp.minimum(i + ahead, steps - 1), 0, 0),
                                          memory_space=pltpu.SMEM)
    return pl.pallas_call(
        functools.partial(_combine_kernel, alpha=alpha),
        grid=(steps,),
        in_specs=[idx_spec(0), idx_spec(1),
                  pl.BlockSpec(memory_space=pl.ANY),
                  pl.BlockSpec((nt, D), lambda i: (i, 0)),
                  pl.BlockSpec((nt, TOP_K), lambda i: (i, 0)),
                  pl.BlockSpec((1, D), lambda i: (0, 0)), pl.BlockSpec((1, D), lambda i: (0, 0))],
        out_specs=pl.BlockSpec((nt, D), lambda i: (i, 0)),
        out_shape=jax.ShapeDtypeStruct((T, D), F32),
        scratch_shapes=[pltpu.VMEM((2, TOP_K, nt, D), F32), pltpu.SemaphoreType.DMA((2,))],
        compiler_params=pltpu.CompilerParams(dimension_semantics=("arbitrary",)),
        name="moe_combine",
    )(dest_tiles, dest_tiles, yb, x2, gates_tok, g3, b3)


def _rel_bucket(dist):
    max_exact = N_BUCKETS // 2
    df = jnp.maximum(dist, 1).astype(F32)
    large = max_exact + (jnp.log(df / max_exact) / math.log(MAX_DISTANCE / max_exact)
                         * (N_BUCKETS - max_exact)).astype(I32)
    return jnp.where(dist < max_exact, dist, jnp.minimum(large, N_BUCKETS - 1))


def _bias_tables(rel_table):
    n = BAND
    taps = jnp.arange(n + 1)
    by_tap = jnp.stack([rel_table[_rel_bucket(taps * dil)].astype(F32) for _, dil in PATTERNS])
    w = jnp.concatenate([by_tap[:, ::-1], jnp.full((len(PATTERNS), n - 1, N_HEADS), NEG_INF, F32)], axis=1)
    w = jnp.transpose(w, (0, 2, 1))
    return by_tap, w.reshape(len(PATTERNS), N_SLABS, HEADS_PER_SLAB, 2 * n)


def _tap_bias_by_slot(by_tap, W):
    assert PAST_LEN % W == 0 and PAST_LEN >= WINDOW_MAX
    n = BAND
    out = []
    for p, (window, dil) in enumerate(PATTERNS):
        taps = by_tap[p, :0:-1]
        col = jnp.concatenate([jnp.full((W // dil - n, N_HEADS), NEG_INF, F32), taps], axis=0)
        rest = jnp.full((W // dil, dil - 1, N_HEADS), NEG_INF, F32)
        out.append(jnp.concatenate([col[:, None, :], rest], axis=1).reshape(W, N_HEADS).T)
    return jnp.stack(out)


def _dest_tiles(dest, nt):
    T = dest.shape[1]
    return dest.reshape(TOP_K, T // nt, nt).transpose(1, 0, 2).reshape(T // nt, 1, TOP_K * nt)


def kernel(x_prompt, x_sample, mem_prompt, cache_win_k, cache_win_v, state_conv, cache_mem_k, cache_mem_v,
           rel_bias_table, w_in, w_conv, w_out, ln1_g, ln1_b, w_mem_q, w_mem_k, w_mem_v, w_mem_o,
           ln2_g, ln2_b, w_router, b_router, w_gate_up, b_gate_up, w_down, b_down, ln3_g, ln3_b):
    depth = w_in.shape[0]
    assert depth == 1
    alpha = (2 * depth) ** 0.25
    B, S, D = x_prompt.shape
    DB = x_sample.shape[0]
    T = B * S
    l = 0
    vec = lambda a: a[l].reshape(1, -1)

    by_tap, bias_band = _bias_tables(rel_bias_table)
    wrt = w_router[l].T.astype(BF16)
    br = b_router[l].reshape(N_EXPERTS, 1)
    w_in_bf = w_in[l].astype(BF16)
    wo_bf = w_out[l].astype(BF16)
    wq_bf = w_mem_q[l].astype(BF16)
    wmo_bf = w_mem_o[l].astype(BF16)

    xs = x_sample.reshape(DB, D)
    proj = _inproj_sample(xs, w_in_bf)
    cw = CONV_WIDTH
    q_s = proj[:, 3 * cw:3 * cw + ATTN_WIDTH]
    k_s = proj[:, 3 * cw + ATTN_WIDTH:3 * cw + 2 * ATTN_WIDTH]
    v_s = proj[:, 3 * cw + 2 * ATTN_WIDTH:]
    H, E = N_HEADS, HEAD_DIM
    bias_new = by_tap[:, 0, :, None]
    slots_last = lambda c: jnp.transpose(c, (0, 2, 3, 1))
    corner = lambda a: jnp.pad(a.reshape(DB, H, E), ((0, 0), (0, LANES - H), (0, LANES - E)))
    conv_p, q_p, k_p, v_p, cs_p, attn_s = _inproj_prompt_and_attention_sample(
        x_prompt, w_in_bf, w_conv[l], corner(q_s), corner(k_s), corner(v_s),
        slots_last(cache_win_k[l]), slots_last(cache_win_v[l]), _tap_bias_by_slot(by_tap, WINDOW_MAX), bias_new)
    attn_s = jnp.transpose(attn_s[:, :, :H], (0, 2, 1)).reshape(DB, ATTN_WIDTH)

    sc = state_conv[l]
    x1_s, qm_s, u_s = _tail_a_sample(xs, proj, sc[:, 0], sc[:, 1], w_conv[l], attn_s, wo_bf,
                                     vec(ln1_g), vec(ln1_b), wq_bf, alpha)
    attn_p = _attention_prompt(q_p, k_p, v_p, bias_band)
    mk, mv, mk_bf, mv_bf = _memkv_prompt(mem_prompt, w_mem_k[l].astype(BF16), w_mem_v[l].astype(BF16))
    x2_p, idx_p, gate_p, rank_p, cnt_p, o_s = _mid_prompt_and_memattn_sample(
        x_prompt, conv_p, attn_p, wo_bf[:CONV_WIDTH], wo_bf[CONV_WIDTH:], vec(ln1_g), vec(ln1_b), mk_bf, mv_bf,
        wq_bf, wmo_bf, vec(ln2_g), vec(ln2_b), wrt, br,
        qm_s.reshape(DB, MEM_HEADS, MEM_HEAD_DIM), cache_mem_k[l], cache_mem_v[l], alpha)
    x2_s, idx_s, gate_s, rank_s, cnt = _tail_b_sample(x1_s, o_s.reshape(DB, D), wmo_bf, vec(ln2_g), vec(ln2_b), wrt, br,
                                                      cnt_p, alpha)

    bm = ROW_BLOCK
    counts = cnt[:, 0].astype(I32)
    ends = jnp.cumsum(counts)
    starts = ends - counts
    n_tok_rows = (T + DB) * TOP_K
    n_blocks = -(-n_tok_rows // bm)
    P = n_blocks * bm
    cuts = jnp.sort(jnp.concatenate([jnp.arange(n_blocks, dtype=I32) * bm, ends[:-1]]))
    cut_ends = jnp.concatenate([cuts[1:], jnp.full((1,), P, I32)])
    seg_blk = jnp.minimum(cuts // bm, n_blocks - 1)
    seg_exp = jnp.minimum(jnp.sum((ends[None, :] <= cuts[:, None]).astype(I32), axis=1), N_EXPERTS - 1)
    seg_lo = cuts - seg_blk * bm
    seg_hi = cut_ends - seg_blk * bm
    expert_ids = jnp.arange(N_EXPERTS, dtype=I32)[:, None, None]
    start_of = lambda idx: jnp.sum(jnp.where(idx[None] == expert_ids, starts[:, None, None], 0), axis=0)
    dest_p = start_of(idx_p) + rank_p
    dest_s = start_of(idx_s) + rank_s
    dtiles_p = _dest_tiles(dest_p, COMBINE_TILE)
    dtiles_s = _dest_tiles(dest_s, DB)

    x2_pf = x2_p.reshape(T, D)
    xb = _dispatch(x2_pf, dtiles_p, x2_s, dtiles_s, P)
    yb = _moe(seg_blk, seg_exp, seg_lo, seg_hi, xb, w_gate_up[l], b_gate_up[l], w_down[l], b_down[l])
    y_p = _combine(yb, dtiles_p, x2_pf, gate_p.T, vec(ln3_g), vec(ln3_b), alpha).reshape(B, S, D)
    y_s = _combine(yb, dtiles_s, x2_s, gate_s.T, vec(ln3_g), vec(ln3_b), alpha).reshape(DB, 1, D)

    conv_state_s = jnp.stack([sc[:, 1], u_s], axis=1)
    return (y_p, y_s,
            k_p.reshape(1, B, S, H, E), v_p.reshape(1, B, S, H, E), cs_p[:, 6:8].reshape(1, B, CONV_K - 1, CONV_WIDTH),
            mk.reshape(1, B, MEM_TOKENS, MEM_HEADS, MEM_HEAD_DIM), mv.reshape(1, B, MEM_TOKENS, MEM_HEADS, MEM_HEAD_DIM),
            k_s.reshape(1, DB, 1, H, E), v_s.reshape(1, DB, 1, H, E), conv_state_s.reshape(1, DB, CONV_K - 1, CONV_WIDTH))
```

```python
import functools
import math

import jax
import jax.numpy as jnp
import numpy as np
from jax import lax
from jax.experimental import pallas as pl
from jax.experimental.pallas import tpu as pltpu

F32 = jnp.float32
BF16 = jnp.bfloat16
I32 = jnp.int32

D_MODEL = 1024
CONV_WIDTH = 256
CONV_K = 3
HEAD_DIM = 64
N_HEADS = 12
ATTN_WIDTH = N_HEADS * HEAD_DIM
PATTERNS = ((128, 1), (512, 4), (2048, 16))
BAND = 128
WINDOW_MAX = 2048
ATTN_SCALE = 1.0 / math.sqrt(HEAD_DIM)
N_BUCKETS = 32
MAX_DISTANCE = WINDOW_MAX
MEM_TOKENS = 256
MEM_HEADS = 4
MEM_HEAD_DIM = 256
MEM_SCALE = 1.0 / math.sqrt(MEM_HEAD_DIM)
N_EXPERTS = 32
TOP_K = 4
D_FF = 1024
SWIGLU_LIMIT = 7.0
SWIGLU_ALPHA = 1.702
LN_EPS = 1e-5
PAST_LEN = 8192
NEG_INF = -1e30

LANES = 128
HEADS_PER_SLAB = LANES // HEAD_DIM
N_SLABS = ATTN_WIDTH // LANES

PROJ_TILE = 512
SEQ_TILE = 512
ROW_BLOCK = 512
COMBINE_TILE = 256
ROW_COPY_UNROLL = 32
ATTN_GROUP = 6

HIGHEST = lax.Precision.HIGHEST

_FUSED_PROJ_VMEM_BYTES = 58 * 1024 * 1024


def _layer_norm(x, g, b):
    mu = jnp.mean(x, axis=-1, keepdims=True)
    var = jnp.mean(jnp.square(x - mu), axis=-1, keepdims=True)
    return (x - mu) * lax.rsqrt(var + LN_EPS) * g + b


def _dot(a, b):
    return jnp.dot(a, b, preferred_element_type=F32)


def _dot_nt(a, b):
    return lax.dot_general(a, b, (((1,), (1,)), ((), ())), preferred_element_type=F32)


def _dot_hi(a, b):
    return jnp.dot(a, b, preferred_element_type=F32, precision=HIGHEST)


def _inproj_kernel(x_ref, w_ref, wc_ref, conv_ref, q_ref, k_ref, v_ref, cs_ref, u_s):
    ts = x_ref.shape[0]
    cw = CONV_WIDTH

    @pl.when(pl.program_id(1) == 0)
    def _():
        u_s[0:8, :] = jnp.zeros((8, cw), F32)

    x = x_ref[...].astype(BF16)
    gb = _dot(x, w_ref[:, 0:cw])
    gc = _dot(x, w_ref[:, cw:2 * cw])
    h = _dot(x, w_ref[:, 2 * cw:3 * cw])
    u = gc * h
    u_s[8:8 + ts, :] = u
    wc = wc_ref[...]
    conv = wc[0:1, :] * u_s[6:6 + ts, :] + wc[1:2, :] * u_s[7:7 + ts, :] + wc[2:3, :] * u
    conv_ref[...] = (gb * conv).astype(conv_ref.dtype)
    o = 3 * cw
    q_ref[...] = _dot(x, w_ref[:, o:o + ATTN_WIDTH]) * ATTN_SCALE
    k_ref[...] = _dot(x, w_ref[:, o + ATTN_WIDTH:o + 2 * ATTN_WIDTH])
    v_ref[...] = _dot(x, w_ref[:, o + 2 * ATTN_WIDTH:o + 3 * ATTN_WIDTH])
    tail = u_s[ts:ts + 8, :]
    u_s[0:8, :] = tail
    cs_ref[...] = tail


def _attn_kernel(q_ref, k_ref, v_ref, w_ref, o_ref, bias_ref, m_s, l_s, acc_s):
    S = q_ref.shape[0]
    n = BAND
    lane = lax.broadcasted_iota(I32, (n, LANES), 1)
    head_a = lane < HEAD_DIM

    @pl.when(pl.program_id(1) == 0)
    def _():
        for p in range(len(PATTERNS)):
            for hh in range(HEADS_PER_SLAB):
                first_row = jnp.broadcast_to(w_ref[p, hh:hh + 1, :], (n, 2 * n))
                bias_ref[p, hh * n:(hh + 1) * n, :] = pltpu.roll(first_row, 0, 1, stride=1, stride_axis=0)

    def rows_of(start, count, dil):
        return pl.ds(start, count) if dil == 1 else pl.ds(start, count, stride=dil)

    def blocks(p, dil, specs):
        rows = [rows_of(start, n, dil) for start, _ in specs]
        scores, values = [], []
        for (start, has_prev), r in zip(specs, rows):
            qb = q_ref[r, :]
            q2 = jnp.concatenate([jnp.where(head_a, qb, 0.0), jnp.where(head_a, 0.0, qb)], axis=0).astype(BF16)
            krows = rows_of(start - n * dil, 2 * n, dil) if has_prev else r
            bias = bias_ref[p] if has_prev else bias_ref[p, :, n:2 * n]
            scores.append(_dot_nt(q2, k_ref[krows, :].astype(BF16)) + bias)
            values.append(v_ref[krows, :].astype(BF16))
        ms = [jnp.max(s, axis=-1, keepdims=True) for s in scores]
        es = [jnp.exp(s - m) for s, m in zip(scores, ms)]
        ls = [jnp.sum(e, axis=-1, keepdims=True) for e in es]
        pvs = [_dot(e.astype(BF16), vb) for e, vb in zip(es, values)]
        for r, m, l, pv in zip(rows, ms, ls, pvs):
            m_s[p, r, :] = jnp.where(head_a, m[:n], m[n:])
            l_s[p, r, :] = jnp.where(head_a, l[:n], l[n:])
            acc_s[p, r, :] = jnp.where(head_a, pv[:n], pv[n:])

    for p, (window, dil) in enumerate(PATTERNS):
        nb = (S // dil) // n
        specs = [(i * n * dil + r, i > 0) for r in range(dil) for i in range(nb)]
        for g in range(0, len(specs), ATTN_GROUP):
            blocks(p, dil, specs[g:g + ATTN_GROUP])

    rows_per_step = 256

    def merge(t, c):
        rows = pl.ds(t * rows_per_step, rows_per_step)
        ms = [m_s[p, rows, :] for p in range(len(PATTERNS))]
        m_all = jnp.maximum(jnp.maximum(ms[0], ms[1]), ms[2])
        num = den = None
        for p in range(len(PATTERNS)):
            w = jnp.exp(ms[p] - m_all)
            num = w * acc_s[p, rows, :] if num is None else num + w * acc_s[p, rows, :]
            den = w * l_s[p, rows, :] if den is None else den + w * l_s[p, rows, :]
        o_ref[rows, :] = (num / den).astype(o_ref.dtype)
        return c

    lax.fori_loop(0, S // rows_per_step, merge, 0)


def _attention_prompt(q, k, v, bias_tab):
    B, S, _ = q.shape
    slab = lambda p, b: (b, 0, p)
    spec = pl.BlockSpec((None, S, LANES), slab)
    return pl.pallas_call(
        _attn_kernel,
        grid=(N_SLABS, B),
        in_specs=[spec, spec, spec,
                  pl.BlockSpec((len(PATTERNS), None, HEADS_PER_SLAB, 2 * BAND), lambda p, b: (0, p, 0, 0))],
        out_specs=spec,
        out_shape=jax.ShapeDtypeStruct((B, S, ATTN_WIDTH), BF16),
        scratch_shapes=[pltpu.VMEM((len(PATTERNS), HEADS_PER_SLAB * BAND, 2 * BAND), F32)]
        + [pltpu.VMEM((len(PATTERNS), S, LANES), F32)] * 3,
        compiler_params=pltpu.CompilerParams(dimension_semantics=("arbitrary", "arbitrary")),
        name="dilated_attn_prompt",
    )(q, k, v, bias_tab)


def _memkv_kernel(m_ref, wk_ref, wv_ref, k_ref, v_ref, kb_ref, vb_ref):
    x = m_ref[...].astype(BF16)
    k = _dot(x, wk_ref[...])
    v = _dot(x, wv_ref[...])
    k_ref[...] = k.reshape(k_ref.shape)
    v_ref[...] = v.reshape(v_ref.shape)
    kb_ref[...] = k.astype(BF16)
    vb_ref[...] = v.astype(BF16)


def _memkv_prompt(mem, wk_bf, wv_bf):
    B, M, D = mem.shape
    row = lambda b: (b, 0, 0)
    full = lambda b: (0, 0)
    heads = pl.BlockSpec((None, M, MEM_HEADS, MEM_HEAD_DIM), lambda b: (b, 0, 0, 0))
    return pl.pallas_call(
        _memkv_kernel,
        grid=(B,),
        in_specs=[pl.BlockSpec((None, M, D), row), pl.BlockSpec((D, D), full), pl.BlockSpec((D, D), full)],
        out_specs=[heads, heads, pl.BlockSpec((None, M, D), row), pl.BlockSpec((None, M, D), row)],
        out_shape=[jax.ShapeDtypeStruct((B, M, MEM_HEADS, MEM_HEAD_DIM), F32)] * 2
        + [jax.ShapeDtypeStruct((B, M, D), BF16)] * 2,
        compiler_params=pltpu.CompilerParams(dimension_semantics=("arbitrary",)),
        name="memkv_prompt",
    )(mem, wk_bf, wv_bf)


def _earlier_token_matrix(nt):
    before = lax.broadcasted_iota(I32, (nt, nt), 0) < lax.broadcasted_iota(I32, (nt, nt), 1)
    return jnp.where(before, 1.0, 0.0).astype(BF16)


def _route(logits_t, carry, earlier):
    E, nt = logits_t.shape
    eidx = lax.broadcasted_iota(I32, (E, nt), 0)
    l = logits_t
    vals, idxs = [], []
    for _ in range(TOP_K):
        m = jnp.max(l, axis=0, keepdims=True)
        sel = jnp.min(jnp.where(l == m, eidx, E), axis=0, keepdims=True)
        vals.append(m)
        idxs.append(sel)
        l = jnp.where(eidx == sel, -jnp.inf, l)
    es = [jnp.exp(v - vals[0]) for v in vals]
    den = es[0] + es[1] + es[2] + es[3]
    gates = _stack_rows([e / den for e in es])
    chosen = (l == -jnp.inf)
    onehot = jnp.where(chosen, 1.0, 0.0)
    prefix = _dot(onehot.astype(BF16), earlier) + carry
    ranks = [jnp.sum(jnp.where(eidx == s, prefix, 0.0), axis=0, keepdims=True) for s in idxs]
    rank = _stack_rows(ranks).astype(I32)
    idx = _stack_rows(idxs)
    return idx, gates, rank, carry + jnp.sum(onehot, axis=1, keepdims=True)


def _stack_rows(rows):
    k, nt = len(rows), rows[0].shape[1]
    r = lax.broadcasted_iota(I32, (k, nt), 0)
    out = jnp.broadcast_to(rows[-1], (k, nt))
    for i in range(k - 2, -1, -1):
        out = jnp.where(r == i, rows[i], out)
    return out


def _mid_kernel(x_ref, conv_ref, attn_ref, woc_ref, woa_ref, g1_ref, b1_ref, mk_ref, mv_ref, wq_ref, wmo_ref,
                g2_ref, b2_ref, wrt_ref, br_ref,
                x2_ref, idx_ref, gate_ref, rank_ref, cnt_ref, carry_s, earlier_s, *, alpha):
    first = jnp.logical_and(pl.program_id(0) == 0, pl.program_id(1) == 0)

    @pl.when(first)
    def _():
        carry_s[...] = jnp.zeros_like(carry_s)
        earlier_s[...] = _earlier_token_matrix(earlier_s.shape[0])

    x = x_ref[...]
    mix = _dot(conv_ref[...], woc_ref[...]) + _dot(attn_ref[...], woa_ref[...])
    x1 = _layer_norm(alpha * x + mix, g1_ref[...], b1_ref[...])
    qm = (_dot(x1.astype(BF16), wq_ref[...]) * MEM_SCALE).astype(BF16)
    outs = []
    for h in range(MEM_HEADS):
        hs = slice(h * MEM_HEAD_DIM, (h + 1) * MEM_HEAD_DIM)
        s = _dot_nt(qm[:, hs], mk_ref[:, hs])
        m = jnp.max(s, axis=-1, keepdims=True)
        e = jnp.exp(s - m)
        l = jnp.sum(e, axis=-1, keepdims=True)
        outs.append((_dot(e.astype(BF16), mv_ref[:, hs]) / l).astype(BF16))
    o = jnp.concatenate(outs, axis=-1)
    x2 = _layer_norm(alpha * x1 + _dot(o, wmo_ref[...]), g2_ref[...], b2_ref[...])
    x2_ref[...] = x2
    logits_t = _dot_nt(wrt_ref[...], x2.astype(BF16)) + br_ref[...]
    idx, gates, rank, carry = _route(logits_t, carry_s[:, 0:1], earlier_s[...])
    idx_ref[...] = idx
    gate_ref[...] = gates
    rank_ref[...] = rank
    carry_s[...] = jnp.broadcast_to(carry, carry_s.shape)
    cnt_ref[...] = carry_s[...]


def _round_bf16(x):
    return x.astype(BF16).astype(F32)


def _inproj_sample_kernel(x_ref, w_ref, o_ref):
    o_ref[...] = _dot(x_ref[...].astype(BF16), w_ref[...])


def _inproj_sample(x, w_in):
    n, D = x.shape
    N = w_in.shape[1]
    bn = 512
    return pl.pallas_call(
        _inproj_sample_kernel,
        grid=(N // bn,),
        in_specs=[pl.BlockSpec((n, D), lambda j: (0, 0)), pl.BlockSpec((D, bn), lambda j: (0, j))],
        out_specs=pl.BlockSpec((n, bn), lambda j: (0, j)),
        out_shape=jax.ShapeDtypeStruct((n, N), F32),
        compiler_params=pltpu.CompilerParams(dimension_semantics=("arbitrary",)),
        name="inproj_sample",
    )(x, w_in)


def _attn_sample_kernel(q_ref, kn_ref, vn_ref, kt_ref, vt_ref, bias_ref, bnew_ref, o_ref):
    H, E, W = kt_ref.shape

    def columns(ref):
        t = ref[...].T
        return jnp.stack([t[:E, h:h + 1] for h in range(H)], axis=0)

    q = _round_bf16(columns(q_ref))
    v_new = _round_bf16(columns(vn_ref))
    s_all = jnp.sum(_round_bf16(kt_ref[...]) * q, axis=1) * ATTN_SCALE
    s_new = jnp.sum(_round_bf16(columns(kn_ref)) * q, axis=1) * ATTN_SCALE
    outs, lses = [], []
    for p, (window, dil) in enumerate(PATTERNS):
        lo = W - window
        s = s_all[:, lo:] + bias_ref[p, :, lo:]
        sn = s_new + bnew_ref[p]
        m = jnp.maximum(jnp.max(s, axis=-1, keepdims=True), sn)
        e = jnp.exp(s - m)
        en = jnp.exp(sn - m)
        den = jnp.sum(e, axis=-1, keepdims=True) + en
        pr = _round_bf16(e / den)
        pv = jnp.sum(_round_bf16(vt_ref[:, :, lo:]) * pr[:, None, :], axis=-1, keepdims=True)
        outs.append(pv + _round_bf16(en / den)[:, :, None] * v_new)
        lses.append(m + jnp.log(den))
    lmax = jnp.maximum(jnp.maximum(lses[0], lses[1]), lses[2])
    ws = [jnp.exp(ls - lmax) for ls in lses]
    wsum = ws[0] + ws[1] + ws[2]
    acc = None
    for p in range(len(PATTERNS)):
        term = _round_bf16(ws[p] / wsum)[:, :, None] * _round_bf16(outs[p])
        acc = term if acc is None else acc + term
    lane = lax.broadcasted_iota(I32, o_ref.shape, 1)
    out = jnp.zeros(o_ref.shape, F32)
    for h in range(H):
        out = jnp.where(lane == h, acc[h], out)
    o_ref[...] = out


def _inproj_and_sample_attn_kernel(x_ref, w_ref, wc_ref, q4_ref, kn_ref, vn_ref, kt_ref, vt_ref, bpos_ref, bnew_ref,
                                   conv_ref, q_ref, k_ref, v_ref, cs_ref, attn_ref, u_s):
    _inproj_kernel(x_ref, w_ref, wc_ref, conv_ref, q_ref, k_ref, v_ref, cs_ref, u_s)
    _attn_sample_kernel(q4_ref, kn_ref, vn_ref, kt_ref, vt_ref, bpos_ref, bnew_ref, attn_ref)


def _inproj_prompt_and_attention_sample(x, w_in_bf, w_conv, q, k_new, v_new, win_kt, win_vt, bias_pos, bias_new):
    B, S, D = x.shape
    ts = PROJ_TILE
    n_s = S // ts
    DB, H, E, W = win_kt.shape
    assert B * n_s == DB
    row = lambda b, s: (b, s, 0)
    tok = lambda b, s: (b * n_s + s, 0, 0, 0)
    vec = pl.BlockSpec((None, LANES, LANES), lambda b, s: (b * n_s + s, 0, 0))
    vec_out = pl.BlockSpec((None, E, LANES), lambda b, s: (b * n_s + s, 0, 0))
    cache = pl.BlockSpec((None, H, E, W), tok)
    const = lambda a: pl.BlockSpec(a.shape, lambda b, s: (0,) * a.ndim)
    return pl.pallas_call(
        _inproj_and_sample_attn_kernel,
        grid=(B, n_s),
        in_specs=[pl.BlockSpec((None, ts, D), row), const(w_in_bf), const(w_conv),
                  vec, vec, vec, cache, cache, const(bias_pos), const(bias_new)],
        out_specs=[pl.BlockSpec((None, ts, CONV_WIDTH), row),
                   pl.BlockSpec((None, ts, ATTN_WIDTH), row),
                   pl.BlockSpec((None, ts, ATTN_WIDTH), row),
                   pl.BlockSpec((None, ts, ATTN_WIDTH), row),
                   pl.BlockSpec((None, 8, CONV_WIDTH), lambda b, s: (b, 0, 0)),
                   vec_out],
        out_shape=[jax.ShapeDtypeStruct((B, S, CONV_WIDTH), BF16),
                   jax.ShapeDtypeStruct((B, S, ATTN_WIDTH), F32),
                   jax.ShapeDtypeStruct((B, S, ATTN_WIDTH), F32),
                   jax.ShapeDtypeStruct((B, S, ATTN_WIDTH), F32),
                   jax.ShapeDtypeStruct((B, 8, CONV_WIDTH), F32),
                   jax.ShapeDtypeStruct((DB, E, LANES), F32)],
        scratch_shapes=[pltpu.VMEM((ts + 8, CONV_WIDTH), F32)],
        compiler_params=pltpu.CompilerParams(dimension_semantics=("arbitrary", "arbitrary"),
                                             vmem_limit_bytes=_FUSED_PROJ_VMEM_BYTES),
        name="inproj_prompt_attn_sample",
    )(x, w_in_bf, w_conv, q, k_new, v_new, win_kt, win_vt, bias_pos, bias_new)


def _tail_a_sample_kernel(x_ref, proj_ref, s0_ref, s1_ref, wc_ref, attn_ref, wo_ref, g1_ref, b1_ref, wq_ref,
                          x1_ref, qm_ref, u_ref, *, alpha):
    cw = CONV_WIDTH
    gb = proj_ref[:, 0:cw]
    u = proj_ref[:, cw:2 * cw] * proj_ref[:, 2 * cw:3 * cw]
    u_ref[...] = u
    wc = wc_ref[...]
    conv = gb * (wc[0:1, :] * s0_ref[...] + wc[1:2, :] * s1_ref[...] + wc[2:3, :] * u)
    mixed = jnp.concatenate([conv, attn_ref[...]], axis=-1).astype(BF16)
    x1 = _layer_norm(alpha * x_ref[...] + _dot(mixed, wo_ref[...]), g1_ref[...], b1_ref[...])
    x1_ref[...] = x1
    qm_ref[...] = _dot(x1.astype(BF16), wq_ref[...])


def _tail_a_sample(x, proj, s0, s1, w_conv, attn, w_out, g1, b1, w_mem_q, alpha):
    n, D = x.shape
    return pl.pallas_call(
        functools.partial(_tail_a_sample_kernel, alpha=alpha),
        out_shape=[jax.ShapeDtypeStruct((n, D), F32)] * 2 + [jax.ShapeDtypeStruct((n, CONV_WIDTH), F32)],
        name="tail_a_sample",
    )(x, proj, s0, s1, w_conv, attn, w_out, g1, b1, w_mem_q)


def _memattn_sample_kernel(q_ref, k_ref, v_ref, o_ref):
    q = _round_bf16(q_ref[...])
    s = jnp.sum(_round_bf16(k_ref[...]) * q[None], axis=-1, keepdims=True) * MEM_SCALE
    m = jnp.max(s, axis=0)
    e = jnp.exp(s - m[None])
    den = jnp.sum(e, axis=0)
    pr = _round_bf16(e / den[None])
    o_ref[...] = jnp.sum(pr * _round_bf16(v_ref[...]), axis=0)


_N_MID_IN = 15


def _mid_and_sample_memattn_kernel(*refs, alpha):
    mid_in, (qm_ref, smk_ref, smv_ref) = refs[:_N_MID_IN], refs[_N_MID_IN:_N_MID_IN + 3]
    mid_out, so_ref, (carry_s, earlier_s) = refs[_N_MID_IN + 3:-3], refs[-3], refs[-2:]
    _mid_kernel(*mid_in, *mid_out, carry_s, earlier_s, alpha=alpha)
    _memattn_sample_kernel(qm_ref, smk_ref, smv_ref, so_ref)


def _mid_prompt_and_memattn_sample(x, conv, attn, woc, woa, g1, b1, mk, mv, wq, wmo, g2, b2, wrt, br,
                                   qm_s, mem_k_s, mem_v_s, alpha):
    B, S, D = x.shape
    ts = SEQ_TILE
    n_s = S // ts
    T = B * S
    DB, M, H, E = mem_k_s.shape
    assert B * n_s == DB
    row = lambda b, s: (b, s, 0)
    full = lambda b, s: (0, 0)
    tok = lambda b, s: (0, b * n_s + s)
    mem = lambda b, s: (b, 0, 0)
    vec = pl.BlockSpec((1, D), full)
    s_vec = pl.BlockSpec((None, H, E), lambda b, s: (b * n_s + s, 0, 0))
    s_mem = pl.BlockSpec((None, M, H, E), lambda b, s: (b * n_s + s, 0, 0, 0))
    in_specs = [pl.BlockSpec((None, ts, D), row),
                pl.BlockSpec((None, ts, CONV_WIDTH), row),
                pl.BlockSpec((None, ts, ATTN_WIDTH), row),
                pl.BlockSpec(woc.shape, full), pl.BlockSpec(woa.shape, full), vec, vec,
                pl.BlockSpec((None, MEM_TOKENS, D), mem), pl.BlockSpec((None, MEM_TOKENS, D), mem),
                pl.BlockSpec(wq.shape, full), pl.BlockSpec(wmo.shape, full), vec, vec,
                pl.BlockSpec(wrt.shape, full), pl.BlockSpec(br.shape, full)]
    assert len(in_specs) == _N_MID_IN
    return pl.pallas_call(
        functools.partial(_mid_and_sample_memattn_kernel, alpha=alpha),
        grid=(B, n_s),
        in_specs=in_specs + [s_vec, s_mem, s_mem],
        out_specs=[pl.BlockSpec((None, ts, D), row),
                   pl.BlockSpec((TOP_K, ts), tok), pl.BlockSpec((TOP_K, ts), tok), pl.BlockSpec((TOP_K, ts), tok),
                   pl.BlockSpec((N_EXPERTS, LANES), full), s_vec],
        out_shape=[jax.ShapeDtypeStruct((B, S, D), F32),
                   jax.ShapeDtypeStruct((TOP_K, T), I32),
                   jax.ShapeDtypeStruct((TOP_K, T), F32),
                   jax.ShapeDtypeStruct((TOP_K, T), I32),
                   jax.ShapeDtypeStruct((N_EXPERTS, LANES), F32),
                   jax.ShapeDtypeStruct((DB, H, E), F32)],
        scratch_shapes=[pltpu.VMEM((N_EXPERTS, LANES), F32), pltpu.VMEM((ts, ts), BF16)],
        compiler_params=pltpu.CompilerParams(dimension_semantics=("arbitrary", "arbitrary")),
        name="mid_prompt_memattn_sample",
    )(x, conv, attn, woc, woa, g1, b1, mk, mv, wq, wmo, g2, b2, wrt, br, qm_s, mem_k_s, mem_v_s)


def _tail_b_sample_kernel(x1_ref, o_ref, wmo_ref, g2_ref, b2_ref, wrt_ref, br_ref, cnt_ref,
                          x2_ref, idx_ref, gate_ref, rank_ref, cnt_out_ref, *, alpha):
    x2 = _layer_norm(alpha * x1_ref[...] + _dot(o_ref[...].astype(BF16), wmo_ref[...]), g2_ref[...], b2_ref[...])
    x2_ref[...] = x2
    logits_t = _dot_nt(wrt_ref[...], x2.astype(BF16)) + br_ref[...]
    idx, gates, rank, carry = _route(logits_t, cnt_ref[:, 0:1], _earlier_token_matrix(logits_t.shape[1]))
    idx_ref[...] = idx
    gate_ref[...] = gates
    rank_ref[...] = rank
    cnt_out_ref[...] = jnp.broadcast_to(carry, cnt_out_ref.shape)


def _tail_b_sample(x1, o, w_mem_o, g2, b2, wrt, br, cnt, alpha):
    n, D = x1.shape
    return pl.pallas_call(
        functools.partial(_tail_b_sample_kernel, alpha=alpha),
        out_shape=[jax.ShapeDtypeStruct((n, D), F32),
                   jax.ShapeDtypeStruct((TOP_K, n), I32),
                   jax.ShapeDtypeStruct((TOP_K, n), F32),
                   jax.ShapeDtypeStruct((TOP_K, n), I32),
                   jax.ShapeDtypeStruct(cnt.shape, F32)],
        name="tail_b_sample",
    )(x1, o, w_mem_o, g2, b2, wrt, br, cnt)


def _row_copy(src, dst, sem, src_row, dst_row):
    return pltpu.make_async_copy(src.at[pl.ds(src_row, 1)], dst.at[pl.ds(dst_row, 1)], sem)


def _dispatch_kernel(dest_ref, x_ref, dest2_ref, x2_ref, xb_ref, zeros_s, stage_s, sems, zsem, *, n_tok_rows):
    i = pl.program_id(0)
    last = pl.num_programs(0) - 1
    slot = i % 2
    nt = dest_ref.shape[1] // TOP_K

    def start(src_ref, first, count, idx_ref, sem):
        def issue(g, c):
            for u in range(ROW_COPY_UNROLL):
                t = g * ROW_COPY_UNROLL + u
                for k in range(TOP_K):
                    _row_copy(src_ref, xb_ref, sem, first + t, idx_ref[0, k * count + t]).start(priority=k % 2)
            return c

        lax.fori_loop(0, count // ROW_COPY_UNROLL, issue, 0)

    def drain(src_ref, count, sem):
        for k in range(TOP_K):
            pltpu.make_async_copy(src_ref.at[pl.ds(0, count)], xb_ref.at[pl.ds(0, count)], sem).wait()

    @pl.when(i < last)
    def _():
        stage_s[slot] = x_ref[...]
        start(stage_s.at[slot], 0, nt, dest_ref, sems.at[slot])

    @pl.when(i == last)
    def _():
        zeros_s[...] = jnp.zeros_like(zeros_s)
        pltpu.make_async_copy(zeros_s, xb_ref.at[pl.ds(n_tok_rows, zeros_s.shape[0])], zsem).start()
        start(x2_ref, 0, x2_ref.shape[0], dest2_ref, sems.at[slot])

    @pl.when(i > 0)
    def _():
        drain(stage_s.at[1 - slot], nt, sems.at[1 - slot])

    @pl.when(i == last)
    def _():
        drain(x2_ref, x2_ref.shape[0], sems.at[slot])
        pltpu.make_async_copy(zeros_s, xb_ref.at[pl.ds(n_tok_rows, zeros_s.shape[0])], zsem).wait()


def _dispatch(x_a, dest_a, x_b, dest_b, n_rows):
    Ta, D = x_a.shape
    Tb = x_b.shape[0]
    nt = dest_a.shape[2] // TOP_K
    steps = Ta // nt
    assert nt % ROW_COPY_UNROLL == 0 and Tb % ROW_COPY_UNROLL == 0
    n_tok_rows = (Ta + Tb) * TOP_K
    tile = lambda i: (jnp.minimum(i, steps - 1), 0, 0)
    return pl.pallas_call(
        functools.partial(_dispatch_kernel, n_tok_rows=n_tok_rows),
        grid=(steps + 1,),
        in_specs=[pl.BlockSpec((None, 1, TOP_K * nt), tile, memory_space=pltpu.SMEM),
                  pl.BlockSpec((nt, D), lambda i: (jnp.minimum(i, steps - 1), 0)),
                  pl.BlockSpec((None, 1, TOP_K * Tb), lambda i: (0, 0, 0), memory_space=pltpu.SMEM),
                  pl.BlockSpec((Tb, D), lambda i: (0, 0))],
        out_specs=pl.BlockSpec(memory_space=pl.ANY),
        out_shape=jax.ShapeDtypeStruct((n_rows, D), x_a.dtype),
        scratch_shapes=[pltpu.VMEM((n_rows - n_tok_rows, D), x_a.dtype), pltpu.VMEM((2, nt, D), x_a.dtype),
                        pltpu.SemaphoreType.DMA((2,)), pltpu.SemaphoreType.DMA],
        compiler_params=pltpu.CompilerParams(dimension_semantics=("arbitrary",)),
        name="moe_dispatch",
    )(dest_a, x_a, dest_b, x_b)


def _moe_kernel(blk_ref, exp_ref, lo_ref, hi_ref, xb_ref, wgu_ref, bgu_ref, wd_ref, bd_ref, yb_ref,
                wgu_s, wd_s, cast_s):
    s = pl.program_id(0)
    new_block = jnp.logical_or(s == 0, blk_ref[s] != blk_ref[jnp.maximum(s - 1, 0)])
    lo, hi = lo_ref[s], hi_ref[s]
    used = hi > lo

    @pl.when(s == 0)
    def _():
        cast_s[0] = -1

    @pl.when(jnp.logical_and(used, cast_s[0] != exp_ref[s]))
    def _():
        wgu_s[...] = wgu_ref[...].astype(BF16)
        wd_s[...] = wd_ref[...].astype(BF16)
        cast_s[0] = exp_ref[s]

    @pl.when(new_block)
    def _():
        yb_ref[...] = jnp.zeros_like(yb_ref)

    @pl.when(used)
    def _():
        gu = _dot(xb_ref[...].astype(BF16), wgu_s[...]) + bgu_ref[...]
        g = jnp.minimum(gu[:, :D_FF], SWIGLU_LIMIT)
        u = jnp.clip(gu[:, D_FF:], -SWIGLU_LIMIT, SWIGLU_LIMIT)
        act = (u + 1.0) * g * jax.nn.sigmoid(SWIGLU_ALPHA * g)
        y = _dot(act.astype(BF16), wd_s[...]) + bd_ref[...]
        row = lax.broadcasted_iota(I32, (xb_ref.shape[0], 1), 0)
        yb_ref[...] = jnp.where(jnp.logical_and(row >= lo, row < hi), y, yb_ref[...])


def _moe(seg_blk, seg_exp, seg_lo, seg_hi, xb, w_gate_up, b_gate_up, w_down, b_down):
    P, D = xb.shape
    bm = ROW_BLOCK
    E, _, F2 = w_gate_up.shape
    ex = lambda s, blk, exp, lo, hi: (exp[s], 0, 0)
    rows = lambda s, blk, exp, lo, hi: (blk[s], 0)
    expert_bytes = (D * F2 + D_FF * D) * (2 * 4 + 2)
    block_bytes = bm * D * 4 * 2 * 2 + bm * F2 * 4 * 3
    return pl.pallas_call(
        _moe_kernel,
        grid_spec=pltpu.PrefetchScalarGridSpec(
            num_scalar_prefetch=4,
            grid=(seg_blk.shape[0],),
            in_specs=[pl.BlockSpec((bm, D), rows),
                      pl.BlockSpec((None, D, F2), ex), pl.BlockSpec((None, 1, F2), ex),
                      pl.BlockSpec((None, D_FF, D), ex), pl.BlockSpec((None, 1, D), ex)],
            out_specs=pl.BlockSpec((bm, D), rows),
            scratch_shapes=[pltpu.VMEM((D, F2), BF16), pltpu.VMEM((D_FF, D), BF16), pltpu.SMEM((1,), I32)]),
        out_shape=jax.ShapeDtypeStruct((P, D), F32),
        compiler_params=pltpu.CompilerParams(dimension_semantics=("arbitrary",),
                                             vmem_limit_bytes=expert_bytes + block_bytes),
        name="moe_grouped_ffn",
    )(seg_blk, seg_exp, seg_lo, seg_hi, xb, w_gate_up, b_gate_up.reshape(E, 1, F2), w_down, b_down.reshape(E, 1, D))


def _combine_kernel(dest_ref, next_ref, yb_ref, x2_ref, gate_ref, g3_ref, b3_ref, o_ref, rows_s, sems, *, alpha):
    nt = x2_ref.shape[0]
    i = pl.program_id(0)
    slot = i % 2

    def gather(idx_ref, buf):
        def issue(g, c):
            for u in range(ROW_COPY_UNROLL):
                t = g * ROW_COPY_UNROLL + u
                for k in range(TOP_K):
                    _row_copy(yb_ref, rows_s.at[buf, k], sems.at[buf], idx_ref[0, k * nt + t], t).start(priority=k % 2)
            return c
        lax.fori_loop(0, nt // ROW_COPY_UNROLL, issue, 0)

    @pl.when(i == 0)
    def _():
        gather(dest_ref, 0)

    @pl.when(i + 1 < pl.num_programs(0))
    def _():
        gather(next_ref, 1 - slot)

    for k in range(TOP_K):
        pltpu.make_async_copy(yb_ref.at[pl.ds(0, nt)], rows_s.at[slot, k], sems.at[slot]).wait()
    gates = gate_ref[...]
    y = gates[:, 0:1] * rows_s[slot, 0]
    for k in range(1, TOP_K):
        y = y + gates[:, k:k + 1] * rows_s[slot, k]
    o_ref[...] = _layer_norm(alpha * x2_ref[...] + y, g3_ref[...], b3_ref[...])


def _combine(yb, dest_tiles, x2, gates_tok, g3, b3, alpha):
    T, D = x2.shape
    steps = dest_tiles.shape[0]
    nt = dest_tiles.shape[2] // TOP_K
    assert nt % ROW_COPY_UNROLL == 0
    idx_spec = lambda ahead: pl.BlockSpec((None, 1, TOP_K * nt), lambda i: (jnp.minimum(i + ahead, steps - 1), 0, 0),
                                          memory_space=pltpu.SMEM)
    return pl.pallas_call(
        functools.partial(_combine_kernel, alpha=alpha),
        grid=(steps,),
        in_specs=[idx_spec(0), idx_spec(1),
                  pl.BlockSpec(memory_space=pl.ANY),
                  pl.BlockSpec((nt, D), lambda i: (i, 0)),
                  pl.BlockSpec((nt, TOP_K), lambda i: (i, 0)),
                  pl.BlockSpec((1, D), lambda i: (0, 0)), pl.BlockSpec((1, D), lambda i: (0, 0))],
        out_specs=pl.BlockSpec((nt, D), lambda i: (i, 0)),
        out_shape=jax.ShapeDtypeStruct((T, D), F32),
        scratch_shapes=[pltpu.VMEM((2, TOP_K, nt, D), F32), pltpu.SemaphoreType.DMA((2,))],
        compiler_params=pltpu.CompilerParams(dimension_semantics=("arbitrary",)),
        name="moe_combine",
    )(dest_tiles, dest_tiles, yb, x2, gates_tok, g3, b3)


def _rel_bucket(dist):
    max_exact = N_BUCKETS // 2
    df = jnp.maximum(dist, 1).astype(F32)
    large = max_exact + (jnp.log(df / max_exact) / math.log(MAX_DISTANCE / max_exact)
                         * (N_BUCKETS - max_exact)).astype(I32)
    return jnp.where(dist < max_exact, dist, jnp.minimum(large, N_BUCKETS - 1))


def _bias_tables(rel_table):
    n = BAND
    taps = jnp.arange(n + 1)
    by_tap = jnp.stack([rel_table[_rel_bucket(taps * dil)].astype(F32) for _, dil in PATTERNS])
    w = jnp.concatenate([by_tap[:, ::-1], jnp.full((len(PATTERNS), n - 1, N_HEADS), NEG_INF, F32)], axis=1)
    w = jnp.transpose(w, (0, 2, 1))
    return by_tap, w.reshape(len(PATTERNS), N_SLABS, HEADS_PER_SLAB, 2 * n)


def _tap_bias_by_slot(by_tap, W):
    assert PAST_LEN % W == 0 and PAST_LEN >= WINDOW_MAX
    n = BAND
    out = []
    for p, (window, dil) in enumerate(PATTERNS):
        taps = by_tap[p, :0:-1]
        col = jnp.concatenate([jnp.full((W // dil - n, N_HEADS), NEG_INF, F32), taps], axis=0)
        rest = jnp.full((W // dil, dil - 1, N_HEADS), NEG_INF, F32)
        out.append(jnp.concatenate([col[:, None, :], rest], axis=1).reshape(W, N_HEADS).T)
    return jnp.stack(out)


def _dest_tiles(dest, nt):
    T = dest.shape[1]
    return dest.reshape(TOP_K, T // nt, nt).transpose(1, 0, 2).reshape(T // nt, 1, TOP_K * nt)


def kernel(x_prompt, x_sample, mem_prompt, cache_win_k, cache_win_v, state_conv, cache_mem_k, cache_mem_v,
           rel_bias_table, w_in, w_conv, w_out, ln1_g, ln1_b, w_mem_q, w_mem_k, w_mem_v, w_mem_o,
           ln2_g, ln2_b, w_router, b_router, w_gate_up, b_gate_up, w_down, b_down, ln3_g, ln3_b):
    depth = w_in.shape[0]
    assert depth == 1
    alpha = (2 * depth) ** 0.25
    B, S, D = x_prompt.shape
    DB = x_sample.shape[0]
    T = B * S
    l = 0
    vec = lambda a: a[l].reshape(1, -1)

    by_tap, bias_band = _bias_tables(rel_bias_table)
    wrt = w_router[l].T.astype(BF16)
    br = b_router[l].reshape(N_EXPERTS, 1)
    w_in_bf = w_in[l].astype(BF16)
    wo_bf = w_out[l].astype(BF16)
    wq_bf = w_mem_q[l].astype(BF16)
    wmo_bf = w_mem_o[l].astype(BF16)

    xs = x_sample.reshape(DB, D)
    proj = _inproj_sample(xs, w_in_bf)
    cw = CONV_WIDTH
    q_s = proj[:, 3 * cw:3 * cw + ATTN_WIDTH]
    k_s = proj[:, 3 * cw + ATTN_WIDTH:3 * cw + 2 * ATTN_WIDTH]
    v_s = proj[:, 3 * cw + 2 * ATTN_WIDTH:]
    H, E = N_HEADS, HEAD_DIM
    bias_new = by_tap[:, 0, :, None]
    slots_last = lambda c: jnp.transpose(c, (0, 2, 3, 1))
    corner = lambda a: jnp.pad(a.reshape(DB, H, E), ((0, 0), (0, LANES - H), (0, LANES - E)))
    conv_p, q_p, k_p, v_p, cs_p, attn_s = _inproj_prompt_and_attention_sample(
        x_prompt, w_in_bf, w_conv[l], corner(q_s), corner(k_s), corner(v_s),
        slots_last(cache_win_k[l]), slots_last(cache_win_v[l]), _tap_bias_by_slot(by_tap, WINDOW_MAX), bias_new)
    attn_s = jnp.transpose(attn_s[:, :, :H], (0, 2, 1)).reshape(DB, ATTN_WIDTH)

    sc = state_conv[l]
    x1_s, qm_s, u_s = _tail_a_sample(xs, proj, sc[:, 0], sc[:, 1], w_conv[l], attn_s, wo_bf,
                                     vec(ln1_g), vec(ln1_b), wq_bf, alpha)
    attn_p = _attention_prompt(q_p, k_p, v_p, bias_band)
    mk, mv, mk_bf, mv_bf = _memkv_prompt(mem_prompt, w_mem_k[l].astype(BF16), w_mem_v[l].astype(BF16))
    x2_p, idx_p, gate_p, rank_p, cnt_p, o_s = _mid_prompt_and_memattn_sample(
        x_prompt, conv_p, attn_p, wo_bf[:CONV_WIDTH], wo_bf[CONV_WIDTH:], vec(ln1_g), vec(ln1_b), mk_bf, mv_bf,
        wq_bf, wmo_bf, vec(ln2_g), vec(ln2_b), wrt, br,
        qm_s.reshape(DB, MEM_HEADS, MEM_HEAD_DIM), cache_mem_k[l], cache_mem_v[l], alpha)
    x2_s, idx_s, gate_s, rank_s, cnt = _tail_b_sample(x1_s, o_s.reshape(DB, D), wmo_bf, vec(ln2_g), vec(ln2_b), wrt, br,
                                                      cnt_p, alpha)

    bm = ROW_BLOCK
    counts = cnt[:, 0].astype(I32)
    ends = jnp.cumsum(counts)
    starts = ends - counts
    n_tok_rows = (T + DB) * TOP_K
    n_blocks = -(-n_tok_rows // bm)
    P = n_blocks * bm
    cuts = jnp.sort(jnp.concatenate([jnp.arange(n_blocks, dtype=I32) * bm, ends[:-1]]))
    cut_ends = jnp.concatenate([cuts[1:], jnp.full((1,), P, I32)])
    seg_blk = jnp.minimum(cuts // bm, n_blocks - 1)
    seg_exp = jnp.minimum(jnp.sum((ends[None, :] <= cuts[:, None]).astype(I32), axis=1), N_EXPERTS - 1)
    seg_lo = cuts - seg_blk * bm
    seg_hi = cut_ends - seg_blk * bm
    expert_ids = jnp.arange(N_EXPERTS, dtype=I32)[:, None, None]
    start_of = lambda idx: jnp.sum(jnp.where(idx[None] == expert_ids, starts[:, None, None], 0), axis=0)
    dest_p = start_of(idx_p) + rank_p
    dest_s = start_of(idx_s) + rank_s
    dtiles_p = _dest_tiles(dest_p, COMBINE_TILE)
    dtiles_s = _dest_tiles(dest_s, DB)

    x2_pf = x2_p.reshape(T, D)
    xb = _dispatch(x2_pf, dtiles_p, x2_s, dtiles_s, P)
    yb = _moe(seg_blk, seg_exp, seg_lo, seg_hi, xb, w_gate_up[l], b_gate_up[l], w_down[l], b_down[l])
    y_p = _combine(yb, dtiles_p, x2_pf, gate_p.T, vec(ln3_g), vec(ln3_b), alpha).reshape(B, S, D)
    y_s = _combine(yb, dtiles_s, x2_s, gate_s.T, vec(ln3_g), vec(ln3_b), alpha).reshape(DB, 1, D)

    conv_state_s = jnp.stack([sc[:, 1], u_s], axis=1)
    return (y_p, y_s,
            k_p.reshape(1, B, S, H, E), v_p.reshape(1, B, S, H, E), cs_p[:, 6:8].reshape(1, B, CONV_K - 1, CONV_WIDTH),
            mk[None], mv[None],
            k_s.reshape(1, DB, 1, H, E), v_s.reshape(1, DB, 1, H, E), conv_state_s.reshape(1, DB, CONV_K - 1, CONV_WIDTH))
```

```python
import functools
import math

import jax
import jax.numpy as jnp
from jax import lax
from jax.experimental import pallas as pl
from jax.experimental.pallas import tpu as pltpu

F32 = jnp.float32
BF16 = jnp.bfloat16
I32 = jnp.int32

D_MODEL = 1024
CONV_WIDTH = 256
CONV_K = 3
HEAD_DIM = 64
N_HEADS = 12
ATTN_WIDTH = N_HEADS * HEAD_DIM
PATTERNS = ((128, 1), (512, 4), (2048, 16))
BAND = 128
WINDOW_MAX = 2048
ATTN_SCALE = 1.0 / math.sqrt(HEAD_DIM)
N_BUCKETS = 32
MAX_DISTANCE = WINDOW_MAX
MEM_TOKENS = 256
MEM_HEADS = 4
MEM_HEAD_DIM = 256
MEM_SCALE = 1.0 / math.sqrt(MEM_HEAD_DIM)
N_EXPERTS = 32
TOP_K = 4
D_FF = 1024
SWIGLU_LIMIT = 7.0
SWIGLU_ALPHA = 1.702
LN_EPS = 1e-5
PAST_LEN = 8192
NEG_INF = -1e30

LANES = 128
HEADS_PER_SLAB = LANES // HEAD_DIM
N_SLABS = ATTN_WIDTH // LANES

PROJ_TILE = 512
SEQ_TILE = 512
ROW_BLOCK = 512
COMBINE_TILE = 256
ROW_COPY_UNROLL = 32
ATTN_GROUP = 6

_FUSED_PROJ_VMEM_BYTES = 58 * 1024 * 1024


def _layer_norm(x, g, b):
    mu = jnp.mean(x, axis=-1, keepdims=True)
    var = jnp.mean(jnp.square(x - mu), axis=-1, keepdims=True)
    return (x - mu) * lax.rsqrt(var + LN_EPS) * g + b


def _dot(a, b):
    return jnp.dot(a, b, preferred_element_type=F32)


def _dot_nt(a, b):
    return lax.dot_general(a, b, (((1,), (1,)), ((), ())), preferred_element_type=F32)


def _inproj_kernel(x_ref, w_ref, wc_ref, conv_ref, q_ref, k_ref, v_ref, cs_ref, u_s):
    ts = x_ref.shape[0]
    cw = CONV_WIDTH

    @pl.when(pl.program_id(1) == 0)
    def _():
        u_s[0:8, :] = jnp.zeros((8, cw), F32)

    x = x_ref[...].astype(BF16)
    gb = _dot(x, w_ref[:, 0:cw])
    gc = _dot(x, w_ref[:, cw:2 * cw])
    h = _dot(x, w_ref[:, 2 * cw:3 * cw])
    u = gc * h
    u_s[8:8 + ts, :] = u
    wc = wc_ref[...]
    conv = wc[0:1, :] * u_s[6:6 + ts, :] + wc[1:2, :] * u_s[7:7 + ts, :] + wc[2:3, :] * u
    conv_ref[...] = (gb * conv).astype(conv_ref.dtype)
    o = 3 * cw
    q_ref[...] = _dot(x, w_ref[:, o:o + ATTN_WIDTH]) * ATTN_SCALE
    k_ref[...] = _dot(x, w_ref[:, o + ATTN_WIDTH:o + 2 * ATTN_WIDTH])
    v_ref[...] = _dot(x, w_ref[:, o + 2 * ATTN_WIDTH:o + 3 * ATTN_WIDTH])
    tail = u_s[ts:ts + 8, :]
    u_s[0:8, :] = tail
    cs_ref[...] = tail


def _attn_kernel(q_ref, k_ref, v_ref, w_ref, o_ref, bias_ref, m_s, l_s, acc_s):
    S = q_ref.shape[0]
    n = BAND
    lane = lax.broadcasted_iota(I32, (n, LANES), 1)
    head_a = lane < HEAD_DIM

    @pl.when(pl.program_id(1) == 0)
    def _():
        for p in range(len(PATTERNS)):
            for hh in range(HEADS_PER_SLAB):
                first_row = jnp.broadcast_to(w_ref[p, hh:hh + 1, :], (n, 2 * n))
                bias_ref[p, hh * n:(hh + 1) * n, :] = pltpu.roll(first_row, 0, 1, stride=1, stride_axis=0)

    def rows_of(start, count, dil):
        return pl.ds(start, count) if dil == 1 else pl.ds(start, count, stride=dil)

    def blocks(p, dil, specs):
        rows = [rows_of(start, n, dil) for start, _ in specs]
        scores, values = [], []
        for (start, has_prev), r in zip(specs, rows):
            qb = q_ref[r, :]
            q2 = jnp.concatenate([jnp.where(head_a, qb, 0.0), jnp.where(head_a, 0.0, qb)], axis=0).astype(BF16)
            krows = rows_of(start - n * dil, 2 * n, dil) if has_prev else r
            bias = bias_ref[p] if has_prev else bias_ref[p, :, n:2 * n]
            scores.append(_dot_nt(q2, k_ref[krows, :].astype(BF16)) + bias)
            values.append(v_ref[krows, :].astype(BF16))
        ms = [jnp.max(s, axis=-1, keepdims=True) for s in scores]
        es = [jnp.exp(s - m) for s, m in zip(scores, ms)]
        ls = [jnp.sum(e, axis=-1, keepdims=True) for e in es]
        pvs = [_dot(e.astype(BF16), vb) for e, vb in zip(es, values)]
        for r, m, l, pv in zip(rows, ms, ls, pvs):
            m_s[p, r, :] = jnp.where(head_a, m[:n], m[n:])
            l_s[p, r, :] = jnp.where(head_a, l[:n], l[n:])
            acc_s[p, r, :] = jnp.where(head_a, pv[:n], pv[n:])

    for p, (window, dil) in enumerate(PATTERNS):
        nb = (S // dil) // n
        specs = [(i * n * dil + r, i > 0) for r in range(dil) for i in range(nb)]
        for g in range(0, len(specs), ATTN_GROUP):
            blocks(p, dil, specs[g:g + ATTN_GROUP])

    rows_per_step = 256

    def merge(t, c):
        rows = pl.ds(t * rows_per_step, rows_per_step)
        ms = [m_s[p, rows, :] for p in range(len(PATTERNS))]
        m_all = jnp.maximum(jnp.maximum(ms[0], ms[1]), ms[2])
        num = den = None
        for p in range(len(PATTERNS)):
            w = jnp.exp(ms[p] - m_all)
            num = w * acc_s[p, rows, :] if num is None else num + w * acc_s[p, rows, :]
            den = w * l_s[p, rows, :] if den is None else den + w * l_s[p, rows, :]
        o_ref[rows, :] = (num / den).astype(o_ref.dtype)
        return c

    lax.fori_loop(0, S // rows_per_step, merge, 0)


def _attention_prompt(q, k, v, bias_tab):
    B, S, _ = q.shape
    slab = lambda p, b: (b, 0, p)
    spec = pl.BlockSpec((None, S, LANES), slab)
    return pl.pallas_call(
        _attn_kernel,
        grid=(N_SLABS, B),
        in_specs=[spec, spec, spec,
                  pl.BlockSpec((len(PATTERNS), None, HEADS_PER_SLAB, 2 * BAND), lambda p, b: (0, p, 0, 0))],
        out_specs=spec,
        out_shape=jax.ShapeDtypeStruct((B, S, ATTN_WIDTH), BF16),
        scratch_shapes=[pltpu.VMEM((len(PATTERNS), HEADS_PER_SLAB * BAND, 2 * BAND), F32)]
        + [pltpu.VMEM((len(PATTERNS), S, LANES), F32)] * 3,
        compiler_params=pltpu.CompilerParams(dimension_semantics=("arbitrary", "arbitrary")),
        name="dilated_attn_prompt",
    )(q, k, v, bias_tab)


def _memkv_kernel(m_ref, wk_ref, wv_ref, k_ref, v_ref, kb_ref, vb_ref):
    x = m_ref[...].astype(BF16)
    k = _dot(x, wk_ref[...])
    v = _dot(x, wv_ref[...])
    k_ref[...] = k.reshape(k_ref.shape)
    v_ref[...] = v.reshape(v_ref.shape)
    kb_ref[...] = k.astype(BF16)
    vb_ref[...] = v.astype(BF16)


def _memkv_prompt(mem, wk_bf, wv_bf):
    B, M, D = mem.shape
    row = lambda b: (b, 0, 0)
    full = lambda b: (0, 0)
    heads = pl.BlockSpec((None, M, MEM_HEADS, MEM_HEAD_DIM), lambda b: (b, 0, 0, 0))
    return pl.pallas_call(
        _memkv_kernel,
        grid=(B,),
        in_specs=[pl.BlockSpec((None, M, D), row), pl.BlockSpec((D, D), full), pl.BlockSpec((D, D), full)],
        out_specs=[heads, heads, pl.BlockSpec((None, M, D), row), pl.BlockSpec((None, M, D), row)],
        out_shape=[jax.ShapeDtypeStruct((B, M, MEM_HEADS, MEM_HEAD_DIM), F32)] * 2
        + [jax.ShapeDtypeStruct((B, M, D), BF16)] * 2,
        compiler_params=pltpu.CompilerParams(dimension_semantics=("arbitrary",)),
        name="memkv_prompt",
    )(mem, wk_bf, wv_bf)


def _earlier_token_matrix(nt):
    before = lax.broadcasted_iota(I32, (nt, nt), 0) < lax.broadcasted_iota(I32, (nt, nt), 1)
    return jnp.where(before, 1.0, 0.0).astype(BF16)


def _route(logits_t, carry, earlier):
    E, nt = logits_t.shape
    eidx = lax.broadcasted_iota(I32, (E, nt), 0)
    l = logits_t
    vals, idxs = [], []
    for _ in range(TOP_K):
        m = jnp.max(l, axis=0, keepdims=True)
        sel = jnp.min(jnp.where(l == m, eidx, E), axis=0, keepdims=True)
        vals.append(m)
        idxs.append(sel)
        l = jnp.where(eidx == sel, -jnp.inf, l)
    es = [jnp.exp(v - vals[0]) for v in vals]
    den = es[0] + es[1] + es[2] + es[3]
    gates = _stack_rows([e / den for e in es])
    chosen = (l == -jnp.inf)
    onehot = jnp.where(chosen, 1.0, 0.0)
    prefix = _dot(onehot.astype(BF16), earlier) + carry
    ranks = [jnp.sum(jnp.where(eidx == s, prefix, 0.0), axis=0, keepdims=True) for s in idxs]
    rank = _stack_rows(ranks).astype(I32)
    idx = _stack_rows(idxs)
    return idx, gates, rank, carry + jnp.sum(onehot, axis=1, keepdims=True)


def _stack_rows(rows):
    k, nt = len(rows), rows[0].shape[1]
    r = lax.broadcasted_iota(I32, (k, nt), 0)
    out = jnp.broadcast_to(rows[-1], (k, nt))
    for i in range(k - 2, -1, -1):
        out = jnp.where(r == i, rows[i], out)
    return out


def _mid_kernel(x_ref, conv_ref, attn_ref, woc_ref, woa_ref, g1_ref, b1_ref, mk_ref, mv_ref, wq_ref, wmo_ref,
                g2_ref, b2_ref, wrt_ref, br_ref,
                x2_ref, idx_ref, gate_ref, rank_ref, cnt_ref, carry_s, earlier_s, *, alpha):
    first = jnp.logical_and(pl.program_id(0) == 0, pl.program_id(1) == 0)

    @pl.when(first)
    def _():
        carry_s[...] = jnp.zeros_like(carry_s)
        earlier_s[...] = _earlier_token_matrix(earlier_s.shape[0])

    x = x_ref[...]
    mix = _dot(conv_ref[...], woc_ref[...]) + _dot(attn_ref[...], woa_ref[...])
    x1 = _layer_norm(alpha * x + mix, g1_ref[...], b1_ref[...])
    qm = (_dot(x1.astype(BF16), wq_ref[...]) * MEM_SCALE).astype(BF16)
    outs = []
    for h in range(MEM_HEADS):
        hs = slice(h * MEM_HEAD_DIM, (h + 1) * MEM_HEAD_DIM)
        s = _dot_nt(qm[:, hs], mk_ref[:, hs])
        m = jnp.max(s, axis=-1, keepdims=True)
        e = jnp.exp(s - m)
        l = jnp.sum(e, axis=-1, keepdims=True)
        outs.append((_dot(e.astype(BF16), mv_ref[:, hs]) / l).astype(BF16))
    o = jnp.concatenate(outs, axis=-1)
    x2 = _layer_norm(alpha * x1 + _dot(o, wmo_ref[...]), g2_ref[...], b2_ref[...])
    x2_ref[...] = x2
    logits_t = _dot_nt(wrt_ref[...], x2.astype(BF16)) + br_ref[...]
    idx, gates, rank, carry = _route(logits_t, carry_s[:, 0:1], earlier_s[...])
    idx_ref[...] = idx
    gate_ref[...] = gates
    rank_ref[...] = rank
    carry_s[...] = jnp.broadcast_to(carry, carry_s.shape)
    cnt_ref[...] = carry_s[...]


def _round_bf16(x):
    return x.astype(BF16).astype(F32)


def _inproj_sample_kernel(x_ref, w_ref, o_ref):
    o_ref[...] = _dot(x_ref[...].astype(BF16), w_ref[...])


def _inproj_sample(x, w_in):
    n, D = x.shape
    N = w_in.shape[1]
    bn = 512
    return pl.pallas_call(
        _inproj_sample_kernel,
        grid=(N // bn,),
        in_specs=[pl.BlockSpec((n, D), lambda j: (0, 0)), pl.BlockSpec((D, bn), lambda j: (0, j))],
        out_specs=pl.BlockSpec((n, bn), lambda j: (0, j)),
        out_shape=jax.ShapeDtypeStruct((n, N), F32),
        compiler_params=pltpu.CompilerParams(dimension_semantics=("arbitrary",)),
        name="inproj_sample",
    )(x, w_in)


def _attn_sample_kernel(q_ref, kn_ref, vn_ref, kt_ref, vt_ref, bias_ref, bnew_ref, o_ref):
    H, E, W = kt_ref.shape

    def columns(ref):
        t = ref[...].T
        return jnp.stack([t[:E, h:h + 1] for h in range(H)], axis=0)

    q = _round_bf16(columns(q_ref))
    v_new = _round_bf16(columns(vn_ref))
    s_all = jnp.sum(_round_bf16(kt_ref[...]) * q, axis=1) * ATTN_SCALE
    s_new = jnp.sum(_round_bf16(columns(kn_ref)) * q, axis=1) * ATTN_SCALE
    outs, lses = [], []
    for p, (window, dil) in enumerate(PATTERNS):
        lo = W - window
        s = s_all[:, lo:] + bias_ref[p, :, lo:]
        sn = s_new + bnew_ref[p]
        m = jnp.maximum(jnp.max(s, axis=-1, keepdims=True), sn)
        e = jnp.exp(s - m)
        en = jnp.exp(sn - m)
        den = jnp.sum(e, axis=-1, keepdims=True) + en
        pr = _round_bf16(e / den)
        pv = jnp.sum(_round_bf16(vt_ref[:, :, lo:]) * pr[:, None, :], axis=-1, keepdims=True)
        outs.append(pv + _round_bf16(en / den)[:, :, None] * v_new)
        lses.append(m + jnp.log(den))
    lmax = jnp.maximum(jnp.maximum(lses[0], lses[1]), lses[2])
    ws = [jnp.exp(ls - lmax) for ls in lses]
    wsum = ws[0] + ws[1] + ws[2]
    acc = None
    for p in range(len(PATTERNS)):
        term = _round_bf16(ws[p] / wsum)[:, :, None] * _round_bf16(outs[p])
        acc = term if acc is None else acc + term
    lane = lax.broadcasted_iota(I32, o_ref.shape, 1)
    out = jnp.zeros(o_ref.shape, F32)
    for h in range(H):
        out = jnp.where(lane == h, acc[h], out)
    o_ref[...] = out


def _inproj_and_sample_attn_kernel(x_ref, w_ref, wc_ref, q4_ref, kn_ref, vn_ref, kt_ref, vt_ref, bpos_ref, bnew_ref,
                                   conv_ref, q_ref, k_ref, v_ref, cs_ref, attn_ref, u_s):
    _inproj_kernel(x_ref, w_ref, wc_ref, conv_ref, q_ref, k_ref, v_ref, cs_ref, u_s)
    _attn_sample_kernel(q4_ref, kn_ref, vn_ref, kt_ref, vt_ref, bpos_ref, bnew_ref, attn_ref)


def _inproj_prompt_and_attention_sample(x, w_in_bf, w_conv, q, k_new, v_new, win_kt, win_vt, bias_pos, bias_new):
    B, S, D = x.shape
    ts = PROJ_TILE
    n_s = S // ts
    DB, H, E, W = win_kt.shape
    assert B * n_s == DB
    row = lambda b, s: (b, s, 0)
    tok = lambda b, s: (b * n_s + s, 0, 0, 0)
    vec = pl.BlockSpec((None, LANES, LANES), lambda b, s: (b * n_s + s, 0, 0))
    vec_out = pl.BlockSpec((None, E, LANES), lambda b, s: (b * n_s + s, 0, 0))
    cache = pl.BlockSpec((None, H, E, W), tok)
    const = lambda a: pl.BlockSpec(a.shape, lambda b, s: (0,) * a.ndim)
    return pl.pallas_call(
        _inproj_and_sample_attn_kernel,
        grid=(B, n_s),
        in_specs=[pl.BlockSpec((None, ts, D), row), const(w_in_bf), const(w_conv),
                  vec, vec, vec, cache, cache, const(bias_pos), const(bias_new)],
        out_specs=[pl.BlockSpec((None, ts, CONV_WIDTH), row),
                   pl.BlockSpec((None, ts, ATTN_WIDTH), row),
                   pl.BlockSpec((None, ts, ATTN_WIDTH), row),
                   pl.BlockSpec((None, ts, ATTN_WIDTH), row),
                   pl.BlockSpec((None, 8, CONV_WIDTH), lambda b, s: (b, 0, 0)),
                   vec_out],
        out_shape=[jax.ShapeDtypeStruct((B, S, CONV_WIDTH), BF16),
                   jax.ShapeDtypeStruct((B, S, ATTN_WIDTH), F32),
                   jax.ShapeDtypeStruct((B, S, ATTN_WIDTH), F32),
                   jax.ShapeDtypeStruct((B, S, ATTN_WIDTH), F32),
                   jax.ShapeDtypeStruct((B, 8, CONV_WIDTH), F32),
                   jax.ShapeDtypeStruct((DB, E, LANES), F32)],
        scratch_shapes=[pltpu.VMEM((ts + 8, CONV_WIDTH), F32)],
        compiler_params=pltpu.CompilerParams(dimension_semantics=("arbitrary", "arbitrary"),
                                             vmem_limit_bytes=_FUSED_PROJ_VMEM_BYTES),
        name="inproj_prompt_attn_sample",
    )(x, w_in_bf, w_conv, q, k_new, v_new, win_kt, win_vt, bias_pos, bias_new)


def _tail_a_sample_kernel(x_ref, proj_ref, s0_ref, s1_ref, wc_ref, attn_ref, wo_ref, g1_ref, b1_ref, wq_ref,
                          x1_ref, qm_ref, u_ref, *, alpha):
    cw = CONV_WIDTH
    gb = proj_ref[:, 0:cw]
    u = proj_ref[:, cw:2 * cw] * proj_ref[:, 2 * cw:3 * cw]
    u_ref[...] = u
    wc = wc_ref[...]
    conv = gb * (wc[0:1, :] * s0_ref[...] + wc[1:2, :] * s1_ref[...] + wc[2:3, :] * u)
    mixed = jnp.concatenate([conv, attn_ref[...]], axis=-1).astype(BF16)
    x1 = _layer_norm(alpha * x_ref[...] + _dot(mixed, wo_ref[...]), g1_ref[...], b1_ref[...])
    x1_ref[...] = x1
    qm_ref[...] = _dot(x1.astype(BF16), wq_ref[...])


def _tail_a_sample(x, proj, s0, s1, w_conv, attn, w_out, g1, b1, w_mem_q, alpha):
    n, D = x.shape
    return pl.pallas_call(
        functools.partial(_tail_a_sample_kernel, alpha=alpha),
        out_shape=[jax.ShapeDtypeStruct((n, D), F32)] * 2 + [jax.ShapeDtypeStruct((n, CONV_WIDTH), F32)],
        name="tail_a_sample",
    )(x, proj, s0, s1, w_conv, attn, w_out, g1, b1, w_mem_q)


def _memattn_sample_kernel(q_ref, k_ref, v_ref, o_ref):
    q = _round_bf16(q_ref[...])
    s = jnp.sum(_round_bf16(k_ref[...]) * q[None], axis=-1, keepdims=True) * MEM_SCALE
    m = jnp.max(s, axis=0)
    e = jnp.exp(s - m[None])
    den = jnp.sum(e, axis=0)
    pr = _round_bf16(e / den[None])
    o_ref[...] = jnp.sum(pr * _round_bf16(v_ref[...]), axis=0)


_N_MID_IN = 15


def _mid_and_sample_memattn_kernel(*refs, alpha):
    mid_in, (qm_ref, smk_ref, smv_ref) = refs[:_N_MID_IN], refs[_N_MID_IN:_N_MID_IN + 3]
    mid_out, so_ref, (carry_s, earlier_s) = refs[_N_MID_IN + 3:-3], refs[-3], refs[-2:]
    _mid_kernel(*mid_in, *mid_out, carry_s, earlier_s, alpha=alpha)
    _memattn_sample_kernel(qm_ref, smk_ref, smv_ref, so_ref)


def _mid_prompt_and_memattn_sample(x, conv, attn, woc, woa, g1, b1, mk, mv, wq, wmo, g2, b2, wrt, br,
                                   qm_s, mem_k_s, mem_v_s, alpha):
    B, S, D = x.shape
    ts = SEQ_TILE
    n_s = S // ts
    T = B * S
    DB, M, H, E = mem_k_s.shape
    assert B * n_s == DB
    row = lambda b, s: (b, s, 0)
    full = lambda b, s: (0, 0)
    tok = lambda b, s: (0, b * n_s + s)
    mem = lambda b, s: (b, 0, 0)
    vec = pl.BlockSpec((1, D), full)
    s_vec = pl.BlockSpec((None, H, E), lambda b, s: (b * n_s + s, 0, 0))
    s_mem = pl.BlockSpec((None, M, H, E), lambda b, s: (b * n_s + s, 0, 0, 0))
    in_specs = [pl.BlockSpec((None, ts, D), row),
                pl.BlockSpec((None, ts, CONV_WIDTH), row),
                pl.BlockSpec((None, ts, ATTN_WIDTH), row),
                pl.BlockSpec(woc.shape, full), pl.BlockSpec(woa.shape, full), vec, vec,
                pl.BlockSpec((None, MEM_TOKENS, D), mem), pl.BlockSpec((None, MEM_TOKENS, D), mem),
                pl.BlockSpec(wq.shape, full), pl.BlockSpec(wmo.shape, full), vec, vec,
                pl.BlockSpec(wrt.shape, full), pl.BlockSpec(br.shape, full)]
    assert len(in_specs) == _N_MID_IN
    return pl.pallas_call(
        functools.partial(_mid_and_sample_memattn_kernel, alpha=alpha),
        grid=(B, n_s),
        in_specs=in_specs + [s_vec, s_mem, s_mem],
        out_specs=[pl.BlockSpec((None, ts, D), row),
                   pl.BlockSpec((TOP_K, ts), tok), pl.BlockSpec((TOP_K, ts), tok), pl.BlockSpec((TOP_K, ts), tok),
                   pl.BlockSpec((N_EXPERTS, LANES), full), s_vec],
        out_shape=[jax.ShapeDtypeStruct((B, S, D), F32),
                   jax.ShapeDtypeStruct((TOP_K, T), I32),
                   jax.ShapeDtypeStruct((TOP_K, T), F32),
                   jax.ShapeDtypeStruct((TOP_K, T), I32),
                   jax.ShapeDtypeStruct((N_EXPERTS, LANES), F32),
                   jax.ShapeDtypeStruct((DB, H, E), F32)],
        scratch_shapes=[pltpu.VMEM((N_EXPERTS, LANES), F32), pltpu.VMEM((ts, ts), BF16)],
        compiler_params=pltpu.CompilerParams(dimension_semantics=("arbitrary", "arbitrary")),
        name="mid_prompt_memattn_sample",
    )(x, conv, attn, woc, woa, g1, b1, mk, mv, wq, wmo, g2, b2, wrt, br, qm_s, mem_k_s, mem_v_s)


def _tail_b_sample_kernel(x1_ref, o_ref, wmo_ref, g2_ref, b2_ref, wrt_ref, br_ref, cnt_ref,
                          x2_ref, idx_ref, gate_ref, rank_ref, cnt_out_ref, *, alpha):
    x2 = _layer_norm(alpha * x1_ref[...] + _dot(o_ref[...].astype(BF16), wmo_ref[...]), g2_ref[...], b2_ref[...])
    x2_ref[...] = x2
    logits_t = _dot_nt(wrt_ref[...], x2.astype(BF16)) + br_ref[...]
    idx, gates, rank, carry = _route(logits_t, cnt_ref[:, 0:1], _earlier_token_matrix(logits_t.shape[1]))
    idx_ref[...] = idx
    gate_ref[...] = gates
    rank_ref[...] = rank
    cnt_out_ref[...] = jnp.broadcast_to(carry, cnt_out_ref.shape)


def _tail_b_sample(x1, o, w_mem_o, g2, b2, wrt, br, cnt, alpha):
    n, D = x1.shape
    return pl.pallas_call(
        functools.partial(_tail_b_sample_kernel, alpha=alpha),
        out_shape=[jax.ShapeDtypeStruct((n, D), F32),
                   jax.ShapeDtypeStruct((TOP_K, n), I32),
                   jax.ShapeDtypeStruct((TOP_K, n), F32),
                   jax.ShapeDtypeStruct((TOP_K, n), I32),
                   jax.ShapeDtypeStruct(cnt.shape, F32)],
        name="tail_b_sample",
    )(x1, o, w_mem_o, g2, b2, wrt, br, cnt)


def _row_copy(src, dst, sem, src_row, dst_row):
    return pltpu.make_async_copy(src.at[pl.ds(src_row, 1)], dst.at[pl.ds(dst_row, 1)], sem)


def _dispatch_kernel(dest_ref, x_ref, dest2_ref, x2_ref, xb_ref, zeros_s, stage_s, sems, zsem, *, n_tok_rows):
    i = pl.program_id(0)
    last = pl.num_programs(0) - 1
    slot = i % 2
    nt = dest_ref.shape[1] // TOP_K

    def start(src_ref, first, count, idx_ref, sem):
        def issue(g, c):
            for u in range(ROW_COPY_UNROLL):
                t = g * ROW_COPY_UNROLL + u
                for k in range(TOP_K):
                    _row_copy(src_ref, xb_ref, sem, first + t, idx_ref[0, k * count + t]).start(priority=k % 2)
            return c

        lax.fori_loop(0, count // ROW_COPY_UNROLL, issue, 0)

    def drain(src_ref, count, sem):
        for k in range(TOP_K):
            pltpu.make_async_copy(src_ref.at[pl.ds(0, count)], xb_ref.at[pl.ds(0, count)], sem).wait()

    @pl.when(i < last)
    def _():
        stage_s[slot] = x_ref[...]
        start(stage_s.at[slot], 0, nt, dest_ref, sems.at[slot])

    @pl.when(i == last)
    def _():
        zeros_s[...] = jnp.zeros_like(zeros_s)
        pltpu.make_async_copy(zeros_s, xb_ref.at[pl.ds(n_tok_rows, zeros_s.shape[0])], zsem).start()
        start(x2_ref, 0, x2_ref.shape[0], dest2_ref, sems.at[slot])

    @pl.when(i > 0)
    def _():
        drain(stage_s.at[1 - slot], nt, sems.at[1 - slot])

    @pl.when(i == last)
    def _():
        drain(x2_ref, x2_ref.shape[0], sems.at[slot])
        pltpu.make_async_copy(zeros_s, xb_ref.at[pl.ds(n_tok_rows, zeros_s.shape[0])], zsem).wait()


def _dispatch(x_a, dest_a, x_b, dest_b, n_rows):
    Ta, D = x_a.shape
    Tb = x_b.shape[0]
    nt = dest_a.shape[2] // TOP_K
    steps = Ta // nt
    assert nt % ROW_COPY_UNROLL == 0 and Tb % ROW_COPY_UNROLL == 0
    n_tok_rows = (Ta + Tb) * TOP_K
    tile = lambda i: (jnp.minimum(i, steps - 1), 0, 0)
    return pl.pallas_call(
        functools.partial(_dispatch_kernel, n_tok_rows=n_tok_rows),
        grid=(steps + 1,),
        in_specs=[pl.BlockSpec((None, 1, TOP_K * nt), tile, memory_space=pltpu.SMEM),
                  pl.BlockSpec((nt, D), lambda i: (jnp.minimum(i, steps - 1), 0)),
                  pl.BlockSpec((None, 1, TOP_K * Tb), lambda i: (0, 0, 0), memory_space=pltpu.SMEM),
                  pl.BlockSpec((Tb, D), lambda i: (0, 0))],
        out_specs=pl.BlockSpec(memory_space=pl.ANY),
        out_shape=jax.ShapeDtypeStruct((n_rows, D), x_a.dtype),
        scratch_shapes=[pltpu.VMEM((n_rows - n_tok_rows, D), x_a.dtype), pltpu.VMEM((2, nt, D), x_a.dtype),
                        pltpu.SemaphoreType.DMA((2,)), pltpu.SemaphoreType.DMA],
        compiler_params=pltpu.CompilerParams(dimension_semantics=("arbitrary",)),
        name="moe_dispatch",
    )(dest_a, x_a, dest_b, x_b)


def _moe_kernel(blk_ref, exp_ref, lo_ref, hi_ref, xb_ref, wgu_ref, bgu_ref, wd_ref, bd_ref, yb_ref,
                wgu_s, wd_s, cast_s):
    s = pl.program_id(0)
    new_block = jnp.logical_or(s == 0, blk_ref[s] != blk_ref[jnp.maximum(s - 1, 0)])
    lo, hi = lo_ref[s], hi_ref[s]
    used = hi > lo

    @pl.when(s == 0)
    def _():
        cast_s[0] = -1

    @pl.when(jnp.logical_and(used, cast_s[0] != exp_ref[s]))
    def _():
        wgu_s[...] = wgu_ref[...].astype(BF16)
        wd_s[...] = wd_ref[...].astype(BF16)
        cast_s[0] = exp_ref[s]

    @pl.when(new_block)
    def _():
        yb_ref[...] = jnp.zeros_like(yb_ref)

    @pl.when(used)
    def _():
        gu = _dot(xb_ref[...].astype(BF16), wgu_s[...]) + bgu_ref[...]
        g = jnp.minimum(gu[:, :D_FF], SWIGLU_LIMIT)
        u = jnp.clip(gu[:, D_FF:], -SWIGLU_LIMIT, SWIGLU_LIMIT)
        act = (u + 1.0) * g * jax.nn.sigmoid(SWIGLU_ALPHA * g)
        y = _dot(act.astype(BF16), wd_s[...]) + bd_ref[...]
        row = lax.broadcasted_iota(I32, (xb_ref.shape[0], 1), 0)
        yb_ref[...] = jnp.where(jnp.logical_and(row >= lo, row < hi), y, yb_ref[...])


def _moe(seg_blk, seg_exp, seg_lo, seg_hi, xb, w_gate_up, b_gate_up, w_down, b_down):
    P, D = xb.shape
    bm = ROW_BLOCK
    E, _, F2 = w_gate_up.shape
    ex = lambda s, blk, exp, lo, hi: (exp[s], 0, 0)
    rows = lambda s, blk, exp, lo, hi: (blk[s], 0)
    expert_bytes = (D * F2 + D_FF * D) * (2 * 4 + 2)
    block_bytes = bm * D * 4 * 2 * 2 + bm * F2 * 4 * 3
    return pl.pallas_call(
        _moe_kernel,
        grid_spec=pltpu.PrefetchScalarGridSpec(
            num_scalar_prefetch=4,
            grid=(seg_blk.shape[0],),
            in_specs=[pl.BlockSpec((bm, D), rows),
                      pl.BlockSpec((None, D, F2), ex), pl.BlockSpec((None, 1, F2), ex),
                      pl.BlockSpec((None, D_FF, D), ex), pl.BlockSpec((None, 1, D), ex)],
            out_specs=pl.BlockSpec((bm, D), rows),
            scratch_shapes=[pltpu.VMEM((D, F2), BF16), pltpu.VMEM((D_FF, D), BF16), pltpu.SMEM((1,), I32)]),
        out_shape=jax.ShapeDtypeStruct((P, D), F32),
        compiler_params=pltpu.CompilerParams(dimension_semantics=("arbitrary",),
                                             vmem_limit_bytes=expert_bytes + block_bytes),
        name="moe_grouped_ffn",
    )(seg_blk, seg_exp, seg_lo, seg_hi, xb, w_gate_up, b_gate_up.reshape(E, 1, F2), w_down, b_down.reshape(E, 1, D))


def _combine_kernel(dest_ref, next_ref, yb_ref, x2_ref, gate_ref, g3_ref, b3_ref, o_ref, rows_s, sems, *, alpha):
    nt = x2_ref.shape[0]
    i = pl.program_id(0)
    slot = i % 2

    def gather(idx_ref, buf):
        def issue(g, c):
            for u in range(ROW_COPY_UNROLL):
                t = g * ROW_COPY_UNROLL + u
                for k in range(TOP_K):
                    _row_copy(yb_ref, rows_s.at[buf, k], sems.at[buf], idx_ref[0, k * nt + t], t).start(priority=k % 2)
            return c
        lax.fori_loop(0, nt // ROW_COPY_UNROLL, issue, 0)

    @pl.when(i == 0)
    def _():
        gather(dest_ref, 0)

    @pl.when(i + 1 < pl.num_programs(0))
    def _():
        gather(next_ref, 1 - slot)

    for k in range(TOP_K):
        pltpu.make_async_copy(yb_ref.at[pl.ds(0, nt)], rows_s.at[slot, k], sems.at[slot]).wait()
    gates = gate_ref[...]
    y = gates[:, 0:1] * rows_s[slot, 0]
    for k in range(1, TOP_K):
        y = y + gates[:, k:k + 1] * rows_s[slot, k]
    o_ref[...] = _layer_norm(alpha * x2_ref[...] + y, g3_ref[...], b3_ref[...])


def _combine(yb, dest_tiles, x2, gates_tok, g3, b3, alpha):
    T, D = x2.shape
    steps = dest_tiles.shape[0]
    nt = dest_tiles.shape[2] // TOP_K
    assert nt % ROW_COPY_UNROLL == 0
    idx_spec = lambda ahead: pl.BlockSpec((None, 1, TOP_K * nt), lambda i: (jnp.minimum(i + ahead, steps - 1), 0, 0),
                                          memory_space=pltpu.SMEM)
    return pl.pallas_call(
        functools.partial(_combine_kernel, alpha=alpha),
        grid=(steps,),
        in_specs=[idx_spec(0), idx_spec(1),
                  pl.BlockSpec(memory_space=pl.ANY),
                  pl.BlockSpec((nt, D), lambda i: (i, 0)),
                  pl.BlockSpec((nt, TOP_K), lambda i: (i, 0)),
                  pl.BlockSpec((1, D), lambda i: (0, 0)), pl.BlockSpec((1, D), lambda i: (0, 0))],
        out_specs=pl.BlockSpec((nt, D), lambda i: (i, 0)),
        out_shape=jax.ShapeDtypeStruct((T, D), F32),
        scratch_shapes=[pltpu.VMEM((2, TOP_K, nt, D), F32), pltpu.SemaphoreType.DMA((2,))],
        compiler_params=pltpu.CompilerParams(dimension_semantics=("arbitrary",)),
        name="moe_combine",
    )(dest_tiles, dest_tiles, yb, x2, gates_tok, g3, b3)


def _rel_bucket(dist):
    max_exact = N_BUCKETS // 2
    df = jnp.maximum(dist, 1).astype(F32)
    large = max_exact + (jnp.log(df / max_exact) / math.log(MAX_DISTANCE / max_exact)
                         * (N_BUCKETS - max_exact)).astype(I32)
    return jnp.where(dist < max_exact, dist, jnp.minimum(large, N_BUCKETS - 1))


def _bias_tables(rel_table):
    n = BAND
    taps = jnp.arange(n + 1)
    by_tap = jnp.stack([rel_table[_rel_bucket(taps * dil)].astype(F32) for _, dil in PATTERNS])
    w = jnp.concatenate([by_tap[:, ::-1], jnp.full((len(PATTERNS), n - 1, N_HEADS), NEG_INF, F32)], axis=1)
    w = jnp.transpose(w, (0, 2, 1))
    return by_tap, w.reshape(len(PATTERNS), N_SLABS, HEADS_PER_SLAB, 2 * n)


def _tap_bias_by_slot(by_tap, W):
    assert PAST_LEN % W == 0 and PAST_LEN >= WINDOW_MAX
    n = BAND
    out = []
    for p, (window, dil) in enumerate(PATTERNS):
        taps = by_tap[p, :0:-1]
        col = jnp.concatenate([jnp.full((W // dil - n, N_HEADS), NEG_INF, F32), taps], axis=0)
        rest = jnp.full((W // dil, dil - 1, N_HEADS), NEG_INF, F32)
        out.append(jnp.concatenate([col[:, None, :], rest], axis=1).reshape(W, N_HEADS).T)
    return jnp.stack(out)


def _dest_tiles(dest, nt):
    T = dest.shape[1]
    return dest.reshape(TOP_K, T // nt, nt).transpose(1, 0, 2).reshape(T // nt, 1, TOP_K * nt)


def kernel(x_prompt, x_sample, mem_prompt, cache_win_k, cache_win_v, state_conv, cache_mem_k, cache_mem_v,
           rel_bias_table, w_in, w_conv, w_out, ln1_g, ln1_b, w_mem_q, w_mem_k, w_mem_v, w_mem_o,
           ln2_g, ln2_b, w_router, b_router, w_gate_up, b_gate_up, w_down, b_down, ln3_g, ln3_b):
    depth = w_in.shape[0]
    assert depth == 1
    alpha = (2 * depth) ** 0.25
    B, S, D = x_prompt.shape
    DB = x_sample.shape[0]
    T = B * S
    l = 0
    vec = lambda a: a[l].reshape(1, -1)

    by_tap, bias_band = _bias_tables(rel_bias_table)
    wrt = w_router[l].T.astype(BF16)
    br = b_router[l].reshape(N_EXPERTS, 1)
    w_in_bf = w_in[l].astype(BF16)
    wo_bf = w_out[l].astype(BF16)
    wq_bf = w_mem_q[l].astype(BF16)
    wmo_bf = w_mem_o[l].astype(BF16)

    xs = x_sample.reshape(DB, D)
    proj = _inproj_sample(xs, w_in_bf)
    cw = CONV_WIDTH
    q_s = proj[:, 3 * cw:3 * cw + ATTN_WIDTH]
    k_s = proj[:, 3 * cw + ATTN_WIDTH:3 * cw + 2 * ATTN_WIDTH]
    v_s = proj[:, 3 * cw + 2 * ATTN_WIDTH:]
    H, E = N_HEADS, HEAD_DIM
    bias_new = by_tap[:, 0, :, None]
    slots_last = lambda c: jnp.transpose(c, (0, 2, 3, 1))
    corner = lambda a: jnp.pad(a.reshape(DB, H, E), ((0, 0), (0, LANES - H), (0, LANES - E)))
    conv_p, q_p, k_p, v_p, cs_p, attn_s = _inproj_prompt_and_attention_sample(
        x_prompt, w_in_bf, w_conv[l], corner(q_s), corner(k_s), corner(v_s),
        slots_last(cache_win_k[l]), slots_last(cache_win_v[l]), _tap_bias_by_slot(by_tap, WINDOW_MAX), bias_new)
    attn_s = jnp.transpose(attn_s[:, :, :H], (0, 2, 1)).reshape(DB, ATTN_WIDTH)

    sc = state_conv[l]
    x1_s, qm_s, u_s = _tail_a_sample(xs, proj, sc[:, 0], sc[:, 1], w_conv[l], attn_s, wo_bf,
                                     vec(ln1_g), vec(ln1_b), wq_bf, alpha)
    attn_p = _attention_prompt(q_p, k_p, v_p, bias_band)
    mk, mv, mk_bf, mv_bf = _memkv_prompt(mem_prompt, w_mem_k[l].astype(BF16), w_mem_v[l].astype(BF16))
    x2_p, idx_p, gate_p, rank_p, cnt_p, o_s = _mid_prompt_and_memattn_sample(
        x_prompt, conv_p, attn_p, wo_bf[:CONV_WIDTH], wo_bf[CONV_WIDTH:], vec(ln1_g), vec(ln1_b), mk_bf, mv_bf,
        wq_bf, wmo_bf, vec(ln2_g), vec(ln2_b), wrt, br,
        qm_s.reshape(DB, MEM_HEADS, MEM_HEAD_DIM), cache_mem_k[l], cache_mem_v[l], alpha)
    x2_s, idx_s, gate_s, rank_s, cnt = _tail_b_sample(x1_s, o_s.reshape(DB, D), wmo_bf, vec(ln2_g), vec(ln2_b), wrt, br,
                                                      cnt_p, alpha)

    bm = ROW_BLOCK
    counts = cnt[:, 0].astype(I32)
    ends = jnp.cumsum(counts)
    starts = ends - counts
    n_tok_rows = (T + DB) * TOP_K
    n_blocks = -(-n_tok_rows // bm)
    P = n_blocks * bm
    cuts = jnp.sort(jnp.concatenate([jnp.arange(n_blocks, dtype=I32) * bm, ends[:-1]]))
    cut_ends = jnp.concatenate([cuts[1:], jnp.full((1,), P, I32)])
    seg_blk = jnp.minimum(cuts // bm, n_blocks - 1)
    seg_exp = jnp.minimum(jnp.sum((ends[None, :] <= cuts[:, None]).astype(I32), axis=1), N_EXPERTS - 1)
    seg_lo = cuts - seg_blk * bm
    seg_hi = cut_ends - seg_blk * bm
    expert_ids = jnp.arange(N_EXPERTS, dtype=I32)[:, None, None]
    start_of = lambda idx: jnp.sum(jnp.where(idx[None] == expert_ids, starts[:, None, None], 0), axis=0)
    dest_p = start_of(idx_p) + rank_p
    dest_s = start_of(idx_s) + rank_s
    dtiles_p = _dest_tiles(dest_p, COMBINE_TILE)
    dtiles_s = _dest_tiles(dest_s, DB)

    x2_pf = x2_p.reshape(T, D)
    xb = _dispatch(x2_pf, dtiles_p, x2_s, dtiles_s, P)
    yb = _moe(seg_blk, seg_exp, seg_lo, seg_hi, xb, w_gate_up[l], b_gate_up[l], w_down[l], b_down[l])
    y_p = _combine(yb, dtiles_p, x2_pf, gate_p.T, vec(ln3_g), vec(ln3_b), alpha).reshape(B, S, D)
    y_s = _combine(yb, dtiles_s, x2_s, gate_s.T, vec(ln3_g), vec(ln3_b), alpha).reshape(DB, 1, D)

    conv_state_s = jnp.stack([sc[:, 1], u_s], axis=1)
    return (y_p, y_s,
            k_p.reshape(1, B, S, H, E), v_p.reshape(1, B, S, H, E), cs_p[:, 6:8].reshape(1, B, CONV_K - 1, CONV_WIDTH),
            mk[None], mv[None],
            k_s.reshape(1, DB, 1, H, E), v_s.reshape(1, DB, 1, H, E), conv_state_s.reshape(1, DB, CONV_K - 1, CONV_WIDTH))
```

```python
import functools
import math

import jax
import jax.numpy as jnp
from jax import lax
from jax.experimental import pallas as pl
from jax.experimental.pallas import tpu as pltpu

F32 = jnp.float32
BF16 = jnp.bfloat16
I32 = jnp.int32

D_MODEL = 1024
CONV_WIDTH = 256
CONV_K = 3
HEAD_DIM = 64
N_HEADS = 12
ATTN_WIDTH = N_HEADS * HEAD_DIM
PATTERNS = ((128, 1), (512, 4), (2048, 16))
BAND = 128
WINDOW_MAX = 2048
ATTN_SCALE = 1.0 / math.sqrt(HEAD_DIM)
N_BUCKETS = 32
MAX_DISTANCE = WINDOW_MAX
MEM_TOKENS = 256
MEM_HEADS = 4
MEM_HEAD_DIM = 256
MEM_SCALE = 1.0 / math.sqrt(MEM_HEAD_DIM)
N_EXPERTS = 32
TOP_K = 4
D_FF = 1024
SWIGLU_LIMIT = 7.0
SWIGLU_ALPHA = 1.702
LN_EPS = 1e-5
PAST_LEN = 8192
NEG_INF = -1e30

LANES = 128
HEADS_PER_SLAB = LANES // HEAD_DIM
N_SLABS = ATTN_WIDTH // LANES

PROJ_TILE = 512
SEQ_TILE = 512
ROW_BLOCK = 512
COMBINE_TILE = 256
ROW_COPY_UNROLL = 32
ATTN_GROUP = 6

_FUSED_PROJ_VMEM_BYTES = 58 * 1024 * 1024


def _layer_norm(x, g, b):
    mu = jnp.mean(x, axis=-1, keepdims=True)
    var = jnp.mean(jnp.square(x - mu), axis=-1, keepdims=True)
    return (x - mu) * lax.rsqrt(var + LN_EPS) * g + b


def _dot(a, b):
    return jnp.dot(a, b, preferred_element_type=F32)


def _dot_nt(a, b):
    return lax.dot_general(a, b, (((1,), (1,)), ((), ())), preferred_element_type=F32)


def _inproj_kernel(x_ref, w_ref, wc_ref, conv_ref, q_ref, k_ref, v_ref, cs_ref, u_s):
    ts = x_ref.shape[0]
    cw = CONV_WIDTH

    @pl.when(pl.program_id(1) == 0)
    def _():
        u_s[0:8, :] = jnp.zeros((8, cw), F32)

    x = x_ref[...].astype(BF16)
    gb = _dot(x, w_ref[:, 0:cw])
    gc = _dot(x, w_ref[:, cw:2 * cw])
    h = _dot(x, w_ref[:, 2 * cw:3 * cw])
    u = gc * h
    u_s[8:8 + ts, :] = u
    wc = wc_ref[...]
    conv = wc[0:1, :] * u_s[6:6 + ts, :] + wc[1:2, :] * u_s[7:7 + ts, :] + wc[2:3, :] * u
    conv_ref[...] = (gb * conv).astype(conv_ref.dtype)
    o = 3 * cw
    q_ref[...] = _dot(x, w_ref[:, o:o + ATTN_WIDTH]) * ATTN_SCALE
    k_ref[...] = _dot(x, w_ref[:, o + ATTN_WIDTH:o + 2 * ATTN_WIDTH])
    v_ref[...] = _dot(x, w_ref[:, o + 2 * ATTN_WIDTH:o + 3 * ATTN_WIDTH])
    tail = u_s[ts:ts + 8, :]
    u_s[0:8, :] = tail
    cs_ref[...] = tail


def _attn_kernel(q_ref, k_ref, v_ref, w_ref, o_ref, bias_ref, m_s, l_s, acc_s):
    S = q_ref.shape[0]
    n = BAND
    lane = lax.broadcasted_iota(I32, (n, LANES), 1)
    head_a = lane < HEAD_DIM

    @pl.when(pl.program_id(1) == 0)
    def _():
        for p in range(len(PATTERNS)):
            for hh in range(HEADS_PER_SLAB):
                first_row = jnp.broadcast_to(w_ref[p, hh:hh + 1, :], (n, 2 * n))
                bias_ref[p, hh * n:(hh + 1) * n, :] = pltpu.roll(first_row, 0, 1, stride=1, stride_axis=0)

    def rows_of(start, count, dil):
        return pl.ds(start, count) if dil == 1 else pl.ds(start, count, stride=dil)

    def blocks(p, dil, specs):
        rows = [rows_of(start, n, dil) for start, _ in specs]
        scores, values = [], []
        for (start, has_prev), r in zip(specs, rows):
            qb = q_ref[r, :]
            q2 = jnp.concatenate([jnp.where(head_a, qb, 0.0), jnp.where(head_a, 0.0, qb)], axis=0).astype(BF16)
            krows = rows_of(start - n * dil, 2 * n, dil) if has_prev else r
            bias = bias_ref[p] if has_prev else bias_ref[p, :, n:2 * n]
            scores.append(_dot_nt(q2, k_ref[krows, :].astype(BF16)) + bias)
            values.append(v_ref[krows, :].astype(BF16))
        ms = [jnp.max(s, axis=-1, keepdims=True) for s in scores]
        es = [jnp.exp(s - m) for s, m in zip(scores, ms)]
        ls = [jnp.sum(e, axis=-1, keepdims=True) for e in es]
        pvs = [_dot(e.astype(BF16), vb) for e, vb in zip(es, values)]
        for r, m, l, pv in zip(rows, ms, ls, pvs):
            m_s[p, r, :] = jnp.where(head_a, m[:n], m[n:])
            l_s[p, r, :] = jnp.where(head_a, l[:n], l[n:])
            acc_s[p, r, :] = jnp.where(head_a, pv[:n], pv[n:])

    for p, (window, dil) in enumerate(PATTERNS):
        nb = (S // dil) // n
        specs = [(i * n * dil + r, i > 0) for r in range(dil) for i in range(nb)]
        for g in range(0, len(specs), ATTN_GROUP):
            blocks(p, dil, specs[g:g + ATTN_GROUP])

    rows_per_step = 256

    def merge(t, c):
        rows = pl.ds(t * rows_per_step, rows_per_step)
        ms = [m_s[p, rows, :] for p in range(len(PATTERNS))]
        m_all = jnp.maximum(jnp.maximum(ms[0], ms[1]), ms[2])
        num = den = None
        for p in range(len(PATTERNS)):
            w = jnp.exp(ms[p] - m_all)
            num = w * acc_s[p, rows, :] if num is None else num + w * acc_s[p, rows, :]
            den = w * l_s[p, rows, :] if den is None else den + w * l_s[p, rows, :]
        o_ref[rows, :] = (num / den).astype(o_ref.dtype)
        return c

    lax.fori_loop(0, S // rows_per_step, merge, 0)


def _attention_prompt(q, k, v, bias_tab):
    B, S, _ = q.shape
    slab = lambda p, b: (b, 0, p)
    spec = pl.BlockSpec((None, S, LANES), slab)
    return pl.pallas_call(
        _attn_kernel,
        grid=(N_SLABS, B),
        in_specs=[spec, spec, spec,
                  pl.BlockSpec((len(PATTERNS), None, HEADS_PER_SLAB, 2 * BAND), lambda p, b: (0, p, 0, 0))],
        out_specs=spec,
        out_shape=jax.ShapeDtypeStruct((B, S, ATTN_WIDTH), BF16),
        scratch_shapes=[pltpu.VMEM((len(PATTERNS), HEADS_PER_SLAB * BAND, 2 * BAND), F32)]
        + [pltpu.VMEM((len(PATTERNS), S, LANES), F32)] * 3,
        compiler_params=pltpu.CompilerParams(dimension_semantics=("arbitrary", "arbitrary")),
        name="dilated_attn_prompt",
    )(q, k, v, bias_tab)


def _memkv_kernel(m_ref, wk_ref, wv_ref, k_ref, v_ref, kb_ref, vb_ref):
    x = m_ref[...].astype(BF16)
    k = _dot(x, wk_ref[...])
    v = _dot(x, wv_ref[...])
    k_ref[...] = k.reshape(k_ref.shape)
    v_ref[...] = v.reshape(v_ref.shape)
    kb_ref[...] = k.astype(BF16)
    vb_ref[...] = v.astype(BF16)


def _memkv_prompt(mem, wk_bf, wv_bf):
    B, M, D = mem.shape
    row = lambda b: (b, 0, 0)
    full = lambda b: (0, 0)
    heads = pl.BlockSpec((None, M, MEM_HEADS, MEM_HEAD_DIM), lambda b: (b, 0, 0, 0))
    return pl.pallas_call(
        _memkv_kernel,
        grid=(B,),
        in_specs=[pl.BlockSpec((None, M, D), row), pl.BlockSpec((D, D), full), pl.BlockSpec((D, D), full)],
        out_specs=[heads, heads, pl.BlockSpec((None, M, D), row), pl.BlockSpec((None, M, D), row)],
        out_shape=[jax.ShapeDtypeStruct((B, M, MEM_HEADS, MEM_HEAD_DIM), F32)] * 2
        + [jax.ShapeDtypeStruct((B, M, D), BF16)] * 2,
        compiler_params=pltpu.CompilerParams(dimension_semantics=("arbitrary",)),
        name="memkv_prompt",
    )(mem, wk_bf, wv_bf)


def _earlier_token_matrix(nt):
    before = lax.broadcasted_iota(I32, (nt, nt), 0) < lax.broadcasted_iota(I32, (nt, nt), 1)
    return jnp.where(before, 1.0, 0.0).astype(BF16)


def _route(logits_t, carry, earlier):
    E, nt = logits_t.shape
    eidx = lax.broadcasted_iota(I32, (E, nt), 0)
    l = logits_t
    vals, idxs = [], []
    for _ in range(TOP_K):
        m = jnp.max(l, axis=0, keepdims=True)
        sel = jnp.min(jnp.where(l == m, eidx, E), axis=0, keepdims=True)
        vals.append(m)
        idxs.append(sel)
        l = jnp.where(eidx == sel, -jnp.inf, l)
    es = [jnp.exp(v - vals[0]) for v in vals]
    den = es[0] + es[1] + es[2] + es[3]
    gates = _stack_rows([e / den for e in es])
    chosen = (l == -jnp.inf)
    onehot = jnp.where(chosen, 1.0, 0.0)
    prefix = _dot(onehot.astype(BF16), earlier) + carry
    ranks = [jnp.sum(jnp.where(eidx == s, prefix, 0.0), axis=0, keepdims=True) for s in idxs]
    rank = _stack_rows(ranks).astype(I32)
    idx = _stack_rows(idxs)
    return idx, gates, rank, carry + jnp.sum(onehot, axis=1, keepdims=True)


def _stack_rows(rows):
    k, nt = len(rows), rows[0].shape[1]
    r = lax.broadcasted_iota(I32, (k, nt), 0)
    out = jnp.broadcast_to(rows[-1], (k, nt))
    for i in range(k - 2, -1, -1):
        out = jnp.where(r == i, rows[i], out)
    return out


def _mid_kernel(x_ref, conv_ref, attn_ref, woc_ref, woa_ref, g1_ref, b1_ref, mk_ref, mv_ref, wq_ref, wmo_ref,
                g2_ref, b2_ref, wrt_ref, br_ref,
                x2_ref, idx_ref, gate_ref, rank_ref, cnt_ref, carry_s, earlier_s, *, alpha):
    first = jnp.logical_and(pl.program_id(0) == 0, pl.program_id(1) == 0)

    @pl.when(first)
    def _():
        carry_s[...] = jnp.zeros_like(carry_s)
        earlier_s[...] = _earlier_token_matrix(earlier_s.shape[0])

    x = x_ref[...]
    mix = _dot(conv_ref[...], woc_ref[...]) + _dot(attn_ref[...], woa_ref[...])
    x1 = _layer_norm(alpha * x + mix, g1_ref[...], b1_ref[...])
    qm = (_dot(x1.astype(BF16), wq_ref[...]) * MEM_SCALE).astype(BF16)
    outs = []
    for h in range(MEM_HEADS):
        hs = slice(h * MEM_HEAD_DIM, (h + 1) * MEM_HEAD_DIM)
        s = _dot_nt(qm[:, hs], mk_ref[:, hs])
        m = jnp.max(s, axis=-1, keepdims=True)
        e = jnp.exp(s - m)
        l = jnp.sum(e, axis=-1, keepdims=True)
        outs.append((_dot(e.astype(BF16), mv_ref[:, hs]) / l).astype(BF16))
    o = jnp.concatenate(outs, axis=-1)
    x2 = _layer_norm(alpha * x1 + _dot(o, wmo_ref[...]), g2_ref[...], b2_ref[...])
    x2_ref[...] = x2
    logits_t = _dot_nt(wrt_ref[...], x2.astype(BF16)) + br_ref[...]
    idx, gates, rank, carry = _route(logits_t, carry_s[:, 0:1], earlier_s[...])
    idx_ref[...] = idx
    gate_ref[...] = gates
    rank_ref[...] = rank
    carry_s[...] = jnp.broadcast_to(carry, carry_s.shape)
    cnt_ref[...] = carry_s[...]


def _round_bf16(x):
    return x.astype(BF16).astype(F32)


def _inproj_sample_kernel(x_ref, w_ref, o_ref):
    o_ref[...] = _dot(x_ref[...].astype(BF16), w_ref[...])


def _inproj_sample(x, w_in):
    n, D = x.shape
    N = w_in.shape[1]
    bn = 512
    return pl.pallas_call(
        _inproj_sample_kernel,
        grid=(N // bn,),
        in_specs=[pl.BlockSpec((n, D), lambda j: (0, 0)), pl.BlockSpec((D, bn), lambda j: (0, j))],
        out_specs=pl.BlockSpec((n, bn), lambda j: (0, j)),
        out_shape=jax.ShapeDtypeStruct((n, N), F32),
        compiler_params=pltpu.CompilerParams(dimension_semantics=("arbitrary",)),
        name="inproj_sample",
    )(x, w_in)


def _attn_sample_kernel(q_ref, kn_ref, vn_ref, kt_ref, vt_ref, bias_ref, bnew_ref, o_ref):
    H, E, W = kt_ref.shape

    def columns(ref):
        t = ref[...].T
        return jnp.stack([t[:E, h:h + 1] for h in range(H)], axis=0)

    q = _round_bf16(columns(q_ref))
    v_new = _round_bf16(columns(vn_ref))
    s_all = jnp.sum(_round_bf16(kt_ref[...]) * q, axis=1) * ATTN_SCALE
    s_new = jnp.sum(_round_bf16(columns(kn_ref)) * q, axis=1) * ATTN_SCALE
    outs, lses = [], []
    for p, (window, dil) in enumerate(PATTERNS):
        lo = W - window
        s = s_all[:, lo:] + bias_ref[p, :, lo:]
        sn = s_new + bnew_ref[p]
        m = jnp.maximum(jnp.max(s, axis=-1, keepdims=True), sn)
        e = jnp.exp(s - m)
        en = jnp.exp(sn - m)
        den = jnp.sum(e, axis=-1, keepdims=True) + en
        pr = _round_bf16(e / den)
        pv = jnp.sum(_round_bf16(vt_ref[:, :, lo:]) * pr[:, None, :], axis=-1, keepdims=True)
        outs.append(pv + _round_bf16(en / den)[:, :, None] * v_new)
        lses.append(m + jnp.log(den))
    lmax = jnp.maximum(jnp.maximum(lses[0], lses[1]), lses[2])
    ws = [jnp.exp(ls - lmax) for ls in lses]
    wsum = ws[0] + ws[1] + ws[2]
    acc = None
    for p in range(len(PATTERNS)):
        term = _round_bf16(ws[p] / wsum)[:, :, None] * _round_bf16(outs[p])
        acc = term if acc is None else acc + term
    lane = lax.broadcasted_iota(I32, o_ref.shape, 1)
    out = jnp.zeros(o_ref.shape, F32)
    for h in range(H):
        out = jnp.where(lane == h, acc[h], out)
    o_ref[...] = out


def _inproj_and_sample_attn_kernel(x_ref, w_ref, wc_ref, q4_ref, kn_ref, vn_ref, kt_ref, vt_ref, bpos_ref, bnew_ref,
                                   conv_ref, q_ref, k_ref, v_ref, cs_ref, attn_ref, u_s):
    _inproj_kernel(x_ref, w_ref, wc_ref, conv_ref, q_ref, k_ref, v_ref, cs_ref, u_s)
    _attn_sample_kernel(q4_ref, kn_ref, vn_ref, kt_ref, vt_ref, bpos_ref, bnew_ref, attn_ref)


def _inproj_prompt_and_attention_sample(x, w_in_bf, w_conv, q, k_new, v_new, win_kt, win_vt, bias_pos, bias_new):
    B, S, D = x.shape
    ts = PROJ_TILE
    n_s = S // ts
    DB, H, E, W = win_kt.shape
    assert B * n_s == DB
    row = lambda b, s: (b, s, 0)
    tok = lambda b, s: (b * n_s + s, 0, 0, 0)
    vec = pl.BlockSpec((None, LANES, LANES), lambda b, s: (b * n_s + s, 0, 0))
    vec_out = pl.BlockSpec((None, E, LANES), lambda b, s: (b * n_s + s, 0, 0))
    cache = pl.BlockSpec((None, H, E, W), tok)
    const = lambda a: pl.BlockSpec(a.shape, lambda b, s: (0,) * a.ndim)
    return pl.pallas_call(
        _inproj_and_sample_attn_kernel,
        grid=(B, n_s),
        in_specs=[pl.BlockSpec((None, ts, D), row), const(w_in_bf), const(w_conv),
                  vec, vec, vec, cache, cache, const(bias_pos), const(bias_new)],
        out_specs=[pl.BlockSpec((None, ts, CONV_WIDTH), row),
                   pl.BlockSpec((None, ts, ATTN_WIDTH), row),
                   pl.BlockSpec((None, ts, ATTN_WIDTH), row),
                   pl.BlockSpec((None, ts, ATTN_WIDTH), row),
                   pl.BlockSpec((None, 8, CONV_WIDTH), lambda b, s: (b, 0, 0)),
                   vec_out],
        out_shape=[jax.ShapeDtypeStruct((B, S, CONV_WIDTH), BF16),
                   jax.ShapeDtypeStruct((B, S, ATTN_WIDTH), F32),
                   jax.ShapeDtypeStruct((B, S, ATTN_WIDTH), F32),
                   jax.ShapeDtypeStruct((B, S, ATTN_WIDTH), F32),
                   jax.ShapeDtypeStruct((B, 8, CONV_WIDTH), F32),
                   jax.ShapeDtypeStruct((DB, E, LANES), F32)],
        scratch_shapes=[pltpu.VMEM((ts + 8, CONV_WIDTH), F32)],
        compiler_params=pltpu.CompilerParams(dimension_semantics=("arbitrary", "arbitrary"),
                                             vmem_limit_bytes=_FUSED_PROJ_VMEM_BYTES),
        name="inproj_prompt_attn_sample",
    )(x, w_in_bf, w_conv, q, k_new, v_new, win_kt, win_vt, bias_pos, bias_new)


def _tail_a_sample_kernel(x_ref, proj_ref, s0_ref, s1_ref, wc_ref, attn_ref, wo_ref, g1_ref, b1_ref, wq_ref,
                          x1_ref, qm_ref, u_ref, *, alpha):
    cw = CONV_WIDTH
    gb = proj_ref[:, 0:cw]
    u = proj_ref[:, cw:2 * cw] * proj_ref[:, 2 * cw:3 * cw]
    u_ref[...] = u
    wc = wc_ref[...]
    conv = gb * (wc[0:1, :] * s0_ref[...] + wc[1:2, :] * s1_ref[...] + wc[2:3, :] * u)
    mixed = jnp.concatenate([conv, attn_ref[...]], axis=-1).astype(BF16)
    x1 = _layer_norm(alpha * x_ref[...] + _dot(mixed, wo_ref[...]), g1_ref[...], b1_ref[...])
    x1_ref[...] = x1
    qm_ref[...] = _dot(x1.astype(BF16), wq_ref[...])


def _tail_a_sample(x, proj, s0, s1, w_conv, attn, w_out, g1, b1, w_mem_q, alpha):
    n, D = x.shape
    return pl.pallas_call(
        functools.partial(_tail_a_sample_kernel, alpha=alpha),
        out_shape=[jax.ShapeDtypeStruct((n, D), F32)] * 2 + [jax.ShapeDtypeStruct((n, CONV_WIDTH), F32)],
        name="tail_a_sample",
    )(x, proj, s0, s1, w_conv, attn, w_out, g1, b1, w_mem_q)


def _memattn_sample_kernel(q_ref, k_ref, v_ref, o_ref):
    q = _round_bf16(q_ref[...])
    s = jnp.sum(_round_bf16(k_ref[...]) * q[None], axis=-1, keepdims=True) * MEM_SCALE
    m = jnp.max(s, axis=0)
    e = jnp.exp(s - m[None])
    den = jnp.sum(e, axis=0)
    pr = _round_bf16(e / den[None])
    o_ref[...] = jnp.sum(pr * _round_bf16(v_ref[...]), axis=0)


_N_MID_IN = 15


def _mid_and_sample_memattn_kernel(*refs, alpha):
    mid_in, (qm_ref, smk_ref, smv_ref) = refs[:_N_MID_IN], refs[_N_MID_IN:_N_MID_IN + 3]
    mid_out, so_ref, (carry_s, earlier_s) = refs[_N_MID_IN + 3:-3], refs[-3], refs[-2:]
    _mid_kernel(*mid_in, *mid_out, carry_s, earlier_s, alpha=alpha)
    _memattn_sample_kernel(qm_ref, smk_ref, smv_ref, so_ref)


def _mid_prompt_and_memattn_sample(x, conv, attn, woc, woa, g1, b1, mk, mv, wq, wmo, g2, b2, wrt, br,
                                   qm_s, mem_k_s, mem_v_s, alpha):
    B, S, D = x.shape
    ts = SEQ_TILE
    n_s = S // ts
    T = B * S
    DB, M, H, E = mem_k_s.shape
    assert B * n_s == DB
    row = lambda b, s: (b, s, 0)
    full = lambda b, s: (0, 0)
    tok = lambda b, s: (0, b * n_s + s)
    mem = lambda b, s: (b, 0, 0)
    vec = pl.BlockSpec((1, D), full)
    s_vec = pl.BlockSpec((None, H, E), lambda b, s: (b * n_s + s, 0, 0))
    s_mem = pl.BlockSpec((None, M, H, E), lambda b, s: (b * n_s + s, 0, 0, 0))
    in_specs = [pl.BlockSpec((None, ts, D), row),
                pl.BlockSpec((None, ts, CONV_WIDTH), row),
                pl.BlockSpec((None, ts, ATTN_WIDTH), row),
                pl.BlockSpec(woc.shape, full), pl.BlockSpec(woa.shape, full), vec, vec,
                pl.BlockSpec((None, MEM_TOKENS, D), mem), pl.BlockSpec((None, MEM_TOKENS, D), mem),
                pl.BlockSpec(wq.shape, full), pl.BlockSpec(wmo.shape, full), vec, vec,
                pl.BlockSpec(wrt.shape, full), pl.BlockSpec(br.shape, full)]
    assert len(in_specs) == _N_MID_IN
    return pl.pallas_call(
        functools.partial(_mid_and_sample_memattn_kernel, alpha=alpha),
        grid=(B, n_s),
        in_specs=in_specs + [s_vec, s_mem, s_mem],
        out_specs=[pl.BlockSpec((None, ts, D), row),
                   pl.BlockSpec((TOP_K, ts), tok), pl.BlockSpec((TOP_K, ts), tok), pl.BlockSpec((TOP_K, ts), tok),
                   pl.BlockSpec((N_EXPERTS, LANES), full), s_vec],
        out_shape=[jax.ShapeDtypeStruct((B, S, D), F32),
                   jax.ShapeDtypeStruct((TOP_K, T), I32),
                   jax.ShapeDtypeStruct((TOP_K, T), F32),
                   jax.ShapeDtypeStruct((TOP_K, T), I32),
                   jax.ShapeDtypeStruct((N_EXPERTS, LANES), F32),
                   jax.ShapeDtypeStruct((DB, H, E), F32)],
        scratch_shapes=[pltpu.VMEM((N_EXPERTS, LANES), F32), pltpu.VMEM((ts, ts), BF16)],
        compiler_params=pltpu.CompilerParams(dimension_semantics=("arbitrary", "arbitrary")),
        name="mid_prompt_memattn_sample",
    )(x, conv, attn, woc, woa, g1, b1, mk, mv, wq, wmo, g2, b2, wrt, br, qm_s, mem_k_s, mem_v_s)


def _tail_b_sample_kernel(x1_ref, o_ref, wmo_ref, g2_ref, b2_ref, wrt_ref, br_ref, cnt_ref,
                          x2_ref, idx_ref, gate_ref, rank_ref, cnt_out_ref, *, alpha):
    x2 = _layer_norm(alpha * x1_ref[...] + _dot(o_ref[...].astype(BF16), wmo_ref[...]), g2_ref[...], b2_ref[...])
    x2_ref[...] = x2
    logits_t = _dot_nt(wrt_ref[...], x2.astype(BF16)) + br_ref[...]
    idx, gates, rank, carry = _route(logits_t, cnt_ref[:, 0:1], _earlier_token_matrix(logits_t.shape[1]))
    idx_ref[...] = idx
    gate_ref[...] = gates
    rank_ref[...] = rank
    cnt_out_ref[...] = jnp.broadcast_to(carry, cnt_out_ref.shape)


def _tail_b_sample(x1, o, w_mem_o, g2, b2, wrt, br, cnt, alpha):
    n, D = x1.shape
    return pl.pallas_call(
        functools.partial(_tail_b_sample_kernel, alpha=alpha),
        out_shape=[jax.ShapeDtypeStruct((n, D), F32),
                   jax.ShapeDtypeStruct((TOP_K, n), I32),
                   jax.ShapeDtypeStruct((TOP_K, n), F32),
                   jax.ShapeDtypeStruct((TOP_K, n), I32),
                   jax.ShapeDtypeStruct(cnt.shape, F32)],
        name="tail_b_sample",
    )(x1, o, w_mem_o, g2, b2, wrt, br, cnt)


def _row_copy(src, dst, sem, src_row, dst_row):
    return pltpu.make_async_copy(src.at[pl.ds(src_row, 1)], dst.at[pl.ds(dst_row, 1)], sem)


def _dispatch_kernel(dest_ref, x_ref, dest2_ref, x2_ref, xb_ref, zeros_s, stage_s, sems, zsem, *, n_tok_rows):
    i = pl.program_id(0)
    last = pl.num_programs(0) - 1
    slot = i % 2
    nt = dest_ref.shape[1] // TOP_K

    def start(src_ref, first, count, idx_ref, sem):
        def issue(g, c):
            for u in range(ROW_COPY_UNROLL):
                t = g * ROW_COPY_UNROLL + u
                for k in range(TOP_K):
                    _row_copy(src_ref, xb_ref, sem, first + t, idx_ref[0, k * count + t]).start(priority=k % 2)
            return c

        lax.fori_loop(0, count // ROW_COPY_UNROLL, issue, 0)

    def drain(src_ref, count, sem):
        for k in range(TOP_K):
            pltpu.make_async_copy(src_ref.at[pl.ds(0, count)], xb_ref.at[pl.ds(0, count)], sem).wait()

    @pl.when(i < last)
    def _():
        stage_s[slot] = x_ref[...]
        start(stage_s.at[slot], 0, nt, dest_ref, sems.at[slot])

    @pl.when(i == last)
    def _():
        zeros_s[...] = jnp.zeros_like(zeros_s)
        pltpu.make_async_copy(zeros_s, xb_ref.at[pl.ds(n_tok_rows, zeros_s.shape[0])], zsem).start()
        start(x2_ref, 0, x2_ref.shape[0], dest2_ref, sems.at[slot])

    @pl.when(i > 0)
    def _():
        drain(stage_s.at[1 - slot], nt, sems.at[1 - slot])

    @pl.when(i == last)
    def _():
        drain(x2_ref, x2_ref.shape[0], sems.at[slot])
        pltpu.make_async_copy(zeros_s, xb_ref.at[pl.ds(n_tok_rows, zeros_s.shape[0])], zsem).wait()


def _dispatch(x_a, dest_a, x_b, dest_b, n_rows):
    Ta, D = x_a.shape
    Tb = x_b.shape[0]
    nt = dest_a.shape[2] // TOP_K
    steps = Ta // nt
    assert nt % ROW_COPY_UNROLL == 0 and Tb % ROW_COPY_UNROLL == 0
    n_tok_rows = (Ta + Tb) * TOP_K
    tile = lambda i: (jnp.minimum(i, steps - 1), 0, 0)
    return pl.pallas_call(
        functools.partial(_dispatch_kernel, n_tok_rows=n_tok_rows),
        grid=(steps + 1,),
        in_specs=[pl.BlockSpec((None, 1, TOP_K * nt), tile, memory_space=pltpu.SMEM),
                  pl.BlockSpec((nt, D), lambda i: (jnp.minimum(i, steps - 1), 0)),
                  pl.BlockSpec((None, 1, TOP_K * Tb), lambda i: (0, 0, 0), memory_space=pltpu.SMEM),
                  pl.BlockSpec((Tb, D), lambda i: (0, 0))],
        out_specs=pl.BlockSpec(memory_space=pl.ANY),
        out_shape=jax.ShapeDtypeStruct((n_rows, D), x_a.dtype),
        scratch_shapes=[pltpu.VMEM((n_rows - n_tok_rows, D), x_a.dtype), pltpu.VMEM((2, nt, D), x_a.dtype),
                        pltpu.SemaphoreType.DMA((2,)), pltpu.SemaphoreType.DMA],
        compiler_params=pltpu.CompilerParams(dimension_semantics=("arbitrary",)),
        name="moe_dispatch",
    )(dest_a, x_a, dest_b, x_b)


def _moe_kernel(blk_ref, exp_ref, lo_ref, hi_ref, xb_ref, wgu_ref, bgu_ref, wd_ref, bd_ref, yb_ref,
                wgu_s, wd_s, cast_s):
    s = pl.program_id(0)
    new_block = jnp.logical_or(s == 0, blk_ref[s] != blk_ref[jnp.maximum(s - 1, 0)])
    lo, hi = lo_ref[s], hi_ref[s]
    used = hi > lo

    @pl.when(s == 0)
    def _():
        cast_s[0] = -1

    @pl.when(jnp.logical_and(used, cast_s[0] != exp_ref[s]))
    def _():
        wgu_s[...] = wgu_ref[...].astype(BF16)
        wd_s[...] = wd_ref[...].astype(BF16)
        cast_s[0] = exp_ref[s]

    @pl.when(new_block)
    def _():
        yb_ref[...] = jnp.zeros_like(yb_ref)

    @pl.when(used)
    def _():
        gu = _dot(xb_ref[...].astype(BF16), wgu_s[...]) + bgu_ref[...]
        g = jnp.minimum(gu[:, :D_FF], SWIGLU_LIMIT)
        u = jnp.clip(gu[:, D_FF:], -SWIGLU_LIMIT, SWIGLU_LIMIT)
        act = (u + 1.0) * g * jax.nn.sigmoid(SWIGLU_ALPHA * g)
        y = _dot(act.astype(BF16), wd_s[...]) + bd_ref[...]
        row = lax.broadcasted_iota(I32, (xb_ref.shape[0], 1), 0)
        yb_ref[...] = jnp.where(jnp.logical_and(row >= lo, row < hi), y, yb_ref[...])


def _moe(seg_blk, seg_exp, seg_lo, seg_hi, xb, w_gate_up, b_gate_up, w_down, b_down):
    P, D = xb.shape
    bm = ROW_BLOCK
    E, _, F2 = w_gate_up.shape
    ex = lambda s, blk, exp, lo, hi: (exp[s], 0, 0)
    rows = lambda s, blk, exp, lo, hi: (blk[s], 0)
    expert_bytes = (D * F2 + D_FF * D) * (2 * 4 + 2)
    block_bytes = bm * D * 4 * 2 * 2 + bm * F2 * 4 * 3
    return pl.pallas_call(
        _moe_kernel,
        grid_spec=pltpu.PrefetchScalarGridSpec(
            num_scalar_prefetch=4,
            grid=(seg_blk.shape[0],),
            in_specs=[pl.BlockSpec((bm, D), rows),
                      pl.BlockSpec((None, D, F2), ex), pl.BlockSpec((None, 1, F2), ex),
                      pl.BlockSpec((None, D_FF, D), ex), pl.BlockSpec((None, 1, D), ex)],
            out_specs=pl.BlockSpec((bm, D), rows),
            scratch_shapes=[pltpu.VMEM((D, F2), BF16), pltpu.VMEM((D_FF, D), BF16), pltpu.SMEM((1,), I32)]),
        out_shape=jax.ShapeDtypeStruct((P, D), F32),
        compiler_params=pltpu.CompilerParams(dimension_semantics=("arbitrary",),
                                             vmem_limit_bytes=expert_bytes + block_bytes),
        name="moe_grouped_ffn",
    )(seg_blk, seg_exp, seg_lo, seg_hi, xb, w_gate_up, b_gate_up.reshape(E, 1, F2), w_down, b_down.reshape(E, 1, D))


def _combine_kernel(dest_ref, next_ref, yb_ref, x2_ref, gate_ref, g3_ref, b3_ref, o_ref, rows_s, sems, *, alpha):
    nt = x2_ref.shape[0]
    i = pl.program_id(0)
    slot = i % 2

    def gather(idx_ref, buf):
        def issue(g, c):
            for u in range(ROW_COPY_UNROLL):
                t = g * ROW_COPY_UNROLL + u
                for k in range(TOP_K):
                    _row_copy(yb_ref, rows_s.at[buf, k], sems.at[buf], idx_ref[0, k * nt + t], t).start(priority=k % 2)
            return c
        lax.fori_loop(0, nt // ROW_COPY_UNROLL, issue, 0)

    def drain(buf):
        for k in range(TOP_K):
            pltpu.make_async_copy(yb_ref.at[pl.ds(0, nt)], rows_s.at[buf, k], sems.at[buf]).wait()

    @pl.when(i == 0)
    def _():
        gather(dest_ref, 0)

    drain(slot)
    for t in range(nt):
        for k in range(TOP_K):
            _row_copy(yb_ref, rows_s.at[1 - slot, k], sems.at[1 - slot], next_ref[0, k * nt + t], t).start(priority=k % 2)
    gates = gate_ref[...]
    y = gates[:, 0:1] * rows_s[slot, 0]
    for k in range(1, TOP_K):
        y = y + gates[:, k:k + 1] * rows_s[slot, k]
    o_ref[...] = _layer_norm(alpha * x2_ref[...] + y, g3_ref[...], b3_ref[...])

    @pl.when(i + 1 == pl.num_programs(0))
    def _():
        drain(1 - slot)


def _combine(yb, dest_tiles, x2, gates_tok, g3, b3, alpha):
    T, D = x2.shape
    steps = dest_tiles.shape[0]
    nt = dest_tiles.shape[2] // TOP_K
    assert nt % ROW_COPY_UNROLL == 0
    idx_spec = lambda ahead: pl.BlockSpec((None, 1, TOP_K * nt), lambda i: (jnp.minimum(i + ahead, steps - 1), 0, 0),
                                          memory_space=pltpu.SMEM)
    return pl.pallas_call(
        functools.partial(_combine_kernel, alpha=alpha),
        grid=(steps,),
        in_specs=[idx_spec(0), idx_spec(1),
                  pl.BlockSpec(memory_space=pl.ANY),
                  pl.BlockSpec((nt, D), lambda i: (i, 0)),
                  pl.BlockSpec((nt, TOP_K), lambda i: (i, 0)),
                  pl.BlockSpec((1, D), lambda i: (0, 0)), pl.BlockSpec((1, D), lambda i: (0, 0))],
        out_specs=pl.BlockSpec((nt, D), lambda i: (i, 0)),
        out_shape=jax.ShapeDtypeStruct((T, D), F32),
        scratch_shapes=[pltpu.VMEM((2, TOP_K, nt, D), F32), pltpu.SemaphoreType.DMA((2,))],
        compiler_params=pltpu.CompilerParams(dimension_semantics=("arbitrary",)),
        name="moe_combine",
    )(dest_tiles, dest_tiles, yb, x2, gates_tok, g3, b3)


def _rel_bucket(dist):
    max_exact = N_BUCKETS // 2
    df = jnp.maximum(dist, 1).astype(F32)
    large = max_exact + (jnp.log(df / max_exact) / math.log(MAX_DISTANCE / max_exact)
                         * (N_BUCKETS - max_exact)).astype(I32)
    return jnp.where(dist < max_exact, dist, jnp.minimum(large, N_BUCKETS - 1))


def _bias_tables(rel_table):
    n = BAND
    taps = jnp.arange(n + 1)
    by_tap = jnp.stack([rel_table[_rel_bucket(taps * dil)].astype(F32) for _, dil in PATTERNS])
    w = jnp.concatenate([by_tap[:, ::-1], jnp.full((len(PATTERNS), n - 1, N_HEADS), NEG_INF, F32)], axis=1)
    w = jnp.transpose(w, (0, 2, 1))
    return by_tap, w.reshape(len(PATTERNS), N_SLABS, HEADS_PER_SLAB, 2 * n)


def _tap_bias_by_slot(by_tap, W):
    assert PAST_LEN % W == 0 and PAST_LEN >= WINDOW_MAX
    n = BAND
    out = []
    for p, (window, dil) in enumerate(PATTERNS):
        taps = by_tap[p, :0:-1]
        col = jnp.concatenate([jnp.full((W // dil - n, N_HEADS), NEG_INF, F32), taps], axis=0)
        rest = jnp.full((W // dil, dil - 1, N_HEADS), NEG_INF, F32)
        out.append(jnp.concatenate([col[:, None, :], rest], axis=1).reshape(W, N_HEADS).T)
    return jnp.stack(out)


def _dest_tiles(dest, nt):
    T = dest.shape[1]
    return dest.reshape(TOP_K, T // nt, nt).transpose(1, 0, 2).reshape(T // nt, 1, TOP_K * nt)


def kernel(x_prompt, x_sample, mem_prompt, cache_win_k, cache_win_v, state_conv, cache_mem_k, cache_mem_v,
           rel_bias_table, w_in, w_conv, w_out, ln1_g, ln1_b, w_mem_q, w_mem_k, w_mem_v, w_mem_o,
           ln2_g, ln2_b, w_router, b_router, w_gate_up, b_gate_up, w_down, b_down, ln3_g, ln3_b):
    depth = w_in.shape[0]
    assert depth == 1
    alpha = (2 * depth) ** 0.25
    B, S, D = x_prompt.shape
    DB = x_sample.shape[0]
    T = B * S
    l = 0
    vec = lambda a: a[l].reshape(1, -1)

    by_tap, bias_band = _bias_tables(rel_bias_table)
    wrt = w_router[l].T.astype(BF16)
    br = b_router[l].reshape(N_EXPERTS, 1)
    w_in_bf = w_in[l].astype(BF16)
    wo_bf = w_out[l].astype(BF16)
    wq_bf = w_mem_q[l].astype(BF16)
    wmo_bf = w_mem_o[l].astype(BF16)

    xs = x_sample.reshape(DB, D)
    proj = _inproj_sample(xs, w_in_bf)
    cw = CONV_WIDTH
    q_s = proj[:, 3 * cw:3 * cw + ATTN_WIDTH]
    k_s = proj[:, 3 * cw + ATTN_WIDTH:3 * cw + 2 * ATTN_WIDTH]
    v_s = proj[:, 3 * cw + 2 * ATTN_WIDTH:]
    H, E = N_HEADS, HEAD_DIM
    bias_new = by_tap[:, 0, :, None]
    slots_last = lambda c: jnp.transpose(c, (0, 2, 3, 1))
    corner = lambda a: jnp.pad(a.reshape(DB, H, E), ((0, 0), (0, LANES - H), (0, LANES - E)))
    conv_p, q_p, k_p, v_p, cs_p, attn_s = _inproj_prompt_and_attention_sample(
        x_prompt, w_in_bf, w_conv[l], corner(q_s), corner(k_s), corner(v_s),
        slots_last(cache_win_k[l]), slots_last(cache_win_v[l]), _tap_bias_by_slot(by_tap, WINDOW_MAX), bias_new)
    attn_s = jnp.transpose(attn_s[:, :, :H], (0, 2, 1)).reshape(DB, ATTN_WIDTH)

    sc = state_conv[l]
    x1_s, qm_s, u_s = _tail_a_sample(xs, proj, sc[:, 0], sc[:, 1], w_conv[l], attn_s, wo_bf,
                                     vec(ln1_g), vec(ln1_b), wq_bf, alpha)
    attn_p = _attention_prompt(q_p, k_p, v_p, bias_band)
    mk, mv, mk_bf, mv_bf = _memkv_prompt(mem_prompt, w_mem_k[l].astype(BF16), w_mem_v[l].astype(BF16))
    x2_p, idx_p, gate_p, rank_p, cnt_p, o_s = _mid_prompt_and_memattn_sample(
        x_prompt, conv_p, attn_p, wo_bf[:CONV_WIDTH], wo_bf[CONV_WIDTH:], vec(ln1_g), vec(ln1_b), mk_bf, mv_bf,
        wq_bf, wmo_bf, vec(ln2_g), vec(ln2_b), wrt, br,
        qm_s.reshape(DB, MEM_HEADS, MEM_HEAD_DIM), cache_mem_k[l], cache_mem_v[l], alpha)
    x2_s, idx_s, gate_s, rank_s, cnt = _tail_b_sample(x1_s, o_s.reshape(DB, D), wmo_bf, vec(ln2_g), vec(ln2_b), wrt, br,
                                                      cnt_p, alpha)

    bm = ROW_BLOCK
    counts = cnt[:, 0].astype(I32)
    ends = jnp.cumsum(counts)
    starts = ends - counts
    n_tok_rows = (T + DB) * TOP_K
    n_blocks = -(-n_tok_rows // bm)
    P = n_blocks * bm
    cuts = jnp.sort(jnp.concatenate([jnp.arange(n_blocks, dtype=I32) * bm, ends[:-1]]))
    cut_ends = jnp.concatenate([cuts[1:], jnp.full((1,), P, I32)])
    seg_blk = jnp.minimum(cuts // bm, n_blocks - 1)
    seg_exp = jnp.minimum(jnp.sum((ends[None, :] <= cuts[:, None]).astype(I32), axis=1), N_EXPERTS - 1)
    seg_lo = cuts - seg_blk * bm
    seg_hi = cut_ends - seg_blk * bm
    expert_ids = jnp.arange(N_EXPERTS, dtype=I32)[:, None, None]
    start_of = lambda idx: jnp.sum(jnp.where(idx[None] == expert_ids, starts[:, None, None], 0), axis=0)
    dest_p = start_of(idx_p) + rank_p
    dest_s = start_of(idx_s) + rank_s
    dtiles_p = _dest_tiles(dest_p, COMBINE_TILE)
    dtiles_s = _dest_tiles(dest_s, DB)

    x2_pf = x2_p.reshape(T, D)
    xb = _dispatch(x2_pf, dtiles_p, x2_s, dtiles_s, P)
    yb = _moe(seg_blk, seg_exp, seg_lo, seg_hi, xb, w_gate_up[l], b_gate_up[l], w_down[l], b_down[l])
    y_p = _combine(yb, dtiles_p, x2_pf, gate_p.T, vec(ln3_g), vec(ln3_b), alpha).reshape(B, S, D)
    y_s = _combine(yb, dtiles_s, x2_s, gate_s.T, vec(ln3_g), vec(ln3_b), alpha).reshape(DB, 1, D)

    conv_state_s = jnp.stack([sc[:, 1], u_s], axis=1)
    return (y_p, y_s,
            k_p.reshape(1, B, S, H, E), v_p.reshape(1, B, S, H, E), cs_p[:, 6:8].reshape(1, B, CONV_K - 1, CONV_WIDTH),
            mk[None], mv[None],
            k_s.reshape(1, DB, 1, H, E), v_s.reshape(1, DB, 1, H, E), conv_state_s.reshape(1, DB, CONV_K - 1, CONV_WIDTH))
```

```python
import functools
import math

import jax
import jax.numpy as jnp
from jax import lax
from jax.experimental import pallas as pl
from jax.experimental.pallas import tpu as pltpu

F32 = jnp.float32
BF16 = jnp.bfloat16
I32 = jnp.int32

D_MODEL = 1024
CONV_WIDTH = 256
CONV_K = 3
HEAD_DIM = 64
N_HEADS = 12
ATTN_WIDTH = N_HEADS * HEAD_DIM
PATTERNS = ((128, 1), (512, 4), (2048, 16))
BAND = 128
WINDOW_MAX = 2048
ATTN_SCALE = 1.0 / math.sqrt(HEAD_DIM)
N_BUCKETS = 32
MAX_DISTANCE = WINDOW_MAX
MEM_TOKENS = 256
MEM_HEADS = 4
MEM_HEAD_DIM = 256
MEM_SCALE = 1.0 / math.sqrt(MEM_HEAD_DIM)
N_EXPERTS = 32
TOP_K = 4
D_FF = 1024
SWIGLU_LIMIT = 7.0
SWIGLU_ALPHA = 1.702
LN_EPS = 1e-5
PAST_LEN = 8192
NEG_INF = -1e30

LANES = 128
HEADS_PER_SLAB = LANES // HEAD_DIM
N_SLABS = ATTN_WIDTH // LANES

PROJ_TILE = 512
SEQ_TILE = 512
ROW_BLOCK = 512
COMBINE_TILE = 256
ROW_COPY_UNROLL = 32
ATTN_GROUP = 6

_FUSED_PROJ_VMEM_BYTES = 58 * 1024 * 1024


def _layer_norm(x, g, b):
    mu = jnp.mean(x, axis=-1, keepdims=True)
    var = jnp.mean(jnp.square(x - mu), axis=-1, keepdims=True)
    return (x - mu) * lax.rsqrt(var + LN_EPS) * g + b


def _dot(a, b):
    return jnp.dot(a, b, preferred_element_type=F32)


def _dot_nt(a, b):
    return lax.dot_general(a, b, (((1,), (1,)), ((), ())), preferred_element_type=F32)


def _inproj_kernel(x_ref, w_ref, wc_ref, conv_ref, q_ref, k_ref, v_ref, cs_ref, u_s):
    ts = x_ref.shape[0]
    cw = CONV_WIDTH

    @pl.when(pl.program_id(1) == 0)
    def _():
        u_s[0:8, :] = jnp.zeros((8, cw), F32)

    x = x_ref[...].astype(BF16)
    gb = _dot(x, w_ref[:, 0:cw])
    gc = _dot(x, w_ref[:, cw:2 * cw])
    h = _dot(x, w_ref[:, 2 * cw:3 * cw])
    u = gc * h
    u_s[8:8 + ts, :] = u
    wc = wc_ref[...]
    conv = wc[0:1, :] * u_s[6:6 + ts, :] + wc[1:2, :] * u_s[7:7 + ts, :] + wc[2:3, :] * u
    conv_ref[...] = (gb * conv).astype(conv_ref.dtype)
    o = 3 * cw
    q_ref[...] = _dot(x, w_ref[:, o:o + ATTN_WIDTH]) * ATTN_SCALE
    k_ref[...] = _dot(x, w_ref[:, o + ATTN_WIDTH:o + 2 * ATTN_WIDTH])
    v_ref[...] = _dot(x, w_ref[:, o + 2 * ATTN_WIDTH:o + 3 * ATTN_WIDTH])
    tail = u_s[ts:ts + 8, :]
    u_s[0:8, :] = tail
    cs_ref[...] = tail


def _attn_kernel(q_ref, k_ref, v_ref, w_ref, o_ref, bias_ref, m_s, l_s, acc_s):
    S = q_ref.shape[0]
    n = BAND
    lane = lax.broadcasted_iota(I32, (n, LANES), 1)
    head_a = lane < HEAD_DIM

    @pl.when(pl.program_id(1) == 0)
    def _():
        for p in range(len(PATTERNS)):
            for hh in range(HEADS_PER_SLAB):
                first_row = jnp.broadcast_to(w_ref[p, hh:hh + 1, :], (n, 2 * n))
                bias_ref[p, hh * n:(hh + 1) * n, :] = pltpu.roll(first_row, 0, 1, stride=1, stride_axis=0)

    def rows_of(start, count, dil):
        return pl.ds(start, count) if dil == 1 else pl.ds(start, count, stride=dil)

    def blocks(p, dil, specs):
        rows = [rows_of(start, n, dil) for start, _ in specs]
        scores, values = [], []
        for (start, has_prev), r in zip(specs, rows):
            qb = q_ref[r, :]
            q2 = jnp.concatenate([jnp.where(head_a, qb, 0.0), jnp.where(head_a, 0.0, qb)], axis=0).astype(BF16)
            krows = rows_of(start - n * dil, 2 * n, dil) if has_prev else r
            bias = bias_ref[p] if has_prev else bias_ref[p, :, n:2 * n]
            scores.append(_dot_nt(q2, k_ref[krows, :].astype(BF16)) + bias)
            values.append(v_ref[krows, :].astype(BF16))
        ms = [jnp.max(s, axis=-1, keepdims=True) for s in scores]
        es = [jnp.exp(s - m) for s, m in zip(scores, ms)]
        ls = [jnp.sum(e, axis=-1, keepdims=True) for e in es]
        pvs = [_dot(e.astype(BF16), vb) for e, vb in zip(es, values)]
        for r, m, l, pv in zip(rows, ms, ls, pvs):
            m_s[p, r, :] = jnp.where(head_a, m[:n], m[n:])
            l_s[p, r, :] = jnp.where(head_a, l[:n], l[n:])
            acc_s[p, r, :] = jnp.where(head_a, pv[:n], pv[n:])

    for p, (window, dil) in enumerate(PATTERNS):
        nb = (S // dil) // n
        specs = [(i * n * dil + r, i > 0) for r in range(dil) for i in range(nb)]
        for g in range(0, len(specs), ATTN_GROUP):
            blocks(p, dil, specs[g:g + ATTN_GROUP])

    rows_per_step = 256

    def merge(t, c):
        rows = pl.ds(t * rows_per_step, rows_per_step)
        ms = [m_s[p, rows, :] for p in range(len(PATTERNS))]
        m_all = jnp.maximum(jnp.maximum(ms[0], ms[1]), ms[2])
        num = den = None
        for p in range(len(PATTERNS)):
            w = jnp.exp(ms[p] - m_all)
            num = w * acc_s[p, rows, :] if num is None else num + w * acc_s[p, rows, :]
            den = w * l_s[p, rows, :] if den is None else den + w * l_s[p, rows, :]
        o_ref[rows, :] = (num / den).astype(o_ref.dtype)
        return c

    lax.fori_loop(0, S // rows_per_step, merge, 0)


def _attention_prompt(q, k, v, bias_tab):
    B, S, _ = q.shape
    slab = lambda p, b: (b, 0, p)
    spec = pl.BlockSpec((None, S, LANES), slab)
    return pl.pallas_call(
        _attn_kernel,
        grid=(N_SLABS, B),
        in_specs=[spec, spec, spec,
                  pl.BlockSpec((len(PATTERNS), None, HEADS_PER_SLAB, 2 * BAND), lambda p, b: (0, p, 0, 0))],
        out_specs=spec,
        out_shape=jax.ShapeDtypeStruct((B, S, ATTN_WIDTH), BF16),
        scratch_shapes=[pltpu.VMEM((len(PATTERNS), HEADS_PER_SLAB * BAND, 2 * BAND), F32)]
        + [pltpu.VMEM((len(PATTERNS), S, LANES), F32)] * 3,
        compiler_params=pltpu.CompilerParams(dimension_semantics=("arbitrary", "arbitrary")),
        name="dilated_attn_prompt",
    )(q, k, v, bias_tab)


def _memkv_kernel(m_ref, wk_ref, wv_ref, k_ref, v_ref, kb_ref, vb_ref):
    x = m_ref[...].astype(BF16)
    k = _dot(x, wk_ref[...])
    v = _dot(x, wv_ref[...])
    k_ref[...] = k.reshape(k_ref.shape)
    v_ref[...] = v.reshape(v_ref.shape)
    kb_ref[...] = k.astype(BF16)
    vb_ref[...] = v.astype(BF16)


def _memkv_prompt(mem, wk_bf, wv_bf):
    B, M, D = mem.shape
    row = lambda b: (b, 0, 0)
    full = lambda b: (0, 0)
    heads = pl.BlockSpec((None, M, MEM_HEADS, MEM_HEAD_DIM), lambda b: (b, 0, 0, 0))
    return pl.pallas_call(
        _memkv_kernel,
        grid=(B,),
        in_specs=[pl.BlockSpec((None, M, D), row), pl.BlockSpec((D, D), full), pl.BlockSpec((D, D), full)],
        out_specs=[heads, heads, pl.BlockSpec((None, M, D), row), pl.BlockSpec((None, M, D), row)],
        out_shape=[jax.ShapeDtypeStruct((B, M, MEM_HEADS, MEM_HEAD_DIM), F32)] * 2
        + [jax.ShapeDtypeStruct((B, M, D), BF16)] * 2,
        compiler_params=pltpu.CompilerParams(dimension_semantics=("arbitrary",)),
        name="memkv_prompt",
    )(mem, wk_bf, wv_bf)


def _earlier_token_matrix(nt):
    before = lax.broadcasted_iota(I32, (nt, nt), 0) < lax.broadcasted_iota(I32, (nt, nt), 1)
    return jnp.where(before, 1.0, 0.0).astype(BF16)


def _route(logits_t, carry, earlier):
    E, nt = logits_t.shape
    eidx = lax.broadcasted_iota(I32, (E, nt), 0)
    l = logits_t
    vals, idxs = [], []
    for _ in range(TOP_K):
        m = jnp.max(l, axis=0, keepdims=True)
        sel = jnp.min(jnp.where(l == m, eidx, E), axis=0, keepdims=True)
        vals.append(m)
        idxs.append(sel)
        l = jnp.where(eidx == sel, -jnp.inf, l)
    es = [jnp.exp(v - vals[0]) for v in vals]
    den = es[0] + es[1] + es[2] + es[3]
    gates = _stack_rows([e / den for e in es])
    chosen = (l == -jnp.inf)
    onehot = jnp.where(chosen, 1.0, 0.0)
    prefix = _dot(onehot.astype(BF16), earlier) + carry
    ranks = [jnp.sum(jnp.where(eidx == s, prefix, 0.0), axis=0, keepdims=True) for s in idxs]
    rank = _stack_rows(ranks).astype(I32)
    idx = _stack_rows(idxs)
    return idx, gates, rank, carry + jnp.sum(onehot, axis=1, keepdims=True)


def _stack_rows(rows):
    k, nt = len(rows), rows[0].shape[1]
    r = lax.broadcasted_iota(I32, (k, nt), 0)
    out = jnp.broadcast_to(rows[-1], (k, nt))
    for i in range(k - 2, -1, -1):
        out = jnp.where(r == i, rows[i], out)
    return out


def _mid_kernel(x_ref, conv_ref, attn_ref, woc_ref, woa_ref, g1_ref, b1_ref, mk_ref, mv_ref, wq_ref, wmo_ref,
                g2_ref, b2_ref, wrt_ref, br_ref,
                x2_ref, idx_ref, gate_ref, rank_ref, cnt_ref, carry_s, earlier_s, *, alpha):
    first = jnp.logical_and(pl.program_id(0) == 0, pl.program_id(1) == 0)

    @pl.when(first)
    def _():
        carry_s[...] = jnp.zeros_like(carry_s)
        earlier_s[...] = _earlier_token_matrix(earlier_s.shape[0])

    x = x_ref[...]
    mix = _dot(conv_ref[...], woc_ref[...]) + _dot(attn_ref[...], woa_ref[...])
    x1 = _layer_norm(alpha * x + mix, g1_ref[...], b1_ref[...])
    qm = (_dot(x1.astype(BF16), wq_ref[...]) * MEM_SCALE).astype(BF16)
    outs = []
    for h in range(MEM_HEADS):
        hs = slice(h * MEM_HEAD_DIM, (h + 1) * MEM_HEAD_DIM)
        s = _dot_nt(qm[:, hs], mk_ref[:, hs])
        m = jnp.max(s, axis=-1, keepdims=True)
        e = jnp.exp(s - m)
        l = jnp.sum(e, axis=-1, keepdims=True)
        outs.append((_dot(e.astype(BF16), mv_ref[:, hs]) / l).astype(BF16))
    o = jnp.concatenate(outs, axis=-1)
    x2 = _layer_norm(alpha * x1 + _dot(o, wmo_ref[...]), g2_ref[...], b2_ref[...])
    x2_ref[...] = x2
    logits_t = _dot_nt(wrt_ref[...], x2.astype(BF16)) + br_ref[...]
    idx, gates, rank, carry = _route(logits_t, carry_s[:, 0:1], earlier_s[...])
    idx_ref[...] = idx
    gate_ref[...] = gates
    rank_ref[...] = rank
    carry_s[...] = jnp.broadcast_to(carry, carry_s.shape)
    cnt_ref[...] = carry_s[...]


def _round_bf16(x):
    return x.astype(BF16).astype(F32)


def _inproj_sample_kernel(x_ref, w_ref, o_ref):
    o_ref[...] = _dot(x_ref[...].astype(BF16), w_ref[...])


def _inproj_sample(x, w_in):
    n, D = x.shape
    N = w_in.shape[1]
    bn = 512
    return pl.pallas_call(
        _inproj_sample_kernel,
        grid=(N // bn,),
        in_specs=[pl.BlockSpec((n, D), lambda j: (0, 0)), pl.BlockSpec((D, bn), lambda j: (0, j))],
        out_specs=pl.BlockSpec((n, bn), lambda j: (0, j)),
        out_shape=jax.ShapeDtypeStruct((n, N), F32),
        compiler_params=pltpu.CompilerParams(dimension_semantics=("arbitrary",)),
        name="inproj_sample",
    )(x, w_in)


def _attn_sample_kernel(q_ref, kn_ref, vn_ref, kt_ref, vt_ref, bias_ref, bnew_ref, o_ref):
    H, E, W = kt_ref.shape

    def columns(ref):
        t = ref[...].T
        return jnp.stack([t[:E, h:h + 1] for h in range(H)], axis=0)

    q = _round_bf16(columns(q_ref))
    v_new = _round_bf16(columns(vn_ref))
    s_all = jnp.sum(_round_bf16(kt_ref[...]) * q, axis=1) * ATTN_SCALE
    s_new = jnp.sum(_round_bf16(columns(kn_ref)) * q, axis=1) * ATTN_SCALE
    outs, lses = [], []
    for p, (window, dil) in enumerate(PATTERNS):
        lo = W - window
        s = s_all[:, lo:] + bias_ref[p, :, lo:]
        sn = s_new + bnew_ref[p]
        m = jnp.maximum(jnp.max(s, axis=-1, keepdims=True), sn)
        e = jnp.exp(s - m)
        en = jnp.exp(sn - m)
        den = jnp.sum(e, axis=-1, keepdims=True) + en
        pr = _round_bf16(e / den)
        pv = jnp.sum(_round_bf16(vt_ref[:, :, lo:]) * pr[:, None, :], axis=-1, keepdims=True)
        outs.append(pv + _round_bf16(en / den)[:, :, None] * v_new)
        lses.append(m + jnp.log(den))
    lmax = jnp.maximum(jnp.maximum(lses[0], lses[1]), lses[2])
    ws = [jnp.exp(ls - lmax) for ls in lses]
    wsum = ws[0] + ws[1] + ws[2]
    acc = None
    for p in range(len(PATTERNS)):
        term = _round_bf16(ws[p] / wsum)[:, :, None] * _round_bf16(outs[p])
        acc = term if acc is None else acc + term
    lane = lax.broadcasted_iota(I32, o_ref.shape, 1)
    out = jnp.zeros(o_ref.shape, F32)
    for h in range(H):
        out = jnp.where(lane == h, acc[h], out)
    o_ref[...] = out


def _inproj_and_sample_attn_kernel(x_ref, w_ref, wc_ref, q4_ref, kn_ref, vn_ref, kt_ref, vt_ref, bpos_ref, bnew_ref,
                                   conv_ref, q_ref, k_ref, v_ref, cs_ref, attn_ref, u_s):
    _inproj_kernel(x_ref, w_ref, wc_ref, conv_ref, q_ref, k_ref, v_ref, cs_ref, u_s)
    _attn_sample_kernel(q4_ref, kn_ref, vn_ref, kt_ref, vt_ref, bpos_ref, bnew_ref, attn_ref)


def _inproj_prompt_and_attention_sample(x, w_in_bf, w_conv, q, k_new, v_new, win_kt, win_vt, bias_pos, bias_new):
    B, S, D = x.shape
    ts = PROJ_TILE
    n_s = S // ts
    DB, H, E, W = win_kt.shape
    assert B * n_s == DB
    row = lambda b, s: (b, s, 0)
    tok = lambda b, s: (b * n_s + s, 0, 0, 0)
    vec = pl.BlockSpec((None, LANES, LANES), lambda b, s: (b * n_s + s, 0, 0))
    vec_out = pl.BlockSpec((None, E, LANES), lambda b, s: (b * n_s + s, 0, 0))
    cache = pl.BlockSpec((None, H, E, W), tok)
    const = lambda a: pl.BlockSpec(a.shape, lambda b, s: (0,) * a.ndim)
    return pl.pallas_call(
        _inproj_and_sample_attn_kernel,
        grid=(B, n_s),
        in_specs=[pl.BlockSpec((None, ts, D), row), const(w_in_bf), const(w_conv),
                  vec, vec, vec, cache, cache, const(bias_pos), const(bias_new)],
        out_specs=[pl.BlockSpec((None, ts, CONV_WIDTH), row),
                   pl.BlockSpec((None, ts, ATTN_WIDTH), row),
                   pl.BlockSpec((None, ts, ATTN_WIDTH), row),
                   pl.BlockSpec((None, ts, ATTN_WIDTH), row),
                   pl.BlockSpec((None, 8, CONV_WIDTH), lambda b, s: (b, 0, 0)),
                   vec_out],
        out_shape=[jax.ShapeDtypeStruct((B, S, CONV_WIDTH), BF16),
                   jax.ShapeDtypeStruct((B, S, ATTN_WIDTH), F32),
                   jax.ShapeDtypeStruct((B, S, ATTN_WIDTH), F32),
                   jax.ShapeDtypeStruct((B, S, ATTN_WIDTH), F32),
                   jax.ShapeDtypeStruct((B, 8, CONV_WIDTH), F32),
                   jax.ShapeDtypeStruct((DB, E, LANES), F32)],
        scratch_shapes=[pltpu.VMEM((ts + 8, CONV_WIDTH), F32)],
        compiler_params=pltpu.CompilerParams(dimension_semantics=("arbitrary", "arbitrary"),
                                             vmem_limit_bytes=_FUSED_PROJ_VMEM_BYTES),
        name="inproj_prompt_attn_sample",
    )(x, w_in_bf, w_conv, q, k_new, v_new, win_kt, win_vt, bias_pos, bias_new)


def _tail_a_sample_kernel(x_ref, proj_ref, s0_ref, s1_ref, wc_ref, attn_ref, wo_ref, g1_ref, b1_ref, wq_ref,
                          x1_ref, qm_ref, u_ref, *, alpha):
    cw = CONV_WIDTH
    gb = proj_ref[:, 0:cw]
    u = proj_ref[:, cw:2 * cw] * proj_ref[:, 2 * cw:3 * cw]
    u_ref[...] = u
    wc = wc_ref[...]
    conv = gb * (wc[0:1, :] * s0_ref[...] + wc[1:2, :] * s1_ref[...] + wc[2:3, :] * u)
    mixed = jnp.concatenate([conv, attn_ref[...]], axis=-1).astype(BF16)
    x1 = _layer_norm(alpha * x_ref[...] + _dot(mixed, wo_ref[...]), g1_ref[...], b1_ref[...])
    x1_ref[...] = x1
    qm_ref[...] = _dot(x1.astype(BF16), wq_ref[...])


def _tail_a_sample(x, proj, s0, s1, w_conv, attn, w_out, g1, b1, w_mem_q, alpha):
    n, D = x.shape
    return pl.pallas_call(
        functools.partial(_tail_a_sample_kernel, alpha=alpha),
        out_shape=[jax.ShapeDtypeStruct((n, D), F32)] * 2 + [jax.ShapeDtypeStruct((n, CONV_WIDTH), F32)],
        name="tail_a_sample",
    )(x, proj, s0, s1, w_conv, attn, w_out, g1, b1, w_mem_q)


def _memattn_sample_kernel(q_ref, k_ref, v_ref, o_ref):
    q = _round_bf16(q_ref[...])
    s = jnp.sum(_round_bf16(k_ref[...]) * q[None], axis=-1, keepdims=True) * MEM_SCALE
    m = jnp.max(s, axis=0)
    e = jnp.exp(s - m[None])
    den = jnp.sum(e, axis=0)
    pr = _round_bf16(e / den[None])
    o_ref[...] = jnp.sum(pr * _round_bf16(v_ref[...]), axis=0)


_N_MID_IN = 15


def _mid_and_sample_memattn_kernel(*refs, alpha):
    mid_in, (qm_ref, smk_ref, smv_ref) = refs[:_N_MID_IN], refs[_N_MID_IN:_N_MID_IN + 3]
    mid_out, so_ref, (carry_s, earlier_s) = refs[_N_MID_IN + 3:-3], refs[-3], refs[-2:]
    _mid_kernel(*mid_in, *mid_out, carry_s, earlier_s, alpha=alpha)
    _memattn_sample_kernel(qm_ref, smk_ref, smv_ref, so_ref)


def _mid_prompt_and_memattn_sample(x, conv, attn, woc, woa, g1, b1, mk, mv, wq, wmo, g2, b2, wrt, br,
                                   qm_s, mem_k_s, mem_v_s, alpha):
    B, S, D = x.shape
    ts = SEQ_TILE
    n_s = S // ts
    T = B * S
    DB, M, H, E = mem_k_s.shape
    assert B * n_s == DB
    row = lambda b, s: (b, s, 0)
    full = lambda b, s: (0, 0)
    tok = lambda b, s: (0, b * n_s + s)
    mem = lambda b, s: (b, 0, 0)
    vec = pl.BlockSpec((1, D), full)
    s_vec = pl.BlockSpec((None, H, E), lambda b, s: (b * n_s + s, 0, 0))
    s_mem = pl.BlockSpec((None, M, H, E), lambda b, s: (b * n_s + s, 0, 0, 0))
    in_specs = [pl.BlockSpec((None, ts, D), row),
                pl.BlockSpec((None, ts, CONV_WIDTH), row),
                pl.BlockSpec((None, ts, ATTN_WIDTH), row),
                pl.BlockSpec(woc.shape, full), pl.BlockSpec(woa.shape, full), vec, vec,
                pl.BlockSpec((None, MEM_TOKENS, D), mem), pl.BlockSpec((None, MEM_TOKENS, D), mem),
                pl.BlockSpec(wq.shape, full), pl.BlockSpec(wmo.shape, full), vec, vec,
                pl.BlockSpec(wrt.shape, full), pl.BlockSpec(br.shape, full)]
    assert len(in_specs) == _N_MID_IN
    return pl.pallas_call(
        functools.partial(_mid_and_sample_memattn_kernel, alpha=alpha),
        grid=(B, n_s),
        in_specs=in_specs + [s_vec, s_mem, s_mem],
        out_specs=[pl.BlockSpec((None, ts, D), row),
                   pl.BlockSpec((TOP_K, ts), tok), pl.BlockSpec((TOP_K, ts), tok), pl.BlockSpec((TOP_K, ts), tok),
                   pl.BlockSpec((N_EXPERTS, LANES), full), s_vec],
        out_shape=[jax.ShapeDtypeStruct((B, S, D), F32),
                   jax.ShapeDtypeStruct((TOP_K, T), I32),
                   jax.ShapeDtypeStruct((TOP_K, T), F32),
                   jax.ShapeDtypeStruct((TOP_K, T), I32),
                   jax.ShapeDtypeStruct((N_EXPERTS, LANES), F32),
                   jax.ShapeDtypeStruct((DB, H, E), F32)],
        scratch_shapes=[pltpu.VMEM((N_EXPERTS, LANES), F32), pltpu.VMEM((ts, ts), BF16)],
        compiler_params=pltpu.CompilerParams(dimension_semantics=("arbitrary", "arbitrary")),
        name="mid_prompt_memattn_sample",
    )(x, conv, attn, woc, woa, g1, b1, mk, mv, wq, wmo, g2, b2, wrt, br, qm_s, mem_k_s, mem_v_s)


def _tail_b_sample_kernel(x1_ref, o_ref, wmo_ref, g2_ref, b2_ref, wrt_ref, br_ref, cnt_ref,
                          x2_ref, idx_ref, gate_ref, rank_ref, cnt_out_ref, *, alpha):
    x2 = _layer_norm(alpha * x1_ref[...] + _dot(o_ref[...].astype(BF16), wmo_ref[...]), g2_ref[...], b2_ref[...])
    x2_ref[...] = x2
    logits_t = _dot_nt(wrt_ref[...], x2.astype(BF16)) + br_ref[...]
    idx, gates, rank, carry = _route(logits_t, cnt_ref[:, 0:1], _earlier_token_matrix(logits_t.shape[1]))
    idx_ref[...] = idx
    gate_ref[...] = gates
    rank_ref[...] = rank
    cnt_out_ref[...] = jnp.broadcast_to(carry, cnt_out_ref.shape)


def _tail_b_sample(x1, o, w_mem_o, g2, b2, wrt, br, cnt, alpha):
    n, D = x1.shape
    return pl.pallas_call(
        functools.partial(_tail_b_sample_kernel, alpha=alpha),
        out_shape=[jax.ShapeDtypeStruct((n, D), F32),
                   jax.ShapeDtypeStruct((TOP_K, n), I32),
                   jax.ShapeDtypeStruct((TOP_K, n), F32),
                   jax.ShapeDtypeStruct((TOP_K, n), I32),
                   jax.ShapeDtypeStruct(cnt.shape, F32)],
        name="tail_b_sample",
    )(x1, o, w_mem_o, g2, b2, wrt, br, cnt)


def _row_copy(src, dst, sem, src_row, dst_row):
    return pltpu.make_async_copy(src.at[pl.ds(src_row, 1)], dst.at[pl.ds(dst_row, 1)], sem)


def _dispatch_kernel(dest_ref, x_ref, dest2_ref, x2_ref, xb_ref, zeros_s, stage_s, sems, zsem, *, n_tok_rows):
    i = pl.program_id(0)
    last = pl.num_programs(0) - 1
    slot = i % 2
    nt = dest_ref.shape[1] // TOP_K

    def start(src_ref, first, count, idx_ref, sem):
        def issue(g, c):
            for u in range(ROW_COPY_UNROLL):
                t = g * ROW_COPY_UNROLL + u
                for k in range(TOP_K):
                    _row_copy(src_ref, xb_ref, sem, first + t, idx_ref[0, k * count + t]).start(priority=k % 2)
            return c

        lax.fori_loop(0, count // ROW_COPY_UNROLL, issue, 0)

    def drain(src_ref, count, sem):
        for k in range(TOP_K):
            pltpu.make_async_copy(src_ref.at[pl.ds(0, count)], xb_ref.at[pl.ds(0, count)], sem).wait()

    @pl.when(i < last)
    def _():
        stage_s[slot] = x_ref[...]
        start(stage_s.at[slot], 0, nt, dest_ref, sems.at[slot])

    @pl.when(i == last)
    def _():
        zeros_s[...] = jnp.zeros_like(zeros_s)
        pltpu.make_async_copy(zeros_s, xb_ref.at[pl.ds(n_tok_rows, zeros_s.shape[0])], zsem).start()
        start(x2_ref, 0, x2_ref.shape[0], dest2_ref, sems.at[slot])

    @pl.when(i > 0)
    def _():
        drain(stage_s.at[1 - slot], nt, sems.at[1 - slot])

    @pl.when(i == last)
    def _():
        drain(x2_ref, x2_ref.shape[0], sems.at[slot])
        pltpu.make_async_copy(zeros_s, xb_ref.at[pl.ds(n_tok_rows, zeros_s.shape[0])], zsem).wait()


def _dispatch(x_a, dest_a, x_b, dest_b, n_rows):
    Ta, D = x_a.shape
    Tb = x_b.shape[0]
    nt = dest_a.shape[2] // TOP_K
    steps = Ta // nt
    assert nt % ROW_COPY_UNROLL == 0 and Tb % ROW_COPY_UNROLL == 0
    n_tok_rows = (Ta + Tb) * TOP_K
    tile = lambda i: (jnp.minimum(i, steps - 1), 0, 0)
    return pl.pallas_call(
        functools.partial(_dispatch_kernel, n_tok_rows=n_tok_rows),
        grid=(steps + 1,),
        in_specs=[pl.BlockSpec((None, 1, TOP_K * nt), tile, memory_space=pltpu.SMEM),
                  pl.BlockSpec((nt, D), lambda i: (jnp.minimum(i, steps - 1), 0)),
                  pl.BlockSpec((None, 1, TOP_K * Tb), lambda i: (0, 0, 0), memory_space=pltpu.SMEM),
                  pl.BlockSpec((Tb, D), lambda i: (0, 0))],
        out_specs=pl.BlockSpec(memory_space=pl.ANY),
        out_shape=jax.ShapeDtypeStruct((n_rows, D), x_a.dtype),
        scratch_shapes=[pltpu.VMEM((n_rows - n_tok_rows, D), x_a.dtype), pltpu.VMEM((2, nt, D), x_a.dtype),
                        pltpu.SemaphoreType.DMA((2,)), pltpu.SemaphoreType.DMA],
        compiler_params=pltpu.CompilerParams(dimension_semantics=("arbitrary",)),
        name="moe_dispatch",
    )(dest_a, x_a, dest_b, x_b)


def _moe_kernel(blk_ref, exp_ref, lo_ref, hi_ref, xb_ref, wgu_ref, bgu_ref, wd_ref, bd_ref, yb_ref,
                wgu_s, wd_s, cast_s):
    s = pl.program_id(0)
    new_block = jnp.logical_or(s == 0, blk_ref[s] != blk_ref[jnp.maximum(s - 1, 0)])
    lo, hi = lo_ref[s], hi_ref[s]
    used = hi > lo

    @pl.when(s == 0)
    def _():
        cast_s[0] = -1

    @pl.when(jnp.logical_and(used, cast_s[0] != exp_ref[s]))
    def _():
        wgu_s[...] = wgu_ref[...].astype(BF16)
        wd_s[...] = wd_ref[...].astype(BF16)
        cast_s[0] = exp_ref[s]

    @pl.when(new_block)
    def _():
        yb_ref[...] = jnp.zeros_like(yb_ref)

    @pl.when(used)
    def _():
        gu = _dot(xb_ref[...].astype(BF16), wgu_s[...]) + bgu_ref[...]
        g = jnp.minimum(gu[:, :D_FF], SWIGLU_LIMIT)
        u = jnp.clip(gu[:, D_FF:], -SWIGLU_LIMIT, SWIGLU_LIMIT)
        act = (u + 1.0) * g * jax.nn.sigmoid(SWIGLU_ALPHA * g)
        y = _dot(act.astype(BF16), wd_s[...]) + bd_ref[...]
        row = lax.broadcasted_iota(I32, (xb_ref.shape[0], 1), 0)
        yb_ref[...] = jnp.where(jnp.logical_and(row >= lo, row < hi), y, yb_ref[...])


def _moe(seg_blk, seg_exp, seg_lo, seg_hi, xb, w_gate_up, b_gate_up, w_down, b_down):
    P, D = xb.shape
    bm = ROW_BLOCK
    E, _, F2 = w_gate_up.shape
    ex = lambda s, blk, exp, lo, hi: (exp[s], 0, 0)
    rows = lambda s, blk, exp, lo, hi: (blk[s], 0)
    expert_bytes = (D * F2 + D_FF * D) * (2 * 4 + 2)
    block_bytes = bm * D * 4 * 2 * 2 + bm * F2 * 4 * 3
    return pl.pallas_call(
        _moe_kernel,
        grid_spec=pltpu.PrefetchScalarGridSpec(
            num_scalar_prefetch=4,
            grid=(seg_blk.shape[0],),
            in_specs=[pl.BlockSpec((bm, D), rows),
                      pl.BlockSpec((None, D, F2), ex), pl.BlockSpec((None, 1, F2), ex),
                      pl.BlockSpec((None, D_FF, D), ex), pl.BlockSpec((None, 1, D), ex)],
            out_specs=pl.BlockSpec((bm, D), rows),
            scratch_shapes=[pltpu.VMEM((D, F2), BF16), pltpu.VMEM((D_FF, D), BF16), pltpu.SMEM((1,), I32)]),
        out_shape=jax.ShapeDtypeStruct((P, D), F32),
        compiler_params=pltpu.CompilerParams(dimension_semantics=("arbitrary",),
                                             vmem_limit_bytes=expert_bytes + block_bytes),
        name="moe_grouped_ffn",
    )(seg_blk, seg_exp, seg_lo, seg_hi, xb, w_gate_up, b_gate_up.reshape(E, 1, F2), w_down, b_down.reshape(E, 1, D))


def _combine_kernel(dest_ref, next_ref, yb_ref, x2_ref, gate_ref, g3_ref, b3_ref, o_ref, rows_s, sems, *, alpha):
    nt = x2_ref.shape[0]
    i = pl.program_id(0)
    slot = i % 2

    def gather(idx_ref, buf):
        def issue(g, c):
            for u in range(ROW_COPY_UNROLL):
                t = g * ROW_COPY_UNROLL + u
                for k in range(TOP_K):
                    _row_copy(yb_ref, rows_s.at[buf, k], sems.at[buf], idx_ref[0, k * nt + t], t).start(priority=k % 2)
            return c
        lax.fori_loop(0, nt // ROW_COPY_UNROLL, issue, 0)

    def drain(buf):
        for k in range(TOP_K):
            pltpu.make_async_copy(yb_ref.at[pl.ds(0, nt)], rows_s.at[buf, k], sems.at[buf]).wait()

    @pl.when(i == 0)
    def _():
        gather(dest_ref, 0)

    drain(slot)
    gates = gate_ref[...]
    y = gates[:, 0:1] * rows_s[slot, 0]
    for k in range(1, TOP_K):
        y = y + gates[:, k:k + 1] * rows_s[slot, k]
    o_ref[...] = _layer_norm(alpha * x2_ref[...] + y, g3_ref[...], b3_ref[...])
    for t in range(nt):
        for k in range(TOP_K):
            _row_copy(yb_ref, rows_s.at[1 - slot, k], sems.at[1 - slot], next_ref[0, k * nt + t], t).start(priority=k % 2)

    @pl.when(i + 1 == pl.num_programs(0))
    def _():
        drain(1 - slot)


def _combine(yb, dest_tiles, x2, gates_tok, g3, b3, alpha):
    T, D = x2.shape
    steps = dest_tiles.shape[0]
    nt = dest_tiles.shape[2] // TOP_K
    assert nt % ROW_COPY_UNROLL == 0
    idx_spec = lambda ahead: pl.BlockSpec((None, 1, TOP_K * nt), lambda i: (jnp.minimum(i + ahead, steps - 1), 0, 0),
                                          memory_space=pltpu.SMEM)
    return pl.pallas_call(
        functools.partial(_combine_kernel, alpha=alpha),
        grid=(steps,),
        in_specs=[idx_spec(0), idx_spec(1),
                  pl.BlockSpec(memory_space=pl.ANY),
                  pl.BlockSpec((nt, D), lambda i: (i, 0)),
                  pl.BlockSpec((nt, TOP_K), lambda i: (i, 0)),
                  pl.BlockSpec((1, D), lambda i: (0, 0)), pl.BlockSpec((1, D), lambda i: (0, 0))],
        out_specs=pl.BlockSpec((nt, D), lambda i: (i, 0)),
        out_shape=jax.ShapeDtypeStruct((T, D), F32),
        scratch_shapes=[pltpu.VMEM((2, TOP_K, nt, D), F32), pltpu.SemaphoreType.DMA((2,))],
        compiler_params=pltpu.CompilerParams(dimension_semantics=("arbitrary",)),
        name="moe_combine",
    )(dest_tiles, dest_tiles, yb, x2, gates_tok, g3, b3)


def _rel_bucket(dist):
    max_exact = N_BUCKETS // 2
    df = jnp.maximum(dist, 1).astype(F32)
    large = max_exact + (jnp.log(df / max_exact) / math.log(MAX_DISTANCE / max_exact)
                         * (N_BUCKETS - max_exact)).astype(I32)
    return jnp.where(dist < max_exact, dist, jnp.minimum(large, N_BUCKETS - 1))


def _bias_tables(rel_table):
    n = BAND
    taps = jnp.arange(n + 1)
    by_tap = jnp.stack([rel_table[_rel_bucket(taps * dil)].astype(F32) for _, dil in PATTERNS])
    w = jnp.concatenate([by_tap[:, ::-1], jnp.full((len(PATTERNS), n - 1, N_HEADS), NEG_INF, F32)], axis=1)
    w = jnp.transpose(w, (0, 2, 1))
    return by_tap, w.reshape(len(PATTERNS), N_SLABS, HEADS_PER_SLAB, 2 * n)


def _tap_bias_by_slot(by_tap, W):
    assert PAST_LEN % W == 0 and PAST_LEN >= WINDOW_MAX
    n = BAND
    out = []
    for p, (window, dil) in enumerate(PATTERNS):
        taps = by_tap[p, :0:-1]
        col = jnp.concatenate([jnp.full((W // dil - n, N_HEADS), NEG_INF, F32), taps], axis=0)
        rest = jnp.full((W // dil, dil - 1, N_HEADS), NEG_INF, F32)
        out.append(jnp.concatenate([col[:, None, :], rest], axis=1).reshape(W, N_HEADS).T)
    return jnp.stack(out)


def _dest_tiles(dest, nt):
    T = dest.shape[1]
    return dest.reshape(TOP_K, T // nt, nt).transpose(1, 0, 2).reshape(T // nt, 1, TOP_K * nt)


def kernel(x_prompt, x_sample, mem_prompt, cache_win_k, cache_win_v, state_conv, cache_mem_k, cache_mem_v,
           rel_bias_table, w_in, w_conv, w_out, ln1_g, ln1_b, w_mem_q, w_mem_k, w_mem_v, w_mem_o,
           ln2_g, ln2_b, w_router, b_router, w_gate_up, b_gate_up, w_down, b_down, ln3_g, ln3_b):
    depth = w_in.shape[0]
    assert depth == 1
    alpha = (2 * depth) ** 0.25
    B, S, D = x_prompt.shape
    DB = x_sample.shape[0]
    T = B * S
    l = 0
    vec = lambda a: a[l].reshape(1, -1)

    by_tap, bias_band = _bias_tables(rel_bias_table)
    wrt = w_router[l].T.astype(BF16)
    br = b_router[l].reshape(N_EXPERTS, 1)
    w_in_bf = w_in[l].astype(BF16)
    wo_bf = w_out[l].astype(BF16)
    wq_bf = w_mem_q[l].astype(BF16)
    wmo_bf = w_mem_o[l].astype(BF16)

    xs = x_sample.reshape(DB, D)
    proj = _inproj_sample(xs, w_in_bf)
    cw = CONV_WIDTH
    q_s = proj[:, 3 * cw:3 * cw + ATTN_WIDTH]
    k_s = proj[:, 3 * cw + ATTN_WIDTH:3 * cw + 2 * ATTN_WIDTH]
    v_s = proj[:, 3 * cw + 2 * ATTN_WIDTH:]
    H, E = N_HEADS, HEAD_DIM
    bias_new = by_tap[:, 0, :, None]
    slots_last = lambda c: jnp.transpose(c, (0, 2, 3, 1))
    corner = lambda a: jnp.pad(a.reshape(DB, H, E), ((0, 0), (0, LANES - H), (0, LANES - E)))
    conv_p, q_p, k_p, v_p, cs_p, attn_s = _inproj_prompt_and_attention_sample(
        x_prompt, w_in_bf, w_conv[l], corner(q_s), corner(k_s), corner(v_s),
        slots_last(cache_win_k[l]), slots_last(cache_win_v[l]), _tap_bias_by_slot(by_tap, WINDOW_MAX), bias_new)
    attn_s = jnp.transpose(attn_s[:, :, :H], (0, 2, 1)).reshape(DB, ATTN_WIDTH)

    sc = state_conv[l]
    x1_s, qm_s, u_s = _tail_a_sample(xs, proj, sc[:, 0], sc[:, 1], w_conv[l], attn_s, wo_bf,
                                     vec(ln1_g), vec(ln1_b), wq_bf, alpha)
    attn_p = _attention_prompt(q_p, k_p, v_p, bias_band)
    mk, mv, mk_bf, mv_bf = _memkv_prompt(mem_prompt, w_mem_k[l].astype(BF16), w_mem_v[l].astype(BF16))
    x2_p, idx_p, gate_p, rank_p, cnt_p, o_s = _mid_prompt_and_memattn_sample(
        x_prompt, conv_p, attn_p, wo_bf[:CONV_WIDTH], wo_bf[CONV_WIDTH:], vec(ln1_g), vec(ln1_b), mk_bf, mv_bf,
        wq_bf, wmo_bf, vec(ln2_g), vec(ln2_b), wrt, br,
        qm_s.reshape(DB, MEM_HEADS, MEM_HEAD_DIM), cache_mem_k[l], cache_mem_v[l], alpha)
    x2_s, idx_s, gate_s, rank_s, cnt = _tail_b_sample(x1_s, o_s.reshape(DB, D), wmo_bf, vec(ln2_g), vec(ln2_b), wrt, br,
                                                      cnt_p, alpha)

    bm = ROW_BLOCK
    counts = cnt[:, 0].astype(I32)
    ends = jnp.cumsum(counts)
    starts = ends - counts
    n_tok_rows = (T + DB) * TOP_K
    n_blocks = -(-n_tok_rows // bm)
    P = n_blocks * bm
    cuts = jnp.sort(jnp.concatenate([jnp.arange(n_blocks, dtype=I32) * bm, ends[:-1]]))
    cut_ends = jnp.concatenate([cuts[1:], jnp.full((1,), P, I32)])
    seg_blk = jnp.minimum(cuts // bm, n_blocks - 1)
    seg_exp = jnp.minimum(jnp.sum((ends[None, :] <= cuts[:, None]).astype(I32), axis=1), N_EXPERTS - 1)
    seg_lo = cuts - seg_blk * bm
    seg_hi = cut_ends - seg_blk * bm
    expert_ids = jnp.arange(N_EXPERTS, dtype=I32)[:, None, None]
    start_of = lambda idx: jnp.sum(jnp.where(idx[None] == expert_ids, starts[:, None, None], 0), axis=0)
    dest_p = start_of(idx_p) + rank_p
    dest_s = start_of(idx_s) + rank_s
    dtiles_p = _dest_tiles(dest_p, COMBINE_TILE)
    dtiles_s = _dest_tiles(dest_s, DB)

    x2_pf = x2_p.reshape(T, D)
    xb = _dispatch(x2_pf, dtiles_p, x2_s, dtiles_s, P)
    yb = _moe(seg_blk, seg_exp, seg_lo, seg_hi, xb, w_gate_up[l], b_gate_up[l], w_down[l], b_down[l])
    y_p = _combine(yb, dtiles_p, x2_pf, gate_p.T, vec(ln3_g), vec(ln3_b), alpha).reshape(B, S, D)
    y_s = _combine(yb, dtiles_s, x2_s, gate_s.T, vec(ln3_g), vec(ln3_b), alpha).reshape(DB, 1, D)

    conv_state_s = jnp.stack([sc[:, 1], u_s], axis=1)
    return (y_p, y_s,
            k_p.reshape(1, B, S, H, E), v_p.reshape(1, B, S, H, E), cs_p[:, 6:8].reshape(1, B, CONV_K - 1, CONV_WIDTH),
            mk[None], mv[None],
            k_s.reshape(1, DB, 1, H, E), v_s.reshape(1, DB, 1, H, E), conv_state_s.reshape(1, DB, CONV_K - 1, CONV_WIDTH))
```
